```python
import math
import jax, jax.numpy as jnp
from jax import lax
import numpy as np

D_MODEL = 1024
BATCH = 8
SEQ = 8192
DEPTH = 2

PLE_DIM = 256
N_EVEN = (DEPTH + 1) // 2
N_ODD = DEPTH // 2
A_WIDTH = D_MODEL // 2
A_HEAD_DIM = 64
A_HEADS = A_WIDTH // (2 * A_HEAD_DIM)
A_SCALE = A_HEAD_DIM ** -0.5
Q_BLOCK = 128
REL_BUCKETS = 32
REL_MAX_DIST = 128
B_WIDTH = D_MODEL - A_WIDTH
LRU_BLOCKS = 8
LRU_BW = B_WIDTH // LRU_BLOCKS
CONV_W = 4
LRU_C = 8.0
AB_IN = 3 * A_WIDTH + 2 * B_WIDTH
RWKV_HEAD = 64
RWKV_HEADS = D_MODEL // RWKV_HEAD
DECAY_LORA = 64
AAA_LORA = 64
GATE_LORA = 160
RWKV_GN_EPS = 64e-5
MOE_GROUPS = 4
EXPERTS_PER_GROUP = 8
N_EXPERTS = MOE_GROUPS * EXPERTS_PER_GROUP
MOE_TOPK = 2
EXPERT_HIDDEN = 512
MOE_BLOCK = 512

kernel_name = "hybrid_diffattn_rglru_rwkv7_hmoe"


def _rmsnorm(x, g, eps=1e-6):
    x32 = x.astype(jnp.float32)
    y = x32 * lax.rsqrt(jnp.mean(x32 * x32, axis=-1, keepdims=True) + eps)
    return (y * g.astype(jnp.float32)).astype(x.dtype)


def _t5_bucket(rel):
    n = jnp.maximum(rel, 0)
    max_exact = REL_BUCKETS // 2
    large = max_exact + (jnp.log(jnp.maximum(n, 1).astype(jnp.float32) / max_exact)
                         / math.log(REL_MAX_DIST / max_exact)
                         * (REL_BUCKETS - max_exact)).astype(jnp.int32)
    large = jnp.minimum(large, REL_BUCKETS - 1)
    return jnp.where(n < max_exact, n, large)


def _diff_attention(q, k, v, rel_bias, lam, sub_g, lam_init):
    B, S = q.shape[:2]
    nblk = S // Q_BLOCK
    kt = k.transpose(0, 2, 3, 1, 4)
    vt = v.transpose(0, 2, 1, 3)
    qb = q.reshape(B, nblk, Q_BLOCK, A_HEADS, 2, A_HEAD_DIM).transpose(1, 0, 3, 4, 2, 5)
    kpos = jnp.arange(S)

    def block(args):
        qblk, bi = args
        qpos = bi * Q_BLOCK + jnp.arange(Q_BLOCK)
        rel = qpos[:, None] - kpos[None, :]
        bias = rel_bias[_t5_bucket(rel)].transpose(2, 0, 1).astype(jnp.float32)
        s = jnp.einsum('bhmqd,bhmkd->bhmqk', qblk, kt,
                       preferred_element_type=jnp.float32) * A_SCALE + bias[None, :, None]
        s = jnp.where(rel >= 0, s, -jnp.inf)
        pr = jax.nn.softmax(s, axis=-1)
        attn = pr[:, :, 0] - lam * pr[:, :, 1]
        return jnp.einsum('bhqk,bhkd->bhqd', attn.astype(vt.dtype), vt)

    o = lax.map(block, (qb, jnp.arange(nblk)))
    o = o.transpose(1, 0, 3, 2, 4).reshape(B, S, A_HEADS, 2 * A_HEAD_DIM)
    o = _rmsnorm(o, sub_g, eps=1e-5) * (1.0 - lam_init)
    return o.reshape(B, S, A_WIDTH)


def _rglru(u, wa, ba, wx, bx, lru_lambda):
    B, S, W = u.shape
    ub = u.reshape(B, S, LRU_BLOCKS, LRU_BW)
    r = jax.nn.sigmoid(jnp.einsum('bsnc,ncd->bsnd', ub, wa).reshape(B, S, W) + ba)
    ig = jax.nn.sigmoid(jnp.einsum('bsnc,ncd->bsnd', ub, wx).reshape(B, S, W) + bx)
    log_a = (-LRU_C * r.astype(jnp.float32)) * jax.nn.softplus(-lru_lambda.astype(jnp.float32))
    a = jnp.exp(log_a)
    b = jnp.sqrt(-jnp.expm1(2.0 * log_a)) * (ig * u).astype(jnp.float32)

    def combine(left, right):
        a1, b1 = left
        a2, b2 = right
        return a1 * a2, a2 * b1 + b2

    _, hs = lax.associative_scan(combine, (a, b), axis=1)
    return hs.astype(u.dtype)


def _mixer_ab(h, w_in, w_out, lq1, lk1, lq2, lk2, sub_g, conv_w, conv_b,
              wa, ba, wx, bx, lru_lambda, rel_bias, lam_init):
    B, S, _ = h.shape
    z = h @ w_in
    q, k, v, xb, gb = jnp.split(
        z, [A_WIDTH, 2 * A_WIDTH, 3 * A_WIDTH, 3 * A_WIDTH + B_WIDTH], axis=-1)
    q = q.reshape(B, S, A_HEADS, 2, A_HEAD_DIM)
    k = k.reshape(B, S, A_HEADS, 2, A_HEAD_DIM)
    v = v.reshape(B, S, A_HEADS, 2 * A_HEAD_DIM)
    lam = (jnp.exp(jnp.sum(lq1.astype(jnp.float32) * lk1.astype(jnp.float32)))
           - jnp.exp(jnp.sum(lq2.astype(jnp.float32) * lk2.astype(jnp.float32))) + lam_init)
    ya = _diff_attention(q, k, v, rel_bias, lam, sub_g, lam_init)
    u = lax.conv_general_dilated(
        xb, conv_w[:, None, :], window_strides=(1,), padding=[(CONV_W - 1, 0)],
        dimension_numbers=('NWC', 'WIO', 'NWC'), feature_group_count=B_WIDTH) + conv_b
    yb = _rglru(u, wa, ba, wx, bx, lru_lambda) * jax.nn.gelu(gb)
    return jnp.concatenate([ya, yb], axis=-1) @ w_out


def _rwkv7_scan(r, w, k, v, kk, a):
    B, S_len, H, N = r.shape
    tm = lambda t: jnp.moveaxis(t.astype(jnp.float32), 1, 0)

    def step(state, inp):
        r_t, w_t, k_t, v_t, kk_t, a_t = inp
        sa = jnp.einsum('bhij,bhj->bhi', state, -kk_t)
        state = (state * w_t[:, :, None, :] + sa[..., None] * (kk_t * a_t)[:, :, None, :]
                 + v_t[..., None] * k_t[:, :, None, :])
        return state, jnp.einsum('bhij,bhj->bhi', state, r_t)

    s0 = jnp.zeros((B, H, N, N), jnp.float32)
    _, ys = lax.scan(step, s0, (tm(r), tm(w), tm(k), tm(v), tm(kk), tm(a)))
    return jnp.moveaxis(ys, 0, 1)


def _mixer_rwkv(h, mu, wr, wk, wv, wo, w0, w1, w2, a0, a1, a2, g1, g2,
                k_k, k_a, r_k, ln_g, ln_b):
    B, S, D = h.shape
    H, N = RWKV_HEADS, RWKV_HEAD
    xx = jnp.pad(h, ((0, 0), (1, 0), (0, 0)))[:, :-1] - h
    xr, xw, xk, xv, xa, xg = [h + xx * mu[n] for n in range(6)]
    r = xr @ wr
    w_log = -jax.nn.softplus(-(w0 + jnp.tanh(xw @ w1) @ w2).astype(jnp.float32)) - 0.5
    decay = jnp.exp(-jnp.exp(w_log))
    k = xk @ wk
    v = xv @ wv
    a = jax.nn.sigmoid(a0 + (xa @ a1) @ a2)
    g = jax.nn.sigmoid(xg @ g1) @ g2
    hd = lambda t: t.reshape(B, S, H, N)
    kk = hd(k * k_k).astype(jnp.float32)
    kk = kk / jnp.maximum(jnp.sqrt(jnp.sum(kk * kk, axis=-1, keepdims=True)), 1e-12)
    k = k * (1 + (a - 1) * k_a)
    r4, k4, v4, a4 = hd(r), hd(k), hd(v), hd(a)
    y = _rwkv7_scan(r4, hd(decay), k4, v4, kk, a4)
    m = jnp.mean(y, axis=-1, keepdims=True)
    var = jnp.mean(jnp.square(y - m), axis=-1, keepdims=True)
    yn = ((y - m) * lax.rsqrt(var + RWKV_GN_EPS)).reshape(B, S, D)
    yn = (yn * ln_g.astype(jnp.float32) + ln_b.astype(jnp.float32)).astype(h.dtype)
    bonus = (jnp.sum(r4 * k4 * r_k, axis=-1, keepdims=True) * v4).reshape(B, S, D)
    return ((yn + bonus) * g) @ wo


def _hier_moe(h, wc, bc, wf, bf, w_gu, w_down):
    B, S, D = h.shape
    T = B * S
    M = T * MOE_TOPK
    n_blk = -(-(M + N_EXPERTS * (MOE_BLOCK - 1)) // MOE_BLOCK)
    xf = h.reshape(T, D)
    pc = jax.nn.softmax((xf @ wc + bc).astype(jnp.float32), axis=-1)
    gprob, gidx = lax.top_k(pc, 1)
    lf = (xf @ wf + bf).astype(jnp.float32).reshape(T, MOE_GROUPS, EXPERTS_PER_GROUP)
    lf = jnp.take_along_axis(lf, gidx[:, :, None], axis=1)[:, 0]
    fv, fi = lax.top_k(jax.nn.softmax(lf, axis=-1), MOE_TOPK)
    gate = gprob * fv / jnp.sum(fv, axis=-1, keepdims=True)
    eid = (gidx * EXPERTS_PER_GROUP + fi).reshape(-1)
    tok = jnp.repeat(jnp.arange(T, dtype=jnp.int32), MOE_TOPK)
    order = jnp.argsort(eid)
    e_s, tok_s, gate_s = eid[order], tok[order], gate.reshape(-1)[order]
    sizes = jnp.bincount(eid, length=N_EXPERTS)
    padded = (sizes + MOE_BLOCK - 1) // MOE_BLOCK * MOE_BLOCK
    pend = jnp.cumsum(padded)
    dest = (pend - padded)[e_s] + jnp.arange(M) - (jnp.cumsum(sizes) - sizes)[e_s]
    rows_tok = jnp.full((n_blk * MOE_BLOCK,), T, jnp.int32).at[dest].set(tok_s)
    rows_gate = jnp.zeros((n_blk * MOE_BLOCK,), jnp.float32).at[dest].set(gate_s)
    blk_e = jnp.minimum(jnp.searchsorted(pend, jnp.arange(n_blk) * MOE_BLOCK, side='right'),
                        N_EXPERTS - 1)
    xpad = jnp.concatenate([xf, jnp.zeros((1, D), xf.dtype)], axis=0)

    def expert_block(args):
        idx, e = args
        gpart, upart = jnp.split(xpad[idx] @ w_gu[e], 2, axis=-1)
        return (jax.nn.silu(gpart) * upart) @ w_down[e]

    ys = lax.map(expert_block, (rows_tok.reshape(n_blk, MOE_BLOCK), blk_e))
    ys = ys.reshape(-1, D) * rows_gate[:, None].astype(ys.dtype)
    out = jnp.zeros((T + 1, D), ys.dtype).at[rows_tok].add(ys)[:T]
    return out.reshape(B, S, D)


def _per_layer_embed(x, p_i, g_norm, w_gate, w_proj):
    gate = jax.nn.sigmoid(_rmsnorm(x, g_norm) @ w_gate)
    return (p_i @ w_proj) * gate


def setup_inputs(seed: int = 0) -> dict:
    key = jax.random.key(seed)
    ks = iter(jax.random.split(key, 64))
    f32 = jnp.float32
    nrm = lambda shape, fan_in, s=1.0: jax.random.normal(next(ks), shape, f32) * (s * fan_in ** -0.5)
    gain = lambda shape: 1.0 + 0.02 * jax.random.normal(next(ks), shape, f32)
    small = lambda shape, s=0.01: s * jax.random.normal(next(ks), shape, f32)
    unif = lambda shape, lo, hi: jax.random.uniform(next(ks), shape, f32, lo, hi)
    D = D_MODEL
    a_c = unif((N_EVEN, B_WIDTH), 0.9, 0.999)
    a_base = a_c ** (1.0 / LRU_C)
    lru_lambda = jnp.log(a_base) - jnp.log1p(-a_base)
    return {
        "x": jax.random.normal(next(ks), (BATCH, SEQ, D), f32),
        "p": jax.random.normal(next(ks), (DEPTH, BATCH, SEQ, PLE_DIM), f32),
        "rel_bias": small((REL_BUCKETS, A_HEADS), 0.5),
        "norm_mix": gain((DEPTH, D)),
        "ab_w_in": nrm((N_EVEN, D, AB_IN), D),
        "ab_w_out": nrm((N_EVEN, A_WIDTH + B_WIDTH, D), A_WIDTH + B_WIDTH),
        "lam_q1": small((N_EVEN, A_HEAD_DIM), 0.1),
        "lam_k1": small((N_EVEN, A_HEAD_DIM), 0.1),
        "lam_q2": small((N_EVEN, A_HEAD_DIM), 0.1),
        "lam_k2": small((N_EVEN, A_HEAD_DIM), 0.1),
        "sub_g": gain((N_EVEN, 2 * A_HEAD_DIM)),
        "conv_w": nrm((N_EVEN, CONV_W, B_WIDTH), CONV_W),
        "conv_b": small((N_EVEN, B_WIDTH)),
        "lru_wa": nrm((N_EVEN, LRU_BLOCKS, LRU_BW, LRU_BW), LRU_BW),
        "lru_ba": small((N_EVEN, B_WIDTH)),
        "lru_wx": nrm((N_EVEN, LRU_BLOCKS, LRU_BW, LRU_BW), LRU_BW),
        "lru_bx": small((N_EVEN, B_WIDTH)),
        "lru_lambda": lru_lambda,
        "rwkv_mu": unif((N_ODD, 6, D), 0.0, 1.0),
        "rwkv_wr": nrm((N_ODD, D, D), D),
        "rwkv_wk": nrm((N_ODD, D, D), D),
        "rwkv_wv": nrm((N_ODD, D, D), D),
        "rwkv_wo": nrm((N_ODD, D, D), D),
        "rwkv_w0": unif((N_ODD, D), -5.0, 0.0),
        "rwkv_w1": nrm((N_ODD, D, DECAY_LORA), D),
        "rwkv_w2": nrm((N_ODD, DECAY_LORA, D), DECAY_LORA, 0.5),
        "rwkv_a0": small((N_ODD, D), 0.1),
        "rwkv_a1": nrm((N_ODD, D, AAA_LORA), D),
        "rwkv_a2": nrm((N_ODD, AAA_LORA, D), AAA_LORA),
        "rwkv_g1": nrm((N_ODD, D, GATE_LORA), D),
        "rwkv_g2": nrm((N_ODD, GATE_LORA, D), GATE_LORA),
        "rwkv_kk": 0.85 * gain((N_ODD, D)),
        "rwkv_ka": gain((N_ODD, D)),
        "rwkv_rk": small((N_ODD, RWKV_HEADS, RWKV_HEAD), 0.1),
        "rwkv_ln_g": gain((N_ODD, D)),
        "rwkv_ln_b": small((N_ODD, D)),
        "norm_ffn": gain((DEPTH, D)),
        "moe_wc": nrm((DEPTH, D, MOE_GROUPS), D),
        "moe_bc": small((DEPTH, MOE_GROUPS)),
        "moe_wf": nrm((DEPTH, D, N_EXPERTS), D),
        "moe_bf": small((DEPTH, N_EXPERTS)),
        "moe_w_gu": nrm((DEPTH, N_EXPERTS, D, 2 * EXPERT_HIDDEN), D),
        "moe_w_down": nrm((DEPTH, N_EXPERTS, EXPERT_HIDDEN, D), EXPERT_HIDDEN),
        "ple_norm": gain((DEPTH, D)),
        "ple_gate": nrm((DEPTH, D, D), D),
        "ple_proj": nrm((DEPTH, PLE_DIM, D), PLE_DIM),
        "final_norm": gain((D,)),
    }


def reference(x, p, rel_bias, norm_mix, ab_w_in, ab_w_out, lam_q1, lam_k1, lam_q2, lam_k2,
              sub_g, conv_w, conv_b, lru_wa, lru_ba, lru_wx, lru_bx, lru_lambda,
              rwkv_mu, rwkv_wr, rwkv_wk, rwkv_wv, rwkv_wo, rwkv_w0, rwkv_w1, rwkv_w2,
              rwkv_a0, rwkv_a1, rwkv_a2, rwkv_g1, rwkv_g2, rwkv_kk, rwkv_ka, rwkv_rk,
              rwkv_ln_g, rwkv_ln_b, norm_ffn, moe_wc, moe_bc, moe_wf, moe_bf,
              moe_w_gu, moe_w_down, ple_norm, ple_gate, ple_proj, final_norm):
    for i in range(DEPTH):
        j = i // 2
        h = _rmsnorm(x, norm_mix[i])
        if i % 2 == 0:
            lam_init = 0.8 - 0.6 * math.exp(-0.3 * i)
            x = x + _mixer_ab(h, ab_w_in[j], ab_w_out[j], lam_q1[j], lam_k1[j], lam_q2[j],
                              lam_k2[j], sub_g[j], conv_w[j], conv_b[j], lru_wa[j], lru_ba[j],
                              lru_wx[j], lru_bx[j], lru_lambda[j], rel_bias, lam_init)
        else:
            x = x + _mixer_rwkv(h, rwkv_mu[j], rwkv_wr[j], rwkv_wk[j], rwkv_wv[j], rwkv_wo[j],
                                rwkv_w0[j], rwkv_w1[j], rwkv_w2[j], rwkv_a0[j], rwkv_a1[j],
                                rwkv_a2[j], rwkv_g1[j], rwkv_g2[j], rwkv_kk[j], rwkv_ka[j],
                                rwkv_rk[j], rwkv_ln_g[j], rwkv_ln_b[j])
        x = x + _hier_moe(_rmsnorm(x, norm_ffn[i]), moe_wc[i], moe_bc[i], moe_wf[i], moe_bf[i],
                          moe_w_gu[i], moe_w_down[i])
        x = x + _per_layer_embed(x, p[i], ple_norm[i], ple_gate[i], ple_proj[i])
    return _rmsnorm(x, final_norm)
```

```python
import functools
import math

import jax
import jax.numpy as jnp
from jax import lax
from jax.experimental import pallas as pl
from jax.experimental.pallas import tpu as pltpu

F32, BF16, I32 = jnp.float32, jnp.bfloat16, jnp.int32

A_HEADS = 4
A_HEAD_DIM = 64
A_SCALE = A_HEAD_DIM ** -0.5
REL_BUCKETS = 32
REL_MAX_DIST = 128
LRU_BLOCKS = 8
CONV_W = 4
LRU_C = 8.0
RWKV_HEAD = 64
RWKV_GN_EPS = 64e-5
MOE_GROUPS = 4
EXPERTS_PER_GROUP = 8
N_EXPERTS = MOE_GROUPS * EXPERTS_PER_GROUP
MOE_TOPK = 2
NORM_EPS = 1e-6
SUBNORM_EPS = 1e-5

LANES = 128
VMEM_LIMIT = 56 * 1024 * 1024
ROW_TILE = 512
ATTN_TILE = 256
LRU_TILE = 512
RWKV_CHUNK = 64
RWKV_TILE = 512
MOE_BLOCK = 512
COMBINE_TILE = 256
NEG_BIG = -1e30


def _cparams(*sem):
    return pltpu.CompilerParams(dimension_semantics=sem, vmem_limit_bytes=VMEM_LIMIT)


def _rms(x, g, eps):
    return x * lax.rsqrt(jnp.mean(x * x, axis=-1, keepdims=True) + eps) * g


def _dot(a, b):
    return jnp.dot(a, b, preferred_element_type=F32)


def _dot_nt(a, b):
    return lax.dot_general(a, b, (((1,), (1,)), ((), ())), preferred_element_type=F32)


def _dot_tn(a, b):
    return lax.dot_general(a, b, (((0,), (0,)), ((), ())), preferred_element_type=F32)


def _split2(x):
    hi = x.astype(BF16)
    lo = (x - hi.astype(F32)).astype(BF16)
    return hi, lo


def _softplus(x):
    return jnp.maximum(x, 0.0) + jnp.log1p(jnp.exp(-jnp.abs(x)))


def _neg_expm1(y):
    poly = 1.0 + y / 8.0
    for d in (7.0, 6.0, 5.0, 4.0, 3.0, 2.0):
        poly = 1.0 + (y / d) * poly
    return jnp.where(y > -0.25, -y * poly, 1.0 - jnp.exp(y))


def _gelu_tanh(x):
    c = math.sqrt(2.0 / math.pi)
    return 0.5 * x * (1.0 + jnp.tanh(c * (x + 0.044715 * (x * x * x))))


def _inproj_kernel(x_ref, g_ref, w_ref, q_ref, k_ref, v_ref, xb_ref, gb_ref):
    hn = _rms(x_ref[...], g_ref[...], NORM_EPS).astype(BF16)
    width = q_ref.shape[1]
    for c, o_ref in enumerate((q_ref, k_ref, v_ref, xb_ref, gb_ref)):
        r = _dot(hn, w_ref[:, c * width:(c + 1) * width])
        if c == 0:
            r = r * A_SCALE
        o_ref[...] = r.astype(o_ref.dtype)


def _inproj(x2, g, w_in):
    t, d = x2.shape
    width = w_in.shape[1] // 5
    tm = ROW_TILE
    row = lambda i: (i, 0)
    fixed = lambda i: (0, 0)
    return pl.pallas_call(
        _inproj_kernel,
        grid=(t // tm,),
        in_specs=[pl.BlockSpec((tm, d), row), pl.BlockSpec((1, d), fixed),
                  pl.BlockSpec(w_in.shape, fixed)],
        out_specs=[pl.BlockSpec((tm, width), row)] * 5,
        out_shape=[jax.ShapeDtypeStruct((t, width), BF16)] * 3
        + [jax.ShapeDtypeStruct((t, width), F32)] * 2,
        compiler_params=_cparams("arbitrary"),
        name="inproj",
    )(x2, g.reshape(1, d), w_in)


def _t5_bucket(rel):
    n = jnp.maximum(rel, 0)
    max_exact = REL_BUCKETS // 2
    large = max_exact + (jnp.log(jnp.maximum(n, 1).astype(F32) / max_exact)
                         / math.log(REL_MAX_DIST / max_exact)
                         * (REL_BUCKETS - max_exact)).astype(I32)
    large = jnp.minimum(large, REL_BUCKETS - 1)
    return jnp.where(n < max_exact, n, large)


def _bias_tables(rel_bias, tile):
    i = jnp.arange(tile)[:, None]
    j = jnp.arange(tile)[None, :]
    rel0 = i - j
    rel1 = rel0 + tile
    b0 = jnp.where((rel0 >= 0)[..., None], rel_bias[_t5_bucket(rel0)].astype(F32), NEG_BIG)
    b1 = rel_bias[_t5_bucket(rel1)].astype(F32)
    return jnp.stack([b0, b1]).transpose(3, 0, 1, 2)


def _attn_kernel(cfar_ref, q_ref, k_ref, v_ref, bias_ref, lam_ref, subg_ref, o_ref,
                 m_sc, l_sc, acc_sc, *, lam_init):
    h = pl.program_id(1)
    i = pl.program_id(2)
    tile = q_ref.shape[0]
    q = q_ref[...]
    lane = lax.broadcasted_iota(I32, q.shape, 1)
    zero = jnp.zeros_like(q)
    qm = (jnp.where(lane < A_HEAD_DIM, q, zero), jnp.where(lane >= A_HEAD_DIM, q, zero))
    m_sc[...] = jnp.full(m_sc.shape, NEG_BIG, F32)
    l_sc[...] = jnp.zeros(l_sc.shape, F32)
    acc_sc[...] = jnp.zeros(acc_sc.shape, F32)

    def block(j, bias):
        start = pl.multiple_of(j * tile, tile)
        kb = k_ref[pl.ds(start, tile), :]
        vb = v_ref[pl.ds(start, tile), :]
        for mi in range(2):
            s = _dot_nt(qm[mi], kb) + bias
            m_prev = m_sc[mi]
            m_new = jnp.maximum(m_prev, jnp.max(s, axis=-1, keepdims=True))
            alpha = jnp.exp(m_prev - m_new)
            p = jnp.exp(s - m_new)
            l_sc[mi] = alpha * l_sc[mi] + jnp.sum(p, axis=-1, keepdims=True)
            acc_sc[mi] = alpha * acc_sc[mi] + _dot(p.astype(BF16), vb)
            m_sc[mi] = m_new

    cfar = cfar_ref[h]

    def far(j, c):
        block(j, cfar)
        return c

    lax.fori_loop(0, jnp.maximum(i - 1, 0), far, 0)

    @pl.when(i >= 1)
    def _():
        block(i - 1, bias_ref[0, 1])

    block(i, bias_ref[0, 0])

    lam_rows = lam_ref[...]
    lam = (jnp.exp(jnp.sum(lam_rows[0:1] * lam_rows[1:2], axis=-1, keepdims=True))
           - jnp.exp(jnp.sum(lam_rows[2:3] * lam_rows[3:4], axis=-1, keepdims=True)) + lam_init)
    o = acc_sc[0] / l_sc[0] - lam * (acc_sc[1] / l_sc[1])
    o = _rms(o, subg_ref[...], SUBNORM_EPS) * (1.0 - lam_init)
    o_ref[...] = o.astype(o_ref.dtype)


def _diff_attention(q, k, v, rel_bias, lam_rows, sub_g, lam_init, batch, seq):
    t, width = q.shape
    hw = 2 * A_HEAD_DIM
    tile = ATTN_TILE
    nq = seq // tile
    tables = _bias_tables(rel_bias, tile)
    cfar = rel_bias[REL_BUCKETS - 1].astype(F32)
    return pl.pallas_call(
        functools.partial(_attn_kernel, lam_init=lam_init),
        grid=(batch, A_HEADS, nq),
        in_specs=[
            pl.BlockSpec(memory_space=pltpu.SMEM),
            pl.BlockSpec((tile, hw), lambda b, h, i: (b * nq + i, h)),
            pl.BlockSpec((seq, hw), lambda b, h, i: (b, h)),
            pl.BlockSpec((seq, hw), lambda b, h, i: (b, h)),
            pl.BlockSpec((1, 2, tile, tile), lambda b, h, i: (h, 0, 0, 0)),
            pl.BlockSpec(lam_rows.shape, lambda b, h, i: (0, 0)),
            pl.BlockSpec((1, hw), lambda b, h, i: (0, 0)),
        ],
        out_specs=pl.BlockSpec((tile, hw), lambda b, h, i: (b * nq + i, h)),
        out_shape=jax.ShapeDtypeStruct((t, width), BF16),
        scratch_shapes=[pltpu.VMEM((2, tile, 1), F32), pltpu.VMEM((2, tile, 1), F32),
                        pltpu.VMEM((2, tile, hw), F32)],
        compiler_params=_cparams("arbitrary", "arbitrary", "arbitrary"),
        name="diff_attn",
    )(cfar, q, k, v, tables, lam_rows, sub_g.reshape(1, hw))


def _lru_kernel(xb_ref, gb_ref, cw_ref, vec_ref, wa_ref, wx_ref, y_ref, prev_sc, h_sc, a_sc, b_sc):
    ts = xb_ref.shape[0]

    @pl.when(pl.program_id(1) == 0)
    def _():
        prev_sc[...] = jnp.zeros(prev_sc.shape, F32)
        h_sc[...] = jnp.zeros(h_sc.shape, F32)

    xb = xb_ref[...]
    hist = prev_sc.shape[0]
    ext = jnp.concatenate([prev_sc[...], xb], axis=0)
    u = vec_ref[0:1, :]
    for j in range(CONV_W):
        off = hist - (CONV_W - 1) + j
        u = u + cw_ref[j:j + 1, :] * ext[off:off + ts, :]
    prev_sc[...] = xb[ts - hist:, :]
    ub = u.astype(BF16)
    r = jax.nn.sigmoid(_dot(ub, wa_ref[...]) + vec_ref[1:2, :])
    ig = jax.nn.sigmoid(_dot(ub, wx_ref[...]) + vec_ref[2:3, :])
    log_a = (-LRU_C * r) * _softplus(-vec_ref[3:4, :])
    a_sc[...] = jnp.exp(log_a)
    b_sc[...] = jnp.sqrt(_neg_expm1(2.0 * log_a)) * (ig * u)

    def step(tt, h):
        h = a_sc[pl.ds(tt, 1), :] * h + b_sc[pl.ds(tt, 1), :]
        b_sc[pl.ds(tt, 1), :] = h
        return h

    h_sc[...] = lax.fori_loop(0, ts, step, h_sc[...], unroll=8)
    y_ref[...] = (b_sc[...] * _gelu_tanh(gb_ref[...])).astype(y_ref.dtype)


def _block_diag(w):
    n, c, d = w.shape
    eye = jnp.eye(n, dtype=w.dtype)
    return (w[:, :, None, :] * eye[:, None, :, None]).reshape(n * c, n * d)


def _rglru(xb, gb, conv_w, conv_b, wa, ba, wx, bx, lru_lambda, batch, seq):
    t, w = xb.shape
    ts = LRU_TILE
    nt = seq // ts
    vec = jnp.zeros((8, w), F32).at[0].set(conv_b).at[1].set(ba).at[2].set(bx).at[3].set(lru_lambda)
    row = lambda b, s: (b * nt + s, 0)
    fixed = lambda b, s: (0, 0)
    return pl.pallas_call(
        _lru_kernel,
        grid=(batch, nt),
        in_specs=[pl.BlockSpec((ts, w), row), pl.BlockSpec((ts, w), row),
                  pl.BlockSpec((CONV_W, w), fixed), pl.BlockSpec((8, w), fixed),
                  pl.BlockSpec((w, w), fixed), pl.BlockSpec((w, w), fixed)],
        out_specs=pl.BlockSpec((ts, w), row),
        out_shape=jax.ShapeDtypeStruct((t, w), BF16),
        scratch_shapes=[pltpu.VMEM((8, w), F32), pltpu.VMEM((1, w), F32),
                        pltpu.VMEM((ts, w), F32), pltpu.VMEM((ts, w), F32)],
        compiler_params=_cparams("arbitrary", "arbitrary"),
        name="rglru",
    )(xb, gb, conv_w, vec, _block_diag(wa).astype(BF16), _block_diag(wx).astype(BF16))


def _route(x, gf_ref, wrh_ref, wrl_ref, br_ref):
    hn = _rms(x, gf_ref[...], NORM_EPS)
    hh, hl = _split2(hn)
    wh = wrh_ref[...]
    lg = _dot(hh, wh) + _dot(hh, wrl_ref[...]) + _dot(hl, wh) + br_ref[...]
    lane = lax.broadcasted_iota(I32, lg.shape, 1)
    big = jnp.int32(1 << 20)
    is_g = lane < MOE_GROUPS
    gl = jnp.where(is_g, lg, NEG_BIG)
    gmax = jnp.max(gl, axis=-1, keepdims=True)
    gidx = jnp.min(jnp.where(gl == gmax, lane, big), axis=-1, keepdims=True)
    gsum = jnp.sum(jnp.where(is_g, jnp.exp(gl - gmax), 0.0), axis=-1, keepdims=True)
    gprob = 1.0 / gsum
    lo = MOE_GROUPS + gidx * EXPERTS_PER_GROUP
    fmask = (lane >= lo) & (lane < lo + EXPERTS_PER_GROUP)
    fl = jnp.where(fmask, lg, NEG_BIG)
    f1 = jnp.max(fl, axis=-1, keepdims=True)
    i1 = jnp.min(jnp.where(fl == f1, lane, big), axis=-1, keepdims=True)
    fl2 = jnp.where(lane == i1, NEG_BIG, fl)
    f2 = jnp.max(fl2, axis=-1, keepdims=True)
    i2 = jnp.min(jnp.where(fl2 == f2, lane, big), axis=-1, keepdims=True)
    e2 = jnp.exp(f2 - f1)
    den = 1.0 + e2
    g1 = gprob / den
    g2 = gprob * e2 / den
    id1 = (i1 - MOE_GROUPS).astype(F32)
    id2 = (i2 - MOE_GROUPS).astype(F32)
    return jnp.where(lane == 0, g1, jnp.where(lane == 1, g2,
                     jnp.where(lane == 2, id1, jnp.where(lane == 3, id2, 0.0))))


def _router_params(wc, bc, wf, bf):
    d = wc.shape[0]
    n = MOE_GROUPS + N_EXPERTS
    w = jnp.zeros((d, LANES), F32).at[:, :MOE_GROUPS].set(wc).at[:, MOE_GROUPS:n].set(wf)
    b = jnp.zeros((1, LANES), F32).at[0, :MOE_GROUPS].set(bc).at[0, MOE_GROUPS:n].set(bf)
    hi, lo = _split2(w)
    return hi, lo, b


def _outproj_kernel(x_ref, ya_ref, yb_ref, wo_ref, gf_ref, wrh_ref, wrl_ref, br_ref, x1_ref, rt_ref):
    half = ya_ref.shape[1]
    x1 = x_ref[...] + _dot(ya_ref[...], wo_ref[:half, :]) + _dot(yb_ref[...], wo_ref[half:, :])
    x1_ref[...] = x1
    rt_ref[...] = _route(x1, gf_ref, wrh_ref, wrl_ref, br_ref)


def _outproj(x2, ya, yb, w_out, gf, router):
    t, d = x2.shape
    half = ya.shape[1]
    tm = ROW_TILE
    row = lambda i: (i, 0)
    fixed = lambda i: (0, 0)
    wrh, wrl, br = router
    return pl.pallas_call(
        _outproj_kernel,
        grid=(t // tm,),
        in_specs=[pl.BlockSpec((tm, d), row), pl.BlockSpec((tm, half), row), pl.BlockSpec((tm, half), row),
                  pl.BlockSpec((2 * half, d), fixed), pl.BlockSpec((1, d), fixed),
                  pl.BlockSpec((d, LANES), fixed), pl.BlockSpec((d, LANES), fixed),
                  pl.BlockSpec((1, LANES), fixed)],
        out_specs=[pl.BlockSpec((tm, d), row), pl.BlockSpec((tm, LANES), row)],
        out_shape=[jax.ShapeDtypeStruct((t, d), F32), jax.ShapeDtypeStruct((t, LANES), F32)],
        compiler_params=_cparams("arbitrary"),
        name="outproj_route",
    )(x2, ya, yb, w_out, gf.reshape(1, d), wrh, wrl, br)


def _moe_plan(route, n_tok):
    m = n_tok * MOE_TOPK
    n_blk = -(-(m + N_EXPERTS * (MOE_BLOCK - 1)) // MOE_BLOCK)
    eid = route[:, 2:4].astype(I32).reshape(-1)
    onehot = (eid[:, None] == jnp.arange(N_EXPERTS, dtype=I32)[None, :]).astype(I32)
    csum = jnp.cumsum(onehot, axis=0)
    rank = jnp.sum(csum * onehot, axis=1) - 1
    sizes = csum[-1]
    padded = (sizes + MOE_BLOCK - 1) // MOE_BLOCK * MOE_BLOCK
    pend = jnp.cumsum(padded)
    dest = (pend - padded)[eid] + rank
    tok = jnp.repeat(jnp.arange(n_tok, dtype=I32), MOE_TOPK)
    rows_tok = jnp.zeros((n_blk * MOE_BLOCK,), I32).at[dest].set(tok)
    blk_e = jnp.minimum(jnp.searchsorted(pend, jnp.arange(n_blk, dtype=I32) * MOE_BLOCK, side='right'),
                        N_EXPERTS - 1).astype(I32)
    n_used = (pend[-1] // MOE_BLOCK).astype(I32).reshape(1)
    return rows_tok.reshape(n_blk, MOE_BLOCK), blk_e, n_used, dest.astype(I32)


def _idx_copy(idx_hbm, idx_sm, sem, step, slot):
    return pltpu.make_async_copy(idx_hbm.at[step], idx_sm.at[slot], sem.at[slot])


def _expert_kernel(blk_e_ref, nused_ref, rows_hbm, x_hbm, gf_ref, wgu_ref, wd_ref, ys_ref,
                   idx_sm, idx_sem, xbuf, row_sem):
    i = pl.program_id(0)
    n = pl.num_programs(0)
    slot = i % 2
    blk = xbuf.shape[0]

    @pl.when(i == 0)
    def _():
        _idx_copy(rows_hbm, idx_sm, idx_sem, 0, 0).start()

    _idx_copy(rows_hbm, idx_sm, idx_sem, i, slot).wait()

    @pl.when(i + 1 < n)
    def _():
        _idx_copy(rows_hbm, idx_sm, idx_sem, i + 1, 1 - slot).start()

    @pl.when(i < nused_ref[0])
    def _():
        def issue(r, c):
            tok = idx_sm[slot, r]
            pltpu.make_async_copy(x_hbm.at[pl.ds(tok, 1), :], xbuf.at[pl.ds(r, 1), :], row_sem).start()
            return c

        lax.fori_loop(0, blk, issue, 0, unroll=8)
        pltpu.make_async_copy(x_hbm.at[pl.ds(0, blk), :], xbuf, row_sem).wait()
        hn = _rms(xbuf[...], gf_ref[...], NORM_EPS).astype(BF16)
        gu = _dot(hn, wgu_ref[0])
        eh = gu.shape[1] // 2
        g = gu[:, :eh]
        act = (g * jax.nn.sigmoid(g) * gu[:, eh:]).astype(BF16)
        ys_ref[...] = _dot(act, wd_ref[0])

    @pl.when(i >= nused_ref[0])
    def _():
        ys_ref[...] = jnp.zeros(ys_ref.shape, F32)


def _experts(x1, rows_tok, blk_e, n_used, gf, w_gu, w_down):
    t, d = x1.shape
    n_blk, blk = rows_tok.shape
    eh2 = w_gu.shape[2]
    grid_spec = pltpu.PrefetchScalarGridSpec(
        num_scalar_prefetch=2,
        grid=(n_blk,),
        in_specs=[
            pl.BlockSpec(memory_space=pl.ANY),
            pl.BlockSpec(memory_space=pl.ANY),
            pl.BlockSpec((1, d), lambda i, e, n: (0, 0)),
            pl.BlockSpec((1, d, eh2), lambda i, e, n: (e[i], 0, 0)),
            pl.BlockSpec((1, eh2 // 2, d), lambda i, e, n: (e[i], 0, 0)),
        ],
        out_specs=pl.BlockSpec((blk, d), lambda i, e, n: (i, 0)),
        scratch_shapes=[pltpu.SMEM((2, blk), I32), pltpu.SemaphoreType.DMA((2,)),
                        pltpu.VMEM((blk, d), F32), pltpu.SemaphoreType.DMA],
    )
    return pl.pallas_call(
        _expert_kernel,
        grid_spec=grid_spec,
        out_shape=jax.ShapeDtypeStruct((n_blk * blk, d), F32),
        compiler_params=_cparams("arbitrary"),
        name="moe_experts",
    )(blk_e, n_used, rows_tok, x1, gf.reshape(1, d), w_gu, w_down)


def _combine_kernel(pos_hbm, ys_hbm, x1_ref, rt_ref, p_ref, gn_ref, wg_ref, wp_ref, fn_ref, o_ref,
                    idx_sm, idx_sem, ybuf, row_sem, *, final):
    i = pl.program_id(0)
    n = pl.num_programs(0)
    slot = i % 2
    tm = x1_ref.shape[0]

    @pl.when(i == 0)
    def _():
        _idx_copy(pos_hbm, idx_sm, idx_sem, 0, 0).start()

    _idx_copy(pos_hbm, idx_sm, idx_sem, i, slot).wait()

    @pl.when(i + 1 < n)
    def _():
        _idx_copy(pos_hbm, idx_sm, idx_sem, i + 1, 1 - slot).start()

    def issue(r, c):
        for s in range(MOE_TOPK):
            pos = idx_sm[slot, MOE_TOPK * r + s]
            pltpu.make_async_copy(ys_hbm.at[pl.ds(pos, 1), :], ybuf.at[s, pl.ds(r, 1), :], row_sem).start()
        return c

    lax.fori_loop(0, tm, issue, 0, unroll=8)
    for s in range(MOE_TOPK):
        pltpu.make_async_copy(ys_hbm.at[pl.ds(0, tm), :], ybuf.at[s], row_sem).wait()

    rt = rt_ref[...]
    x2 = x1_ref[...] + (rt[:, 0:1] * ybuf[0] + rt[:, 1:2] * ybuf[1])
    gate = jax.nn.sigmoid(_dot(_rms(x2, gn_ref[...], NORM_EPS).astype(BF16), wg_ref[...]))
    x3 = x2 + _dot(p_ref[...].astype(BF16), wp_ref[...]) * gate
    if final:
        x3 = _rms(x3, fn_ref[...], NORM_EPS)
    o_ref[...] = x3


def _combine(x1, route, ys, dest, p_i, ple_norm, ple_gate, ple_proj, final_norm, final):
    t, d = x1.shape
    tm = COMBINE_TILE
    pd = p_i.shape[1]
    pos = dest.reshape(t // tm, MOE_TOPK * tm)
    row = lambda i: (i, 0)
    fixed = lambda i: (0, 0)
    return pl.pallas_call(
        functools.partial(_combine_kernel, final=final),
        grid=(t // tm,),
        in_specs=[pl.BlockSpec(memory_space=pl.ANY), pl.BlockSpec(memory_space=pl.ANY),
                  pl.BlockSpec((tm, d), row), pl.BlockSpec((tm, LANES), row), pl.BlockSpec((tm, pd), row),
                  pl.BlockSpec((1, d), fixed), pl.BlockSpec((d, d), fixed), pl.BlockSpec((pd, d), fixed),
                  pl.BlockSpec((1, d), fixed)],
        out_specs=pl.BlockSpec((tm, d), row),
        out_shape=jax.ShapeDtypeStruct((t, d), F32),
        scratch_shapes=[pltpu.SMEM((2, MOE_TOPK * tm), I32), pltpu.SemaphoreType.DMA((2,)),
                        pltpu.VMEM((MOE_TOPK, tm, d), F32), pltpu.SemaphoreType.DMA],
        compiler_params=_cparams("arbitrary"),
        name="moe_combine_ple",
    )(pos, ys, x1, route, p_i, ple_norm.reshape(1, d), ple_gate, ple_proj, final_norm.reshape(1, d))


def _seg_sum(x, eseg_ref, eexp_ref):
    hi, lo = _split2(x)
    e = eseg_ref[...]
    s = _dot(hi, e) + _dot(lo, e)
    shi, slo = _split2(s)
    ex = eexp_ref[...]
    return _dot(shi, ex) + _dot(slo, ex)


def _rwkv_pre_kernel(x_ref, xp_ref, gm_ref, mu_ref, vec_ref, wr_ref, wk_ref, wv_ref, w1_ref, w2_ref,
                     a1_ref, a2_ref, g1_ref, g2_ref, tri_ref, eseg_ref, eexp_ref,
                     la_ref, lr_ref, rb_ref, rk_ref, sb_ref, sk_ref, v_ref, bon_ref, g_ref, gc_ref,
                     c_sc):
    tl = x_ref.shape[0]
    gm = gm_ref[...]
    h = _rms(x_ref[...], gm, NORM_EPS)
    hp = _rms(xp_ref[...], gm, NORM_EPS)[7:8, :]
    hp = jnp.where(pl.program_id(1) == 0, 0.0, hp)
    row = lax.broadcasted_iota(I32, h.shape, 0)
    hs = jnp.where(row == 0, hp, pltpu.roll(h, 1, axis=0))
    xx = hs - h
    mix = lambda n: (h + xx * mu_ref[n:n + 1, :]).astype(BF16)
    vec = lambda n: vec_ref[n:n + 1, :]
    r = _dot(mix(0), wr_ref[...])
    wl = vec(0) + _dot(jnp.tanh(_dot(mix(1), w1_ref[...])).astype(BF16), w2_ref[...])
    lw = -jnp.exp(-_softplus(-wl) - 0.5)
    k = _dot(mix(2), wk_ref[...])
    v = _dot(mix(3), wv_ref[...])
    a = jax.nn.sigmoid(vec(1) + _dot(_dot(mix(4), a1_ref[...]).astype(BF16), a2_ref[...]))
    g = _dot(jax.nn.sigmoid(_dot(mix(5), g1_ref[...])).astype(BF16), g2_ref[...])
    kk = k * vec(2)
    kk = kk / jnp.maximum(jnp.sqrt(_seg_sum(kk * kk, eseg_ref, eexp_ref)), 1e-12)
    k2 = k * (1.0 + (a - 1.0) * vec(3))
    bon_ref[...] = (_seg_sum(r * k2 * vec(4), eseg_ref, eexp_ref) * v).astype(bon_ref.dtype)
    v_ref[...] = v.astype(v_ref.dtype)
    g_ref[...] = g.astype(g_ref.dtype)

    half = tri_ref.shape[1]
    tri = tri_ref[...]
    for s in range(tl // half):
        part = lw[s * half:(s + 1) * half, :]
        p1 = part.astype(BF16)
        rem = part - p1.astype(F32)
        p2 = rem.astype(BF16)
        p3 = (rem - p2.astype(F32)).astype(BF16)
        cs = _dot(tri, p1) + _dot(tri, p2) + _dot(tri, p3)
        c_sc[0, s * half:(s + 1) * half, :] = cs[:half]
        c_sc[1, s * half:(s + 1) * half, :] = cs[half:]
    c_in = c_sc[0]
    c_end = c_sc[1]
    nc = tl // RWKV_CHUNK
    ends = [c_sc[0, (n + 1) * RWKV_CHUNK - 1:(n + 1) * RWKV_CHUNK, :] for n in range(nc)]
    gc_ref[...] = jnp.exp(jnp.concatenate(ends, axis=0))
    e_in = jnp.exp(c_in)
    e_neg = jnp.exp(-c_in)
    e_end = jnp.exp(c_end)
    b = kk * a
    la_ref[...] = (-kk * jnp.exp(c_in - lw)).astype(la_ref.dtype)
    lr_ref[...] = (r * e_in).astype(lr_ref.dtype)
    rb_ref[...] = (b * e_neg).astype(rb_ref.dtype)
    rk_ref[...] = (k2 * e_neg).astype(rk_ref.dtype)
    sb_ref[...] = (b * e_end).astype(sb_ref.dtype)
    sk_ref[...] = (k2 * e_end).astype(sk_ref.dtype)


def _head_indicator(d):
    heads = d // RWKV_HEAD
    e = (jnp.arange(d)[:, None] // RWKV_HEAD == jnp.arange(LANES)[None, :]).astype(BF16)
    del heads
    return e, e.T


def _cumsum_matrix(half):
    i = jnp.arange(half)[:, None]
    j = jnp.arange(half)[None, :]
    same = (i // RWKV_CHUNK) == (j // RWKV_CHUNK)
    lower = same & (j <= i)
    upper = same & (j > i)
    return jnp.concatenate([lower, upper], axis=0).astype(BF16)


def _pad_cols(w, n):
    return jnp.pad(w, ((0, 0), (0, n - w.shape[1])))


def _pad_rows(w, n):
    return jnp.pad(w, ((0, n - w.shape[0]), (0, 0)))


def _rwkv_pre(x3, gm, mu, vecs, wr, wk, wv, w1, w2, a1, a2, g1, g2, batch, seq):
    t, d = x3.shape
    tl = RWKV_TILE
    nt = seq // tl
    half = 256
    lora = lambda w_in, w_out: (_pad_cols(w_in, -(-w_in.shape[1] // LANES) * LANES).astype(BF16),
                                _pad_rows(w_out, -(-w_out.shape[0] // LANES) * LANES).astype(BF16))
    w1p, w2p = lora(w1, w2)
    a1p, a2p = lora(a1, a2)
    g1p, g2p = lora(g1, g2)
    eseg, eexp = _head_indicator(d)
    tri = _cumsum_matrix(half)
    row = lambda b, s: (b * nt + s, 0)
    fixed = lambda b, s: (0, 0)
    full = lambda a: pl.BlockSpec(a.shape, fixed)
    prev = lambda b, s: (jnp.maximum((b * nt + s) * (tl // 8) - 1, 0), 0)
    weights = (wr.astype(BF16), wk.astype(BF16), wv.astype(BF16), w1p, w2p, a1p, a2p, g1p, g2p, tri, eseg, eexp)
    outs = pl.pallas_call(
        _rwkv_pre_kernel,
        grid=(batch, nt),
        in_specs=[pl.BlockSpec((tl, d), row), pl.BlockSpec((8, d), prev), pl.BlockSpec((1, d), fixed),
                  full(mu), full(vecs)] + [full(w) for w in weights],
        out_specs=[pl.BlockSpec((tl, d), row)] * 9 + [pl.BlockSpec((tl // RWKV_CHUNK, d), row)],
        out_shape=[jax.ShapeDtypeStruct((t, d), BF16)] * 9 + [jax.ShapeDtypeStruct((t // RWKV_CHUNK, d), F32)],
        scratch_shapes=[pltpu.VMEM((2, tl, d), F32)],
        compiler_params=_cparams("arbitrary", "arbitrary"),
        name="rwkv_pre",
    )(x3, x3, gm.reshape(1, d), mu, vecs, *weights)
    return outs


def _rwkv_scan_kernel(la_ref, lr_ref, rb_ref, rk_ref, sb_ref, sk_ref, v_ref, gc_ref, bon_ref, g_ref, x_ref,
                      vec_ref, wo_ref, eseg_ref, eexp_ref, gf_ref, wrh_ref, wrl_ref, br_ref,
                      x4_ref, rt_ref, s_sc, y_sc):
    tl, d = x_ref.shape
    ch = RWKV_CHUNK
    pw = 2 * RWKV_HEAD
    npair = d // pw

    @pl.when(pl.program_id(1) == 0)
    def _():
        s_sc[...] = jnp.zeros(s_sc.shape, F32)

    ri = lax.broadcasted_iota(I32, (pw, pw), 0)
    ci = lax.broadcasted_iota(I32, (pw, pw), 1)
    same_half = (ri < RWKV_HEAD) == (ci < RWKV_HEAD)
    eye = (ri == ci).astype(F32)
    rt_ = lax.broadcasted_iota(I32, (ch, pw), 0)
    ct_ = lax.broadcasted_iota(I32, (ch, pw), 1)
    first = ct_ < RWKV_HEAD
    strict = rt_ > (ct_ & (RWKV_HEAD - 1))
    incl = rt_ >= (ct_ & (RWKV_HEAD - 1))
    m_s0, m_s1 = strict & first, strict & ~first
    m_i0, m_i1 = incl & first, incl & ~first
    first_full = ci < RWKV_HEAD
    zf = jnp.zeros((ch, pw), F32)
    gc_row = lax.broadcasted_iota(I32, (tl // ch, pw), 0)

    def chunk(c, carry):
        rows = pl.ds(pl.multiple_of(c * ch, ch), ch)
        for p in range(npair):
            cols = slice(p * pw, (p + 1) * pw)
            la, lr = la_ref[rows, cols], lr_ref[rows, cols]
            rb, rk = rb_ref[rows, cols], rk_ref[rows, cols]
            vv = v_ref[rows, cols]
            lcat = jnp.concatenate([la, lr], axis=0)
            zb = jnp.zeros_like(lcat)
            a0 = _dot_nt(jnp.where(first_full, lcat, zb), jnp.concatenate([rb, rk], axis=0))
            a1 = _dot_nt(jnp.where(first_full, zb, lcat), jnp.concatenate([rk, rb], axis=0))
            a0t, a0b, a1t, a1b = a0[:ch], a0[ch:], a1[:ch], a1[ch:]
            pm = jnp.concatenate([jnp.where(m_s0, a0t, zf), jnp.where(m_s1, a1t, zf)], axis=0)
            aak = jnp.concatenate([jnp.where(m_s1, a0t, zf), jnp.where(m_s0, a1t, zf)], axis=0)
            arb = jnp.concatenate([jnp.where(m_i0, a0b, zf), jnp.where(m_i1, a1b, zf)], axis=0)
            ark = jnp.concatenate([jnp.where(m_i1, a0b, zf), jnp.where(m_i0, a1b, zf)], axis=0)
            tm = eye + pm
            qm = pm
            for _ in range(5):
                qb = qm.astype(BF16)
                qm = _dot(qb, qb)
                tm = tm + _dot(tm.astype(BF16), qm.astype(BF16))
            st = s_sc[p]
            w12 = _dot_nt(lcat, st.astype(BF16))
            w1, w2 = w12[:ch], w12[ch:]
            v2 = jnp.concatenate([vv, vv], axis=0)
            rhs = jnp.concatenate([w1, w1], axis=0) + _dot(aak.astype(BF16), v2)
            ust = _dot(tm.astype(BF16), rhs.astype(BF16))
            uv = jnp.concatenate([ust.astype(BF16), v2], axis=0)
            yst = jnp.concatenate([w2, w2], axis=0) + _dot(
                jnp.concatenate([arb, ark], axis=1).astype(BF16), uv)
            y_sc[rows, cols] = jnp.where(first, yst[:ch], yst[ch:])
            u = jnp.where(first, ust[:ch], ust[ch:])
            ds_ = _dot_tn(jnp.concatenate([u.astype(BF16), vv], axis=0),
                          jnp.concatenate([sb_ref[rows, cols], sk_ref[rows, cols]], axis=0))
            gcr = jnp.sum(jnp.where(gc_row == c, gc_ref[:, cols], 0.0), axis=0, keepdims=True)
            s_sc[p] = jnp.where(same_half, st * gcr + ds_, 0.0)
        return carry

    lax.fori_loop(0, tl // ch, chunk, 0)

    y = y_sc[...]
    inv_n = 1.0 / RWKV_HEAD
    mean = _seg_sum(y, eseg_ref, eexp_ref) * inv_n
    dlt = y - mean
    var = _seg_sum(dlt * dlt, eseg_ref, eexp_ref) * inv_n
    yn = dlt * lax.rsqrt(var + RWKV_GN_EPS) * vec_ref[5:6, :] + vec_ref[6:7, :]
    z = (yn + bon_ref[...].astype(F32)) * g_ref[...].astype(F32)
    x4 = x_ref[...] + _dot(z.astype(BF16), wo_ref[...])
    x4_ref[...] = x4
    rt_ref[...] = _route(x4, gf_ref, wrh_ref, wrl_ref, br_ref)


def _rwkv_scan(pre, x3, vecs, wo, gf, router, batch, seq):
    la, lr, rb, rk, sb, sk, v, bon, g, gc = pre
    t, d = x3.shape
    tl = RWKV_TILE
    nt = seq // tl
    eseg, eexp = _head_indicator(d)
    wrh, wrl, br = router
    row = lambda b, s: (b * nt + s, 0)
    fixed = lambda b, s: (0, 0)
    big = pl.BlockSpec((tl, d), row)
    full = lambda a: pl.BlockSpec(a.shape, fixed)
    pw = 2 * RWKV_HEAD
    return pl.pallas_call(
        _rwkv_scan_kernel,
        grid=(batch, nt),
        in_specs=[big] * 7 + [pl.BlockSpec((tl // RWKV_CHUNK, d), row), big, big, big,
                              full(vecs), full(wo), full(eseg), full(eexp),
                              pl.BlockSpec((1, d), fixed), full(wrh), full(wrl), full(br)],
        out_specs=[big, pl.BlockSpec((tl, LANES), row)],
        out_shape=[jax.ShapeDtypeStruct((t, d), F32), jax.ShapeDtypeStruct((t, LANES), F32)],
        scratch_shapes=[pltpu.VMEM((d // pw, pw, pw), F32), pltpu.VMEM((tl, d), F32)],
        compiler_params=_cparams("arbitrary", "arbitrary"),
        name="rwkv_scan_out",
    )(la, lr, rb, rk, sb, sk, v, gc, bon, g, x3, vecs, wo, eseg, eexp, gf.reshape(1, d), wrh, wrl, br)


def _moe_and_embed(x1, route, i, p, norm_ffn, moe_w_gu, moe_w_down, ple_norm, ple_gate, ple_proj,
                   final_norm, final):
    t = x1.shape[0]
    rows_tok, blk_e, n_used, dest = _moe_plan(route, t)
    ys = _experts(x1, rows_tok, blk_e, n_used, norm_ffn[i], moe_w_gu[i].astype(BF16),
                  moe_w_down[i].astype(BF16))
    p_i = p[i].reshape(t, p.shape[-1])
    return _combine(x1, route, ys, dest, p_i, ple_norm[i], ple_gate[i].astype(BF16),
                    ple_proj[i].astype(BF16), final_norm, final)


def kernel(x, p, rel_bias, norm_mix, ab_w_in, ab_w_out, lam_q1, lam_k1, lam_q2, lam_k2, sub_g, conv_w, conv_b, lru_wa, lru_ba, lru_wx, lru_bx, lru_lambda, rwkv_mu, rwkv_wr, rwkv_wk, rwkv_wv, rwkv_wo, rwkv_w0, rwkv_w1, rwkv_w2, rwkv_a0, rwkv_a1, rwkv_a2, rwkv_g1, rwkv_g2, rwkv_kk, rwkv_ka, rwkv_rk, rwkv_ln_g, rwkv_ln_b, norm_ffn, moe_wc, moe_bc, moe_wf, moe_bf, moe_w_gu, moe_w_down, ple_norm, ple_gate, ple_proj, final_norm):
    batch, seq, d = x.shape
    t = batch * seq
    depth = norm_mix.shape[0]
    assert depth == 2 and seq % RWKV_TILE == 0 and seq % LRU_TILE == 0 and t % ROW_TILE == 0
    xs = x.reshape(t, d)
    for i in range(depth):
        j = i // 2
        router = _router_params(moe_wc[i], moe_bc[i], moe_wf[i], moe_bf[i])
        if i % 2 == 0:
            lam_init = 0.8 - 0.6 * math.exp(-0.3 * i)
            q, k, v, xb, gb = _inproj(xs, norm_mix[i], ab_w_in[j].astype(BF16))
            lam_rows = jnp.stack([lam_q1[j], lam_k1[j], lam_q2[j], lam_k2[j]]).astype(F32)
            ya = _diff_attention(q, k, v, rel_bias, lam_rows, sub_g[j], lam_init, batch, seq)
            yb = _rglru(xb, gb, conv_w[j], conv_b[j], lru_wa[j], lru_ba[j], lru_wx[j], lru_bx[j],
                        lru_lambda[j], batch, seq)
            x1, route = _outproj(xs, ya, yb, ab_w_out[j].astype(BF16), norm_ffn[i], router)
        else:
            vecs = jnp.stack([rwkv_w0[j], rwkv_a0[j], rwkv_kk[j], rwkv_ka[j], rwkv_rk[j].reshape(-1),
                              rwkv_ln_g[j], rwkv_ln_b[j], jnp.zeros((d,), F32)]).astype(F32)
            pre = _rwkv_pre(xs, norm_mix[i], rwkv_mu[j], vecs, rwkv_wr[j], rwkv_wk[j], rwkv_wv[j],
                            rwkv_w1[j], rwkv_w2[j], rwkv_a1[j], rwkv_a2[j], rwkv_g1[j], rwkv_g2[j],
                            batch, seq)
            x1, route = _rwkv_scan(pre, xs, vecs, rwkv_wo[j].astype(BF16), norm_ffn[i], router, batch, seq)
        xs = _moe_and_embed(x1, route, i, p, norm_ffn, moe_w_gu, moe_w_down, ple_norm, ple_gate,
                            ple_proj, final_norm, final=(i == depth - 1))
    return xs.reshape(batch, seq, d)
```

```python
import functools
import math

import jax
import jax.numpy as jnp
from jax import lax
from jax.experimental import pallas as pl
from jax.experimental.pallas import tpu as pltpu

F32, BF16, I32 = jnp.float32, jnp.bfloat16, jnp.int32

A_HEADS = 4
A_HEAD_DIM = 64
A_SCALE = A_HEAD_DIM ** -0.5
REL_BUCKETS = 32
REL_MAX_DIST = 128
LRU_BLOCKS = 8
CONV_W = 4
LRU_C = 8.0
RWKV_HEAD = 64
RWKV_GN_EPS = 64e-5
MOE_GROUPS = 4
EXPERTS_PER_GROUP = 8
N_EXPERTS = MOE_GROUPS * EXPERTS_PER_GROUP
MOE_TOPK = 2
NORM_EPS = 1e-6
SUBNORM_EPS = 1e-5

LANES = 128
VMEM_LIMIT = 56 * 1024 * 1024
ROW_TILE = 512
ATTN_TILE = 512
LRU_TILE = 512
RWKV_CHUNK = 64
RWKV_TILE = 512
MOE_BLOCK = 512
COMBINE_TILE = 256
NEG_BIG = -1e30


def _cparams(*sem):
    return pltpu.CompilerParams(dimension_semantics=sem, vmem_limit_bytes=VMEM_LIMIT)


def _rms(x, g, eps):
    return x * lax.rsqrt(jnp.mean(x * x, axis=-1, keepdims=True) + eps) * g


def _dot(a, b):
    return jnp.dot(a, b, preferred_element_type=F32)


def _dot_nt(a, b):
    return lax.dot_general(a, b, (((1,), (1,)), ((), ())), preferred_element_type=F32)


def _dot_tn(a, b):
    return lax.dot_general(a, b, (((0,), (0,)), ((), ())), preferred_element_type=F32)


def _split2(x):
    hi = x.astype(BF16)
    lo = (x - hi.astype(F32)).astype(BF16)
    return hi, lo


def _softplus(x):
    return jnp.maximum(x, 0.0) + jnp.log1p(jnp.exp(-jnp.abs(x)))


def _neg_expm1(y):
    poly = 1.0 + y / 8.0
    for d in (7.0, 6.0, 5.0, 4.0, 3.0, 2.0):
        poly = 1.0 + (y / d) * poly
    return jnp.where(y > -0.25, -y * poly, 1.0 - jnp.exp(y))


def _gelu_tanh(x):
    c = math.sqrt(2.0 / math.pi)
    return 0.5 * x * (1.0 + jnp.tanh(c * (x + 0.044715 * (x * x * x))))


def _inproj_kernel(x_ref, g_ref, w_ref, q_ref, k_ref, vt_ref, xb_ref, gb_ref):
    hn = _rms(x_ref[...], g_ref[...], NORM_EPS).astype(BF16)
    width = q_ref.shape[1]
    for c, o_ref in enumerate((q_ref, k_ref, vt_ref, xb_ref, gb_ref)):
        r = _dot(hn, w_ref[:, c * width:(c + 1) * width])
        if c == 0:
            r = r * A_SCALE
        if c == 2:
            r = r.T
        o_ref[...] = r.astype(o_ref.dtype)


def _inproj(x2, g, w_in):
    t, d = x2.shape
    width = w_in.shape[1] // 5
    tm = ROW_TILE
    row = lambda i: (i, 0)
    fixed = lambda i: (0, 0)
    rows = pl.BlockSpec((tm, width), row)
    return pl.pallas_call(
        _inproj_kernel,
        grid=(t // tm,),
        in_specs=[pl.BlockSpec((tm, d), row), pl.BlockSpec((1, d), fixed),
                  pl.BlockSpec(w_in.shape, fixed)],
        out_specs=[rows, rows, pl.BlockSpec((width, tm), lambda i: (0, i)), rows, rows],
        out_shape=[jax.ShapeDtypeStruct((t, width), BF16)] * 2 + [jax.ShapeDtypeStruct((width, t), BF16)]
        + [jax.ShapeDtypeStruct((t, width), F32)] * 2,
        compiler_params=_cparams("arbitrary"),
        name="inproj",
    )(x2, g.reshape(1, d), w_in)


def _t5_bucket(rel):
    n = jnp.maximum(rel, 0)
    max_exact = REL_BUCKETS // 2
    large = max_exact + (jnp.log(jnp.maximum(n, 1).astype(F32) / max_exact)
                         / math.log(REL_MAX_DIST / max_exact)
                         * (REL_BUCKETS - max_exact)).astype(I32)
    large = jnp.minimum(large, REL_BUCKETS - 1)
    return jnp.where(n < max_exact, n, large)


def _bias_tables(rel_bias, tile):
    kj = jnp.arange(tile)[:, None]
    qi = jnp.arange(tile)[None, :]
    rel0 = qi - kj
    rel1 = rel0 + tile
    b0 = jnp.where((rel0 >= 0)[..., None], rel_bias[_t5_bucket(rel0)].astype(F32), NEG_BIG)
    b1 = rel_bias[_t5_bucket(rel1)].astype(F32)
    return jnp.stack([b0, b1]).transpose(3, 0, 1, 2)


def _attn_kernel(cfar_ref, q_ref, k_ref, vt_ref, bias_ref, lam_ref, subg_ref, o_ref,
                 m_sc, l_sc, acc_sc, *, lam_init):
    h = pl.program_id(1)
    i = pl.program_id(2)
    tile = q_ref.shape[0]
    q = q_ref[...]
    lane = lax.broadcasted_iota(I32, q.shape, 1)
    zero = jnp.zeros_like(q)
    qm = (jnp.where(lane < A_HEAD_DIM, q, zero), jnp.where(lane >= A_HEAD_DIM, q, zero))
    m_sc[...] = jnp.full(m_sc.shape, NEG_BIG, F32)
    l_sc[...] = jnp.zeros(l_sc.shape, F32)
    acc_sc[...] = jnp.zeros(acc_sc.shape, F32)

    def block(j, table, const):
        start = pl.multiple_of(j * tile, tile)
        kb = k_ref[pl.ds(start, tile), :]
        vtb = vt_ref[:, pl.ds(start, tile)]
        maps = range(2)
        s = [_dot_nt(kb, qm[mi]) for mi in maps]
        if table is not None:
            s = [s[mi] + table for mi in maps]
        m_prev = [m_sc[mi] for mi in maps]
        m_new = [jnp.maximum(m_prev[mi], jnp.max(s[mi], axis=0, keepdims=True) + const) for mi in maps]
        p = [jnp.exp(s[mi] - (m_new[mi] - const)) for mi in maps]
        for mi in maps:
            alpha = jnp.exp(m_prev[mi] - m_new[mi])
            l_sc[mi] = alpha * l_sc[mi] + jnp.sum(p[mi], axis=0, keepdims=True)
            acc_sc[mi] = alpha * acc_sc[mi] + _dot(vtb, p[mi].astype(BF16))
            m_sc[mi] = m_new[mi]

    cfar = cfar_ref[h]

    def far(j, c):
        block(j, None, cfar)
        return c

    lax.fori_loop(0, jnp.maximum(i - 1, 0), far, 0)

    @pl.when(i >= 1)
    def _():
        block(i - 1, bias_ref[0, 1], 0.0)

    block(i, bias_ref[0, 0], 0.0)

    lam_rows = lam_ref[...]
    lam = (jnp.exp(jnp.sum(lam_rows[0:1] * lam_rows[1:2], axis=-1, keepdims=True))
           - jnp.exp(jnp.sum(lam_rows[2:3] * lam_rows[3:4], axis=-1, keepdims=True)) + lam_init)
    o = acc_sc[0] / l_sc[0] - lam * (acc_sc[1] / l_sc[1])
    o = o * lax.rsqrt(jnp.mean(o * o, axis=0, keepdims=True) + SUBNORM_EPS)
    o_ref[...] = (o.T * subg_ref[...] * (1.0 - lam_init)).astype(o_ref.dtype)


def _diff_attention(q, k, vt, rel_bias, lam_rows, sub_g, lam_init, batch, seq):
    t, width = q.shape
    hw = 2 * A_HEAD_DIM
    tile = ATTN_TILE
    nq = seq // tile
    tables = _bias_tables(rel_bias, tile)
    cfar = rel_bias[REL_BUCKETS - 1].astype(F32)
    return pl.pallas_call(
        functools.partial(_attn_kernel, lam_init=lam_init),
        grid=(batch, A_HEADS, nq),
        in_specs=[
            pl.BlockSpec(memory_space=pltpu.SMEM),
            pl.BlockSpec((tile, hw), lambda b, h, i: (b * nq + i, h)),
            pl.BlockSpec((seq, hw), lambda b, h, i: (b, h)),
            pl.BlockSpec((hw, seq), lambda b, h, i: (h, b)),
            pl.BlockSpec((1, 2, tile, tile), lambda b, h, i: (h, 0, 0, 0)),
            pl.BlockSpec(lam_rows.shape, lambda b, h, i: (0, 0)),
            pl.BlockSpec((1, hw), lambda b, h, i: (0, 0)),
        ],
        out_specs=pl.BlockSpec((tile, hw), lambda b, h, i: (b * nq + i, h)),
        out_shape=jax.ShapeDtypeStruct((t, width), BF16),
        scratch_shapes=[pltpu.VMEM((2, 1, tile), F32), pltpu.VMEM((2, 1, tile), F32),
                        pltpu.VMEM((2, hw, tile), F32)],
        compiler_params=_cparams("arbitrary", "arbitrary", "arbitrary"),
        name="diff_attn",
    )(cfar, q, k, vt, tables, lam_rows, sub_g.reshape(1, hw))


def _lru_kernel(xb_ref, gb_ref, cw_ref, vec_ref, wa_ref, wx_ref, y_ref, prev_sc, h_sc, a_sc, b_sc):
    ts = xb_ref.shape[0]

    @pl.when(pl.program_id(1) == 0)
    def _():
        prev_sc[...] = jnp.zeros(prev_sc.shape, F32)
        h_sc[...] = jnp.zeros(h_sc.shape, F32)

    xb = xb_ref[...]
    hist = prev_sc.shape[0]
    ext = jnp.concatenate([prev_sc[...], xb], axis=0)
    u = vec_ref[0:1, :]
    for j in range(CONV_W):
        off = hist - (CONV_W - 1) + j
        u = u + cw_ref[j:j + 1, :] * ext[off:off + ts, :]
    prev_sc[...] = xb[ts - hist:, :]
    ub = u.astype(BF16)
    r = jax.nn.sigmoid(_dot(ub, wa_ref[...]) + vec_ref[1:2, :])
    ig = jax.nn.sigmoid(_dot(ub, wx_ref[...]) + vec_ref[2:3, :])
    log_a = (-LRU_C * r) * _softplus(-vec_ref[3:4, :])
    a_sc[...] = jnp.exp(log_a)
    b_sc[...] = jnp.sqrt(_neg_expm1(2.0 * log_a)) * (ig * u)

    def step(tt, h):
        h = a_sc[pl.ds(tt, 1), :] * h + b_sc[pl.ds(tt, 1), :]
        b_sc[pl.ds(tt, 1), :] = h
        return h

    h_sc[...] = lax.fori_loop(0, ts, step, h_sc[...], unroll=8)
    y_ref[...] = (b_sc[...] * _gelu_tanh(gb_ref[...])).astype(y_ref.dtype)


def _block_diag(w):
    n, c, d = w.shape
    eye = jnp.eye(n, dtype=w.dtype)
    return (w[:, :, None, :] * eye[:, None, :, None]).reshape(n * c, n * d)


def _rglru(xb, gb, conv_w, conv_b, wa, ba, wx, bx, lru_lambda, batch, seq):
    t, w = xb.shape
    ts = LRU_TILE
    nt = seq // ts
    vec = jnp.zeros((8, w), F32).at[0].set(conv_b).at[1].set(ba).at[2].set(bx).at[3].set(lru_lambda)
    row = lambda b, s: (b * nt + s, 0)
    fixed = lambda b, s: (0, 0)
    return pl.pallas_call(
        _lru_kernel,
        grid=(batch, nt),
        in_specs=[pl.BlockSpec((ts, w), row), pl.BlockSpec((ts, w), row),
                  pl.BlockSpec((CONV_W, w), fixed), pl.BlockSpec((8, w), fixed),
                  pl.BlockSpec((w, w), fixed), pl.BlockSpec((w, w), fixed)],
        out_specs=pl.BlockSpec((ts, w), row),
        out_shape=jax.ShapeDtypeStruct((t, w), BF16),
        scratch_shapes=[pltpu.VMEM((8, w), F32), pltpu.VMEM((1, w), F32),
                        pltpu.VMEM((ts, w), F32), pltpu.VMEM((ts, w), F32)],
        compiler_params=_cparams("arbitrary", "arbitrary"),
        name="rglru",
    )(xb, gb, conv_w, vec, _block_diag(wa).astype(BF16), _block_diag(wx).astype(BF16))


def _route(x, gf_ref, wrh_ref, wrl_ref, br_ref):
    hn = _rms(x, gf_ref[...], NORM_EPS)
    hh, hl = _split2(hn)
    wh = wrh_ref[...]
    lg = _dot(hh, wh) + _dot(hh, wrl_ref[...]) + _dot(hl, wh) + br_ref[...]
    lane = lax.broadcasted_iota(I32, lg.shape, 1)
    big = jnp.int32(1 << 20)
    is_g = lane < MOE_GROUPS
    gl = jnp.where(is_g, lg, NEG_BIG)
    gmax = jnp.max(gl, axis=-1, keepdims=True)
    gidx = jnp.min(jnp.where(gl == gmax, lane, big), axis=-1, keepdims=True)
    gsum = jnp.sum(jnp.where(is_g, jnp.exp(gl - gmax), 0.0), axis=-1, keepdims=True)
    gprob = 1.0 / gsum
    lo = MOE_GROUPS + gidx * EXPERTS_PER_GROUP
    fmask = (lane >= lo) & (lane < lo + EXPERTS_PER_GROUP)
    fl = jnp.where(fmask, lg, NEG_BIG)
    f1 = jnp.max(fl, axis=-1, keepdims=True)
    i1 = jnp.min(jnp.where(fl == f1, lane, big), axis=-1, keepdims=True)
    fl2 = jnp.where(lane == i1, NEG_BIG, fl)
    f2 = jnp.max(fl2, axis=-1, keepdims=True)
    i2 = jnp.min(jnp.where(fl2 == f2, lane, big), axis=-1, keepdims=True)
    e2 = jnp.exp(f2 - f1)
    den = 1.0 + e2
    g1 = gprob / den
    g2 = gprob * e2 / den
    id1 = (i1 - MOE_GROUPS).astype(F32)
    id2 = (i2 - MOE_GROUPS).astype(F32)
    return jnp.where(lane == 0, g1, jnp.where(lane == 1, g2,
                     jnp.where(lane == 2, id1, jnp.where(lane == 3, id2, 0.0))))


def _router_params(wc, bc, wf, bf):
    d = wc.shape[0]
    n = MOE_GROUPS + N_EXPERTS
    w = jnp.zeros((d, LANES), F32).at[:, :MOE_GROUPS].set(wc).at[:, MOE_GROUPS:n].set(wf)
    b = jnp.zeros((1, LANES), F32).at[0, :MOE_GROUPS].set(bc).at[0, MOE_GROUPS:n].set(bf)
    hi, lo = _split2(w)
    return hi, lo, b


def _outproj_kernel(x_ref, ya_ref, yb_ref, wo_ref, gf_ref, wrh_ref, wrl_ref, br_ref, x1_ref, rt_ref):
    half = ya_ref.shape[1]
    x1 = x_ref[...] + _dot(ya_ref[...], wo_ref[:half, :]) + _dot(yb_ref[...], wo_ref[half:, :])
    x1_ref[...] = x1
    rt_ref[...] = _route(x1, gf_ref, wrh_ref, wrl_ref, br_ref)


def _outproj(x2, ya, yb, w_out, gf, router):
    t, d = x2.shape
    half = ya.shape[1]
    tm = ROW_TILE
    row = lambda i: (i, 0)
    fixed = lambda i: (0, 0)
    wrh, wrl, br = router
    return pl.pallas_call(
        _outproj_kernel,
        grid=(t // tm,),
        in_specs=[pl.BlockSpec((tm, d), row), pl.BlockSpec((tm, half), row), pl.BlockSpec((tm, half), row),
                  pl.BlockSpec((2 * half, d), fixed), pl.BlockSpec((1, d), fixed),
                  pl.BlockSpec((d, LANES), fixed), pl.BlockSpec((d, LANES), fixed),
                  pl.BlockSpec((1, LANES), fixed)],
        out_specs=[pl.BlockSpec((tm, d), row), pl.BlockSpec((tm, LANES), row)],
        out_shape=[jax.ShapeDtypeStruct((t, d), F32), jax.ShapeDtypeStruct((t, LANES), F32)],
        compiler_params=_cparams("arbitrary"),
        name="outproj_route",
    )(x2, ya, yb, w_out, gf.reshape(1, d), wrh, wrl, br)


def _moe_plan(route, n_tok):
    m = n_tok * MOE_TOPK
    n_blk = -(-(m + N_EXPERTS * (MOE_BLOCK - 1)) // MOE_BLOCK)
    eid = route[:, 2:4].astype(I32).reshape(-1)
    onehot = (eid[:, None] == jnp.arange(N_EXPERTS, dtype=I32)[None, :]).astype(I32)
    csum = jnp.cumsum(onehot, axis=0)
    rank = jnp.sum(csum * onehot, axis=1) - 1
    sizes = csum[-1]
    padded = (sizes + MOE_BLOCK - 1) // MOE_BLOCK * MOE_BLOCK
    pend = jnp.cumsum(padded)
    dest = (pend - padded)[eid] + rank
    tok = jnp.repeat(jnp.arange(n_tok, dtype=I32), MOE_TOPK)
    rows_tok = jnp.zeros((n_blk * MOE_BLOCK,), I32).at[dest].set(tok)
    blk_e = jnp.minimum(jnp.searchsorted(pend, jnp.arange(n_blk, dtype=I32) * MOE_BLOCK, side='right'),
                        N_EXPERTS - 1).astype(I32)
    n_used = (pend[-1] // MOE_BLOCK).astype(I32).reshape(1)
    return rows_tok.reshape(n_blk, MOE_BLOCK), blk_e, n_used, dest.astype(I32)


def _idx_copy(idx_hbm, idx_sm, sem, step, slot):
    return pltpu.make_async_copy(idx_hbm.at[step], idx_sm.at[slot], sem.at[slot])


def _gather_ahead(idx_hbm, idx_sm, idx_sem, issue_rows, n_active):
    i = pl.program_id(0)
    n = pl.num_programs(0)

    @pl.when(i == 0)
    def _():
        first = _idx_copy(idx_hbm, idx_sm, idx_sem, 0, 0)
        first.start()
        first.wait()

        @pl.when(0 < n_active)
        def _():
            issue_rows(0)

        @pl.when(1 < n)
        def _():
            _idx_copy(idx_hbm, idx_sm, idx_sem, 1, 1).start()

    @pl.when(i + 1 < n)
    def _():
        _idx_copy(idx_hbm, idx_sm, idx_sem, i + 1, (i + 1) % 2).wait()

        @pl.when(i + 1 < n_active)
        def _():
            issue_rows((i + 1) % 2)

    @pl.when(i + 2 < n)
    def _():
        _idx_copy(idx_hbm, idx_sm, idx_sem, i + 2, i % 2).start()


def _expert_kernel(blk_e_ref, nused_ref, rows_hbm, x_hbm, gf_ref, wgu_ref, wd_ref, ys_ref,
                   idx_sm, idx_sem, xbuf, row_sem):
    i = pl.program_id(0)
    slot = i % 2
    blk = xbuf.shape[1]

    def issue_rows(sl):
        def issue(r, c):
            tok = idx_sm[sl, r]
            pltpu.make_async_copy(x_hbm.at[pl.ds(tok, 1), :], xbuf.at[sl, pl.ds(r, 1), :],
                                  row_sem.at[sl]).start()
            return c

        lax.fori_loop(0, blk, issue, 0, unroll=8)

    _gather_ahead(rows_hbm, idx_sm, idx_sem, issue_rows, nused_ref[0])

    @pl.when(i < nused_ref[0])
    def _():
        pltpu.make_async_copy(x_hbm.at[pl.ds(0, blk), :], xbuf.at[slot], row_sem.at[slot]).wait()
        hn = _rms(xbuf[slot], gf_ref[...], NORM_EPS).astype(BF16)
        gu = _dot(hn, wgu_ref[0])
        eh = gu.shape[1] // 2
        g = gu[:, :eh]
        act = (g * jax.nn.sigmoid(g) * gu[:, eh:]).astype(BF16)
        ys_ref[...] = _dot(act, wd_ref[0])

    @pl.when(i >= nused_ref[0])
    def _():
        ys_ref[...] = jnp.zeros(ys_ref.shape, F32)


def _experts(x1, rows_tok, blk_e, n_used, gf, w_gu, w_down):
    t, d = x1.shape
    n_blk, blk = rows_tok.shape
    eh2 = w_gu.shape[2]
    grid_spec = pltpu.PrefetchScalarGridSpec(
        num_scalar_prefetch=2,
        grid=(n_blk,),
        in_specs=[
            pl.BlockSpec(memory_space=pl.ANY),
            pl.BlockSpec(memory_space=pl.ANY),
            pl.BlockSpec((1, d), lambda i, e, n: (0, 0)),
            pl.BlockSpec((1, d, eh2), lambda i, e, n: (e[i], 0, 0)),
            pl.BlockSpec((1, eh2 // 2, d), lambda i, e, n: (e[i], 0, 0)),
        ],
        out_specs=pl.BlockSpec((blk, d), lambda i, e, n: (i, 0)),
        scratch_shapes=[pltpu.SMEM((2, blk), I32), pltpu.SemaphoreType.DMA((2,)),
                        pltpu.VMEM((2, blk, d), F32), pltpu.SemaphoreType.DMA((2,))],
    )
    return pl.pallas_call(
        _expert_kernel,
        grid_spec=grid_spec,
        out_shape=jax.ShapeDtypeStruct((n_blk * blk, d), F32),
        compiler_params=_cparams("arbitrary"),
        name="moe_experts",
    )(blk_e, n_used, rows_tok, x1, gf.reshape(1, d), w_gu, w_down)


def _combine_kernel(pos_hbm, ys_hbm, x1_ref, rt_ref, p_ref, gn_ref, wg_ref, wp_ref, fn_ref, o_ref,
                    idx_sm, idx_sem, ybuf, row_sem, *, final):
    slot = pl.program_id(0) % 2
    tm = x1_ref.shape[0]

    def issue_rows(sl):
        def issue(r, c):
            for s in range(MOE_TOPK):
                pos = idx_sm[sl, MOE_TOPK * r + s]
                pltpu.make_async_copy(ys_hbm.at[pl.ds(pos, 1), :], ybuf.at[sl, s, pl.ds(r, 1), :],
                                      row_sem.at[sl]).start()
            return c

        lax.fori_loop(0, tm, issue, 0, unroll=8)

    _gather_ahead(pos_hbm, idx_sm, idx_sem, issue_rows, pl.num_programs(0))
    for s in range(MOE_TOPK):
        pltpu.make_async_copy(ys_hbm.at[pl.ds(0, tm), :], ybuf.at[slot, s], row_sem.at[slot]).wait()

    rt = rt_ref[...]
    x2 = x1_ref[...] + (rt[:, 0:1] * ybuf[slot, 0] + rt[:, 1:2] * ybuf[slot, 1])
    gate = jax.nn.sigmoid(_dot(_rms(x2, gn_ref[...], NORM_EPS).astype(BF16), wg_ref[...]))
    x3 = x2 + _dot(p_ref[...].astype(BF16), wp_ref[...]) * gate
    if final:
        x3 = _rms(x3, fn_ref[...], NORM_EPS)
    o_ref[...] = x3


def _combine(x1, route, ys, dest, p_i, ple_norm, ple_gate, ple_proj, final_norm, final):
    t, d = x1.shape
    tm = COMBINE_TILE
    pd = p_i.shape[1]
    pos = dest.reshape(t // tm, MOE_TOPK * tm)
    row = lambda i: (i, 0)
    fixed = lambda i: (0, 0)
    return pl.pallas_call(
        functools.partial(_combine_kernel, final=final),
        grid=(t // tm,),
        in_specs=[pl.BlockSpec(memory_space=pl.ANY), pl.BlockSpec(memory_space=pl.ANY),
                  pl.BlockSpec((tm, d), row), pl.BlockSpec((tm, LANES), row), pl.BlockSpec((tm, pd), row),
                  pl.BlockSpec((1, d), fixed), pl.BlockSpec((d, d), fixed), pl.BlockSpec((pd, d), fixed),
                  pl.BlockSpec((1, d), fixed)],
        out_specs=pl.BlockSpec((tm, d), row),
        out_shape=jax.ShapeDtypeStruct((t, d), F32),
        scratch_shapes=[pltpu.SMEM((2, MOE_TOPK * tm), I32), pltpu.SemaphoreType.DMA((2,)),
                        pltpu.VMEM((2, MOE_TOPK, tm, d), F32), pltpu.SemaphoreType.DMA((2,))],
        compiler_params=_cparams("arbitrary"),
        name="moe_combine_ple",
    )(pos, ys, x1, route, p_i, ple_norm.reshape(1, d), ple_gate, ple_proj, final_norm.reshape(1, d))


def _seg_sum(x, eseg_ref, eexp_ref):
    s = _dot(x.astype(BF16), eseg_ref[...])
    shi, slo = _split2(s)
    ex = eexp_ref[...]
    return _dot(shi, ex) + _dot(slo, ex)


def _rwkv_pre_kernel(x_ref, xp_ref, gm_ref, mu_ref, vec_ref, wr_ref, wk_ref, wv_ref, w1_ref, w2_ref,
                     a1_ref, a2_ref, g1_ref, g2_ref, tri_ref, eseg_ref, eexp_ref,
                     la_ref, lr_ref, rb_ref, rk_ref, sb_ref, sk_ref, v_ref, bon_ref, g_ref, gc_ref,
                     c_sc):
    tl = x_ref.shape[0]
    gm = gm_ref[...]
    h = _rms(x_ref[...], gm, NORM_EPS)
    hp = _rms(xp_ref[...], gm, NORM_EPS)[7:8, :]
    hp = jnp.where(pl.program_id(1) == 0, 0.0, hp)
    row = lax.broadcasted_iota(I32, h.shape, 0)
    hs = jnp.where(row == 0, hp, pltpu.roll(h, 1, axis=0))
    xx = hs - h
    mix = lambda n: (h + xx * mu_ref[n:n + 1, :]).astype(BF16)
    vec = lambda n: vec_ref[n:n + 1, :]
    r = _dot(mix(0), wr_ref[...])
    wl = vec(0) + _dot(jnp.tanh(_dot(mix(1), w1_ref[...])).astype(BF16), w2_ref[...])
    lw = -jnp.exp(-_softplus(-wl) - 0.5)
    k = _dot(mix(2), wk_ref[...])
    v = _dot(mix(3), wv_ref[...])
    a = jax.nn.sigmoid(vec(1) + _dot(_dot(mix(4), a1_ref[...]).astype(BF16), a2_ref[...]))
    g = _dot(jax.nn.sigmoid(_dot(mix(5), g1_ref[...])).astype(BF16), g2_ref[...])
    kk = k * vec(2)
    kk = kk / jnp.maximum(jnp.sqrt(_seg_sum(kk * kk, eseg_ref, eexp_ref)), 1e-12)
    k2 = k * (1.0 + (a - 1.0) * vec(3))
    bon_ref[...] = (_seg_sum(r * k2 * vec(4), eseg_ref, eexp_ref) * v).astype(bon_ref.dtype)
    v_ref[...] = v.astype(v_ref.dtype)
    g_ref[...] = g.astype(g_ref.dtype)

    half = tri_ref.shape[1]
    tri = tri_ref[...]
    for s in range(tl // half):
        part = lw[s * half:(s + 1) * half, :]
        p1 = part.astype(BF16)
        rem = part - p1.astype(F32)
        p2 = rem.astype(BF16)
        p3 = (rem - p2.astype(F32)).astype(BF16)
        cs = _dot(tri, p1) + _dot(tri, p2) + _dot(tri, p3)
        c_sc[0, s * half:(s + 1) * half, :] = cs[:half]
        c_sc[1, s * half:(s + 1) * half, :] = cs[half:]
    c_in = c_sc[0]
    c_end = c_sc[1]
    nc = tl // RWKV_CHUNK
    ends = [c_sc[0, (n + 1) * RWKV_CHUNK - 1:(n + 1) * RWKV_CHUNK, :] for n in range(nc)]
    gc_ref[...] = jnp.exp(jnp.concatenate(ends, axis=0))
    e_in = jnp.exp(c_in)
    e_neg = jnp.exp(-c_in)
    e_end = jnp.exp(c_end)
    b = kk * a
    la_ref[...] = (-kk * jnp.exp(c_in - lw)).astype(la_ref.dtype)
    lr_ref[...] = (r * e_in).astype(lr_ref.dtype)
    rb_ref[...] = (b * e_neg).astype(rb_ref.dtype)
    rk_ref[...] = (k2 * e_neg).astype(rk_ref.dtype)
    sb_ref[...] = (b * e_end).astype(sb_ref.dtype)
    sk_ref[...] = (k2 * e_end).astype(sk_ref.dtype)


def _head_indicator(d):
    heads = d // RWKV_HEAD
    e = (jnp.arange(d)[:, None] // RWKV_HEAD == jnp.arange(LANES)[None, :]).astype(BF16)
    del heads
    return e, e.T


def _cumsum_matrix(half):
    i = jnp.arange(half)[:, None]
    j = jnp.arange(half)[None, :]
    same = (i // RWKV_CHUNK) == (j // RWKV_CHUNK)
    lower = same & (j <= i)
    upper = same & (j > i)
    return jnp.concatenate([lower, upper], axis=0).astype(BF16)


def _pad_cols(w, n):
    return jnp.pad(w, ((0, 0), (0, n - w.shape[1])))


def _pad_rows(w, n):
    return jnp.pad(w, ((0, n - w.shape[0]), (0, 0)))


def _rwkv_pre(x3, gm, mu, vecs, wr, wk, wv, w1, w2, a1, a2, g1, g2, batch, seq):
    t, d = x3.shape
    tl = RWKV_TILE
    nt = seq // tl
    half = 256
    lora = lambda w_in, w_out: (_pad_cols(w_in, -(-w_in.shape[1] // LANES) * LANES).astype(BF16),
                                _pad_rows(w_out, -(-w_out.shape[0] // LANES) * LANES).astype(BF16))
    w1p, w2p = lora(w1, w2)
    a1p, a2p = lora(a1, a2)
    g1p, g2p = lora(g1, g2)
    eseg, eexp = _head_indicator(d)
    tri = _cumsum_matrix(half)
    row = lambda b, s: (b * nt + s, 0)
    fixed = lambda b, s: (0, 0)
    full = lambda a: pl.BlockSpec(a.shape, fixed)
    prev = lambda b, s: (jnp.maximum((b * nt + s) * (tl // 8) - 1, 0), 0)
    weights = (wr.astype(BF16), wk.astype(BF16), wv.astype(BF16), w1p, w2p, a1p, a2p, g1p, g2p, tri, eseg, eexp)
    outs = pl.pallas_call(
        _rwkv_pre_kernel,
        grid=(batch, nt),
        in_specs=[pl.BlockSpec((tl, d), row), pl.BlockSpec((8, d), prev), pl.BlockSpec((1, d), fixed),
                  full(mu), full(vecs)] + [full(w) for w in weights],
        out_specs=[pl.BlockSpec((tl, d), row)] * 9 + [pl.BlockSpec((tl // RWKV_CHUNK, d), row)],
        out_shape=[jax.ShapeDtypeStruct((t, d), BF16)] * 9 + [jax.ShapeDtypeStruct((t // RWKV_CHUNK, d), F32)],
        scratch_shapes=[pltpu.VMEM((2, tl, d), F32)],
        compiler_params=_cparams("arbitrary", "arbitrary"),
        name="rwkv_pre",
    )(x3, x3, gm.reshape(1, d), mu, vecs, *weights)
    return outs


def _rwkv_scan_kernel(la_ref, lr_ref, rb_ref, rk_ref, sb_ref, sk_ref, v_ref, gc_ref, bon_ref, g_ref, x_ref,
                      vec_ref, wo_ref, eseg_ref, eexp_ref, gf_ref, wrh_ref, wrl_ref, br_ref,
                      x4_ref, rt_ref, s_sc, y_sc):
    tl, d = x_ref.shape
    ch = RWKV_CHUNK
    pw = 2 * RWKV_HEAD
    npair = d // pw

    @pl.when(pl.program_id(1) == 0)
    def _():
        s_sc[...] = jnp.zeros(s_sc.shape, F32)

    ri = lax.broadcasted_iota(I32, (pw, pw), 0)
    ci = lax.broadcasted_iota(I32, (pw, pw), 1)
    same_half = (ri < RWKV_HEAD) == (ci < RWKV_HEAD)
    eye = (ri == ci).astype(F32)
    rt_ = lax.broadcasted_iota(I32, (ch, pw), 0)
    ct_ = lax.broadcasted_iota(I32, (ch, pw), 1)
    first = ct_ < RWKV_HEAD
    strict = rt_ > (ct_ & (RWKV_HEAD - 1))
    incl = rt_ >= (ct_ & (RWKV_HEAD - 1))
    m_s0, m_s1 = strict & first, strict & ~first
    m_i0, m_i1 = incl & first, incl & ~first
    first_full = ci < RWKV_HEAD
    zf = jnp.zeros((ch, pw), F32)
    gc_row = lax.broadcasted_iota(I32, (tl // ch, pw), 0)

    def chunk(c, carry):
        rows = pl.ds(pl.multiple_of(c * ch, ch), ch)
        pairs = range(npair)
        cols = [slice(p * pw, (p + 1) * pw) for p in pairs]
        cat0 = lambda a, b: jnp.concatenate([a, b], axis=0)
        lcat = [cat0(la_ref[rows, cl], lr_ref[rows, cl]) for cl in cols]
        rb = [rb_ref[rows, cl] for cl in cols]
        rk = [rk_ref[rows, cl] for cl in cols]
        vv = [v_ref[rows, cl] for cl in cols]
        zb = jnp.zeros_like(lcat[0])
        a0 = [_dot_nt(jnp.where(first_full, lcat[p], zb), cat0(rb[p], rk[p])) for p in pairs]
        a1 = [_dot_nt(jnp.where(first_full, zb, lcat[p]), cat0(rk[p], rb[p])) for p in pairs]
        pm = [cat0(jnp.where(m_s0, a0[p][:ch], zf), jnp.where(m_s1, a1[p][:ch], zf)) for p in pairs]
        aak = [cat0(jnp.where(m_s1, a0[p][:ch], zf), jnp.where(m_s0, a1[p][:ch], zf)).astype(BF16)
               for p in pairs]
        arbk = [jnp.concatenate([cat0(jnp.where(m_i0, a0[p][ch:], zf), jnp.where(m_i1, a1[p][ch:], zf)),
                                 cat0(jnp.where(m_i1, a0[p][ch:], zf), jnp.where(m_i0, a1[p][ch:], zf))],
                                axis=1).astype(BF16) for p in pairs]
        tm = [eye + pm[p] for p in pairs]
        qm = pm
        for _ in range(5):
            qb = [q.astype(BF16) for q in qm]
            qm = [_dot(qb[p], qb[p]) for p in pairs]
            tm = [tm[p] + _dot(tm[p].astype(BF16), qm[p].astype(BF16)) for p in pairs]
        st = [s_sc[p] for p in pairs]
        w12 = [_dot_nt(lcat[p], st[p].astype(BF16)) for p in pairs]
        v2 = [cat0(vv[p], vv[p]) for p in pairs]
        rhs = [cat0(w12[p][:ch], w12[p][:ch]) + _dot(aak[p], v2[p]) for p in pairs]
        ust = [_dot(tm[p].astype(BF16), rhs[p].astype(BF16)) for p in pairs]
        yst = [cat0(w12[p][ch:], w12[p][ch:]) + _dot(arbk[p], cat0(ust[p].astype(BF16), v2[p]))
               for p in pairs]
        for p in pairs:
            y_sc[rows, cols[p]] = jnp.where(first, yst[p][:ch], yst[p][ch:])
        u = [jnp.where(first, ust[p][:ch], ust[p][ch:]).astype(BF16) for p in pairs]
        ds_ = [_dot_tn(cat0(u[p], vv[p]), cat0(sb_ref[rows, cols[p]], sk_ref[rows, cols[p]])) for p in pairs]
        for p in pairs:
            gcr = jnp.sum(jnp.where(gc_row == c, gc_ref[:, cols[p]], 0.0), axis=0, keepdims=True)
            s_sc[p] = jnp.where(same_half, st[p] * gcr + ds_[p], 0.0)
        return carry

    lax.fori_loop(0, tl // ch, chunk, 0)

    y = y_sc[...]
    inv_n = 1.0 / RWKV_HEAD
    mean = _seg_sum(y, eseg_ref, eexp_ref) * inv_n
    dlt = y - mean
    var = _seg_sum(dlt * dlt, eseg_ref, eexp_ref) * inv_n
    yn = dlt * lax.rsqrt(var + RWKV_GN_EPS) * vec_ref[5:6, :] + vec_ref[6:7, :]
    z = (yn + bon_ref[...].astype(F32)) * g_ref[...].astype(F32)
    x4 = x_ref[...] + _dot(z.astype(BF16), wo_ref[...])
    x4_ref[...] = x4
    rt_ref[...] = _route(x4, gf_ref, wrh_ref, wrl_ref, br_ref)


def _rwkv_scan(pre, x3, vecs, wo, gf, router, batch, seq):
    la, lr, rb, rk, sb, sk, v, bon, g, gc = pre
    t, d = x3.shape
    tl = RWKV_TILE
    nt = seq // tl
    eseg, eexp = _head_indicator(d)
    wrh, wrl, br = router
    row = lambda b, s: (b * nt + s, 0)
    fixed = lambda b, s: (0, 0)
    big = pl.BlockSpec((tl, d), row)
    full = lambda a: pl.BlockSpec(a.shape, fixed)
    pw = 2 * RWKV_HEAD
    return pl.pallas_call(
        _rwkv_scan_kernel,
        grid=(batch, nt),
        in_specs=[big] * 7 + [pl.BlockSpec((tl // RWKV_CHUNK, d), row), big, big, big,
                              full(vecs), full(wo), full(eseg), full(eexp),
                              pl.BlockSpec((1, d), fixed), full(wrh), full(wrl), full(br)],
        out_specs=[big, pl.BlockSpec((tl, LANES), row)],
        out_shape=[jax.ShapeDtypeStruct((t, d), F32), jax.ShapeDtypeStruct((t, LANES), F32)],
        scratch_shapes=[pltpu.VMEM((d // pw, pw, pw), F32), pltpu.VMEM((tl, d), F32)],
        compiler_params=_cparams("arbitrary", "arbitrary"),
        name="rwkv_scan_out",
    )(la, lr, rb, rk, sb, sk, v, gc, bon, g, x3, vecs, wo, eseg, eexp, gf.reshape(1, d), wrh, wrl, br)


def _moe_and_embed(x1, route, i, p, norm_ffn, moe_w_gu, moe_w_down, ple_norm, ple_gate, ple_proj,
                   final_norm, final):
    t = x1.shape[0]
    rows_tok, blk_e, n_used, dest = _moe_plan(route, t)
    ys = _experts(x1, rows_tok, blk_e, n_used, norm_ffn[i], moe_w_gu[i].astype(BF16),
                  moe_w_down[i].astype(BF16))
    p_i = p[i].reshape(t, p.shape[-1])
    return _combine(x1, route, ys, dest, p_i, ple_norm[i], ple_gate[i].astype(BF16),
                    ple_proj[i].astype(BF16), final_norm, final)


def kernel(x, p, rel_bias, norm_mix, ab_w_in, ab_w_out, lam_q1, lam_k1, lam_q2, lam_k2, sub_g, conv_w, conv_b, lru_wa, lru_ba, lru_wx, lru_bx, lru_lambda, rwkv_mu, rwkv_wr, rwkv_wk, rwkv_wv, rwkv_wo, rwkv_w0, rwkv_w1, rwkv_w2, rwkv_a0, rwkv_a1, rwkv_a2, rwkv_g1, rwkv_g2, rwkv_kk, rwkv_ka, rwkv_rk, rwkv_ln_g, rwkv_ln_b, norm_ffn, moe_wc, moe_bc, moe_wf, moe_bf, moe_w_gu, moe_w_down, ple_norm, ple_gate, ple_proj, final_norm):
    batch, seq, d = x.shape
    t = batch * seq
    depth = norm_mix.shape[0]
    assert depth == 2 and seq % RWKV_TILE == 0 and seq % LRU_TILE == 0 and t % ROW_TILE == 0
    xs = x.reshape(t, d)
    for i in range(depth):
        j = i // 2
        router = _router_params(moe_wc[i], moe_bc[i], moe_wf[i], moe_bf[i])
        if i % 2 == 0:
            lam_init = 0.8 - 0.6 * math.exp(-0.3 * i)
            q, k, vt, xb, gb = _inproj(xs, norm_mix[i], ab_w_in[j].astype(BF16))
            lam_rows = jnp.stack([lam_q1[j], lam_k1[j], lam_q2[j], lam_k2[j]]).astype(F32)
            ya = _diff_attention(q, k, vt, rel_bias, lam_rows, sub_g[j], lam_init, batch, seq)
            yb = _rglru(xb, gb, conv_w[j], conv_b[j], lru_wa[j], lru_ba[j], lru_wx[j], lru_bx[j],
                        lru_lambda[j], batch, seq)
            x1, route = _outproj(xs, ya, yb, ab_w_out[j].astype(BF16), norm_ffn[i], router)
        else:
            vecs = jnp.stack([rwkv_w0[j], rwkv_a0[j], rwkv_kk[j], rwkv_ka[j], rwkv_rk[j].reshape(-1),
                              rwkv_ln_g[j], rwkv_ln_b[j], jnp.zeros((d,), F32)]).astype(F32)
            pre = _rwkv_pre(xs, norm_mix[i], rwkv_mu[j], vecs, rwkv_wr[j], rwkv_wk[j], rwkv_wv[j],
                            rwkv_w1[j], rwkv_w2[j], rwkv_a1[j], rwkv_a2[j], rwkv_g1[j], rwkv_g2[j],
                            batch, seq)
            x1, route = _rwkv_scan(pre, xs, vecs, rwkv_wo[j].astype(BF16), norm_ffn[i], router, batch, seq)
        xs = _moe_and_embed(x1, route, i, p, norm_ffn, moe_w_gu, moe_w_down, ple_norm, ple_gate,
                            ple_proj, final_norm, final=(i == depth - 1))
    return xs.reshape(batch, seq, d)
```

```python
import functools
import math

import jax
import jax.numpy as jnp
from jax import lax
from jax.experimental import pallas as pl
from jax.experimental.pallas import tpu as pltpu

F32, BF16, I32 = jnp.float32, jnp.bfloat16, jnp.int32

A_HEADS = 4
A_HEAD_DIM = 64
A_SCALE = A_HEAD_DIM ** -0.5
LOG2E = math.log2(math.e)
REL_BUCKETS = 32
REL_MAX_DIST = 128
LRU_BLOCKS = 8
CONV_W = 4
LRU_C = 8.0
RWKV_HEAD = 64
RWKV_GN_EPS = 64e-5
MOE_GROUPS = 4
EXPERTS_PER_GROUP = 8
N_EXPERTS = MOE_GROUPS * EXPERTS_PER_GROUP
MOE_TOPK = 2
NORM_EPS = 1e-6
SUBNORM_EPS = 1e-5

LANES = 128
VMEM_LIMIT = 56 * 1024 * 1024
ROW_TILE = 512
ATTN_TILE = 512
LRU_TILE = 512
RWKV_CHUNK = 64
RWKV_TILE = 512
MOE_BLOCK = 512
COMBINE_TILE = 256
NEG_BIG = -1e30


def _cparams(*sem):
    return pltpu.CompilerParams(dimension_semantics=sem, vmem_limit_bytes=VMEM_LIMIT)


def _rms(x, g, eps):
    return x * lax.rsqrt(jnp.mean(x * x, axis=-1, keepdims=True) + eps) * g


def _dot(a, b):
    return jnp.dot(a, b, preferred_element_type=F32)


def _dot_nt(a, b):
    return lax.dot_general(a, b, (((1,), (1,)), ((), ())), preferred_element_type=F32)


def _dot_tn(a, b):
    return lax.dot_general(a, b, (((0,), (0,)), ((), ())), preferred_element_type=F32)


def _split2(x):
    hi = x.astype(BF16)
    lo = (x - hi.astype(F32)).astype(BF16)
    return hi, lo


def _softplus(x):
    return jnp.maximum(x, 0.0) + jnp.log1p(jnp.exp(-jnp.abs(x)))


def _neg_expm1(y):
    poly = 1.0 + y / 8.0
    for d in (7.0, 6.0, 5.0, 4.0, 3.0, 2.0):
        poly = 1.0 + (y / d) * poly
    return jnp.where(y > -0.25, -y * poly, 1.0 - jnp.exp(y))


SUBLANES = 8


def _store_token_tiles(ref, x):
    rows = x.shape[0]
    for s in range(x.shape[1] // LANES):
        ref[pl.ds(s, rows, stride=SUBLANES), :] = x[:, s * LANES:(s + 1) * LANES]


def _load_token_tiles(ref, rows):
    return jnp.concatenate([ref[pl.ds(s, rows, stride=SUBLANES), :] for s in range(SUBLANES)], axis=1)


def _gelu_tanh(x):
    c = math.sqrt(2.0 / math.pi)
    return 0.5 * x * (1.0 + jnp.tanh(c * (x + 0.044715 * (x * x * x))))


def _inproj_kernel(x_ref, g_ref, w_ref, q_ref, k_ref, vt_ref, xb_ref, gb_ref):
    hn = _rms(x_ref[...], g_ref[...], NORM_EPS).astype(BF16)
    width = q_ref.shape[1]
    for c, o_ref in enumerate((q_ref, k_ref, vt_ref, xb_ref, gb_ref)):
        r = _dot(hn, w_ref[:, c * width:(c + 1) * width])
        if c == 0:
            r = r * (A_SCALE * LOG2E)
        if c == 2:
            r = r.T
        o_ref[...] = r.astype(o_ref.dtype)


def _inproj(x2, g, w_in):
    t, d = x2.shape
    width = w_in.shape[1] // 5
    tm = ROW_TILE
    row = lambda i: (i, 0)
    fixed = lambda i: (0, 0)
    rows = pl.BlockSpec((tm, width), row)
    return pl.pallas_call(
        _inproj_kernel,
        grid=(t // tm,),
        in_specs=[pl.BlockSpec((tm, d), row), pl.BlockSpec((1, d), fixed),
                  pl.BlockSpec(w_in.shape, fixed)],
        out_specs=[rows, rows, pl.BlockSpec((width, tm), lambda i: (0, i)), rows, rows],
        out_shape=[jax.ShapeDtypeStruct((t, width), BF16)] * 2 + [jax.ShapeDtypeStruct((width, t), BF16)]
        + [jax.ShapeDtypeStruct((t, width), F32)] * 2,
        compiler_params=_cparams("arbitrary"),
        name="inproj",
    )(x2, g.reshape(1, d), w_in)


def _t5_bucket(rel):
    n = jnp.maximum(rel, 0)
    max_exact = REL_BUCKETS // 2
    large = max_exact + (jnp.log(jnp.maximum(n, 1).astype(F32) / max_exact)
                         / math.log(REL_MAX_DIST / max_exact)
                         * (REL_BUCKETS - max_exact)).astype(I32)
    large = jnp.minimum(large, REL_BUCKETS - 1)
    return jnp.where(n < max_exact, n, large)


def _bias_tables(rel_bias, tile):
    span = 2 * tile

    def toeplitz(first_rel):
        rel = jnp.arange(span - 1) + first_rel
        vals = jnp.where((rel >= 0)[:, None], rel_bias[_t5_bucket(rel)].astype(F32), NEG_BIG)
        w = jnp.concatenate([vals, jnp.zeros((1, vals.shape[1]), F32)], axis=0).T
        skew = jnp.tile(w, (1, tile))[:, :tile * (span - 1)].reshape(-1, tile, span - 1)
        return skew[:, :, tile - 1:]

    return jnp.stack([toeplitz(1 - tile), toeplitz(1)], axis=1)


def _attn_kernel(cfar_ref, q_ref, k_ref, vt_ref, bias_ref, lam_ref, subg_ref, o_ref,
                 m_sc, l_sc, acc_sc, sa_sc, sb_sc, *, lam_init):
    h = pl.program_id(1)
    i = pl.program_id(2)
    tile = q_ref.shape[0]
    q = q_ref[...]
    lane = lax.broadcasted_iota(I32, q.shape, 1)
    zero = jnp.zeros_like(q)
    qm = (jnp.where(lane < A_HEAD_DIM, q, zero), jnp.where(lane >= A_HEAD_DIM, q, zero))
    m_sc[...] = jnp.full(m_sc.shape, NEG_BIG, F32)
    l_sc[...] = jnp.zeros(l_sc.shape, F32)
    acc_sc[...] = jnp.zeros(acc_sc.shape, F32)

    maps = range(2)

    def scores(j):
        kb = k_ref[pl.ds(pl.multiple_of(j * tile, tile), tile), :]
        return [_dot_nt(kb, qm[mi]) for mi in maps]

    def scores_to(dst, j):
        s = scores(j)
        for mi in maps:
            dst[mi] = s[mi]

    def consume(s, j, table, const):
        vtb = vt_ref[:, pl.ds(pl.multiple_of(j * tile, tile), tile)]
        if table is not None:
            s = [s[mi] + table for mi in maps]
        m_prev = [m_sc[mi] for mi in maps]
        m_new = [jnp.maximum(m_prev[mi], jnp.max(s[mi], axis=0, keepdims=True) + const) for mi in maps]
        p = [jnp.exp2(s[mi] - (m_new[mi] - const)) for mi in maps]
        for mi in maps:
            alpha = jnp.exp2(m_prev[mi] - m_new[mi])
            l_sc[mi] = alpha * l_sc[mi] + jnp.sum(p[mi], axis=0, keepdims=True)
            acc_sc[mi] = alpha * acc_sc[mi] + _dot(vtb, p[mi].astype(BF16))
            m_sc[mi] = m_new[mi]

    held = lambda buf: [buf[mi] for mi in maps]
    cfar = cfar_ref[h]
    nfar = jnp.maximum(i - 1, 0)

    @pl.when(i >= 1)
    def _():
        scores_to(sa_sc, 0)

    def far_pair(t, c):
        j = 2 * t
        scores_to(sb_sc, j + 1)
        consume(held(sa_sc), j, None, cfar)

        @pl.when(j + 1 < nfar)
        def _():
            scores_to(sa_sc, j + 2)
            consume(held(sb_sc), j + 1, None, cfar)

        return c

    lax.fori_loop(0, (nfar + 1) // 2, far_pair, 0)

    for parity, buf in enumerate((sa_sc, sb_sc)):
        @pl.when(jnp.logical_and(i >= 1, nfar % 2 == parity))
        def _():
            consume(held(buf), i - 1, bias_ref[0, 1], 0.0)

    consume(scores(i), i, bias_ref[0, 0], 0.0)

    lam_rows = lam_ref[...]
    lam = (jnp.exp(jnp.sum(lam_rows[0:1] * lam_rows[1:2], axis=-1, keepdims=True))
           - jnp.exp(jnp.sum(lam_rows[2:3] * lam_rows[3:4], axis=-1, keepdims=True)) + lam_init)
    o = acc_sc[0] / l_sc[0] - lam * (acc_sc[1] / l_sc[1])
    o = o * lax.rsqrt(jnp.mean(o * o, axis=0, keepdims=True) + SUBNORM_EPS)
    o_ref[...] = (o.T * subg_ref[...] * (1.0 - lam_init)).astype(o_ref.dtype)


def _diff_attention(q, k, vt, rel_bias, lam_rows, sub_g, lam_init, batch, seq):
    t, width = q.shape
    hw = 2 * A_HEAD_DIM
    tile = ATTN_TILE
    nq = seq // tile
    tables = _bias_tables(rel_bias, tile) * LOG2E
    cfar = rel_bias[REL_BUCKETS - 1].astype(F32) * LOG2E
    return pl.pallas_call(
        functools.partial(_attn_kernel, lam_init=lam_init),
        grid=(batch, A_HEADS, nq),
        in_specs=[
            pl.BlockSpec(memory_space=pltpu.SMEM),
            pl.BlockSpec((tile, hw), lambda b, h, i: (b * nq + i, h)),
            pl.BlockSpec((seq, hw), lambda b, h, i: (b, h)),
            pl.BlockSpec((hw, seq), lambda b, h, i: (h, b)),
            pl.BlockSpec((1, 2, tile, tile), lambda b, h, i: (h, 0, 0, 0)),
            pl.BlockSpec(lam_rows.shape, lambda b, h, i: (0, 0)),
            pl.BlockSpec((1, hw), lambda b, h, i: (0, 0)),
        ],
        out_specs=pl.BlockSpec((tile, hw), lambda b, h, i: (b * nq + i, h)),
        out_shape=jax.ShapeDtypeStruct((t, width), BF16),
        scratch_shapes=[pltpu.VMEM((2, 1, tile), F32), pltpu.VMEM((2, 1, tile), F32),
                        pltpu.VMEM((2, hw, tile), F32),
                        pltpu.VMEM((2, tile, tile), F32), pltpu.VMEM((2, tile, tile), F32)],
        compiler_params=_cparams("arbitrary", "arbitrary", "arbitrary"),
        name="diff_attn",
    )(cfar, q, k, vt, tables, lam_rows, sub_g.reshape(1, hw))


def _lru_kernel(xb_ref, gb_ref, cw_ref, vec_ref, wa_ref, wx_ref, y_ref, prev_sc, h_sc, a_sc, b_sc):
    ts = xb_ref.shape[0]

    @pl.when(pl.program_id(1) == 0)
    def _():
        prev_sc[...] = jnp.zeros(prev_sc.shape, F32)
        h_sc[...] = jnp.zeros(h_sc.shape, F32)

    xb = xb_ref[...]
    hist = prev_sc.shape[0]
    ext = jnp.concatenate([prev_sc[...], xb], axis=0)
    u = vec_ref[0:1, :]
    for j in range(CONV_W):
        off = hist - (CONV_W - 1) + j
        u = u + cw_ref[j:j + 1, :] * ext[off:off + ts, :]
    prev_sc[...] = xb[ts - hist:, :]
    ub = u.astype(BF16)
    r = jax.nn.sigmoid(_dot(ub, wa_ref[...]) + vec_ref[1:2, :])
    ig = jax.nn.sigmoid(_dot(ub, wx_ref[...]) + vec_ref[2:3, :])
    log_a = (-LRU_C * r) * _softplus(-vec_ref[3:4, :])
    a_sc[...] = jnp.exp(log_a)
    b_sc[...] = jnp.sqrt(_neg_expm1(2.0 * log_a)) * (ig * u)

    def step(tt, h):
        h = a_sc[pl.ds(tt, 1), :] * h + b_sc[pl.ds(tt, 1), :]
        b_sc[pl.ds(tt, 1), :] = h
        return h

    h_sc[...] = lax.fori_loop(0, ts, step, h_sc[...], unroll=8)
    y_ref[...] = (b_sc[...] * _gelu_tanh(gb_ref[...])).astype(y_ref.dtype)


def _block_diag(w):
    n, c, d = w.shape
    eye = jnp.eye(n, dtype=w.dtype)
    return (w[:, :, None, :] * eye[:, None, :, None]).reshape(n * c, n * d)


def _rglru(xb, gb, conv_w, conv_b, wa, ba, wx, bx, lru_lambda, batch, seq):
    t, w = xb.shape
    ts = LRU_TILE
    nt = seq // ts
    vec = jnp.zeros((8, w), F32).at[0].set(conv_b).at[1].set(ba).at[2].set(bx).at[3].set(lru_lambda)
    row = lambda b, s: (b * nt + s, 0)
    fixed = lambda b, s: (0, 0)
    return pl.pallas_call(
        _lru_kernel,
        grid=(batch, nt),
        in_specs=[pl.BlockSpec((ts, w), row), pl.BlockSpec((ts, w), row),
                  pl.BlockSpec((CONV_W, w), fixed), pl.BlockSpec((8, w), fixed),
                  pl.BlockSpec((w, w), fixed), pl.BlockSpec((w, w), fixed)],
        out_specs=pl.BlockSpec((ts, w), row),
        out_shape=jax.ShapeDtypeStruct((t, w), BF16),
        scratch_shapes=[pltpu.VMEM((8, w), F32), pltpu.VMEM((1, w), F32),
                        pltpu.VMEM((ts, w), F32), pltpu.VMEM((ts, w), F32)],
        compiler_params=_cparams("arbitrary", "arbitrary"),
        name="rglru",
    )(xb, gb, conv_w, vec, _block_diag(wa).astype(BF16), _block_diag(wx).astype(BF16))


def _route(x, gf_ref, wrh_ref, wrl_ref, br_ref):
    hn = _rms(x, gf_ref[...], NORM_EPS)
    hh, hl = _split2(hn)
    wh = wrh_ref[...]
    lg = _dot(hh, wh) + _dot(hh, wrl_ref[...]) + _dot(hl, wh) + br_ref[...]
    lane = lax.broadcasted_iota(I32, lg.shape, 1)
    big = jnp.int32(1 << 20)
    is_g = lane < MOE_GROUPS
    gl = jnp.where(is_g, lg, NEG_BIG)
    gmax = jnp.max(gl, axis=-1, keepdims=True)
    gidx = jnp.min(jnp.where(gl == gmax, lane, big), axis=-1, keepdims=True)
    gsum = jnp.sum(jnp.where(is_g, jnp.exp(gl - gmax), 0.0), axis=-1, keepdims=True)
    gprob = 1.0 / gsum
    lo = MOE_GROUPS + gidx * EXPERTS_PER_GROUP
    fmask = (lane >= lo) & (lane < lo + EXPERTS_PER_GROUP)
    fl = jnp.where(fmask, lg, NEG_BIG)
    f1 = jnp.max(fl, axis=-1, keepdims=True)
    i1 = jnp.min(jnp.where(fl == f1, lane, big), axis=-1, keepdims=True)
    fl2 = jnp.where(lane == i1, NEG_BIG, fl)
    f2 = jnp.max(fl2, axis=-1, keepdims=True)
    i2 = jnp.min(jnp.where(fl2 == f2, lane, big), axis=-1, keepdims=True)
    e2 = jnp.exp(f2 - f1)
    den = 1.0 + e2
    g1 = gprob / den
    g2 = gprob * e2 / den
    id1 = (i1 - MOE_GROUPS).astype(F32)
    id2 = (i2 - MOE_GROUPS).astype(F32)
    return jnp.where(lane == 0, g1, jnp.where(lane == 1, g2,
                     jnp.where(lane == 2, id1, jnp.where(lane == 3, id2, 0.0))))


def _router_params(wc, bc, wf, bf):
    d = wc.shape[0]
    n = MOE_GROUPS + N_EXPERTS
    w = jnp.zeros((d, LANES), F32).at[:, :MOE_GROUPS].set(wc).at[:, MOE_GROUPS:n].set(wf)
    b = jnp.zeros((1, LANES), F32).at[0, :MOE_GROUPS].set(bc).at[0, MOE_GROUPS:n].set(bf)
    hi, lo = _split2(w)
    return hi, lo, b


def _outproj_kernel(x_ref, ya_ref, yb_ref, wo_ref, gf_ref, wrh_ref, wrl_ref, br_ref,
                    x1_ref, x1t_ref, rt_ref):
    half = ya_ref.shape[1]
    x1 = x_ref[...] + _dot(ya_ref[...], wo_ref[:half, :]) + _dot(yb_ref[...], wo_ref[half:, :])
    x1_ref[...] = x1
    _store_token_tiles(x1t_ref, x1)
    rt_ref[...] = _route(x1, gf_ref, wrh_ref, wrl_ref, br_ref)


def _outproj(x2, ya, yb, w_out, gf, router):
    t, d = x2.shape
    half = ya.shape[1]
    tm = ROW_TILE
    row = lambda i: (i, 0)
    fixed = lambda i: (0, 0)
    wrh, wrl, br = router
    return pl.pallas_call(
        _outproj_kernel,
        grid=(t // tm,),
        in_specs=[pl.BlockSpec((tm, d), row), pl.BlockSpec((tm, half), row), pl.BlockSpec((tm, half), row),
                  pl.BlockSpec((2 * half, d), fixed), pl.BlockSpec((1, d), fixed),
                  pl.BlockSpec((d, LANES), fixed), pl.BlockSpec((d, LANES), fixed),
                  pl.BlockSpec((1, LANES), fixed)],
        out_specs=[pl.BlockSpec((tm, d), row), pl.BlockSpec((tm * SUBLANES, LANES), row),
                   pl.BlockSpec((tm, LANES), row)],
        out_shape=[jax.ShapeDtypeStruct((t, d), F32), jax.ShapeDtypeStruct((t * SUBLANES, LANES), F32),
                   jax.ShapeDtypeStruct((t, LANES), F32)],
        compiler_params=_cparams("arbitrary"),
        name="outproj_route",
    )(x2, ya, yb, w_out, gf.reshape(1, d), wrh, wrl, br)


def _moe_plan(route, n_tok):
    m = n_tok * MOE_TOPK
    n_blk = -(-(m + N_EXPERTS * (MOE_BLOCK - 1)) // MOE_BLOCK)
    eid = route[:, 2:4].astype(I32).reshape(-1)
    onehot = (eid[:, None] == jnp.arange(N_EXPERTS, dtype=I32)[None, :]).astype(I32)
    csum = jnp.cumsum(onehot, axis=0)
    rank = jnp.sum(csum * onehot, axis=1) - 1
    sizes = csum[-1]
    padded = (sizes + MOE_BLOCK - 1) // MOE_BLOCK * MOE_BLOCK
    pend = jnp.cumsum(padded)
    dest = (pend - padded)[eid] + rank
    tok = jnp.repeat(jnp.arange(n_tok, dtype=I32), MOE_TOPK)
    rows_tok = jnp.zeros((n_blk * MOE_BLOCK,), I32).at[dest].set(tok)
    blk_e = jnp.minimum(jnp.searchsorted(pend, jnp.arange(n_blk, dtype=I32) * MOE_BLOCK, side='right'),
                        N_EXPERTS - 1).astype(I32)
    n_used = (pend[-1] // MOE_BLOCK).astype(I32).reshape(1)
    return rows_tok.reshape(n_blk, MOE_BLOCK), blk_e, n_used, dest.astype(I32)


def _gather_ahead(idx_hbm, idx_sms, idx_sem, issue_rows, n_active):
    i = pl.program_id(0)
    n = pl.num_programs(0)

    def idx_copy(step, slot):
        return pltpu.make_async_copy(idx_hbm.at[step], idx_sms[slot], idx_sem.at[slot])

    @pl.when(i == 0)
    def _():
        first = idx_copy(0, 0)
        first.start()
        first.wait()

        @pl.when(0 < n_active)
        def _():
            issue_rows(0)

        @pl.when(1 < n)
        def _():
            idx_copy(1, 1).start()

    for slot in range(2):
        @pl.when(jnp.logical_and(i + 1 < n, (i + 1) % 2 == slot))
        def _():
            idx_copy(i + 1, slot).wait()

            @pl.when(i + 1 < n_active)
            def _():
                issue_rows(slot)

        @pl.when(jnp.logical_and(i + 2 < n, i % 2 == slot))
        def _():
            idx_copy(i + 2, slot).start()


def _expert_kernel(blk_e_ref, nused_ref, rows_hbm, xt_hbm, gf_ref, wgu_ref, wd_ref, yst_ref,
                   idx_sm0, idx_sm1, idx_sem, xbuf, row_sem):
    i = pl.program_id(0)
    slot = i % 2
    blk = xbuf.shape[1] // SUBLANES
    idx_sms = (idx_sm0, idx_sm1)

    def issue_rows(sl):
        def issue(r, c):
            src = pl.multiple_of(idx_sms[sl][r], SUBLANES)
            dst = pl.multiple_of(r * SUBLANES, SUBLANES)
            pltpu.make_async_copy(xt_hbm.at[pl.ds(src, SUBLANES), :], xbuf.at[sl, pl.ds(dst, SUBLANES), :],
                                  row_sem.at[sl]).start()
            return c

        lax.fori_loop(0, blk, issue, 0, unroll=8)

    _gather_ahead(rows_hbm, idx_sms, idx_sem, issue_rows, nused_ref[0])

    @pl.when(i < nused_ref[0])
    def _():
        pltpu.make_async_copy(xt_hbm.at[pl.ds(0, blk * SUBLANES), :], xbuf.at[slot], row_sem.at[slot]).wait()
        x = _load_token_tiles(xbuf.at[slot], blk)
        hn = _rms(x, gf_ref[...], NORM_EPS).astype(BF16)
        gu = _dot(hn, wgu_ref[0])
        eh = gu.shape[1] // 2
        g = gu[:, :eh]
        act = (g * jax.nn.sigmoid(g) * gu[:, eh:]).astype(BF16)
        _store_token_tiles(yst_ref, _dot(act, wd_ref[0]))

    @pl.when(i >= nused_ref[0])
    def _():
        yst_ref[...] = jnp.zeros(yst_ref.shape, F32)


def _experts(x1t, rows_tok, blk_e, n_used, gf, w_gu, w_down):
    d = gf.shape[0]
    n_blk, blk = rows_tok.shape
    eh2 = w_gu.shape[2]
    grid_spec = pltpu.PrefetchScalarGridSpec(
        num_scalar_prefetch=2,
        grid=(n_blk,),
        in_specs=[
            pl.BlockSpec(memory_space=pl.ANY),
            pl.BlockSpec(memory_space=pl.ANY),
            pl.BlockSpec((1, d), lambda i, e, n: (0, 0)),
            pl.BlockSpec((1, d, eh2), lambda i, e, n: (e[i], 0, 0)),
            pl.BlockSpec((1, eh2 // 2, d), lambda i, e, n: (e[i], 0, 0)),
        ],
        out_specs=pl.BlockSpec((blk * SUBLANES, LANES), lambda i, e, n: (i, 0)),
        scratch_shapes=[pltpu.SMEM((blk,), I32), pltpu.SMEM((blk,), I32), pltpu.SemaphoreType.DMA((2,)),
                        pltpu.VMEM((2, blk * SUBLANES, LANES), F32), pltpu.SemaphoreType.DMA((2,))],
    )
    return pl.pallas_call(
        _expert_kernel,
        grid_spec=grid_spec,
        out_shape=jax.ShapeDtypeStruct((n_blk * blk * SUBLANES, LANES), F32),
        compiler_params=_cparams("arbitrary"),
        name="moe_experts",
    )(blk_e, n_used, rows_tok * SUBLANES, x1t, gf.reshape(1, d), w_gu, w_down)


def _combine_kernel(pos_hbm, yst_hbm, x1_ref, rt_ref, p_ref, gn_ref, wg_ref, wp_ref, fn_ref, o_ref,
                    idx_sm0, idx_sm1, idx_sem, ybuf, row_sem, *, final):
    slot = pl.program_id(0) % 2
    tm = x1_ref.shape[0]
    idx_sms = (idx_sm0, idx_sm1)

    def issue_rows(sl):
        def issue(r, c):
            dst = pl.multiple_of(r * SUBLANES, SUBLANES)
            for s in range(MOE_TOPK):
                src = pl.multiple_of(idx_sms[sl][MOE_TOPK * r + s], SUBLANES)
                pltpu.make_async_copy(yst_hbm.at[pl.ds(src, SUBLANES), :],
                                      ybuf.at[sl, s, pl.ds(dst, SUBLANES), :], row_sem.at[sl]).start()
            return c

        lax.fori_loop(0, tm, issue, 0, unroll=8)

    _gather_ahead(pos_hbm, idx_sms, idx_sem, issue_rows, pl.num_programs(0))
    for s in range(MOE_TOPK):
        pltpu.make_async_copy(yst_hbm.at[pl.ds(0, tm * SUBLANES), :], ybuf.at[slot, s], row_sem.at[slot]).wait()

    rt = rt_ref[...]
    y0 = _load_token_tiles(ybuf.at[slot, 0], tm)
    y1 = _load_token_tiles(ybuf.at[slot, 1], tm)
    x2 = x1_ref[...] + (rt[:, 0:1] * y0 + rt[:, 1:2] * y1)
    gate = jax.nn.sigmoid(_dot(_rms(x2, gn_ref[...], NORM_EPS).astype(BF16), wg_ref[...]))
    x3 = x2 + _dot(p_ref[...].astype(BF16), wp_ref[...]) * gate
    if final:
        x3 = _rms(x3, fn_ref[...], NORM_EPS)
    o_ref[...] = x3


def _combine(x1, route, yst, dest, p_i, ple_norm, ple_gate, ple_proj, final_norm, final):
    t, d = x1.shape
    tm = COMBINE_TILE
    pd = p_i.shape[1]
    pos = (dest * SUBLANES).reshape(t // tm, MOE_TOPK * tm)
    row = lambda i: (i, 0)
    fixed = lambda i: (0, 0)
    return pl.pallas_call(
        functools.partial(_combine_kernel, final=final),
        grid=(t // tm,),
        in_specs=[pl.BlockSpec(memory_space=pl.ANY), pl.BlockSpec(memory_space=pl.ANY),
                  pl.BlockSpec((tm, d), row), pl.BlockSpec((tm, LANES), row), pl.BlockSpec((tm, pd), row),
                  pl.BlockSpec((1, d), fixed), pl.BlockSpec((d, d), fixed), pl.BlockSpec((pd, d), fixed),
                  pl.BlockSpec((1, d), fixed)],
        out_specs=pl.BlockSpec((tm, d), row),
        out_shape=jax.ShapeDtypeStruct((t, d), F32),
        scratch_shapes=[pltpu.SMEM((MOE_TOPK * tm,), I32), pltpu.SMEM((MOE_TOPK * tm,), I32),
                        pltpu.SemaphoreType.DMA((2,)),
                        pltpu.VMEM((2, MOE_TOPK, tm * SUBLANES, LANES), F32), pltpu.SemaphoreType.DMA((2,))],
        compiler_params=_cparams("arbitrary"),
        name="moe_combine_ple",
    )(pos, yst, x1, route, p_i, ple_norm.reshape(1, d), ple_gate, ple_proj, final_norm.reshape(1, d))


def _seg_sum(x, eseg_ref, eexp_ref):
    s = _dot(x.astype(BF16), eseg_ref[...])
    shi, slo = _split2(s)
    ex = eexp_ref[...]
    return _dot(shi, ex) + _dot(slo, ex)


def _rwkv_pre_kernel(x_ref, xp_ref, gm_ref, mu_ref, vec_ref, wr_ref, wk_ref, wv_ref, w1_ref, w2_ref,
                     a1_ref, a2_ref, g1_ref, g2_ref, tri_ref, eseg_ref, eexp_ref,
                     la_ref, lr_ref, rb_ref, rk_ref, sb_ref, sk_ref, v_ref, bon_ref, g_ref, gc_ref,
                     c_sc):
    tl = x_ref.shape[0]
    gm = gm_ref[...]
    h = _rms(x_ref[...], gm, NORM_EPS)
    hp = _rms(xp_ref[...], gm, NORM_EPS)[7:8, :]
    hp = jnp.where(pl.program_id(1) == 0, 0.0, hp)
    row = lax.broadcasted_iota(I32, h.shape, 0)
    hs = jnp.where(row == 0, hp, pltpu.roll(h, 1, axis=0))
    xx = hs - h
    mix = lambda n: (h + xx * mu_ref[n:n + 1, :]).astype(BF16)
    vec = lambda n: vec_ref[n:n + 1, :]
    r = _dot(mix(0), wr_ref[...])
    wl = vec(0) + _dot(jnp.tanh(_dot(mix(1), w1_ref[...])).astype(BF16), w2_ref[...])
    lw = -jnp.exp(-_softplus(-wl) - 0.5)
    k = _dot(mix(2), wk_ref[...])
    v = _dot(mix(3), wv_ref[...])
    a = jax.nn.sigmoid(vec(1) + _dot(_dot(mix(4), a1_ref[...]).astype(BF16), a2_ref[...]))
    g = _dot(jax.nn.sigmoid(_dot(mix(5), g1_ref[...])).astype(BF16), g2_ref[...])
    kk = k * vec(2)
    kk = kk / jnp.maximum(jnp.sqrt(_seg_sum(kk * kk, eseg_ref, eexp_ref)), 1e-12)
    k2 = k * (1.0 + (a - 1.0) * vec(3))
    bon_ref[...] = (_seg_sum(r * k2 * vec(4), eseg_ref, eexp_ref) * v).astype(bon_ref.dtype)
    v_ref[...] = v.astype(v_ref.dtype)
    g_ref[...] = g.astype(g_ref.dtype)

    half = tri_ref.shape[1]
    tri = tri_ref[...]
    for s in range(tl // half):
        part = lw[s * half:(s + 1) * half, :]
        p1 = part.astype(BF16)
        rem = part - p1.astype(F32)
        p2 = rem.astype(BF16)
        p3 = (rem - p2.astype(F32)).astype(BF16)
        cs = _dot(tri, p1) + _dot(tri, p2) + _dot(tri, p3)
        c_sc[0, s * half:(s + 1) * half, :] = cs[:half]
        c_sc[1, s * half:(s + 1) * half, :] = cs[half:]
    c_in = c_sc[0]
    c_end = c_sc[1]
    nc = tl // RWKV_CHUNK
    ends = [c_sc[0, (n + 1) * RWKV_CHUNK - 1:(n + 1) * RWKV_CHUNK, :] for n in range(nc)]
    gc_ref[...] = jnp.exp(jnp.concatenate(ends, axis=0))
    e_in = jnp.exp(c_in)
    e_neg = jnp.exp(-c_in)
    e_end = jnp.exp(c_end)
    b = kk * a
    la_ref[...] = (-kk * jnp.exp(c_in - lw)).astype(la_ref.dtype)
    lr_ref[...] = (r * e_in).astype(lr_ref.dtype)
    rb_ref[...] = (b * e_neg).astype(rb_ref.dtype)
    rk_ref[...] = (k2 * e_neg).astype(rk_ref.dtype)
    sb_ref[...] = (b * e_end).astype(sb_ref.dtype)
    sk_ref[...] = (k2 * e_end).astype(sk_ref.dtype)


def _head_indicator(d):
    heads = d // RWKV_HEAD
    e = (jnp.arange(d)[:, None] // RWKV_HEAD == jnp.arange(LANES)[None, :]).astype(BF16)
    del heads
    return e, e.T


def _cumsum_matrix(half):
    i = jnp.arange(half)[:, None]
    j = jnp.arange(half)[None, :]
    same = (i // RWKV_CHUNK) == (j // RWKV_CHUNK)
    lower = same & (j <= i)
    upper = same & (j > i)
    return jnp.concatenate([lower, upper], axis=0).astype(BF16)


def _pad_cols(w, n):
    return jnp.pad(w, ((0, 0), (0, n - w.shape[1])))


def _pad_rows(w, n):
    return jnp.pad(w, ((0, n - w.shape[0]), (0, 0)))


def _rwkv_pre(x3, gm, mu, vecs, wr, wk, wv, w1, w2, a1, a2, g1, g2, batch, seq):
    t, d = x3.shape
    tl = RWKV_TILE
    nt = seq // tl
    half = 256
    lora = lambda w_in, w_out: (_pad_cols(w_in, -(-w_in.shape[1] // LANES) * LANES).astype(BF16),
                                _pad_rows(w_out, -(-w_out.shape[0] // LANES) * LANES).astype(BF16))
    w1p, w2p = lora(w1, w2)
    a1p, a2p = lora(a1, a2)
    g1p, g2p = lora(g1, g2)
    eseg, eexp = _head_indicator(d)
    tri = _cumsum_matrix(half)
    row = lambda b, s: (b * nt + s, 0)
    fixed = lambda b, s: (0, 0)
    full = lambda a: pl.BlockSpec(a.shape, fixed)
    prev = lambda b, s: (jnp.maximum((b * nt + s) * (tl // 8) - 1, 0), 0)
    weights = (wr.astype(BF16), wk.astype(BF16), wv.astype(BF16), w1p, w2p, a1p, a2p, g1p, g2p, tri, eseg, eexp)
    outs = pl.pallas_call(
        _rwkv_pre_kernel,
        grid=(batch, nt),
        in_specs=[pl.BlockSpec((tl, d), row), pl.BlockSpec((8, d), prev), pl.BlockSpec((1, d), fixed),
                  full(mu), full(vecs)] + [full(w) for w in weights],
        out_specs=[pl.BlockSpec((tl, d), row)] * 9 + [pl.BlockSpec((tl // RWKV_CHUNK, d), row)],
        out_shape=[jax.ShapeDtypeStruct((t, d), BF16)] * 9 + [jax.ShapeDtypeStruct((t // RWKV_CHUNK, d), F32)],
        scratch_shapes=[pltpu.VMEM((2, tl, d), F32)],
        compiler_params=_cparams("arbitrary", "arbitrary"),
        name="rwkv_pre",
    )(x3, x3, gm.reshape(1, d), mu, vecs, *weights)
    return outs


def _rwkv_scan_kernel(la_ref, lr_ref, rb_ref, rk_ref, sb_ref, sk_ref, v_ref, gc_ref, bon_ref, g_ref, x_ref,
                      vec_ref, wo_ref, eseg_ref, eexp_ref, gf_ref, wrh_ref, wrl_ref, br_ref,
                      x4_ref, x4t_ref, rt_ref, s_sc, y_sc):
    tl, d = x_ref.shape
    ch = RWKV_CHUNK
    pw = 2 * RWKV_HEAD
    npair = d // pw

    @pl.when(pl.program_id(1) == 0)
    def _():
        s_sc[...] = jnp.zeros(s_sc.shape, F32)

    ri = lax.broadcasted_iota(I32, (pw, pw), 0)
    ci = lax.broadcasted_iota(I32, (pw, pw), 1)
    same_half = (ri < RWKV_HEAD) == (ci < RWKV_HEAD)
    eye = (ri == ci).astype(F32)
    rt_ = lax.broadcasted_iota(I32, (ch, pw), 0)
    ct_ = lax.broadcasted_iota(I32, (ch, pw), 1)
    first = ct_ < RWKV_HEAD
    strict = rt_ > (ct_ & (RWKV_HEAD - 1))
    incl = rt_ >= (ct_ & (RWKV_HEAD - 1))
    m_s0, m_s1 = strict & first, strict & ~first
    m_i0, m_i1 = incl & first, incl & ~first
    first_full = ci < RWKV_HEAD
    zf = jnp.zeros((ch, pw), F32)
    gc_row = lax.broadcasted_iota(I32, (tl // ch, pw), 0)

    def chunk(c, carry):
        rows = pl.ds(pl.multiple_of(c * ch, ch), ch)
        pairs = range(npair)
        cols = [slice(p * pw, (p + 1) * pw) for p in pairs]
        cat0 = lambda a, b: jnp.concatenate([a, b], axis=0)
        lcat = [cat0(la_ref[rows, cl], lr_ref[rows, cl]) for cl in cols]
        rb = [rb_ref[rows, cl] for cl in cols]
        rk = [rk_ref[rows, cl] for cl in cols]
        vv = [v_ref[rows, cl] for cl in cols]
        zb = jnp.zeros_like(lcat[0])
        a0 = [_dot_nt(jnp.where(first_full, lcat[p], zb), cat0(rb[p], rk[p])) for p in pairs]
        a1 = [_dot_nt(jnp.where(first_full, zb, lcat[p]), cat0(rk[p], rb[p])) for p in pairs]
        pm = [cat0(jnp.where(m_s0, a0[p][:ch], zf), jnp.where(m_s1, a1[p][:ch], zf)) for p in pairs]
        aak = [cat0(jnp.where(m_s1, a0[p][:ch], zf), jnp.where(m_s0, a1[p][:ch], zf)).astype(BF16)
               for p in pairs]
        arbk = [jnp.concatenate([cat0(jnp.where(m_i0, a0[p][ch:], zf), jnp.where(m_i1, a1[p][ch:], zf)),
                                 cat0(jnp.where(m_i1, a0[p][ch:], zf), jnp.where(m_i0, a1[p][ch:], zf))],
                                axis=1).astype(BF16) for p in pairs]
        tm = [eye + pm[p] for p in pairs]
        qm = pm
        for _ in range(5):
            qb = [q.astype(BF16) for q in qm]
            qm = [_dot(qb[p], qb[p]) for p in pairs]
            tm = [tm[p] + _dot(tm[p].astype(BF16), qm[p].astype(BF16)) for p in pairs]
        st = [s_sc[p] for p in pairs]
        w12 = [_dot_nt(lcat[p], st[p].astype(BF16)) for p in pairs]
        v2 = [cat0(vv[p], vv[p]) for p in pairs]
        rhs = [cat0(w12[p][:ch], w12[p][:ch]) + _dot(aak[p], v2[p]) for p in pairs]
        ust = [_dot(tm[p].astype(BF16), rhs[p].astype(BF16)) for p in pairs]
        yst = [cat0(w12[p][ch:], w12[p][ch:]) + _dot(arbk[p], cat0(ust[p].astype(BF16), v2[p]))
               for p in pairs]
        for p in pairs:
            y_sc[rows, cols[p]] = jnp.where(first, yst[p][:ch], yst[p][ch:])
        u = [jnp.where(first, ust[p][:ch], ust[p][ch:]).astype(BF16) for p in pairs]
        ds_ = [_dot_tn(cat0(u[p], vv[p]), cat0(sb_ref[rows, cols[p]], sk_ref[rows, cols[p]])) for p in pairs]
        for p in pairs:
            gcr = jnp.sum(jnp.where(gc_row == c, gc_ref[:, cols[p]], 0.0), axis=0, keepdims=True)
            s_sc[p] = jnp.where(same_half, st[p] * gcr + ds_[p], 0.0)
        return carry

    lax.fori_loop(0, tl // ch, chunk, 0)

    y = y_sc[...]
    inv_n = 1.0 / RWKV_HEAD
    mean = _seg_sum(y, eseg_ref, eexp_ref) * inv_n
    dlt = y - mean
    var = _seg_sum(dlt * dlt, eseg_ref, eexp_ref) * inv_n
    yn = dlt * lax.rsqrt(var + RWKV_GN_EPS) * vec_ref[5:6, :] + vec_ref[6:7, :]
    z = (yn + bon_ref[...].astype(F32)) * g_ref[...].astype(F32)
    x4 = x_ref[...] + _dot(z.astype(BF16), wo_ref[...])
    x4_ref[...] = x4
    _store_token_tiles(x4t_ref, x4)
    rt_ref[...] = _route(x4, gf_ref, wrh_ref, wrl_ref, br_ref)


def _rwkv_scan(pre, x3, vecs, wo, gf, router, batch, seq):
    la, lr, rb, rk, sb, sk, v, bon, g, gc = pre
    t, d = x3.shape
    tl = RWKV_TILE
    nt = seq // tl
    eseg, eexp = _head_indicator(d)
    wrh, wrl, br = router
    row = lambda b, s: (b * nt + s, 0)
    fixed = lambda b, s: (0, 0)
    big = pl.BlockSpec((tl, d), row)
    full = lambda a: pl.BlockSpec(a.shape, fixed)
    pw = 2 * RWKV_HEAD
    return pl.pallas_call(
        _rwkv_scan_kernel,
        grid=(batch, nt),
        in_specs=[big] * 7 + [pl.BlockSpec((tl // RWKV_CHUNK, d), row), big, big, big,
                              full(vecs), full(wo), full(eseg), full(eexp),
                              pl.BlockSpec((1, d), fixed), full(wrh), full(wrl), full(br)],
        out_specs=[big, pl.BlockSpec((tl * SUBLANES, LANES), row), pl.BlockSpec((tl, LANES), row)],
        out_shape=[jax.ShapeDtypeStruct((t, d), F32), jax.ShapeDtypeStruct((t * SUBLANES, LANES), F32),
                   jax.ShapeDtypeStruct((t, LANES), F32)],
        scratch_shapes=[pltpu.VMEM((d // pw, pw, pw), F32), pltpu.VMEM((tl, d), F32)],
        compiler_params=_cparams("arbitrary", "arbitrary"),
        name="rwkv_scan_out",
    )(la, lr, rb, rk, sb, sk, v, gc, bon, g, x3, vecs, wo, eseg, eexp, gf.reshape(1, d), wrh, wrl, br)


def _moe_and_embed(x1, x1t, route, i, p, norm_ffn, moe_w_gu, moe_w_down, ple_norm, ple_gate, ple_proj,
                   final_norm, final):
    t = x1.shape[0]
    rows_tok, blk_e, n_used, dest = _moe_plan(route, t)
    yst = _experts(x1t, rows_tok, blk_e, n_used, norm_ffn[i], moe_w_gu[i].astype(BF16),
                   moe_w_down[i].astype(BF16))
    p_i = p[i].reshape(t, p.shape[-1])
    return _combine(x1, route, yst, dest, p_i, ple_norm[i], ple_gate[i].astype(BF16),
                    ple_proj[i].astype(BF16), final_norm, final)


def kernel(x, p, rel_bias, norm_mix, ab_w_in, ab_w_out, lam_q1, lam_k1, lam_q2, lam_k2, sub_g, conv_w, conv_b, lru_wa, lru_ba, lru_wx, lru_bx, lru_lambda, rwkv_mu, rwkv_wr, rwkv_wk, rwkv_wv, rwkv_wo, rwkv_w0, rwkv_w1, rwkv_w2, rwkv_a0, rwkv_a1, rwkv_a2, rwkv_g1, rwkv_g2, rwkv_kk, rwkv_ka, rwkv_rk, rwkv_ln_g, rwkv_ln_b, norm_ffn, moe_wc, moe_bc, moe_wf, moe_bf, moe_w_gu, moe_w_down, ple_norm, ple_gate, ple_proj, final_norm):
    batch, seq, d = x.shape
    t = batch * seq
    depth = norm_mix.shape[0]
    assert depth == 2 and seq % RWKV_TILE == 0 and seq % LRU_TILE == 0 and t % ROW_TILE == 0
    assert d == SUBLANES * LANES and seq % ATTN_TILE == 0 and t % COMBINE_TILE == 0
    xs = x.reshape(t, d)
    for i in range(depth):
        j = i // 2
        router = _router_params(moe_wc[i], moe_bc[i], moe_wf[i], moe_bf[i])
        if i % 2 == 0:
            lam_init = 0.8 - 0.6 * math.exp(-0.3 * i)
            q, k, vt, xb, gb = _inproj(xs, norm_mix[i], ab_w_in[j].astype(BF16))
            lam_rows = jnp.stack([lam_q1[j], lam_k1[j], lam_q2[j], lam_k2[j]]).astype(F32)
            ya = _diff_attention(q, k, vt, rel_bias, lam_rows, sub_g[j], lam_init, batch, seq)
            yb = _rglru(xb, gb, conv_w[j], conv_b[j], lru_wa[j], lru_ba[j], lru_wx[j], lru_bx[j],
                        lru_lambda[j], batch, seq)
            x1, x1t, route = _outproj(xs, ya, yb, ab_w_out[j].astype(BF16), norm_ffn[i], router)
        else:
            vecs = jnp.stack([rwkv_w0[j], rwkv_a0[j], rwkv_kk[j], rwkv_ka[j], rwkv_rk[j].reshape(-1),
                              rwkv_ln_g[j], rwkv_ln_b[j], jnp.zeros((d,), F32)]).astype(F32)
            pre = _rwkv_pre(xs, norm_mix[i], rwkv_mu[j], vecs, rwkv_wr[j], rwkv_wk[j], rwkv_wv[j],
                            rwkv_w1[j], rwkv_w2[j], rwkv_a1[j], rwkv_a2[j], rwkv_g1[j], rwkv_g2[j],
                            batch, seq)
            x1, x1t, route = _rwkv_scan(pre, xs, vecs, rwkv_wo[j].astype(BF16), norm_ffn[i], router,
                                        batch, seq)
        xs = _moe_and_embed(x1, x1t, route, i, p, norm_ffn, moe_w_gu, moe_w_down, ple_norm, ple_gate,
                            ple_proj, final_norm, final=(i == depth - 1))
    return xs.reshape(batch, seq, d)
```

```python
import functools
import math

import jax
import jax.numpy as jnp
from jax import lax
from jax.experimental import pallas as pl
from jax.experimental.pallas import tpu as pltpu

F32, BF16, I32 = jnp.float32, jnp.bfloat16, jnp.int32

A_HEADS = 4
A_HEAD_DIM = 64
A_SCALE = A_HEAD_DIM ** -0.5
LOG2E = math.log2(math.e)
REL_BUCKETS = 32
REL_MAX_DIST = 128
LRU_BLOCKS = 8
CONV_W = 4
LRU_C = 8.0
RWKV_HEAD = 64
RWKV_GN_EPS = 64e-5
MOE_GROUPS = 4
EXPERTS_PER_GROUP = 8
N_EXPERTS = MOE_GROUPS * EXPERTS_PER_GROUP
MOE_TOPK = 2
NORM_EPS = 1e-6
SUBNORM_EPS = 1e-5

LANES = 128
VMEM_LIMIT = 56 * 1024 * 1024
ROW_TILE = 512
ATTN_TILE = 512
LRU_TILE = 512
RWKV_CHUNK = 64
RWKV_TILE = 512
MOE_BLOCK = 512
COMBINE_TILE = 256
NEG_BIG = -1e30


def _cparams(*sem):
    return pltpu.CompilerParams(dimension_semantics=sem, vmem_limit_bytes=VMEM_LIMIT)


def _rms(x, g, eps):
    return x * lax.rsqrt(jnp.mean(x * x, axis=-1, keepdims=True) + eps) * g


def _dot(a, b):
    return jnp.dot(a, b, preferred_element_type=F32)


def _dot_nt(a, b):
    return lax.dot_general(a, b, (((1,), (1,)), ((), ())), preferred_element_type=F32)


def _dot_tn(a, b):
    return lax.dot_general(a, b, (((0,), (0,)), ((), ())), preferred_element_type=F32)


def _split2(x):
    hi = x.astype(BF16)
    lo = (x - hi.astype(F32)).astype(BF16)
    return hi, lo


def _softplus(x):
    return jnp.maximum(x, 0.0) + jnp.log1p(jnp.exp(-jnp.abs(x)))


def _neg_expm1(y, exp_y):
    poly = 1.0 + y * (1.0 / 5.0)
    for d in (4.0, 3.0, 2.0):
        poly = 1.0 + (y * (1.0 / d)) * poly
    return jnp.where(y > -1.0 / 16.0, -y * poly, 1.0 - exp_y)


SUBLANES = 8
DMA_PRIORITIES = 2


def _store_token_tiles(ref, x):
    rows = x.shape[0]
    for s in range(x.shape[1] // LANES):
        ref[pl.ds(s, rows, stride=SUBLANES), :] = x[:, s * LANES:(s + 1) * LANES]


def _load_token_tiles(ref, rows):
    return jnp.concatenate([ref[pl.ds(s, rows, stride=SUBLANES), :] for s in range(SUBLANES)], axis=1)


def _gelu_tanh(x):
    c = math.sqrt(2.0 / math.pi)
    return 0.5 * x * (1.0 + jnp.tanh(c * (x + 0.044715 * (x * x * x))))


def _inproj_kernel(x_ref, g_ref, w_ref, q_ref, k_ref, vt_ref, xb_ref, gb_ref):
    hn = _rms(x_ref[...], g_ref[...], NORM_EPS).astype(BF16)
    width = q_ref.shape[1]
    for c, o_ref in enumerate((q_ref, k_ref, vt_ref, xb_ref, gb_ref)):
        r = _dot(hn, w_ref[:, c * width:(c + 1) * width])
        if c == 0:
            r = r * (A_SCALE * LOG2E)
        if c == 2:
            r = r.T
        o_ref[...] = r.astype(o_ref.dtype)


def _inproj(x2, g, w_in):
    t, d = x2.shape
    width = w_in.shape[1] // 5
    tm = ROW_TILE
    row = lambda i: (i, 0)
    fixed = lambda i: (0, 0)
    rows = pl.BlockSpec((tm, width), row)
    return pl.pallas_call(
        _inproj_kernel,
        grid=(t // tm,),
        in_specs=[pl.BlockSpec((tm, d), row), pl.BlockSpec((1, d), fixed),
                  pl.BlockSpec(w_in.shape, fixed)],
        out_specs=[rows, rows, pl.BlockSpec((width, tm), lambda i: (0, i)), rows, rows],
        out_shape=[jax.ShapeDtypeStruct((t, width), BF16)] * 2 + [jax.ShapeDtypeStruct((width, t), BF16)]
        + [jax.ShapeDtypeStruct((t, width), F32)] * 2,
        compiler_params=_cparams("arbitrary"),
        name="inproj",
    )(x2, g.reshape(1, d), w_in)


def _t5_bucket(rel):
    n = jnp.maximum(rel, 0)
    max_exact = REL_BUCKETS // 2
    large = max_exact + (jnp.log(jnp.maximum(n, 1).astype(F32) / max_exact)
                         / math.log(REL_MAX_DIST / max_exact)
                         * (REL_BUCKETS - max_exact)).astype(I32)
    large = jnp.minimum(large, REL_BUCKETS - 1)
    return jnp.where(n < max_exact, n, large)


def _bias_tables(rel_bias, tile):
    span = 2 * tile

    def toeplitz(first_rel):
        rel = jnp.arange(span - 1) + first_rel
        vals = jnp.where((rel >= 0)[:, None], rel_bias[_t5_bucket(rel)].astype(F32), NEG_BIG)
        w = jnp.concatenate([vals, jnp.zeros((1, vals.shape[1]), F32)], axis=0).T
        skew = jnp.tile(w, (1, tile))[:, :tile * (span - 1)].reshape(-1, tile, span - 1)
        return skew[:, :, tile - 1:]

    return jnp.stack([toeplitz(1 - tile), toeplitz(1)], axis=1)


def _attn_kernel(cfar_ref, q_ref, k_ref, vt_ref, bias_ref, lam_ref, subg_ref, o_ref,
                 m_sc, l_sc, acc_sc, sa_sc, sb_sc, *, lam_init):
    h = pl.program_id(1)
    i = pl.program_id(2)
    tile = q_ref.shape[0]
    q = q_ref[...]
    lane = lax.broadcasted_iota(I32, q.shape, 1)
    zero = jnp.zeros_like(q)
    qm = (jnp.where(lane < A_HEAD_DIM, q, zero), jnp.where(lane >= A_HEAD_DIM, q, zero))
    m_sc[...] = jnp.full(m_sc.shape, NEG_BIG, F32)
    l_sc[...] = jnp.zeros(l_sc.shape, F32)
    acc_sc[...] = jnp.zeros(acc_sc.shape, F32)

    maps = range(2)

    def scores(j):
        kb = k_ref[pl.ds(pl.multiple_of(j * tile, tile), tile), :]
        return [_dot_nt(kb, qm[mi]) for mi in maps]

    def scores_to(dst, j):
        s = scores(j)
        for mi in maps:
            dst[mi] = s[mi]

    def consume(s, j, table, const):
        vtb = vt_ref[:, pl.ds(pl.multiple_of(j * tile, tile), tile)]
        if table is not None:
            s = [s[mi] + table for mi in maps]
        m_prev = [m_sc[mi] for mi in maps]
        m_new = [jnp.maximum(m_prev[mi], jnp.max(s[mi], axis=0, keepdims=True) + const) for mi in maps]
        p = [jnp.exp2(s[mi] - (m_new[mi] - const)) for mi in maps]
        for mi in maps:
            alpha = jnp.exp2(m_prev[mi] - m_new[mi])
            l_sc[mi] = alpha * l_sc[mi] + jnp.sum(p[mi], axis=0, keepdims=True)
            acc_sc[mi] = alpha * acc_sc[mi] + _dot(vtb, p[mi].astype(BF16))
            m_sc[mi] = m_new[mi]

    held = lambda buf: [buf[mi] for mi in maps]
    cfar = cfar_ref[h]
    nfar = jnp.maximum(i - 1, 0)

    @pl.when(i >= 1)
    def _():
        scores_to(sa_sc, 0)

    def far_pair(t, c):
        j = 2 * t
        scores_to(sb_sc, j + 1)
        consume(held(sa_sc), j, None, cfar)

        @pl.when(j + 1 < nfar)
        def _():
            scores_to(sa_sc, j + 2)
            consume(held(sb_sc), j + 1, None, cfar)

        return c

    lax.fori_loop(0, (nfar + 1) // 2, far_pair, 0)

    for parity, (cur, oth) in enumerate(((sa_sc, sb_sc), (sb_sc, sa_sc))):
        @pl.when(jnp.logical_and(i >= 1, nfar % 2 == parity))
        def _():
            scores_to(oth, i)
            consume(held(cur), i - 1, bias_ref[0, 1], 0.0)
            consume(held(oth), i, bias_ref[0, 0], 0.0)

    @pl.when(i == 0)
    def _():
        consume(scores(0), 0, bias_ref[0, 0], 0.0)

    lam_rows = lam_ref[...]
    lam = (jnp.exp(jnp.sum(lam_rows[0:1] * lam_rows[1:2], axis=-1, keepdims=True))
           - jnp.exp(jnp.sum(lam_rows[2:3] * lam_rows[3:4], axis=-1, keepdims=True)) + lam_init)
    o = acc_sc[0] / l_sc[0] - lam * (acc_sc[1] / l_sc[1])
    o = o * lax.rsqrt(jnp.mean(o * o, axis=0, keepdims=True) + SUBNORM_EPS)
    o_ref[...] = (o.T * subg_ref[...] * (1.0 - lam_init)).astype(o_ref.dtype)


def _diff_attention(q, k, vt, rel_bias, lam_rows, sub_g, lam_init, batch, seq):
    t, width = q.shape
    hw = 2 * A_HEAD_DIM
    tile = ATTN_TILE
    nq = seq // tile
    tables = _bias_tables(rel_bias, tile) * LOG2E
    cfar = rel_bias[REL_BUCKETS - 1].astype(F32) * LOG2E
    return pl.pallas_call(
        functools.partial(_attn_kernel, lam_init=lam_init),
        grid=(batch, A_HEADS, nq),
        in_specs=[
            pl.BlockSpec(memory_space=pltpu.SMEM),
            pl.BlockSpec((tile, hw), lambda b, h, i: (b * nq + i, h)),
            pl.BlockSpec((seq, hw), lambda b, h, i: (b, h)),
            pl.BlockSpec((hw, seq), lambda b, h, i: (h, b)),
            pl.BlockSpec((1, 2, tile, tile), lambda b, h, i: (h, 0, 0, 0)),
            pl.BlockSpec(lam_rows.shape, lambda b, h, i: (0, 0)),
            pl.BlockSpec((1, hw), lambda b, h, i: (0, 0)),
        ],
        out_specs=pl.BlockSpec((tile, hw), lambda b, h, i: (b * nq + i, h)),
        out_shape=jax.ShapeDtypeStruct((t, width), BF16),
        scratch_shapes=[pltpu.VMEM((2, 1, tile), F32), pltpu.VMEM((2, 1, tile), F32),
                        pltpu.VMEM((2, hw, tile), F32),
                        pltpu.VMEM((2, tile, tile), F32), pltpu.VMEM((2, tile, tile), F32)],
        compiler_params=_cparams("arbitrary", "arbitrary", "arbitrary"),
        name="diff_attn",
    )(cfar, q, k, vt, tables, lam_rows, sub_g.reshape(1, hw))


def _lru_kernel(xb_ref, gb_ref, cw_ref, vec_ref, wa_ref, wx_ref, y_ref, prev_sc, h_sc, a_sc, b_sc, hs_sc):
    ts = xb_ref.shape[0]

    @pl.when(pl.program_id(1) == 0)
    def _():
        prev_sc[...] = jnp.zeros(prev_sc.shape, F32)
        h_sc[...] = jnp.zeros(h_sc.shape, F32)

    xb = xb_ref[...]
    hist = prev_sc.shape[0]
    ext = jnp.concatenate([prev_sc[...], xb], axis=0)
    u = vec_ref[0:1, :]
    for j in range(CONV_W):
        off = hist - (CONV_W - 1) + j
        u = u + cw_ref[j:j + 1, :] * ext[off:off + ts, :]
    prev_sc[...] = xb[ts - hist:, :]
    ub = u.astype(BF16)
    r = jax.nn.sigmoid(_dot(ub, wa_ref[...]) + vec_ref[1:2, :])
    ig = jax.nn.sigmoid(_dot(ub, wx_ref[...]) + vec_ref[2:3, :])
    log_a = (-LRU_C * r) * _softplus(-vec_ref[3:4, :])
    a = jnp.exp(log_a)
    a_sc[...] = a
    b_sc[...] = jnp.sqrt(_neg_expm1(2.0 * log_a, a * a)) * (ig * u)

    def step(tt, h):
        h = a_sc[pl.ds(tt, 1), :] * h + b_sc[pl.ds(tt, 1), :]
        hs_sc[pl.ds(tt, 1), :] = h
        return h

    h_sc[...] = lax.fori_loop(0, ts, step, h_sc[...], unroll=16)
    y_ref[...] = (hs_sc[...] * _gelu_tanh(gb_ref[...])).astype(y_ref.dtype)


def _block_diag(w):
    n, c, d = w.shape
    eye = jnp.eye(n, dtype=w.dtype)
    return (w[:, :, None, :] * eye[:, None, :, None]).reshape(n * c, n * d)


def _rglru(xb, gb, conv_w, conv_b, wa, ba, wx, bx, lru_lambda, batch, seq):
    t, w = xb.shape
    ts = LRU_TILE
    nt = seq // ts
    vec = jnp.zeros((8, w), F32).at[0].set(conv_b).at[1].set(ba).at[2].set(bx).at[3].set(lru_lambda)
    row = lambda b, s: (b * nt + s, 0)
    fixed = lambda b, s: (0, 0)
    return pl.pallas_call(
        _lru_kernel,
        grid=(batch, nt),
        in_specs=[pl.BlockSpec((ts, w), row), pl.BlockSpec((ts, w), row),
                  pl.BlockSpec((CONV_W, w), fixed), pl.BlockSpec((8, w), fixed),
                  pl.BlockSpec((w, w), fixed), pl.BlockSpec((w, w), fixed)],
        out_specs=pl.BlockSpec((ts, w), row),
        out_shape=jax.ShapeDtypeStruct((t, w), BF16),
        scratch_shapes=[pltpu.VMEM((8, w), F32), pltpu.VMEM((1, w), F32),
                        pltpu.VMEM((ts, w), F32), pltpu.VMEM((ts, w), F32), pltpu.VMEM((ts, w), F32)],
        compiler_params=_cparams("arbitrary", "arbitrary"),
        name="rglru",
    )(xb, gb, conv_w, vec, _block_diag(wa).astype(BF16), _block_diag(wx).astype(BF16))


def _route(x, gf_ref, wrh_ref, wrl_ref, br_ref):
    hn = _rms(x, gf_ref[...], NORM_EPS)
    hh, hl = _split2(hn)
    wh = wrh_ref[...]
    lg = _dot(hh, wh) + _dot(hh, wrl_ref[...]) + _dot(hl, wh) + br_ref[...]
    lane = lax.broadcasted_iota(I32, lg.shape, 1)
    big = jnp.int32(1 << 20)
    is_g = lane < MOE_GROUPS
    gl = jnp.where(is_g, lg, NEG_BIG)
    gmax = jnp.max(gl, axis=-1, keepdims=True)
    gidx = jnp.min(jnp.where(gl == gmax, lane, big), axis=-1, keepdims=True)
    gsum = jnp.sum(jnp.where(is_g, jnp.exp(gl - gmax), 0.0), axis=-1, keepdims=True)
    gprob = 1.0 / gsum
    lo = MOE_GROUPS + gidx * EXPERTS_PER_GROUP
    fmask = (lane >= lo) & (lane < lo + EXPERTS_PER_GROUP)
    fl = jnp.where(fmask, lg, NEG_BIG)
    f1 = jnp.max(fl, axis=-1, keepdims=True)
    i1 = jnp.min(jnp.where(fl == f1, lane, big), axis=-1, keepdims=True)
    fl2 = jnp.where(lane == i1, NEG_BIG, fl)
    f2 = jnp.max(fl2, axis=-1, keepdims=True)
    i2 = jnp.min(jnp.where(fl2 == f2, lane, big), axis=-1, keepdims=True)
    e2 = jnp.exp(f2 - f1)
    den = 1.0 + e2
    g1 = gprob / den
    g2 = gprob * e2 / den
    id1 = (i1 - MOE_GROUPS).astype(F32)
    id2 = (i2 - MOE_GROUPS).astype(F32)
    return jnp.where(lane == 0, g1, jnp.where(lane == 1, g2,
                     jnp.where(lane == 2, id1, jnp.where(lane == 3, id2, 0.0))))


def _router_params(wc, bc, wf, bf):
    d = wc.shape[0]
    n = MOE_GROUPS + N_EXPERTS
    w = jnp.zeros((d, LANES), F32).at[:, :MOE_GROUPS].set(wc).at[:, MOE_GROUPS:n].set(wf)
    b = jnp.zeros((1, LANES), F32).at[0, :MOE_GROUPS].set(bc).at[0, MOE_GROUPS:n].set(bf)
    hi, lo = _split2(w)
    return hi, lo, b


def _outproj_kernel(x_ref, ya_ref, yb_ref, wo_ref, gf_ref, wrh_ref, wrl_ref, br_ref,
                    x1_ref, x1t_ref, rt_ref):
    half = ya_ref.shape[1]
    x1 = x_ref[...] + _dot(ya_ref[...], wo_ref[:half, :]) + _dot(yb_ref[...], wo_ref[half:, :])
    x1_ref[...] = x1
    _store_token_tiles(x1t_ref, x1)
    rt_ref[...] = _route(x1, gf_ref, wrh_ref, wrl_ref, br_ref)


def _outproj(x2, ya, yb, w_out, gf, router):
    t, d = x2.shape
    half = ya.shape[1]
    tm = ROW_TILE
    row = lambda i: (i, 0)
    fixed = lambda i: (0, 0)
    wrh, wrl, br = router
    return pl.pallas_call(
        _outproj_kernel,
        grid=(t // tm,),
        in_specs=[pl.BlockSpec((tm, d), row), pl.BlockSpec((tm, half), row), pl.BlockSpec((tm, half), row),
                  pl.BlockSpec((2 * half, d), fixed), pl.BlockSpec((1, d), fixed),
                  pl.BlockSpec((d, LANES), fixed), pl.BlockSpec((d, LANES), fixed),
                  pl.BlockSpec((1, LANES), fixed)],
        out_specs=[pl.BlockSpec((tm, d), row), pl.BlockSpec((tm * SUBLANES, LANES), row),
                   pl.BlockSpec((tm, LANES), row)],
        out_shape=[jax.ShapeDtypeStruct((t, d), F32), jax.ShapeDtypeStruct((t * SUBLANES, LANES), F32),
                   jax.ShapeDtypeStruct((t, LANES), F32)],
        compiler_params=_cparams("arbitrary"),
        name="outproj_route",
    )(x2, ya, yb, w_out, gf.reshape(1, d), wrh, wrl, br)


def _moe_plan(route, n_tok):
    m = n_tok * MOE_TOPK
    n_blk = -(-(m + N_EXPERTS * (MOE_BLOCK - 1)) // MOE_BLOCK)
    eid = route[:, 2:4].astype(I32).reshape(-1)
    onehot = (eid[:, None] == jnp.arange(N_EXPERTS, dtype=I32)[None, :]).astype(I32)
    csum = jnp.cumsum(onehot, axis=0)
    rank = jnp.sum(csum * onehot, axis=1) - 1
    sizes = csum[-1]
    padded = (sizes + MOE_BLOCK - 1) // MOE_BLOCK * MOE_BLOCK
    pend = jnp.cumsum(padded)
    dest = (pend - padded)[eid] + rank
    tok = jnp.repeat(jnp.arange(n_tok, dtype=I32), MOE_TOPK)
    rows_tok = jnp.zeros((n_blk * MOE_BLOCK,), I32).at[dest].set(tok)
    blk_e = jnp.minimum(jnp.searchsorted(pend, jnp.arange(n_blk, dtype=I32) * MOE_BLOCK, side='right'),
                        N_EXPERTS - 1).astype(I32)
    n_used = (pend[-1] // MOE_BLOCK).astype(I32).reshape(1)
    return rows_tok.reshape(n_blk, MOE_BLOCK), blk_e, n_used, dest.astype(I32)


def _gather_ahead(idx_hbm, idx_sms, idx_sem, issue_rows, n_active):
    i = pl.program_id(0)
    n = pl.num_programs(0)

    def idx_copy(step, slot):
        return pltpu.make_async_copy(idx_hbm.at[step], idx_sms[slot], idx_sem.at[slot])

    @pl.when(i == 0)
    def _():
        first = idx_copy(0, 0)
        first.start()
        first.wait()

        @pl.when(0 < n_active)
        def _():
            issue_rows(0)

        @pl.when(1 < n)
        def _():
            idx_copy(1, 1).start()

    for slot in range(2):
        @pl.when(jnp.logical_and(i + 1 < n, (i + 1) % 2 == slot))
        def _():
            idx_copy(i + 1, slot).wait()

            @pl.when(i + 1 < n_active)
            def _():
                issue_rows(slot)

        @pl.when(jnp.logical_and(i + 2 < n, i % 2 == slot))
        def _():
            idx_copy(i + 2, slot).start()


def _expert_kernel(blk_e_ref, nused_ref, rows_hbm, xt_hbm, gf_ref, wgu_ref, wd_ref, yst_ref,
                   idx_sm0, idx_sm1, idx_sem, xbuf, row_sem):
    i = pl.program_id(0)
    slot = i % 2
    blk = xbuf.shape[1] // SUBLANES
    idx_sms = (idx_sm0, idx_sm1)

    def issue_rows(sl):
        def issue(r2, c):
            for prio in range(DMA_PRIORITIES):
                r = DMA_PRIORITIES * r2 + prio
                src = pl.multiple_of(idx_sms[sl][r], SUBLANES)
                dst = pl.multiple_of(r * SUBLANES, SUBLANES)
                pltpu.make_async_copy(xt_hbm.at[pl.ds(src, SUBLANES), :],
                                      xbuf.at[sl, pl.ds(dst, SUBLANES), :],
                                      row_sem.at[sl]).start(priority=prio)
            return c

        lax.fori_loop(0, blk // DMA_PRIORITIES, issue, 0, unroll=4)

    _gather_ahead(rows_hbm, idx_sms, idx_sem, issue_rows, nused_ref[0])

    @pl.when(i < nused_ref[0])
    def _():
        pltpu.make_async_copy(xt_hbm.at[pl.ds(0, blk * SUBLANES), :], xbuf.at[slot], row_sem.at[slot]).wait()
        x = _load_token_tiles(xbuf.at[slot], blk)
        hn = _rms(x, gf_ref[...], NORM_EPS).astype(BF16)
        gu = _dot(hn, wgu_ref[0])
        eh = gu.shape[1] // 2
        g = gu[:, :eh]
        act = (g * jax.nn.sigmoid(g) * gu[:, eh:]).astype(BF16)
        _store_token_tiles(yst_ref, _dot(act, wd_ref[0]))

    @pl.when(i >= nused_ref[0])
    def _():
        yst_ref[...] = jnp.zeros(yst_ref.shape, F32)


def _experts(x1t, rows_tok, blk_e, n_used, gf, w_gu, w_down):
    d = gf.shape[0]
    n_blk, blk = rows_tok.shape
    eh2 = w_gu.shape[2]
    grid_spec = pltpu.PrefetchScalarGridSpec(
        num_scalar_prefetch=2,
        grid=(n_blk,),
        in_specs=[
            pl.BlockSpec(memory_space=pl.ANY),
            pl.BlockSpec(memory_space=pl.ANY),
            pl.BlockSpec((1, d), lambda i, e, n: (0, 0)),
            pl.BlockSpec((1, d, eh2), lambda i, e, n: (e[i], 0, 0)),
            pl.BlockSpec((1, eh2 // 2, d), lambda i, e, n: (e[i], 0, 0)),
        ],
        out_specs=pl.BlockSpec((blk * SUBLANES, LANES), lambda i, e, n: (i, 0)),
        scratch_shapes=[pltpu.SMEM((blk,), I32), pltpu.SMEM((blk,), I32), pltpu.SemaphoreType.DMA((2,)),
                        pltpu.VMEM((2, blk * SUBLANES, LANES), F32), pltpu.SemaphoreType.DMA((2,))],
    )
    return pl.pallas_call(
        _expert_kernel,
        grid_spec=grid_spec,
        out_shape=jax.ShapeDtypeStruct((n_blk * blk * SUBLANES, LANES), F32),
        compiler_params=_cparams("arbitrary"),
        name="moe_experts",
    )(blk_e, n_used, rows_tok * SUBLANES, x1t, gf.reshape(1, d), w_gu, w_down)


def _combine_kernel(pos_hbm, yst_hbm, x1_ref, rt_ref, p_ref, gn_ref, wg_ref, wp_ref, fn_ref, o_ref,
                    idx_sm0, idx_sm1, idx_sem, ybuf, row_sem, *, final):
    slot = pl.program_id(0) % 2
    tm = x1_ref.shape[0]
    idx_sms = (idx_sm0, idx_sm1)

    def issue_rows(sl):
        def issue(r, c):
            dst = pl.multiple_of(r * SUBLANES, SUBLANES)
            for s in range(MOE_TOPK):
                src = pl.multiple_of(idx_sms[sl][MOE_TOPK * r + s], SUBLANES)
                pltpu.make_async_copy(yst_hbm.at[pl.ds(src, SUBLANES), :],
                                      ybuf.at[sl, s, pl.ds(dst, SUBLANES), :],
                                      row_sem.at[sl]).start(priority=s % DMA_PRIORITIES)
            return c

        lax.fori_loop(0, tm, issue, 0, unroll=8)

    _gather_ahead(pos_hbm, idx_sms, idx_sem, issue_rows, pl.num_programs(0))
    for s in range(MOE_TOPK):
        pltpu.make_async_copy(yst_hbm.at[pl.ds(0, tm * SUBLANES), :], ybuf.at[slot, s], row_sem.at[slot]).wait()

    rt = rt_ref[...]
    y0 = _load_token_tiles(ybuf.at[slot, 0], tm)
    y1 = _load_token_tiles(ybuf.at[slot, 1], tm)
    x2 = x1_ref[...] + (rt[:, 0:1] * y0 + rt[:, 1:2] * y1)
    gate = jax.nn.sigmoid(_dot(_rms(x2, gn_ref[...], NORM_EPS).astype(BF16), wg_ref[...]))
    x3 = x2 + _dot(p_ref[...].astype(BF16), wp_ref[...]) * gate
    if final:
        x3 = _rms(x3, fn_ref[...], NORM_EPS)
    o_ref[...] = x3


def _combine(x1, route, yst, dest, p_i, ple_norm, ple_gate, ple_proj, final_norm, final):
    t, d = x1.shape
    tm = COMBINE_TILE
    pd = p_i.shape[1]
    pos = (dest * SUBLANES).reshape(t // tm, MOE_TOPK * tm)
    row = lambda i: (i, 0)
    fixed = lambda i: (0, 0)
    return pl.pallas_call(
        functools.partial(_combine_kernel, final=final),
        grid=(t // tm,),
        in_specs=[pl.BlockSpec(memory_space=pl.ANY), pl.BlockSpec(memory_space=pl.ANY),
                  pl.BlockSpec((tm, d), row), pl.BlockSpec((tm, LANES), row), pl.BlockSpec((tm, pd), row),
                  pl.BlockSpec((1, d), fixed), pl.BlockSpec((d, d), fixed), pl.BlockSpec((pd, d), fixed),
                  pl.BlockSpec((1, d), fixed)],
        out_specs=pl.BlockSpec((tm, d), row),
        out_shape=jax.ShapeDtypeStruct((t, d), F32),
        scratch_shapes=[pltpu.SMEM((MOE_TOPK * tm,), I32), pltpu.SMEM((MOE_TOPK * tm,), I32),
                        pltpu.SemaphoreType.DMA((2,)),
                        pltpu.VMEM((2, MOE_TOPK, tm * SUBLANES, LANES), F32), pltpu.SemaphoreType.DMA((2,))],
        compiler_params=_cparams("arbitrary"),
        name="moe_combine_ple",
    )(pos, yst, x1, route, p_i, ple_norm.reshape(1, d), ple_gate, ple_proj, final_norm.reshape(1, d))


def _seg_sum(x, eseg_ref, eexp_ref):
    s = _dot(x.astype(BF16), eseg_ref[...])
    return _dot(s.astype(BF16), eexp_ref[...])


def _rwkv_pre_kernel(x_ref, xp_ref, gm_ref, mu_ref, vec_ref, wr_ref, wk_ref, wv_ref, w1_ref, w2_ref,
                     a1_ref, a2_ref, g1_ref, g2_ref, tri_ref, eseg_ref, eexp_ref,
                     la_ref, lr_ref, rb_ref, rk_ref, sb_ref, sk_ref, v_ref, bon_ref, g_ref, gc_ref,
                     c_sc):
    tl = x_ref.shape[0]
    gm = gm_ref[...]
    h = _rms(x_ref[...], gm, NORM_EPS)
    hp = _rms(xp_ref[...], gm, NORM_EPS)[7:8, :]
    hp = jnp.where(pl.program_id(1) == 0, 0.0, hp)
    row = lax.broadcasted_iota(I32, h.shape, 0)
    hs = jnp.where(row == 0, hp, pltpu.roll(h, 1, axis=0))
    xx = hs - h
    mix = lambda n: (h + xx * mu_ref[n:n + 1, :]).astype(BF16)
    vec = lambda n: vec_ref[n:n + 1, :]
    r = _dot(mix(0), wr_ref[...])
    wl = vec(0) + _dot(jnp.tanh(_dot(mix(1), w1_ref[...])).astype(BF16), w2_ref[...])
    lw = -jnp.exp(-_softplus(-wl) - 0.5)
    k = _dot(mix(2), wk_ref[...])
    v = _dot(mix(3), wv_ref[...])
    a = jax.nn.sigmoid(vec(1) + _dot(_dot(mix(4), a1_ref[...]).astype(BF16), a2_ref[...]))
    g = _dot(jax.nn.sigmoid(_dot(mix(5), g1_ref[...])).astype(BF16), g2_ref[...])
    kk = k * vec(2)
    kk = kk / jnp.maximum(jnp.sqrt(_seg_sum(kk * kk, eseg_ref, eexp_ref)), 1e-12)
    k2 = k * (1.0 + (a - 1.0) * vec(3))
    bon_ref[...] = (_seg_sum(r * k2 * vec(4), eseg_ref, eexp_ref) * v).astype(bon_ref.dtype)
    v_ref[...] = v.astype(v_ref.dtype)
    g_ref[...] = g.astype(g_ref.dtype)

    half = tri_ref.shape[1]
    tri = tri_ref[...]
    for s in range(tl // half):
        p1, p2 = _split2(lw[s * half:(s + 1) * half, :])
        c_sc[s * half:(s + 1) * half, :] = _dot(tri, p1) + _dot(tri, p2)
    c_in = c_sc[...]
    nc = tl // RWKV_CHUNK
    ends = [c_sc[(n + 1) * RWKV_CHUNK - 1:(n + 1) * RWKV_CHUNK, :] for n in range(nc)]
    c_end = jnp.concatenate([ends[n] - c_sc[n * RWKV_CHUNK:(n + 1) * RWKV_CHUNK, :] for n in range(nc)],
                            axis=0)
    gc_ref[...] = jnp.exp(jnp.concatenate(ends, axis=0))
    e_in = jnp.exp(c_in)
    e_neg = jnp.exp(-c_in)
    e_end = jnp.exp(c_end)
    b = kk * a
    la_ref[...] = (-kk * jnp.exp(c_in - lw)).astype(la_ref.dtype)
    lr_ref[...] = (r * e_in).astype(lr_ref.dtype)
    rb_ref[...] = (b * e_neg).astype(rb_ref.dtype)
    rk_ref[...] = (k2 * e_neg).astype(rk_ref.dtype)
    sb_ref[...] = (b * e_end).astype(sb_ref.dtype)
    sk_ref[...] = (k2 * e_end).astype(sk_ref.dtype)


def _head_indicator(d):
    heads = d // RWKV_HEAD
    e = (jnp.arange(d)[:, None] // RWKV_HEAD == jnp.arange(LANES)[None, :]).astype(BF16)
    del heads
    return e, e.T


def _cumsum_matrix(half):
    i = jnp.arange(half)[:, None]
    j = jnp.arange(half)[None, :]
    return (((i // RWKV_CHUNK) == (j // RWKV_CHUNK)) & (j <= i)).astype(BF16)


def _pad_cols(w, n):
    return jnp.pad(w, ((0, 0), (0, n - w.shape[1])))


def _pad_rows(w, n):
    return jnp.pad(w, ((0, n - w.shape[0]), (0, 0)))


def _rwkv_pre(x3, gm, mu, vecs, wr, wk, wv, w1, w2, a1, a2, g1, g2, batch, seq):
    t, d = x3.shape
    tl = RWKV_TILE
    nt = seq // tl
    half = 256
    lora = lambda w_in, w_out: (_pad_cols(w_in, -(-w_in.shape[1] // LANES) * LANES).astype(BF16),
                                _pad_rows(w_out, -(-w_out.shape[0] // LANES) * LANES).astype(BF16))
    w1p, w2p = lora(w1, w2)
    a1p, a2p = lora(a1, a2)
    g1p, g2p = lora(g1, g2)
    eseg, eexp = _head_indicator(d)
    tri = _cumsum_matrix(half)
    row = lambda b, s: (b * nt + s, 0)
    fixed = lambda b, s: (0, 0)
    full = lambda a: pl.BlockSpec(a.shape, fixed)
    prev = lambda b, s: (jnp.maximum((b * nt + s) * (tl // 8) - 1, 0), 0)
    weights = (wr.astype(BF16), wk.astype(BF16), wv.astype(BF16), w1p, w2p, a1p, a2p, g1p, g2p, tri, eseg, eexp)
    outs = pl.pallas_call(
        _rwkv_pre_kernel,
        grid=(batch, nt),
        in_specs=[pl.BlockSpec((tl, d), row), pl.BlockSpec((8, d), prev), pl.BlockSpec((1, d), fixed),
                  full(mu), full(vecs)] + [full(w) for w in weights],
        out_specs=[pl.BlockSpec((tl, d), row)] * 9 + [pl.BlockSpec((tl // RWKV_CHUNK, d), row)],
        out_shape=[jax.ShapeDtypeStruct((t, d), BF16)] * 9 + [jax.ShapeDtypeStruct((t // RWKV_CHUNK, d), F32)],
        scratch_shapes=[pltpu.VMEM((tl, d), F32)],
        compiler_params=_cparams("arbitrary", "arbitrary"),
        name="rwkv_pre",
    )(x3, x3, gm.reshape(1, d), mu, vecs, *weights)
    return outs


def _rwkv_scan_kernel(la_ref, lr_ref, rb_ref, rk_ref, sb_ref, sk_ref, v_ref, gc_ref, bon_ref, g_ref, x_ref,
                      vec_ref, wo_ref, eseg_ref, eexp_ref, gf_ref, wrh_ref, wrl_ref, br_ref,
                      x4_ref, x4t_ref, rt_ref, s_sc, y_sc):
    tl, d = x_ref.shape
    ch = RWKV_CHUNK
    pw = 2 * RWKV_HEAD
    npair = d // pw

    @pl.when(pl.program_id(1) == 0)
    def _():
        s_sc[...] = jnp.zeros(s_sc.shape, F32)

    ri = lax.broadcasted_iota(I32, (pw, pw), 0)
    ci = lax.broadcasted_iota(I32, (pw, pw), 1)
    same_half = (ri < RWKV_HEAD) == (ci < RWKV_HEAD)
    eye = (ri == ci).astype(F32)
    rt_ = lax.broadcasted_iota(I32, (ch, pw), 0)
    ct_ = lax.broadcasted_iota(I32, (ch, pw), 1)
    first = ct_ < RWKV_HEAD
    strict = rt_ > (ct_ & (RWKV_HEAD - 1))
    incl = rt_ >= (ct_ & (RWKV_HEAD - 1))
    m_s0, m_s1 = strict & first, strict & ~first
    m_i0, m_i1 = incl & first, incl & ~first
    first_full = ci < RWKV_HEAD
    zf = jnp.zeros((ch, pw), F32)
    gc_row = lax.broadcasted_iota(I32, (tl // ch, pw), 0)

    def chunk(c, carry):
        rows = pl.ds(pl.multiple_of(c * ch, ch), ch)
        pairs = range(npair)
        cols = [slice(p * pw, (p + 1) * pw) for p in pairs]
        cat0 = lambda a, b: jnp.concatenate([a, b], axis=0)
        lcat = [cat0(la_ref[rows, cl], lr_ref[rows, cl]) for cl in cols]
        rb = [rb_ref[rows, cl] for cl in cols]
        rk = [rk_ref[rows, cl] for cl in cols]
        vv = [v_ref[rows, cl] for cl in cols]
        zb = jnp.zeros_like(lcat[0])
        a0 = [_dot_nt(jnp.where(first_full, lcat[p], zb), cat0(rb[p], rk[p])) for p in pairs]
        a1 = [_dot_nt(jnp.where(first_full, zb, lcat[p]), cat0(rk[p], rb[p])) for p in pairs]
        pm = [cat0(jnp.where(m_s0, a0[p][:ch], zf), jnp.where(m_s1, a1[p][:ch], zf)) for p in pairs]
        aak = [cat0(jnp.where(m_s1, a0[p][:ch], zf), jnp.where(m_s0, a1[p][:ch], zf)).astype(BF16)
               for p in pairs]
        arbk = [jnp.concatenate([cat0(jnp.where(m_i0, a0[p][ch:], zf), jnp.where(m_i1, a1[p][ch:], zf)),
                                 cat0(jnp.where(m_i1, a0[p][ch:], zf), jnp.where(m_i0, a1[p][ch:], zf))],
                                axis=1).astype(BF16) for p in pairs]
        tm = [eye + pm[p] for p in pairs]
        qm = pm
        for _ in range(5):
            qb = [q.astype(BF16) for q in qm]
            qm = [_dot(qb[p], qb[p]) for p in pairs]
            tm = [tm[p] + _dot(tm[p].astype(BF16), qm[p].astype(BF16)) for p in pairs]
        st = [s_sc[p] for p in pairs]
        w12 = [_dot_nt(lcat[p], st[p].astype(BF16)) for p in pairs]
        v2 = [cat0(vv[p], vv[p]) for p in pairs]
        rhs = [cat0(w12[p][:ch], w12[p][:ch]) + _dot(aak[p], v2[p]) for p in pairs]
        ust = [_dot(tm[p].astype(BF16), rhs[p].astype(BF16)) for p in pairs]
        yst = [cat0(w12[p][ch:], w12[p][ch:]) + _dot(arbk[p], cat0(ust[p].astype(BF16), v2[p]))
               for p in pairs]
        for p in pairs:
            y_sc[rows, cols[p]] = jnp.where(first, yst[p][:ch], yst[p][ch:])
        u = [jnp.where(first, ust[p][:ch], ust[p][ch:]).astype(BF16) for p in pairs]
        ds_ = [_dot_tn(cat0(u[p], vv[p]), cat0(sb_ref[rows, cols[p]], sk_ref[rows, cols[p]])) for p in pairs]
        for p in pairs:
            gcr = jnp.sum(jnp.where(gc_row == c, gc_ref[:, cols[p]], 0.0), axis=0, keepdims=True)
            s_sc[p] = jnp.where(same_half, st[p] * gcr + ds_[p], 0.0)
        return carry

    lax.fori_loop(0, tl // ch, chunk, 0)

    y = y_sc[...]
    inv_n = 1.0 / RWKV_HEAD
    mean = _seg_sum(y, eseg_ref, eexp_ref) * inv_n
    dlt = y - mean
    var = _seg_sum(dlt * dlt, eseg_ref, eexp_ref) * inv_n
    yn = dlt * lax.rsqrt(var + RWKV_GN_EPS) * vec_ref[5:6, :] + vec_ref[6:7, :]
    z = (yn + bon_ref[...].astype(F32)) * g_ref[...].astype(F32)
    x4 = x_ref[...] + _dot(z.astype(BF16), wo_ref[...])
    x4_ref[...] = x4
    _store_token_tiles(x4t_ref, x4)
    rt_ref[...] = _route(x4, gf_ref, wrh_ref, wrl_ref, br_ref)


def _rwkv_scan(pre, x3, vecs, wo, gf, router, batch, seq):
    la, lr, rb, rk, sb, sk, v, bon, g, gc = pre
    t, d = x3.shape
    tl = RWKV_TILE
    nt = seq // tl
    eseg, eexp = _head_indicator(d)
    wrh, wrl, br = router
    row = lambda b, s: (b * nt + s, 0)
    fixed = lambda b, s: (0, 0)
    big = pl.BlockSpec((tl, d), row)
    full = lambda a: pl.BlockSpec(a.shape, fixed)
    pw = 2 * RWKV_HEAD
    return pl.pallas_call(
        _rwkv_scan_kernel,
        grid=(batch, nt),
        in_specs=[big] * 7 + [pl.BlockSpec((tl // RWKV_CHUNK, d), row), big, big, big,
                              full(vecs), full(wo), full(eseg), full(eexp),
                              pl.BlockSpec((1, d), fixed), full(wrh), full(wrl), full(br)],
        out_specs=[big, pl.BlockSpec((tl * SUBLANES, LANES), row), pl.BlockSpec((tl, LANES), row)],
        out_shape=[jax.ShapeDtypeStruct((t, d), F32), jax.ShapeDtypeStruct((t * SUBLANES, LANES), F32),
                   jax.ShapeDtypeStruct((t, LANES), F32)],
        scratch_shapes=[pltpu.VMEM((d // pw, pw, pw), F32), pltpu.VMEM((tl, d), F32)],
        compiler_params=_cparams("arbitrary", "arbitrary"),
        name="rwkv_scan_out",
    )(la, lr, rb, rk, sb, sk, v, gc, bon, g, x3, vecs, wo, eseg, eexp, gf.reshape(1, d), wrh, wrl, br)


def _moe_and_embed(x1, x1t, route, i, p, norm_ffn, moe_w_gu, moe_w_down, ple_norm, ple_gate, ple_proj,
                   final_norm, final):
    t = x1.shape[0]
    rows_tok, blk_e, n_used, dest = _moe_plan(route, t)
    yst = _experts(x1t, rows_tok, blk_e, n_used, norm_ffn[i], moe_w_gu[i].astype(BF16),
                   moe_w_down[i].astype(BF16))
    p_i = p[i].reshape(t, p.shape[-1])
    return _combine(x1, route, yst, dest, p_i, ple_norm[i], ple_gate[i].astype(BF16),
                    ple_proj[i].astype(BF16), final_norm, final)


def kernel(x, p, rel_bias, norm_mix, ab_w_in, ab_w_out, lam_q1, lam_k1, lam_q2, lam_k2, sub_g, conv_w, conv_b, lru_wa, lru_ba, lru_wx, lru_bx, lru_lambda, rwkv_mu, rwkv_wr, rwkv_wk, rwkv_wv, rwkv_wo, rwkv_w0, rwkv_w1, rwkv_w2, rwkv_a0, rwkv_a1, rwkv_a2, rwkv_g1, rwkv_g2, rwkv_kk, rwkv_ka, rwkv_rk, rwkv_ln_g, rwkv_ln_b, norm_ffn, moe_wc, moe_bc, moe_wf, moe_bf, moe_w_gu, moe_w_down, ple_norm, ple_gate, ple_proj, final_norm):
    batch, seq, d = x.shape
    t = batch * seq
    depth = norm_mix.shape[0]
    assert depth == 2 and seq % RWKV_TILE == 0 and seq % LRU_TILE == 0 and t % ROW_TILE == 0
    assert d == SUBLANES * LANES and seq % ATTN_TILE == 0 and t % COMBINE_TILE == 0
    xs = x.reshape(t, d)
    for i in range(depth):
        j = i // 2
        router = _router_params(moe_wc[i], moe_bc[i], moe_wf[i], moe_bf[i])
        if i % 2 == 0:
            lam_init = 0.8 - 0.6 * math.exp(-0.3 * i)
            q, k, vt, xb, gb = _inproj(xs, norm_mix[i], ab_w_in[j].astype(BF16))
            lam_rows = jnp.stack([lam_q1[j], lam_k1[j], lam_q2[j], lam_k2[j]]).astype(F32)
            ya = _diff_attention(q, k, vt, rel_bias, lam_rows, sub_g[j], lam_init, batch, seq)
            yb = _rglru(xb, gb, conv_w[j], conv_b[j], lru_wa[j], lru_ba[j], lru_wx[j], lru_bx[j],
                        lru_lambda[j], batch, seq)
            x1, x1t, route = _outproj(xs, ya, yb, ab_w_out[j].astype(BF16), norm_ffn[i], router)
        else:
            vecs = jnp.stack([rwkv_w0[j], rwkv_a0[j], rwkv_kk[j], rwkv_ka[j], rwkv_rk[j].reshape(-1),
                              rwkv_ln_g[j], rwkv_ln_b[j], jnp.zeros((d,), F32)]).astype(F32)
            pre = _rwkv_pre(xs, norm_mix[i], rwkv_mu[j], vecs, rwkv_wr[j], rwkv_wk[j], rwkv_wv[j],
                            rwkv_w1[j], rwkv_w2[j], rwkv_a1[j], rwkv_a2[j], rwkv_g1[j], rwkv_g2[j],
                            batch, seq)
            x1, x1t, route = _rwkv_scan(pre, xs, vecs, rwkv_wo[j].astype(BF16), norm_ffn[i], router,
                                        batch, seq)
        xs = _moe_and_embed(x1, x1t, route, i, p, norm_ffn, moe_w_gu, moe_w_down, ple_norm, ple_gate,
                            ple_proj, final_norm, final=(i == depth - 1))
    return xs.reshape(batch, seq, d)
```

```python
import functools
import math

import jax
import jax.numpy as jnp
from jax import lax
from jax.experimental import pallas as pl
from jax.experimental.pallas import tpu as pltpu

F32, BF16, I32 = jnp.float32, jnp.bfloat16, jnp.int32

A_HEADS = 4
A_HEAD_DIM = 64
A_SCALE = A_HEAD_DIM ** -0.5
LOG2E = math.log2(math.e)
REL_BUCKETS = 32
REL_MAX_DIST = 128
LRU_BLOCKS = 8
CONV_W = 4
LRU_C = 8.0
RWKV_HEAD = 64
RWKV_GN_EPS = 64e-5
MOE_GROUPS = 4
EXPERTS_PER_GROUP = 8
N_EXPERTS = MOE_GROUPS * EXPERTS_PER_GROUP
MOE_TOPK = 2
NORM_EPS = 1e-6
SUBNORM_EPS = 1e-5

LANES = 128
VMEM_LIMIT = 56 * 1024 * 1024
ROW_TILE = 512
ATTN_TILE = 512
LRU_TILE = 512
RWKV_CHUNK = 64
RWKV_TILE = 512
MOE_BLOCK = 512
COMBINE_TILE = 256
NEG_BIG = -1e30


def _cparams(*sem):
    return pltpu.CompilerParams(dimension_semantics=sem, vmem_limit_bytes=VMEM_LIMIT)


def _rms(x, g, eps):
    return x * lax.rsqrt(jnp.mean(x * x, axis=-1, keepdims=True) + eps) * g


def _dot(a, b):
    return jnp.dot(a, b, preferred_element_type=F32)


def _dot_nt(a, b):
    return lax.dot_general(a, b, (((1,), (1,)), ((), ())), preferred_element_type=F32)


def _dot_tn(a, b):
    return lax.dot_general(a, b, (((0,), (0,)), ((), ())), preferred_element_type=F32)


def _split2(x):
    hi = x.astype(BF16)
    lo = (x - hi.astype(F32)).astype(BF16)
    return hi, lo


def _softplus(x):
    return jnp.maximum(x, 0.0) + jnp.log1p(jnp.exp(-jnp.abs(x)))


def _neg_expm1(y, exp_y):
    poly = 1.0 + y * (1.0 / 5.0)
    for d in (4.0, 3.0, 2.0):
        poly = 1.0 + (y * (1.0 / d)) * poly
    return jnp.where(y > -1.0 / 16.0, -y * poly, 1.0 - exp_y)


SUBLANES = 8
DMA_PRIORITIES = 2


def _store_token_tiles(ref, x):
    rows = x.shape[0]
    for s in range(x.shape[1] // LANES):
        ref[pl.ds(s, rows, stride=SUBLANES), :] = x[:, s * LANES:(s + 1) * LANES]


def _load_token_tiles(ref, rows):
    return jnp.concatenate([ref[pl.ds(s, rows, stride=SUBLANES), :] for s in range(SUBLANES)], axis=1)


def _gelu_tanh(x):
    c = math.sqrt(2.0 / math.pi)
    return 0.5 * x * (1.0 + jnp.tanh(c * (x + 0.044715 * (x * x * x))))


def _inproj_kernel(x_ref, g_ref, w_ref, q_ref, k_ref, vt_ref, xb_ref, gb_ref):
    hn = _rms(x_ref[...], g_ref[...], NORM_EPS).astype(BF16)
    width = q_ref.shape[1]
    for c, o_ref in enumerate((q_ref, k_ref, vt_ref, xb_ref, gb_ref)):
        r = _dot(hn, w_ref[:, c * width:(c + 1) * width])
        if c == 0:
            r = r * (A_SCALE * LOG2E)
        if c == 2:
            r = r.T
        o_ref[...] = r.astype(o_ref.dtype)


def _inproj(x2, g, w_in):
    t, d = x2.shape
    width = w_in.shape[1] // 5
    tm = ROW_TILE
    row = lambda i: (i, 0)
    fixed = lambda i: (0, 0)
    rows = pl.BlockSpec((tm, width), row)
    return pl.pallas_call(
        _inproj_kernel,
        grid=(t // tm,),
        in_specs=[pl.BlockSpec((tm, d), row), pl.BlockSpec((1, d), fixed),
                  pl.BlockSpec(w_in.shape, fixed)],
        out_specs=[rows, rows, pl.BlockSpec((width, tm), lambda i: (0, i)), rows, rows],
        out_shape=[jax.ShapeDtypeStruct((t, width), BF16)] * 2 + [jax.ShapeDtypeStruct((width, t), BF16)]
        + [jax.ShapeDtypeStruct((t, width), F32)] * 2,
        compiler_params=_cparams("arbitrary"),
        name="inproj",
    )(x2, g.reshape(1, d), w_in)


def _t5_bucket(rel):
    n = jnp.maximum(rel, 0)
    max_exact = REL_BUCKETS // 2
    large = max_exact + (jnp.log(jnp.maximum(n, 1).astype(F32) / max_exact)
                         / math.log(REL_MAX_DIST / max_exact)
                         * (REL_BUCKETS - max_exact)).astype(I32)
    large = jnp.minimum(large, REL_BUCKETS - 1)
    return jnp.where(n < max_exact, n, large)


def _bias_tables(rel_bias, tile):
    span = 2 * tile

    def toeplitz(first_rel):
        rel = jnp.arange(span - 1) + first_rel
        vals = jnp.where((rel >= 0)[:, None], rel_bias[_t5_bucket(rel)].astype(F32), NEG_BIG)
        w = jnp.concatenate([vals, jnp.zeros((1, vals.shape[1]), F32)], axis=0).T
        skew = jnp.tile(w, (1, tile))[:, :tile * (span - 1)].reshape(-1, tile, span - 1)
        return skew[:, :, tile - 1:]

    return jnp.stack([toeplitz(1 - tile), toeplitz(1)], axis=1)


def _attn_kernel(cfar_ref, q_ref, k_ref, vt_ref, bias_ref, lam_ref, subg_ref, o_ref,
                 m_sc, l_sc, acc_sc, sa_sc, sb_sc, *, lam_init):
    h = pl.program_id(1)
    i = pl.program_id(2)
    tile = q_ref.shape[0]
    q = q_ref[...]
    lane = lax.broadcasted_iota(I32, q.shape, 1)
    zero = jnp.zeros_like(q)
    qm = (jnp.where(lane < A_HEAD_DIM, q, zero), jnp.where(lane >= A_HEAD_DIM, q, zero))
    m_sc[...] = jnp.full(m_sc.shape, NEG_BIG, F32)
    l_sc[...] = jnp.zeros(l_sc.shape, F32)
    acc_sc[...] = jnp.zeros(acc_sc.shape, F32)

    maps = range(2)

    def scores(j):
        kb = k_ref[pl.ds(pl.multiple_of(j * tile, tile), tile), :]
        return [_dot_nt(kb, qm[mi]) for mi in maps]

    def scores_to(dst, j):
        s = scores(j)
        for mi in maps:
            dst[mi] = s[mi]

    def consume(s, j, table, const):
        vtb = vt_ref[:, pl.ds(pl.multiple_of(j * tile, tile), tile)]
        if table is not None:
            s = [s[mi] + table for mi in maps]
        m_prev = [m_sc[mi] for mi in maps]
        m_new = [jnp.maximum(m_prev[mi], jnp.max(s[mi], axis=0, keepdims=True) + const) for mi in maps]
        p = [jnp.exp2(s[mi] - (m_new[mi] - const)) for mi in maps]
        for mi in maps:
            alpha = jnp.exp2(m_prev[mi] - m_new[mi])
            l_sc[mi] = alpha * l_sc[mi] + jnp.sum(p[mi], axis=0, keepdims=True)
            acc_sc[mi] = alpha * acc_sc[mi] + _dot(vtb, p[mi].astype(BF16))
            m_sc[mi] = m_new[mi]

    held = lambda buf: [buf[mi] for mi in maps]
    cfar = cfar_ref[h]
    nfar = jnp.maximum(i - 1, 0)

    @pl.when(i >= 1)
    def _():
        scores_to(sa_sc, 0)

    def far_pair(t, c):
        j = 2 * t
        scores_to(sb_sc, j + 1)
        consume(held(sa_sc), j, None, cfar)

        @pl.when(j + 1 < nfar)
        def _():
            scores_to(sa_sc, j + 2)
            consume(held(sb_sc), j + 1, None, cfar)

        return c

    lax.fori_loop(0, (nfar + 1) // 2, far_pair, 0)

    for parity, (cur, oth) in enumerate(((sa_sc, sb_sc), (sb_sc, sa_sc))):
        @pl.when(jnp.logical_and(i >= 1, nfar % 2 == parity))
        def _():
            scores_to(oth, i)
            consume(held(cur), i - 1, bias_ref[0, 1], 0.0)
            consume(held(oth), i, bias_ref[0, 0], 0.0)

    @pl.when(i == 0)
    def _():
        consume(scores(0), 0, bias_ref[0, 0], 0.0)

    lam_rows = lam_ref[...]
    lam = (jnp.exp(jnp.sum(lam_rows[0:1] * lam_rows[1:2], axis=-1, keepdims=True))
           - jnp.exp(jnp.sum(lam_rows[2:3] * lam_rows[3:4], axis=-1, keepdims=True)) + lam_init)
    o = acc_sc[0] / l_sc[0] - lam * (acc_sc[1] / l_sc[1])
    o = o * lax.rsqrt(jnp.mean(o * o, axis=0, keepdims=True) + SUBNORM_EPS)
    o_ref[...] = (o.T * subg_ref[...] * (1.0 - lam_init)).astype(o_ref.dtype)


def _diff_attention(q, k, vt, rel_bias, lam_rows, sub_g, lam_init, batch, seq):
    t, width = q.shape
    hw = 2 * A_HEAD_DIM
    tile = ATTN_TILE
    nq = seq // tile
    tables = _bias_tables(rel_bias, tile) * LOG2E
    cfar = rel_bias[REL_BUCKETS - 1].astype(F32) * LOG2E
    return pl.pallas_call(
        functools.partial(_attn_kernel, lam_init=lam_init),
        grid=(batch, A_HEADS, nq),
        in_specs=[
            pl.BlockSpec(memory_space=pltpu.SMEM),
            pl.BlockSpec((tile, hw), lambda b, h, i: (b * nq + i, h)),
            pl.BlockSpec((seq, hw), lambda b, h, i: (b, h)),
            pl.BlockSpec((hw, seq), lambda b, h, i: (h, b)),
            pl.BlockSpec((1, 2, tile, tile), lambda b, h, i: (h, 0, 0, 0)),
            pl.BlockSpec(lam_rows.shape, lambda b, h, i: (0, 0)),
            pl.BlockSpec((1, hw), lambda b, h, i: (0, 0)),
        ],
        out_specs=pl.BlockSpec((tile, hw), lambda b, h, i: (b * nq + i, h)),
        out_shape=jax.ShapeDtypeStruct((t, width), BF16),
        scratch_shapes=[pltpu.VMEM((2, 1, tile), F32), pltpu.VMEM((2, 1, tile), F32),
                        pltpu.VMEM((2, hw, tile), F32),
                        pltpu.VMEM((2, tile, tile), F32), pltpu.VMEM((2, tile, tile), F32)],
        compiler_params=_cparams("arbitrary", "arbitrary", "arbitrary"),
        name="diff_attn",
    )(cfar, q, k, vt, tables, lam_rows, sub_g.reshape(1, hw))


def _lru_kernel(xb_ref, gb_ref, cw_ref, vec_ref, wa_ref, wx_ref, y_ref, prev_sc, h_sc, a_sc, b_sc, hs_sc):
    ts = xb_ref.shape[0]

    @pl.when(pl.program_id(1) == 0)
    def _():
        prev_sc[...] = jnp.zeros(prev_sc.shape, F32)
        h_sc[...] = jnp.zeros(h_sc.shape, F32)

    xb = xb_ref[...]
    hist = prev_sc.shape[0]
    ext = jnp.concatenate([prev_sc[...], xb], axis=0)
    u = vec_ref[0:1, :]
    for j in range(CONV_W):
        off = hist - (CONV_W - 1) + j
        u = u + cw_ref[j:j + 1, :] * ext[off:off + ts, :]
    prev_sc[...] = xb[ts - hist:, :]
    ub = u.astype(BF16)
    r = jax.nn.sigmoid(_dot(ub, wa_ref[...]) + vec_ref[1:2, :])
    ig = jax.nn.sigmoid(_dot(ub, wx_ref[...]) + vec_ref[2:3, :])
    log_a = (-LRU_C * r) * _softplus(-vec_ref[3:4, :])
    a = jnp.exp(log_a)
    a_sc[...] = a
    b_sc[...] = jnp.sqrt(_neg_expm1(2.0 * log_a, a * a)) * (ig * u)

    def step(tt, h):
        h = a_sc[pl.ds(tt, 1), :] * h + b_sc[pl.ds(tt, 1), :]
        hs_sc[pl.ds(tt, 1), :] = h
        return h

    h_sc[...] = lax.fori_loop(0, ts, step, h_sc[...], unroll=16)
    y_ref[...] = (hs_sc[...] * _gelu_tanh(gb_ref[...])).astype(y_ref.dtype)


def _block_diag(w):
    n, c, d = w.shape
    eye = jnp.eye(n, dtype=w.dtype)
    return (w[:, :, None, :] * eye[:, None, :, None]).reshape(n * c, n * d)


def _rglru(xb, gb, conv_w, conv_b, wa, ba, wx, bx, lru_lambda, batch, seq):
    t, w = xb.shape
    ts = LRU_TILE
    nt = seq // ts
    vec = jnp.zeros((8, w), F32).at[0].set(conv_b).at[1].set(ba).at[2].set(bx).at[3].set(lru_lambda)
    row = lambda b, s: (b * nt + s, 0)
    fixed = lambda b, s: (0, 0)
    return pl.pallas_call(
        _lru_kernel,
        grid=(batch, nt),
        in_specs=[pl.BlockSpec((ts, w), row), pl.BlockSpec((ts, w), row),
                  pl.BlockSpec((CONV_W, w), fixed), pl.BlockSpec((8, w), fixed),
                  pl.BlockSpec((w, w), fixed), pl.BlockSpec((w, w), fixed)],
        out_specs=pl.BlockSpec((ts, w), row),
        out_shape=jax.ShapeDtypeStruct((t, w), BF16),
        scratch_shapes=[pltpu.VMEM((8, w), F32), pltpu.VMEM((1, w), F32),
                        pltpu.VMEM((ts, w), F32), pltpu.VMEM((ts, w), F32), pltpu.VMEM((ts, w), F32)],
        compiler_params=_cparams("arbitrary", "arbitrary"),
        name="rglru",
    )(xb, gb, conv_w, vec, _block_diag(wa).astype(BF16), _block_diag(wx).astype(BF16))


def _route(x, gf_ref, wrh_ref, wrl_ref, br_ref, tri_ref, tot_sc, size_ref, first_step):
    @pl.when(first_step)
    def _():
        tot_sc[...] = jnp.zeros(tot_sc.shape, F32)

    hn = _rms(x, gf_ref[...], NORM_EPS)
    hh, hl = _split2(hn)
    wh = wrh_ref[...]
    lg = _dot(hh, wh) + _dot(hh, wrl_ref[...]) + _dot(hl, wh) + br_ref[...]
    lane = lax.broadcasted_iota(I32, lg.shape, 1)
    big = jnp.int32(1 << 20)
    is_g = lane < MOE_GROUPS
    gl = jnp.where(is_g, lg, NEG_BIG)
    gmax = jnp.max(gl, axis=-1, keepdims=True)
    gidx = jnp.min(jnp.where(gl == gmax, lane, big), axis=-1, keepdims=True)
    gsum = jnp.sum(jnp.where(is_g, jnp.exp(gl - gmax), 0.0), axis=-1, keepdims=True)
    gprob = 1.0 / gsum
    lo = MOE_GROUPS + gidx * EXPERTS_PER_GROUP
    fmask = (lane >= lo) & (lane < lo + EXPERTS_PER_GROUP)
    fl = jnp.where(fmask, lg, NEG_BIG)
    f1 = jnp.max(fl, axis=-1, keepdims=True)
    i1 = jnp.min(jnp.where(fl == f1, lane, big), axis=-1, keepdims=True)
    fl2 = jnp.where(lane == i1, NEG_BIG, fl)
    f2 = jnp.max(fl2, axis=-1, keepdims=True)
    i2 = jnp.min(jnp.where(fl2 == f2, lane, big), axis=-1, keepdims=True)
    e2 = jnp.exp(f2 - f1)
    den = 1.0 + e2
    g1 = gprob / den
    g2 = gprob * e2 / den
    id1 = (i1 - MOE_GROUPS).astype(F32)
    id2 = (i2 - MOE_GROUPS).astype(F32)
    onehot = jnp.where((lane == i1) | (lane == i2), 1.0, 0.0)
    before = _dot(tri_ref[...], onehot.astype(BF16)) + tot_sc[...]
    r1 = jnp.sum(jnp.where(lane == i1, before, 0.0), axis=-1, keepdims=True)
    r2 = jnp.sum(jnp.where(lane == i2, before, 0.0), axis=-1, keepdims=True)
    tot = tot_sc[...] + jnp.sum(onehot, axis=0, keepdims=True)
    tot_sc[...] = tot
    size_ref[...] = jnp.broadcast_to(tot, size_ref.shape)
    vals = (g1, g2, id1, id2, r1, r2)
    out = jnp.zeros(lg.shape, F32)
    for n, v in enumerate(vals):
        out = jnp.where(lane == n, v, out)
    return out


def _router_params(wc, bc, wf, bf):
    d = wc.shape[0]
    n = MOE_GROUPS + N_EXPERTS
    w = jnp.zeros((d, LANES), F32).at[:, :MOE_GROUPS].set(wc).at[:, MOE_GROUPS:n].set(wf)
    b = jnp.zeros((1, LANES), F32).at[0, :MOE_GROUPS].set(bc).at[0, MOE_GROUPS:n].set(bf)
    hi, lo = _split2(w)
    return hi, lo, b


def _outproj_kernel(x_ref, ya_ref, yb_ref, wo_ref, gf_ref, wrh_ref, wrl_ref, br_ref, tri_ref,
                    x1_ref, x1t_ref, rt_ref, size_ref, tot_sc):
    half = ya_ref.shape[1]
    x1 = x_ref[...] + _dot(ya_ref[...], wo_ref[:half, :]) + _dot(yb_ref[...], wo_ref[half:, :])
    x1_ref[...] = x1
    _store_token_tiles(x1t_ref, x1)
    rt_ref[...] = _route(x1, gf_ref, wrh_ref, wrl_ref, br_ref, tri_ref, tot_sc, size_ref,
                         pl.program_id(0) == 0)


def _outproj(x2, ya, yb, w_out, gf, router):
    t, d = x2.shape
    half = ya.shape[1]
    tm = ROW_TILE
    row = lambda i: (i, 0)
    fixed = lambda i: (0, 0)
    wrh, wrl, br = router
    tri = _strict_lower(tm)
    return pl.pallas_call(
        _outproj_kernel,
        grid=(t // tm,),
        in_specs=[pl.BlockSpec((tm, d), row), pl.BlockSpec((tm, half), row), pl.BlockSpec((tm, half), row),
                  pl.BlockSpec((2 * half, d), fixed), pl.BlockSpec((1, d), fixed),
                  pl.BlockSpec((d, LANES), fixed), pl.BlockSpec((d, LANES), fixed),
                  pl.BlockSpec((1, LANES), fixed), pl.BlockSpec((tm, tm), fixed)],
        out_specs=[pl.BlockSpec((tm, d), row), pl.BlockSpec((tm * SUBLANES, LANES), row),
                   pl.BlockSpec((tm, LANES), row), pl.BlockSpec((SUBLANES, LANES), fixed)],
        out_shape=[jax.ShapeDtypeStruct((t, d), F32), jax.ShapeDtypeStruct((t * SUBLANES, LANES), F32),
                   jax.ShapeDtypeStruct((t, LANES), F32), jax.ShapeDtypeStruct((SUBLANES, LANES), F32)],
        scratch_shapes=[pltpu.VMEM((1, LANES), F32)],
        compiler_params=_cparams("arbitrary"),
        name="outproj_route",
    )(x2, ya, yb, w_out, gf.reshape(1, d), wrh, wrl, br, tri)


def _strict_lower(n):
    return (jnp.arange(n)[None, :] < jnp.arange(n)[:, None]).astype(BF16)


def _moe_plan(route, sizes_tile, n_tok):
    m = n_tok * MOE_TOPK
    n_blk = -(-(m + N_EXPERTS * (MOE_BLOCK - 1)) // MOE_BLOCK)
    eid = route[:, 2:4].astype(I32)
    rank = route[:, 4:6].astype(I32)
    sizes = sizes_tile[0, MOE_GROUPS:MOE_GROUPS + N_EXPERTS].astype(I32)
    padded = (sizes + MOE_BLOCK - 1) // MOE_BLOCK * MOE_BLOCK
    pend = jnp.cumsum(padded)
    dest = (pend - padded)[eid] + rank
    blk_start = jnp.arange(n_blk, dtype=I32) * MOE_BLOCK
    blk_e = jnp.minimum(jnp.sum((pend[None, :] <= blk_start[:, None]).astype(I32), axis=1),
                        N_EXPERTS - 1).astype(I32)
    n_used = (pend[-1] // MOE_BLOCK).astype(I32).reshape(1)
    tail = jnp.where(padded > 0, pend - MOE_BLOCK, -1).astype(I32)
    return blk_e, n_used, dest.astype(I32), tail, n_blk


def _dispatch_kernel(tail_ref, nused_ref, dest_hbm, xt_ref, xs_hbm,
                     idx_sm0, idx_sm1, idx_sem, zero_sc, zero_sem, stage_sc, row_sem):
    i = pl.program_id(0)
    n = pl.num_programs(0)
    tm = idx_sm0.shape[0] // MOE_TOPK
    idx_sms = (idx_sm0, idx_sm1)
    tile_rows = MOE_BLOCK * SUBLANES
    n_blk = xs_hbm.shape[0] // tile_rows

    @pl.when(i == 0)
    def _():
        zero_sc[...] = jnp.zeros(zero_sc.shape, F32)

        def clear(row, wait):
            cp = pltpu.make_async_copy(
                zero_sc, xs_hbm.at[pl.ds(pl.multiple_of(row * SUBLANES, SUBLANES), tile_rows), :], zero_sem)
            if wait:
                cp.wait()
            else:
                cp.start()

        def fill(e, wait):
            @pl.when(tail_ref[e] >= 0)
            def _():
                clear(tail_ref[e], wait)

        for wait in (False, True):
            lax.fori_loop(0, N_EXPERTS, lambda e, c, w=wait: (fill(e, w), c)[1], 0)
            lax.fori_loop(nused_ref[0], n_blk, lambda b, c, w=wait: (clear(b * MOE_BLOCK, w), c)[1], 0)

    def idx_copy(step, sl):
        return pltpu.make_async_copy(dest_hbm.at[step], idx_sms[sl], idx_sem.at[sl])

    def all_copies(sl):
        for s in range(MOE_TOPK):
            pltpu.make_async_copy(stage_sc.at[sl], xs_hbm.at[pl.ds(0, tm * SUBLANES), :], row_sem.at[sl]).wait()

    @pl.when(i == 0)
    def _():
        idx_copy(0, 0).start()

    for sl in range(2):
        @pl.when(i % 2 == sl)
        def _():
            @pl.when(i >= 2)
            def _():
                all_copies(sl)

            stage_sc[sl] = xt_ref[...]
            idx_copy(i, sl).wait()

            @pl.when(i + 1 < n)
            def _():
                idx_copy(i + 1, 1 - sl).start()

            def issue(r, c):
                src = pl.multiple_of(r * SUBLANES, SUBLANES)
                for s in range(MOE_TOPK):
                    dst = pl.multiple_of(idx_sms[sl][MOE_TOPK * r + s], SUBLANES)
                    pltpu.make_async_copy(stage_sc.at[sl, pl.ds(src, SUBLANES), :],
                                          xs_hbm.at[pl.ds(dst, SUBLANES), :],
                                          row_sem.at[sl]).start(priority=s % DMA_PRIORITIES)
                return c

            lax.fori_loop(0, tm, issue, 0, unroll=8)

            @pl.when(i == n - 1)
            def _():
                all_copies(sl)

                @pl.when(n >= 2)
                def _():
                    all_copies(1 - sl)


def _dispatch(x1t, dest, tail, n_used, n_blk):
    t = dest.shape[0]
    tm = COMBINE_TILE
    pos = (dest * SUBLANES).reshape(t // tm, MOE_TOPK * tm)
    grid_spec = pltpu.PrefetchScalarGridSpec(
        num_scalar_prefetch=2,
        grid=(t // tm,),
        in_specs=[pl.BlockSpec(memory_space=pl.ANY),
                  pl.BlockSpec((tm * SUBLANES, LANES), lambda i, tl, nu: (i, 0))],
        out_specs=pl.BlockSpec(memory_space=pl.ANY),
        scratch_shapes=[pltpu.SMEM((MOE_TOPK * tm,), I32), pltpu.SMEM((MOE_TOPK * tm,), I32),
                        pltpu.SemaphoreType.DMA((2,)), pltpu.VMEM((MOE_BLOCK * SUBLANES, LANES), F32),
                        pltpu.SemaphoreType.DMA, pltpu.VMEM((2, tm * SUBLANES, LANES), F32),
                        pltpu.SemaphoreType.DMA((2,))],
    )
    return pl.pallas_call(
        _dispatch_kernel,
        grid_spec=grid_spec,
        out_shape=jax.ShapeDtypeStruct((n_blk * MOE_BLOCK * SUBLANES, LANES), F32),
        compiler_params=_cparams("arbitrary"),
        name="moe_dispatch",
    )(tail, n_used, pos, x1t)


def _gather_ahead(idx_hbm, idx_sms, idx_sem, issue_rows, n_active):
    i = pl.program_id(0)
    n = pl.num_programs(0)

    def idx_copy(step, slot):
        return pltpu.make_async_copy(idx_hbm.at[step], idx_sms[slot], idx_sem.at[slot])

    @pl.when(i == 0)
    def _():
        first = idx_copy(0, 0)
        first.start()
        first.wait()

        @pl.when(0 < n_active)
        def _():
            issue_rows(0)

        @pl.when(1 < n)
        def _():
            idx_copy(1, 1).start()

    for slot in range(2):
        @pl.when(jnp.logical_and(i + 1 < n, (i + 1) % 2 == slot))
        def _():
            idx_copy(i + 1, slot).wait()

            @pl.when(i + 1 < n_active)
            def _():
                issue_rows(slot)

        @pl.when(jnp.logical_and(i + 2 < n, i % 2 == slot))
        def _():
            idx_copy(i + 2, slot).start()


def _expert_kernel(blk_e_ref, nused_ref, xs_ref, gf_ref, wgu_ref, wd_ref, yst_ref):
    @pl.when(pl.program_id(0) >= nused_ref[0])
    def _():
        yst_ref[...] = jnp.zeros(yst_ref.shape, F32)

    @pl.when(pl.program_id(0) < nused_ref[0])
    def _():
        blk = xs_ref.shape[0] // SUBLANES
        x = _load_token_tiles(xs_ref, blk)
        hn = _rms(x, gf_ref[...], NORM_EPS).astype(BF16)
        gu = _dot(hn, wgu_ref[0])
        eh = gu.shape[1] // 2
        g = gu[:, :eh]
        act = (g * jax.nn.sigmoid(g) * gu[:, eh:]).astype(BF16)
        _store_token_tiles(yst_ref, _dot(act, wd_ref[0]))


def _experts(xs, blk_e, n_used, gf, w_gu, w_down):
    d = gf.shape[0]
    blk = MOE_BLOCK
    n_blk = xs.shape[0] // (blk * SUBLANES)
    eh2 = w_gu.shape[2]
    rows = lambda i, e, n: (jnp.minimum(i, n[0] - 1), 0)
    grid_spec = pltpu.PrefetchScalarGridSpec(
        num_scalar_prefetch=2,
        grid=(n_blk,),
        in_specs=[
            pl.BlockSpec((blk * SUBLANES, LANES), rows),
            pl.BlockSpec((1, d), lambda i, e, n: (0, 0)),
            pl.BlockSpec((1, d, eh2), lambda i, e, n: (e[i], 0, 0)),
            pl.BlockSpec((1, eh2 // 2, d), lambda i, e, n: (e[i], 0, 0)),
        ],
        out_specs=pl.BlockSpec((blk * SUBLANES, LANES), lambda i, e, n: (i, 0)),
    )
    return pl.pallas_call(
        _expert_kernel,
        grid_spec=grid_spec,
        out_shape=jax.ShapeDtypeStruct(xs.shape, F32),
        compiler_params=_cparams("arbitrary"),
        name="moe_experts",
    )(blk_e, n_used, xs, gf.reshape(1, d), w_gu, w_down)


def _combine_kernel(pos_hbm, yst_hbm, x1_ref, rt_ref, p_ref, gn_ref, wg_ref, wp_ref, fn_ref, o_ref,
                    idx_sm0, idx_sm1, idx_sem, ybuf, row_sem, *, final):
    slot = pl.program_id(0) % 2
    tm = x1_ref.shape[0]
    idx_sms = (idx_sm0, idx_sm1)

    def issue_rows(sl):
        def issue(r, c):
            dst = pl.multiple_of(r * SUBLANES, SUBLANES)
            for s in range(MOE_TOPK):
                src = pl.multiple_of(idx_sms[sl][MOE_TOPK * r + s], SUBLANES)
                pltpu.make_async_copy(yst_hbm.at[pl.ds(src, SUBLANES), :],
                                      ybuf.at[sl, s, pl.ds(dst, SUBLANES), :],
                                      row_sem.at[sl]).start(priority=s % DMA_PRIORITIES)
            return c

        lax.fori_loop(0, tm, issue, 0, unroll=8)

    _gather_ahead(pos_hbm, idx_sms, idx_sem, issue_rows, pl.num_programs(0))
    for s in range(MOE_TOPK):
        pltpu.make_async_copy(yst_hbm.at[pl.ds(0, tm * SUBLANES), :], ybuf.at[slot, s], row_sem.at[slot]).wait()

    rt = rt_ref[...]
    y0 = _load_token_tiles(ybuf.at[slot, 0], tm)
    y1 = _load_token_tiles(ybuf.at[slot, 1], tm)
    x2 = x1_ref[...] + (rt[:, 0:1] * y0 + rt[:, 1:2] * y1)
    gate = jax.nn.sigmoid(_dot(_rms(x2, gn_ref[...], NORM_EPS).astype(BF16), wg_ref[...]))
    x3 = x2 + _dot(p_ref[...].astype(BF16), wp_ref[...]) * gate
    if final:
        x3 = _rms(x3, fn_ref[...], NORM_EPS)
    o_ref[...] = x3


def _combine(x1, route, yst, dest, p_i, ple_norm, ple_gate, ple_proj, final_norm, final):
    t, d = x1.shape
    tm = COMBINE_TILE
    pd = p_i.shape[1]
    pos = (dest * SUBLANES).reshape(t // tm, MOE_TOPK * tm)
    row = lambda i: (i, 0)
    fixed = lambda i: (0, 0)
    return pl.pallas_call(
        functools.partial(_combine_kernel, final=final),
        grid=(t // tm,),
        in_specs=[pl.BlockSpec(memory_space=pl.ANY), pl.BlockSpec(memory_space=pl.ANY),
                  pl.BlockSpec((tm, d), row), pl.BlockSpec((tm, LANES), row), pl.BlockSpec((tm, pd), row),
                  pl.BlockSpec((1, d), fixed), pl.BlockSpec((d, d), fixed), pl.BlockSpec((pd, d), fixed),
                  pl.BlockSpec((1, d), fixed)],
        out_specs=pl.BlockSpec((tm, d), row),
        out_shape=jax.ShapeDtypeStruct((t, d), F32),
        scratch_shapes=[pltpu.SMEM((MOE_TOPK * tm,), I32), pltpu.SMEM((MOE_TOPK * tm,), I32),
                        pltpu.SemaphoreType.DMA((2,)),
                        pltpu.VMEM((2, MOE_TOPK, tm * SUBLANES, LANES), F32), pltpu.SemaphoreType.DMA((2,))],
        compiler_params=_cparams("arbitrary"),
        name="moe_combine_ple",
    )(pos, yst, x1, route, p_i, ple_norm.reshape(1, d), ple_gate, ple_proj, final_norm.reshape(1, d))


def _seg_sum(x, eseg_ref, eexp_ref):
    s = _dot(x.astype(BF16), eseg_ref[...])
    return _dot(s.astype(BF16), eexp_ref[...])


def _rwkv_pre_kernel(x_ref, xp_ref, gm_ref, mu_ref, vec_ref, wr_ref, wk_ref, wv_ref, w1_ref, w2_ref,
                     a1_ref, a2_ref, g1_ref, g2_ref, tri_ref, eseg_ref, eexp_ref,
                     la_ref, lr_ref, rb_ref, rk_ref, sb_ref, sk_ref, v_ref, bon_ref, g_ref, gc_ref,
                     c_sc):
    tl = x_ref.shape[0]
    gm = gm_ref[...]
    h = _rms(x_ref[...], gm, NORM_EPS)
    hp = _rms(xp_ref[...], gm, NORM_EPS)[7:8, :]
    hp = jnp.where(pl.program_id(1) == 0, 0.0, hp)
    row = lax.broadcasted_iota(I32, h.shape, 0)
    hs = jnp.where(row == 0, hp, pltpu.roll(h, 1, axis=0))
    xx = hs - h
    mix = lambda n: (h + xx * mu_ref[n:n + 1, :]).astype(BF16)
    vec = lambda n: vec_ref[n:n + 1, :]
    r = _dot(mix(0), wr_ref[...])
    wl = vec(0) + _dot(jnp.tanh(_dot(mix(1), w1_ref[...])).astype(BF16), w2_ref[...])
    lw = -jnp.exp(-_softplus(-wl) - 0.5)
    k = _dot(mix(2), wk_ref[...])
    v = _dot(mix(3), wv_ref[...])
    a = jax.nn.sigmoid(vec(1) + _dot(_dot(mix(4), a1_ref[...]).astype(BF16), a2_ref[...]))
    g = _dot(jax.nn.sigmoid(_dot(mix(5), g1_ref[...])).astype(BF16), g2_ref[...])
    kk = k * vec(2)
    kk = kk / jnp.maximum(jnp.sqrt(_seg_sum(kk * kk, eseg_ref, eexp_ref)), 1e-12)
    k2 = k * (1.0 + (a - 1.0) * vec(3))
    bon_ref[...] = (_seg_sum(r * k2 * vec(4), eseg_ref, eexp_ref) * v).astype(bon_ref.dtype)
    v_ref[...] = v.astype(v_ref.dtype)
    g_ref[...] = g.astype(g_ref.dtype)

    half = tri_ref.shape[1]
    tri = tri_ref[...]
    for s in range(tl // half):
        p1, p2 = _split2(lw[s * half:(s + 1) * half, :])
        c_sc[s * half:(s + 1) * half, :] = _dot(tri, p1) + _dot(tri, p2)
    c_in = c_sc[...]
    nc = tl // RWKV_CHUNK
    ends = [c_sc[(n + 1) * RWKV_CHUNK - 1:(n + 1) * RWKV_CHUNK, :] for n in range(nc)]
    c_end = jnp.concatenate([ends[n] - c_sc[n * RWKV_CHUNK:(n + 1) * RWKV_CHUNK, :] for n in range(nc)],
                            axis=0)
    gc_ref[...] = jnp.exp(jnp.concatenate(ends, axis=0))
    e_in = jnp.exp(c_in)
    e_neg = jnp.exp(-c_in)
    e_end = jnp.exp(c_end)
    b = kk * a
    la_ref[...] = (-kk * jnp.exp(c_in - lw)).astype(la_ref.dtype)
    lr_ref[...] = (r * e_in).astype(lr_ref.dtype)
    rb_ref[...] = (b * e_neg).astype(rb_ref.dtype)
    rk_ref[...] = (k2 * e_neg).astype(rk_ref.dtype)
    sb_ref[...] = (b * e_end).astype(sb_ref.dtype)
    sk_ref[...] = (k2 * e_end).astype(sk_ref.dtype)


def _head_indicator(d):
    heads = d // RWKV_HEAD
    e = (jnp.arange(d)[:, None] // RWKV_HEAD == jnp.arange(LANES)[None, :]).astype(BF16)
    del heads
    return e, e.T


def _cumsum_matrix(half):
    i = jnp.arange(half)[:, None]
    j = jnp.arange(half)[None, :]
    return (((i // RWKV_CHUNK) == (j // RWKV_CHUNK)) & (j <= i)).astype(BF16)


def _pad_cols(w, n):
    return jnp.pad(w, ((0, 0), (0, n - w.shape[1])))


def _pad_rows(w, n):
    return jnp.pad(w, ((0, n - w.shape[0]), (0, 0)))


def _rwkv_pre(x3, gm, mu, vecs, wr, wk, wv, w1, w2, a1, a2, g1, g2, batch, seq):
    t, d = x3.shape
    tl = RWKV_TILE
    nt = seq // tl
    half = 256
    lora = lambda w_in, w_out: (_pad_cols(w_in, -(-w_in.shape[1] // LANES) * LANES).astype(BF16),
                                _pad_rows(w_out, -(-w_out.shape[0] // LANES) * LANES).astype(BF16))
    w1p, w2p = lora(w1, w2)
    a1p, a2p = lora(a1, a2)
    g1p, g2p = lora(g1, g2)
    eseg, eexp = _head_indicator(d)
    tri = _cumsum_matrix(half)
    row = lambda b, s: (b * nt + s, 0)
    fixed = lambda b, s: (0, 0)
    full = lambda a: pl.BlockSpec(a.shape, fixed)
    prev = lambda b, s: (jnp.maximum((b * nt + s) * (tl // 8) - 1, 0), 0)
    weights = (wr.astype(BF16), wk.astype(BF16), wv.astype(BF16), w1p, w2p, a1p, a2p, g1p, g2p, tri, eseg, eexp)
    outs = pl.pallas_call(
        _rwkv_pre_kernel,
        grid=(batch, nt),
        in_specs=[pl.BlockSpec((tl, d), row), pl.BlockSpec((8, d), prev), pl.BlockSpec((1, d), fixed),
                  full(mu), full(vecs)] + [full(w) for w in weights],
        out_specs=[pl.BlockSpec((tl, d), row)] * 9 + [pl.BlockSpec((tl // RWKV_CHUNK, d), row)],
        out_shape=[jax.ShapeDtypeStruct((t, d), BF16)] * 9 + [jax.ShapeDtypeStruct((t // RWKV_CHUNK, d), F32)],
        scratch_shapes=[pltpu.VMEM((tl, d), F32)],
        compiler_params=_cparams("arbitrary", "arbitrary"),
        name="rwkv_pre",
    )(x3, x3, gm.reshape(1, d), mu, vecs, *weights)
    return outs


def _rwkv_scan_kernel(la_ref, lr_ref, rb_ref, rk_ref, sb_ref, sk_ref, v_ref, gc_ref, bon_ref, g_ref, x_ref,
                      vec_ref, wo_ref, eseg_ref, eexp_ref, gf_ref, wrh_ref, wrl_ref, br_ref, tri_ref,
                      x4_ref, x4t_ref, rt_ref, size_ref, s_sc, y_sc, tot_sc):
    tl, d = x_ref.shape
    ch = RWKV_CHUNK
    pw = 2 * RWKV_HEAD
    npair = d // pw

    @pl.when(pl.program_id(1) == 0)
    def _():
        s_sc[...] = jnp.zeros(s_sc.shape, F32)

    ri = lax.broadcasted_iota(I32, (pw, pw), 0)
    ci = lax.broadcasted_iota(I32, (pw, pw), 1)
    same_half = (ri < RWKV_HEAD) == (ci < RWKV_HEAD)
    eye = (ri == ci).astype(F32)
    rt_ = lax.broadcasted_iota(I32, (ch, pw), 0)
    ct_ = lax.broadcasted_iota(I32, (ch, pw), 1)
    first = ct_ < RWKV_HEAD
    strict = rt_ > (ct_ & (RWKV_HEAD - 1))
    incl = rt_ >= (ct_ & (RWKV_HEAD - 1))
    m_s0, m_s1 = strict & first, strict & ~first
    m_i0, m_i1 = incl & first, incl & ~first
    first_full = ci < RWKV_HEAD
    zf = jnp.zeros((ch, pw), F32)
    gc_row = lax.broadcasted_iota(I32, (tl // ch, pw), 0)

    def chunk(c, carry):
        rows = pl.ds(pl.multiple_of(c * ch, ch), ch)
        pairs = range(npair)
        cols = [slice(p * pw, (p + 1) * pw) for p in pairs]
        cat0 = lambda a, b: jnp.concatenate([a, b], axis=0)
        lcat = [cat0(la_ref[rows, cl], lr_ref[rows, cl]) for cl in cols]
        rb = [rb_ref[rows, cl] for cl in cols]
        rk = [rk_ref[rows, cl] for cl in cols]
        vv = [v_ref[rows, cl] for cl in cols]
        zb = jnp.zeros_like(lcat[0])
        a0 = [_dot_nt(jnp.where(first_full, lcat[p], zb), cat0(rb[p], rk[p])) for p in pairs]
        a1 = [_dot_nt(jnp.where(first_full, zb, lcat[p]), cat0(rk[p], rb[p])) for p in pairs]
        pm = [cat0(jnp.where(m_s0, a0[p][:ch], zf), jnp.where(m_s1, a1[p][:ch], zf)) for p in pairs]
        aak = [cat0(jnp.where(m_s1, a0[p][:ch], zf), jnp.where(m_s0, a1[p][:ch], zf)).astype(BF16)
               for p in pairs]
        arbk = [jnp.concatenate([cat0(jnp.where(m_i0, a0[p][ch:], zf), jnp.where(m_i1, a1[p][ch:], zf)),
                                 cat0(jnp.where(m_i1, a0[p][ch:], zf), jnp.where(m_i0, a1[p][ch:], zf))],
                                axis=1).astype(BF16) for p in pairs]
        tm = [eye + pm[p] for p in pairs]
        qm = pm
        for _ in range(5):
            qb = [q.astype(BF16) for q in qm]
            qm = [_dot(qb[p], qb[p]) for p in pairs]
            tm = [tm[p] + _dot(tm[p].astype(BF16), qm[p].astype(BF16)) for p in pairs]
        st = [s_sc[p] for p in pairs]
        w12 = [_dot_nt(lcat[p], st[p].astype(BF16)) for p in pairs]
        v2 = [cat0(vv[p], vv[p]) for p in pairs]
        rhs = [cat0(w12[p][:ch], w12[p][:ch]) + _dot(aak[p], v2[p]) for p in pairs]
        ust = [_dot(tm[p].astype(BF16), rhs[p].astype(BF16)) for p in pairs]
        yst = [cat0(w12[p][ch:], w12[p][ch:]) + _dot(arbk[p], cat0(ust[p].astype(BF16), v2[p]))
               for p in pairs]
        for p in pairs:
            y_sc[rows, cols[p]] = jnp.where(first, yst[p][:ch], yst[p][ch:])
        u = [jnp.where(first, ust[p][:ch], ust[p][ch:]).astype(BF16) for p in pairs]
        ds_ = [_dot_tn(cat0(u[p], vv[p]), cat0(sb_ref[rows, cols[p]], sk_ref[rows, cols[p]])) for p in pairs]
        for p in pairs:
            gcr = jnp.sum(jnp.where(gc_row == c, gc_ref[:, cols[p]], 0.0), axis=0, keepdims=True)
            s_sc[p] = jnp.where(same_half, st[p] * gcr + ds_[p], 0.0)
        return carry

    lax.fori_loop(0, tl // ch, chunk, 0)

    y = y_sc[...]
    inv_n = 1.0 / RWKV_HEAD
    mean = _seg_sum(y, eseg_ref, eexp_ref) * inv_n
    dlt = y - mean
    var = _seg_sum(dlt * dlt, eseg_ref, eexp_ref) * inv_n
    yn = dlt * lax.rsqrt(var + RWKV_GN_EPS) * vec_ref[5:6, :] + vec_ref[6:7, :]
    z = (yn + bon_ref[...].astype(F32)) * g_ref[...].astype(F32)
    x4 = x_ref[...] + _dot(z.astype(BF16), wo_ref[...])
    x4_ref[...] = x4
    _store_token_tiles(x4t_ref, x4)
    first = jnp.logical_and(pl.program_id(0) == 0, pl.program_id(1) == 0)
    rt_ref[...] = _route(x4, gf_ref, wrh_ref, wrl_ref, br_ref, tri_ref, tot_sc, size_ref, first)


def _rwkv_scan(pre, x3, vecs, wo, gf, router, batch, seq):
    la, lr, rb, rk, sb, sk, v, bon, g, gc = pre
    t, d = x3.shape
    tl = RWKV_TILE
    nt = seq // tl
    eseg, eexp = _head_indicator(d)
    wrh, wrl, br = router
    row = lambda b, s: (b * nt + s, 0)
    fixed = lambda b, s: (0, 0)
    big = pl.BlockSpec((tl, d), row)
    full = lambda a: pl.BlockSpec(a.shape, fixed)
    pw = 2 * RWKV_HEAD
    tri = _strict_lower(tl)
    return pl.pallas_call(
        _rwkv_scan_kernel,
        grid=(batch, nt),
        in_specs=[big] * 7 + [pl.BlockSpec((tl // RWKV_CHUNK, d), row), big, big, big,
                              full(vecs), full(wo), full(eseg), full(eexp),
                              pl.BlockSpec((1, d), fixed), full(wrh), full(wrl), full(br), full(tri)],
        out_specs=[big, pl.BlockSpec((tl * SUBLANES, LANES), row), pl.BlockSpec((tl, LANES), row),
                   pl.BlockSpec((SUBLANES, LANES), fixed)],
        out_shape=[jax.ShapeDtypeStruct((t, d), F32), jax.ShapeDtypeStruct((t * SUBLANES, LANES), F32),
                   jax.ShapeDtypeStruct((t, LANES), F32), jax.ShapeDtypeStruct((SUBLANES, LANES), F32)],
        scratch_shapes=[pltpu.VMEM((d // pw, pw, pw), F32), pltpu.VMEM((tl, d), F32),
                        pltpu.VMEM((1, LANES), F32)],
        compiler_params=_cparams("arbitrary", "arbitrary"),
        name="rwkv_scan_out",
    )(la, lr, rb, rk, sb, sk, v, gc, bon, g, x3, vecs, wo, eseg, eexp, gf.reshape(1, d), wrh, wrl, br, tri)


def _moe_and_embed(x1, x1t, route, sizes, i, p, norm_ffn, moe_w_gu, moe_w_down, ple_norm, ple_gate, ple_proj,
                   final_norm, final):
    t = x1.shape[0]
    blk_e, n_used, dest, tail, n_blk = _moe_plan(route, sizes, t)
    xs = _dispatch(x1t, dest, tail, n_used, n_blk)
    yst = _experts(xs, blk_e, n_used, norm_ffn[i], moe_w_gu[i].astype(BF16), moe_w_down[i].astype(BF16))
    p_i = p[i].reshape(t, p.shape[-1])
    return _combine(x1, route, yst, dest, p_i, ple_norm[i], ple_gate[i].astype(BF16),
                    ple_proj[i].astype(BF16), final_norm, final)


def kernel(x, p, rel_bias, norm_mix, ab_w_in, ab_w_out, lam_q1, lam_k1, lam_q2, lam_k2, sub_g, conv_w, conv_b, lru_wa, lru_ba, lru_wx, lru_bx, lru_lambda, rwkv_mu, rwkv_wr, rwkv_wk, rwkv_wv, rwkv_wo, rwkv_w0, rwkv_w1, rwkv_w2, rwkv_a0, rwkv_a1, rwkv_a2, rwkv_g1, rwkv_g2, rwkv_kk, rwkv_ka, rwkv_rk, rwkv_ln_g, rwkv_ln_b, norm_ffn, moe_wc, moe_bc, moe_wf, moe_bf, moe_w_gu, moe_w_down, ple_norm, ple_gate, ple_proj, final_norm):
    batch, seq, d = x.shape
    t = batch * seq
    depth = norm_mix.shape[0]
    assert depth == 2 and seq % RWKV_TILE == 0 and seq % LRU_TILE == 0 and t % ROW_TILE == 0
    assert d == SUBLANES * LANES and seq % ATTN_TILE == 0 and t % COMBINE_TILE == 0
    xs = x.reshape(t, d)
    for i in range(depth):
        j = i // 2
        router = _router_params(moe_wc[i], moe_bc[i], moe_wf[i], moe_bf[i])
        if i % 2 == 0:
            lam_init = 0.8 - 0.6 * math.exp(-0.3 * i)
            q, k, vt, xb, gb = _inproj(xs, norm_mix[i], ab_w_in[j].astype(BF16))
            lam_rows = jnp.stack([lam_q1[j], lam_k1[j], lam_q2[j], lam_k2[j]]).astype(F32)
            ya = _diff_attention(q, k, vt, rel_bias, lam_rows, sub_g[j], lam_init, batch, seq)
            yb = _rglru(xb, gb, conv_w[j], conv_b[j], lru_wa[j], lru_ba[j], lru_wx[j], lru_bx[j],
                        lru_lambda[j], batch, seq)
            x1, x1t, route, sizes = _outproj(xs, ya, yb, ab_w_out[j].astype(BF16), norm_ffn[i], router)
        else:
            vecs = jnp.stack([rwkv_w0[j], rwkv_a0[j], rwkv_kk[j], rwkv_ka[j], rwkv_rk[j].reshape(-1),
                              rwkv_ln_g[j], rwkv_ln_b[j], jnp.zeros((d,), F32)]).astype(F32)
            pre = _rwkv_pre(xs, norm_mix[i], rwkv_mu[j], vecs, rwkv_wr[j], rwkv_wk[j], rwkv_wv[j],
                            rwkv_w1[j], rwkv_w2[j], rwkv_a1[j], rwkv_a2[j], rwkv_g1[j], rwkv_g2[j],
                            batch, seq)
            x1, x1t, route, sizes = _rwkv_scan(pre, xs, vecs, rwkv_wo[j].astype(BF16), norm_ffn[i], router,
                                               batch, seq)
        xs = _moe_and_embed(x1, x1t, route, sizes, i, p, norm_ffn, moe_w_gu, moe_w_down, ple_norm, ple_gate,
                            ple_proj, final_norm, final=(i == depth - 1))
    return xs.reshape(batch, seq, d)
```

```python
import functools
import math

import jax
import jax.numpy as jnp
from jax import lax
from jax.experimental import pallas as pl
from jax.experimental.pallas import tpu as pltpu

F32, BF16, I32 = jnp.float32, jnp.bfloat16, jnp.int32

A_HEADS = 4
A_HEAD_DIM = 64
A_SCALE = A_HEAD_DIM ** -0.5
LOG2E = math.log2(math.e)
REL_BUCKETS = 32
REL_MAX_DIST = 128
LRU_BLOCKS = 8
CONV_W = 4
LRU_C = 8.0
RWKV_HEAD = 64
RWKV_GN_EPS = 64e-5
MOE_GROUPS = 4
EXPERTS_PER_GROUP = 8
N_EXPERTS = MOE_GROUPS * EXPERTS_PER_GROUP
MOE_TOPK = 2
NORM_EPS = 1e-6
SUBNORM_EPS = 1e-5

LANES = 128
VMEM_LIMIT = 56 * 1024 * 1024
ROW_TILE = 512
ATTN_TILE = 512
LRU_TILE = 512
RWKV_CHUNK = 64
RWKV_TILE = 512
RWKV_CHUNKS_PER_ITER = 2
MOE_BLOCK = 512
COMBINE_TILE = 256
NEG_BIG = -1e30


def _cparams(*sem, flags=None):
    return pltpu.CompilerParams(dimension_semantics=sem, vmem_limit_bytes=VMEM_LIMIT, flags=flags)


def _rms(x, g, eps):
    return x * lax.rsqrt(jnp.mean(x * x, axis=-1, keepdims=True) + eps) * g


def _dot(a, b):
    return jnp.dot(a, b, preferred_element_type=F32)


def _dot_nt(a, b):
    return lax.dot_general(a, b, (((1,), (1,)), ((), ())), preferred_element_type=F32)


def _dot_tn(a, b):
    return lax.dot_general(a, b, (((0,), (0,)), ((), ())), preferred_element_type=F32)


def _split2(x):
    hi = x.astype(BF16)
    lo = (x - hi.astype(F32)).astype(BF16)
    return hi, lo


def _softplus(x):
    return jnp.maximum(x, 0.0) + jnp.log1p(jnp.exp(-jnp.abs(x)))


def _neg_expm1(y, exp_y):
    poly = 1.0 + y * (1.0 / 5.0)
    for d in (4.0, 3.0, 2.0):
        poly = 1.0 + (y * (1.0 / d)) * poly
    return jnp.where(y > -1.0 / 16.0, -y * poly, 1.0 - exp_y)


SUBLANES = 8
DMA_PRIORITIES = 2


def _store_token_tiles(ref, x):
    rows = x.shape[0]
    for s in range(x.shape[1] // LANES):
        ref[pl.ds(s, rows, stride=SUBLANES), :] = x[:, s * LANES:(s + 1) * LANES]


def _load_token_tiles(ref, rows):
    return jnp.concatenate([ref[pl.ds(s, rows, stride=SUBLANES), :] for s in range(SUBLANES)], axis=1)


def _gelu_tanh(x):
    c = math.sqrt(2.0 / math.pi)
    return 0.5 * x * (1.0 + jnp.tanh(c * (x + 0.044715 * (x * x * x))))


def _inproj_kernel(x_ref, g_ref, w_ref, q_ref, k_ref, vt_ref, xb_ref, gb_ref):
    hn = _rms(x_ref[...], g_ref[...], NORM_EPS).astype(BF16)
    width = q_ref.shape[1]
    for c, o_ref in enumerate((q_ref, k_ref, vt_ref, xb_ref, gb_ref)):
        r = _dot(hn, w_ref[:, c * width:(c + 1) * width])
        if c == 0:
            r = r * (A_SCALE * LOG2E)
        if c == 2:
            r = r.T
        o_ref[...] = r.astype(o_ref.dtype)


def _inproj(x2, g, w_in):
    t, d = x2.shape
    width = w_in.shape[1] // 5
    tm = ROW_TILE
    row = lambda i: (i, 0)
    fixed = lambda i: (0, 0)
    rows = pl.BlockSpec((tm, width), row)
    return pl.pallas_call(
        _inproj_kernel,
        grid=(t // tm,),
        in_specs=[pl.BlockSpec((tm, d), row), pl.BlockSpec((1, d), fixed),
                  pl.BlockSpec(w_in.shape, fixed)],
        out_specs=[rows, rows, pl.BlockSpec((width, tm), lambda i: (0, i)), rows, rows],
        out_shape=[jax.ShapeDtypeStruct((t, width), BF16)] * 2 + [jax.ShapeDtypeStruct((width, t), BF16)]
        + [jax.ShapeDtypeStruct((t, width), F32)] * 2,
        compiler_params=_cparams("arbitrary"),
        name="inproj",
    )(x2, g.reshape(1, d), w_in)


def _t5_bucket(rel):
    n = jnp.maximum(rel, 0)
    max_exact = REL_BUCKETS // 2
    large = max_exact + (jnp.log(jnp.maximum(n, 1).astype(F32) / max_exact)
                         / math.log(REL_MAX_DIST / max_exact)
                         * (REL_BUCKETS - max_exact)).astype(I32)
    large = jnp.minimum(large, REL_BUCKETS - 1)
    return jnp.where(n < max_exact, n, large)


def _bias_tables(rel_bias, tile):
    span = 2 * tile

    def toeplitz(first_rel):
        rel = jnp.arange(span - 1) + first_rel
        vals = jnp.where((rel >= 0)[:, None], rel_bias[_t5_bucket(rel)].astype(F32), NEG_BIG)
        w = jnp.concatenate([vals, jnp.zeros((1, vals.shape[1]), F32)], axis=0).T
        skew = jnp.tile(w, (1, tile))[:, :tile * (span - 1)].reshape(-1, tile, span - 1)
        return skew[:, :, tile - 1:]

    return jnp.stack([toeplitz(1 - tile), toeplitz(1)], axis=1)


def _attn_kernel(cfar_ref, q_ref, k_ref, vt_ref, bias_ref, lam_ref, subg_ref, o_ref,
                 m_sc, l_sc, acc_sc, sa_sc, sb_sc, *, lam_init):
    h = pl.program_id(1)
    i = pl.program_id(2)
    tile = q_ref.shape[0]
    q = q_ref[...]
    lane = lax.broadcasted_iota(I32, q.shape, 1)
    zero = jnp.zeros_like(q)
    qm = (jnp.where(lane < A_HEAD_DIM, q, zero), jnp.where(lane >= A_HEAD_DIM, q, zero))
    m_sc[...] = jnp.full(m_sc.shape, NEG_BIG, F32)
    l_sc[...] = jnp.zeros(l_sc.shape, F32)
    acc_sc[...] = jnp.zeros(acc_sc.shape, F32)

    maps = range(2)

    def scores(j):
        kb = k_ref[pl.ds(pl.multiple_of(j * tile, tile), tile), :]
        return [_dot_nt(kb, qm[mi]) for mi in maps]

    def scores_to(dst, j):
        s = scores(j)
        for mi in maps:
            dst[mi] = s[mi]

    def consume(s, j, table, const):
        vtb = vt_ref[:, pl.ds(pl.multiple_of(j * tile, tile), tile)]
        if table is not None:
            s = [s[mi] + table for mi in maps]
        m_prev = [m_sc[mi] for mi in maps]
        m_new = [jnp.maximum(m_prev[mi], jnp.max(s[mi], axis=0, keepdims=True) + const) for mi in maps]
        p = [jnp.exp2(s[mi] - (m_new[mi] - const)) for mi in maps]
        for mi in maps:
            alpha = jnp.exp2(m_prev[mi] - m_new[mi])
            l_sc[mi] = alpha * l_sc[mi] + jnp.sum(p[mi], axis=0, keepdims=True)
            acc_sc[mi] = alpha * acc_sc[mi] + _dot(vtb, p[mi].astype(BF16))
            m_sc[mi] = m_new[mi]

    held = lambda buf: [buf[mi] for mi in maps]
    cfar = cfar_ref[h]
    nfar = jnp.maximum(i - 1, 0)

    @pl.when(i >= 1)
    def _():
        scores_to(sa_sc, 0)

    def far_pair(t, c):
        j = 2 * t
        scores_to(sb_sc, j + 1)
        consume(held(sa_sc), j, None, cfar)

        @pl.when(j + 1 < nfar)
        def _():
            scores_to(sa_sc, j + 2)
            consume(held(sb_sc), j + 1, None, cfar)

        return c

    lax.fori_loop(0, (nfar + 1) // 2, far_pair, 0)

    for parity, (cur, oth) in enumerate(((sa_sc, sb_sc), (sb_sc, sa_sc))):
        @pl.when(jnp.logical_and(i >= 1, nfar % 2 == parity))
        def _():
            scores_to(oth, i)
            consume(held(cur), i - 1, bias_ref[0, 1], 0.0)
            consume(held(oth), i, bias_ref[0, 0], 0.0)

    @pl.when(i == 0)
    def _():
        consume(scores(0), 0, bias_ref[0, 0], 0.0)

    lam_rows = lam_ref[...]
    lam = (jnp.exp(jnp.sum(lam_rows[0:1] * lam_rows[1:2], axis=-1, keepdims=True))
           - jnp.exp(jnp.sum(lam_rows[2:3] * lam_rows[3:4], axis=-1, keepdims=True)) + lam_init)
    o = acc_sc[0] / l_sc[0] - lam * (acc_sc[1] / l_sc[1])
    o = o * lax.rsqrt(jnp.mean(o * o, axis=0, keepdims=True) + SUBNORM_EPS)
    o_ref[...] = (o.T * subg_ref[...] * (1.0 - lam_init)).astype(o_ref.dtype)


def _diff_attention(q, k, vt, rel_bias, lam_rows, sub_g, lam_init, batch, seq):
    t, width = q.shape
    hw = 2 * A_HEAD_DIM
    tile = ATTN_TILE
    nq = seq // tile
    tables = _bias_tables(rel_bias, tile) * LOG2E
    cfar = rel_bias[REL_BUCKETS - 1].astype(F32) * LOG2E
    return pl.pallas_call(
        functools.partial(_attn_kernel, lam_init=lam_init),
        grid=(batch, A_HEADS, nq),
        in_specs=[
            pl.BlockSpec(memory_space=pltpu.SMEM),
            pl.BlockSpec((tile, hw), lambda b, h, i: (b * nq + i, h)),
            pl.BlockSpec((seq, hw), lambda b, h, i: (b, h)),
            pl.BlockSpec((hw, seq), lambda b, h, i: (h, b)),
            pl.BlockSpec((1, 2, tile, tile), lambda b, h, i: (h, 0, 0, 0)),
            pl.BlockSpec(lam_rows.shape, lambda b, h, i: (0, 0)),
            pl.BlockSpec((1, hw), lambda b, h, i: (0, 0)),
        ],
        out_specs=pl.BlockSpec((tile, hw), lambda b, h, i: (b * nq + i, h)),
        out_shape=jax.ShapeDtypeStruct((t, width), BF16),
        scratch_shapes=[pltpu.VMEM((2, 1, tile), F32), pltpu.VMEM((2, 1, tile), F32),
                        pltpu.VMEM((2, hw, tile), F32),
                        pltpu.VMEM((2, tile, tile), F32), pltpu.VMEM((2, tile, tile), F32)],
        compiler_params=_cparams("arbitrary", "arbitrary", "arbitrary"),
        name="diff_attn",
    )(cfar, q, k, vt, tables, lam_rows, sub_g.reshape(1, hw))


def _lru_kernel(xb_ref, gb_ref, cw_ref, vec_ref, wa_ref, wx_ref, y_ref, prev_sc, h_sc, a_sc, b_sc, hs_sc):
    ts = xb_ref.shape[0]

    @pl.when(pl.program_id(1) == 0)
    def _():
        prev_sc[...] = jnp.zeros(prev_sc.shape, F32)
        h_sc[...] = jnp.zeros(h_sc.shape, F32)

    xb = xb_ref[...]
    hist = prev_sc.shape[0]
    ext = jnp.concatenate([prev_sc[...], xb], axis=0)
    u = vec_ref[0:1, :]
    for j in range(CONV_W):
        off = hist - (CONV_W - 1) + j
        u = u + cw_ref[j:j + 1, :] * ext[off:off + ts, :]
    prev_sc[...] = xb[ts - hist:, :]
    ub = u.astype(BF16)
    r = jax.nn.sigmoid(_dot(ub, wa_ref[...]) + vec_ref[1:2, :])
    ig = jax.nn.sigmoid(_dot(ub, wx_ref[...]) + vec_ref[2:3, :])
    log_a = (-LRU_C * r) * _softplus(-vec_ref[3:4, :])
    a = jnp.exp(log_a)
    a_sc[...] = a
    b_sc[...] = jnp.sqrt(_neg_expm1(2.0 * log_a, a * a)) * (ig * u)

    def step(tt, h):
        h = a_sc[pl.ds(tt, 1), :] * h + b_sc[pl.ds(tt, 1), :]
        hs_sc[pl.ds(tt, 1), :] = h
        return h

    h_sc[...] = lax.fori_loop(0, ts, step, h_sc[...], unroll=16)
    y_ref[...] = (hs_sc[...] * _gelu_tanh(gb_ref[...])).astype(y_ref.dtype)


def _block_diag(w):
    n, c, d = w.shape
    eye = jnp.eye(n, dtype=w.dtype)
    return (w[:, :, None, :] * eye[:, None, :, None]).reshape(n * c, n * d)


def _rglru(xb, gb, conv_w, conv_b, wa, ba, wx, bx, lru_lambda, batch, seq):
    t, w = xb.shape
    ts = LRU_TILE
    nt = seq // ts
    vec = jnp.zeros((8, w), F32).at[0].set(conv_b).at[1].set(ba).at[2].set(bx).at[3].set(lru_lambda)
    row = lambda b, s: (b * nt + s, 0)
    fixed = lambda b, s: (0, 0)
    return pl.pallas_call(
        _lru_kernel,
        grid=(batch, nt),
        in_specs=[pl.BlockSpec((ts, w), row), pl.BlockSpec((ts, w), row),
                  pl.BlockSpec((CONV_W, w), fixed), pl.BlockSpec((8, w), fixed),
                  pl.BlockSpec((w, w), fixed), pl.BlockSpec((w, w), fixed)],
        out_specs=pl.BlockSpec((ts, w), row),
        out_shape=jax.ShapeDtypeStruct((t, w), BF16),
        scratch_shapes=[pltpu.VMEM((8, w), F32), pltpu.VMEM((1, w), F32),
                        pltpu.VMEM((ts, w), F32), pltpu.VMEM((ts, w), F32), pltpu.VMEM((ts, w), F32)],
        compiler_params=_cparams("arbitrary", "arbitrary"),
        name="rglru",
    )(xb, gb, conv_w, vec, _block_diag(wa).astype(BF16), _block_diag(wx).astype(BF16))


def _route(x, gf_ref, wrh_ref, wrl_ref, br_ref, tri_ref, tot_sc, size_ref, first_step):
    @pl.when(first_step)
    def _():
        tot_sc[...] = jnp.zeros(tot_sc.shape, F32)

    hn = _rms(x, gf_ref[...], NORM_EPS)
    hh, hl = _split2(hn)
    wh = wrh_ref[...]
    lg = _dot(hh, wh) + _dot(hh, wrl_ref[...]) + _dot(hl, wh) + br_ref[...]
    lane = lax.broadcasted_iota(I32, lg.shape, 1)
    big = jnp.int32(1 << 20)
    is_g = lane < MOE_GROUPS
    gl = jnp.where(is_g, lg, NEG_BIG)
    gmax = jnp.max(gl, axis=-1, keepdims=True)
    gidx = jnp.min(jnp.where(gl == gmax, lane, big), axis=-1, keepdims=True)
    gsum = jnp.sum(jnp.where(is_g, jnp.exp(gl - gmax), 0.0), axis=-1, keepdims=True)
    gprob = 1.0 / gsum
    lo = MOE_GROUPS + gidx * EXPERTS_PER_GROUP
    fmask = (lane >= lo) & (lane < lo + EXPERTS_PER_GROUP)
    fl = jnp.where(fmask, lg, NEG_BIG)
    f1 = jnp.max(fl, axis=-1, keepdims=True)
    i1 = jnp.min(jnp.where(fl == f1, lane, big), axis=-1, keepdims=True)
    fl2 = jnp.where(lane == i1, NEG_BIG, fl)
    f2 = jnp.max(fl2, axis=-1, keepdims=True)
    i2 = jnp.min(jnp.where(fl2 == f2, lane, big), axis=-1, keepdims=True)
    e2 = jnp.exp(f2 - f1)
    den = 1.0 + e2
    g1 = gprob / den
    g2 = gprob * e2 / den
    id1 = (i1 - MOE_GROUPS).astype(F32)
    id2 = (i2 - MOE_GROUPS).astype(F32)
    onehot = jnp.where((lane == i1) | (lane == i2), 1.0, 0.0)
    before = _dot(tri_ref[...], onehot.astype(BF16)) + tot_sc[...]
    r1 = jnp.sum(jnp.where(lane == i1, before, 0.0), axis=-1, keepdims=True)
    r2 = jnp.sum(jnp.where(lane == i2, before, 0.0), axis=-1, keepdims=True)
    tot = tot_sc[...] + jnp.sum(onehot, axis=0, keepdims=True)
    tot_sc[...] = tot
    size_ref[...] = jnp.broadcast_to(tot, size_ref.shape)
    vals = (g1, g2, id1, id2, r1, r2)
    out = jnp.zeros(lg.shape, F32)
    for n, v in enumerate(vals):
        out = jnp.where(lane == n, v, out)
    return out


def _router_params(wc, bc, wf, bf):
    d = wc.shape[0]
    n = MOE_GROUPS + N_EXPERTS
    w = jnp.zeros((d, LANES), F32).at[:, :MOE_GROUPS].set(wc).at[:, MOE_GROUPS:n].set(wf)
    b = jnp.zeros((1, LANES), F32).at[0, :MOE_GROUPS].set(bc).at[0, MOE_GROUPS:n].set(bf)
    hi, lo = _split2(w)
    return hi, lo, b


def _outproj_kernel(x_ref, ya_ref, yb_ref, wo_ref, gf_ref, wrh_ref, wrl_ref, br_ref, tri_ref,
                    x1_ref, x1t_ref, rt_ref, size_ref, tot_sc):
    half = ya_ref.shape[1]
    x1 = x_ref[...] + _dot(ya_ref[...], wo_ref[:half, :]) + _dot(yb_ref[...], wo_ref[half:, :])
    x1_ref[...] = x1
    _store_token_tiles(x1t_ref, x1)
    rt_ref[...] = _route(x1, gf_ref, wrh_ref, wrl_ref, br_ref, tri_ref, tot_sc, size_ref,
                         pl.program_id(0) == 0)


def _outproj(x2, ya, yb, w_out, gf, router):
    t, d = x2.shape
    half = ya.shape[1]
    tm = ROW_TILE
    row = lambda i: (i, 0)
    fixed = lambda i: (0, 0)
    wrh, wrl, br = router
    tri = _strict_lower(tm)
    return pl.pallas_call(
        _outproj_kernel,
        grid=(t // tm,),
        in_specs=[pl.BlockSpec((tm, d), row), pl.BlockSpec((tm, half), row), pl.BlockSpec((tm, half), row),
                  pl.BlockSpec((2 * half, d), fixed), pl.BlockSpec((1, d), fixed),
                  pl.BlockSpec((d, LANES), fixed), pl.BlockSpec((d, LANES), fixed),
                  pl.BlockSpec((1, LANES), fixed), pl.BlockSpec((tm, tm), fixed)],
        out_specs=[pl.BlockSpec((tm, d), row), pl.BlockSpec((tm * SUBLANES, LANES), row),
                   pl.BlockSpec((tm, LANES), row), pl.BlockSpec((SUBLANES, LANES), fixed)],
        out_shape=[jax.ShapeDtypeStruct((t, d), F32), jax.ShapeDtypeStruct((t * SUBLANES, LANES), F32),
                   jax.ShapeDtypeStruct((t, LANES), F32), jax.ShapeDtypeStruct((SUBLANES, LANES), F32)],
        scratch_shapes=[pltpu.VMEM((1, LANES), F32)],
        compiler_params=_cparams("arbitrary"),
        name="outproj_route",
    )(x2, ya, yb, w_out, gf.reshape(1, d), wrh, wrl, br, tri)


def _strict_lower(n):
    return (jnp.arange(n)[None, :] < jnp.arange(n)[:, None]).astype(BF16)


def _moe_plan(route, sizes_tile, n_tok):
    m = n_tok * MOE_TOPK
    n_blk = -(-(m + N_EXPERTS * (MOE_BLOCK - 1)) // MOE_BLOCK)
    eid = route[:, 2:4].astype(I32)
    rank = route[:, 4:6].astype(I32)
    sizes = sizes_tile[0, MOE_GROUPS:MOE_GROUPS + N_EXPERTS].astype(I32)
    padded = (sizes + MOE_BLOCK - 1) // MOE_BLOCK * MOE_BLOCK
    pend = jnp.cumsum(padded)
    dest = (pend - padded)[eid] + rank
    blk_start = jnp.arange(n_blk, dtype=I32) * MOE_BLOCK
    blk_e = jnp.minimum(jnp.sum((pend[None, :] <= blk_start[:, None]).astype(I32), axis=1),
                        N_EXPERTS - 1).astype(I32)
    n_used = (pend[-1] // MOE_BLOCK).astype(I32).reshape(1)
    tail = jnp.where(padded > 0, pend - MOE_BLOCK, -1).astype(I32)
    return blk_e, n_used, dest.astype(I32), tail, n_blk


def _dispatch_kernel(tail_ref, nused_ref, dest_hbm, xt_ref, xs_hbm,
                     idx_sm0, idx_sm1, idx_sem, zero_sc, zero_sem, stage_sc, row_sem):
    i = pl.program_id(0)
    n = pl.num_programs(0)
    tm = idx_sm0.shape[0] // MOE_TOPK
    idx_sms = (idx_sm0, idx_sm1)
    tile_rows = MOE_BLOCK * SUBLANES
    n_blk = xs_hbm.shape[0] // tile_rows

    @pl.when(i == 0)
    def _():
        zero_sc[...] = jnp.zeros(zero_sc.shape, F32)

        def clear(row, wait):
            cp = pltpu.make_async_copy(
                zero_sc, xs_hbm.at[pl.ds(pl.multiple_of(row * SUBLANES, SUBLANES), tile_rows), :], zero_sem)
            if wait:
                cp.wait()
            else:
                cp.start()

        def fill(e, wait):
            @pl.when(tail_ref[e] >= 0)
            def _():
                clear(tail_ref[e], wait)

        for wait in (False, True):
            lax.fori_loop(0, N_EXPERTS, lambda e, c, w=wait: (fill(e, w), c)[1], 0)
            lax.fori_loop(nused_ref[0], n_blk, lambda b, c, w=wait: (clear(b * MOE_BLOCK, w), c)[1], 0)

    def idx_copy(step, sl):
        return pltpu.make_async_copy(dest_hbm.at[step], idx_sms[sl], idx_sem.at[sl])

    def all_copies(sl):
        for s in range(MOE_TOPK):
            pltpu.make_async_copy(stage_sc.at[sl], xs_hbm.at[pl.ds(0, tm * SUBLANES), :], row_sem.at[sl]).wait()

    @pl.when(i == 0)
    def _():
        idx_copy(0, 0).start()

    for sl in range(2):
        @pl.when(i % 2 == sl)
        def _():
            @pl.when(i >= 2)
            def _():
                all_copies(sl)

            stage_sc[sl] = xt_ref[...]
            idx_copy(i, sl).wait()

            @pl.when(i + 1 < n)
            def _():
                idx_copy(i + 1, 1 - sl).start()

            def issue(r, c):
                src = pl.multiple_of(r * SUBLANES, SUBLANES)
                for s in range(MOE_TOPK):
                    dst = pl.multiple_of(idx_sms[sl][MOE_TOPK * r + s], SUBLANES)
                    pltpu.make_async_copy(stage_sc.at[sl, pl.ds(src, SUBLANES), :],
                                          xs_hbm.at[pl.ds(dst, SUBLANES), :],
                                          row_sem.at[sl]).start(priority=s % DMA_PRIORITIES)
                return c

            lax.fori_loop(0, tm, issue, 0, unroll=8)

            @pl.when(i == n - 1)
            def _():
                all_copies(sl)

                @pl.when(n >= 2)
                def _():
                    all_copies(1 - sl)


def _dispatch(x1t, dest, tail, n_used, n_blk):
    t = dest.shape[0]
    tm = COMBINE_TILE
    pos = (dest * SUBLANES).reshape(t // tm, MOE_TOPK * tm)
    grid_spec = pltpu.PrefetchScalarGridSpec(
        num_scalar_prefetch=2,
        grid=(t // tm,),
        in_specs=[pl.BlockSpec(memory_space=pl.ANY),
                  pl.BlockSpec((tm * SUBLANES, LANES), lambda i, tl, nu: (i, 0))],
        out_specs=pl.BlockSpec(memory_space=pl.ANY),
        scratch_shapes=[pltpu.SMEM((MOE_TOPK * tm,), I32), pltpu.SMEM((MOE_TOPK * tm,), I32),
                        pltpu.SemaphoreType.DMA((2,)), pltpu.VMEM((MOE_BLOCK * SUBLANES, LANES), F32),
                        pltpu.SemaphoreType.DMA, pltpu.VMEM((2, tm * SUBLANES, LANES), F32),
                        pltpu.SemaphoreType.DMA((2,))],
    )
    return pl.pallas_call(
        _dispatch_kernel,
        grid_spec=grid_spec,
        out_shape=jax.ShapeDtypeStruct((n_blk * MOE_BLOCK * SUBLANES, LANES), F32),
        compiler_params=_cparams("arbitrary"),
        name="moe_dispatch",
    )(tail, n_used, pos, x1t)


def _gather_ahead(idx_hbm, idx_sms, idx_sem, issue_rows, n_active):
    i = pl.program_id(0)
    n = pl.num_programs(0)

    def idx_copy(step, slot):
        return pltpu.make_async_copy(idx_hbm.at[step], idx_sms[slot], idx_sem.at[slot])

    @pl.when(i == 0)
    def _():
        first = idx_copy(0, 0)
        first.start()
        first.wait()

        @pl.when(0 < n_active)
        def _():
            issue_rows(0)

        @pl.when(1 < n)
        def _():
            idx_copy(1, 1).start()

    for slot in range(2):
        @pl.when(jnp.logical_and(i + 1 < n, (i + 1) % 2 == slot))
        def _():
            idx_copy(i + 1, slot).wait()

            @pl.when(i + 1 < n_active)
            def _():
                issue_rows(slot)

        @pl.when(jnp.logical_and(i + 2 < n, i % 2 == slot))
        def _():
            idx_copy(i + 2, slot).start()


def _expert_kernel(blk_e_ref, nused_ref, xs_ref, gf_ref, wgu_ref, wd_ref, yst_ref, wgu_sc, wd_sc):
    i = pl.program_id(0)

    @pl.when(i >= nused_ref[0])
    def _():
        yst_ref[...] = jnp.zeros(yst_ref.shape, F32)

    @pl.when(jnp.logical_or(i == 0, blk_e_ref[i] != blk_e_ref[jnp.maximum(i - 1, 0)]))
    def _():
        wgu_sc[...] = wgu_ref[0, 0].astype(BF16)
        wd_sc[...] = wd_ref[0, 0].astype(BF16)

    @pl.when(i < nused_ref[0])
    def _():
        blk = xs_ref.shape[0] // SUBLANES
        x = _load_token_tiles(xs_ref, blk)
        hn = _rms(x, gf_ref[...], NORM_EPS).astype(BF16)
        gu = _dot(hn, wgu_sc[...])
        eh = gu.shape[1] // 2
        g = gu[:, :eh]
        act = (g * jax.nn.sigmoid(g) * gu[:, eh:]).astype(BF16)
        _store_token_tiles(yst_ref, _dot(act, wd_sc[...]))


def _experts(xs, blk_e, n_used, gf, w_gu, w_down, layer):
    d = gf.shape[0]
    blk = MOE_BLOCK
    n_blk = xs.shape[0] // (blk * SUBLANES)
    eh2 = w_gu.shape[3]
    rows = lambda i, e, n: (jnp.minimum(i, n[0] - 1), 0)
    grid_spec = pltpu.PrefetchScalarGridSpec(
        num_scalar_prefetch=2,
        grid=(n_blk,),
        in_specs=[
            pl.BlockSpec((blk * SUBLANES, LANES), rows),
            pl.BlockSpec((1, d), lambda i, e, n: (0, 0)),
            pl.BlockSpec((1, 1, d, eh2), lambda i, e, n: (layer, e[i], 0, 0)),
            pl.BlockSpec((1, 1, eh2 // 2, d), lambda i, e, n: (layer, e[i], 0, 0)),
        ],
        out_specs=pl.BlockSpec((blk * SUBLANES, LANES), lambda i, e, n: (i, 0)),
        scratch_shapes=[pltpu.VMEM((d, eh2), BF16), pltpu.VMEM((eh2 // 2, d), BF16)],
    )
    return pl.pallas_call(
        _expert_kernel,
        grid_spec=grid_spec,
        out_shape=jax.ShapeDtypeStruct(xs.shape, F32),
        compiler_params=_cparams("arbitrary"),
        name="moe_experts",
    )(blk_e, n_used, xs, gf.reshape(1, d), w_gu, w_down)


def _combine_kernel(pos_hbm, yst_hbm, x1_ref, rt_ref, p_ref, gn_ref, wg_ref, wp_ref, fn_ref, o_ref,
                    idx_sm0, idx_sm1, idx_sem, ybuf, row_sem, *, final):
    slot = pl.program_id(0) % 2
    tm = x1_ref.shape[0]
    idx_sms = (idx_sm0, idx_sm1)

    def issue_rows(sl):
        def issue(r, c):
            dst = pl.multiple_of(r * SUBLANES, SUBLANES)
            for s in range(MOE_TOPK):
                src = pl.multiple_of(idx_sms[sl][MOE_TOPK * r + s], SUBLANES)
                pltpu.make_async_copy(yst_hbm.at[pl.ds(src, SUBLANES), :],
                                      ybuf.at[sl, s, pl.ds(dst, SUBLANES), :],
                                      row_sem.at[sl]).start(priority=s % DMA_PRIORITIES)
            return c

        lax.fori_loop(0, tm, issue, 0, unroll=8)

    _gather_ahead(pos_hbm, idx_sms, idx_sem, issue_rows, pl.num_programs(0))
    for s in range(MOE_TOPK):
        pltpu.make_async_copy(yst_hbm.at[pl.ds(0, tm * SUBLANES), :], ybuf.at[slot, s], row_sem.at[slot]).wait()

    rt = rt_ref[...]
    y0 = _load_token_tiles(ybuf.at[slot, 0], tm)
    y1 = _load_token_tiles(ybuf.at[slot, 1], tm)
    x2 = x1_ref[...] + (rt[:, 0:1] * y0 + rt[:, 1:2] * y1)
    gate = jax.nn.sigmoid(_dot(_rms(x2, gn_ref[...], NORM_EPS).astype(BF16), wg_ref[...]))
    x3 = x2 + _dot(p_ref[...].astype(BF16), wp_ref[...]) * gate
    if final:
        x3 = _rms(x3, fn_ref[...], NORM_EPS)
    o_ref[...] = x3


def _combine(x1, route, yst, dest, p_all, layer, ple_norm, ple_gate, ple_proj, final_norm, final):
    t, d = x1.shape
    tm = COMBINE_TILE
    pd = p_all.shape[1]
    pos = (dest * SUBLANES).reshape(t // tm, MOE_TOPK * tm)
    row = lambda i: (i, 0)
    fixed = lambda i: (0, 0)
    return pl.pallas_call(
        functools.partial(_combine_kernel, final=final),
        grid=(t // tm,),
        in_specs=[pl.BlockSpec(memory_space=pl.ANY), pl.BlockSpec(memory_space=pl.ANY),
                  pl.BlockSpec((tm, d), row), pl.BlockSpec((tm, LANES), row),
                  pl.BlockSpec((tm, pd), lambda i: (layer * (t // tm) + i, 0)),
                  pl.BlockSpec((1, d), fixed), pl.BlockSpec((d, d), fixed), pl.BlockSpec((pd, d), fixed),
                  pl.BlockSpec((1, d), fixed)],
        out_specs=pl.BlockSpec((tm, d), row),
        out_shape=jax.ShapeDtypeStruct((t, d), F32),
        scratch_shapes=[pltpu.SMEM((MOE_TOPK * tm,), I32), pltpu.SMEM((MOE_TOPK * tm,), I32),
                        pltpu.SemaphoreType.DMA((2,)),
                        pltpu.VMEM((2, MOE_TOPK, tm * SUBLANES, LANES), F32), pltpu.SemaphoreType.DMA((2,))],
        compiler_params=_cparams("arbitrary"),
        name="moe_combine_ple",
    )(pos, yst, x1, route, p_all, ple_norm.reshape(1, d), ple_gate, ple_proj, final_norm.reshape(1, d))


def _seg_sum(x, eseg_ref, eexp_ref):
    s = _dot(x.astype(BF16), eseg_ref[...])
    return _dot(s.astype(BF16), eexp_ref[...])


def _rwkv_pre_kernel(x_ref, xp_ref, gm_ref, mu_ref, vec_ref, wr_ref, wk_ref, wv_ref, w1_ref, w2_ref,
                     a1_ref, a2_ref, g1_ref, g2_ref, tri_ref, eseg_ref, eexp_ref,
                     la_ref, lr_ref, rb_ref, rk_ref, sb_ref, sk_ref, v_ref, bon_ref, g_ref, gc_ref,
                     c_sc):
    tl = x_ref.shape[0]
    gm = gm_ref[...]
    h = _rms(x_ref[...], gm, NORM_EPS)
    hp = _rms(xp_ref[...], gm, NORM_EPS)[7:8, :]
    hp = jnp.where(pl.program_id(1) == 0, 0.0, hp)
    row = lax.broadcasted_iota(I32, h.shape, 0)
    hs = jnp.where(row == 0, hp, pltpu.roll(h, 1, axis=0))
    xx = hs - h
    mix = lambda n: (h + xx * mu_ref[n:n + 1, :]).astype(BF16)
    vec = lambda n: vec_ref[n:n + 1, :]
    r = _dot(mix(0), wr_ref[...])
    wl = vec(0) + _dot(jnp.tanh(_dot(mix(1), w1_ref[...])).astype(BF16), w2_ref[...])
    lw = -math.exp(-0.5) * jax.nn.sigmoid(wl)
    k = _dot(mix(2), wk_ref[...])
    v = _dot(mix(3), wv_ref[...])
    a = jax.nn.sigmoid(vec(1) + _dot(_dot(mix(4), a1_ref[...]).astype(BF16), a2_ref[...]))
    g = _dot(jax.nn.sigmoid(_dot(mix(5), g1_ref[...])).astype(BF16), g2_ref[...])
    kk = k * vec(2)
    kk = kk * lax.rsqrt(jnp.maximum(_seg_sum(kk * kk, eseg_ref, eexp_ref), 1e-24))
    k2 = k * (1.0 + (a - 1.0) * vec(3))
    bon_ref[...] = (_seg_sum(r * k2 * vec(4), eseg_ref, eexp_ref) * v).astype(bon_ref.dtype)
    v_ref[...] = v.astype(v_ref.dtype)
    g_ref[...] = g.astype(g_ref.dtype)

    half = tri_ref.shape[1]
    tri = tri_ref[...]
    for s in range(tl // half):
        p1, p2 = _split2(lw[s * half:(s + 1) * half, :])
        c_sc[s * half:(s + 1) * half, :] = _dot(tri, p1) + _dot(tri, p2)
    c_in = c_sc[...]
    nc = tl // RWKV_CHUNK
    ends = [c_sc[(n + 1) * RWKV_CHUNK - 1:(n + 1) * RWKV_CHUNK, :] for n in range(nc)]
    c_end = jnp.concatenate([ends[n] - c_sc[n * RWKV_CHUNK:(n + 1) * RWKV_CHUNK, :] for n in range(nc)],
                            axis=0)
    gc_ref[...] = jnp.exp(jnp.concatenate(ends, axis=0))
    e_in = jnp.exp(c_in)
    e_neg = jnp.exp(-c_in)
    e_end = jnp.exp(c_end)
    b = kk * a
    la_ref[...] = (-kk * jnp.exp(c_in - lw)).astype(la_ref.dtype)
    lr_ref[...] = (r * e_in).astype(lr_ref.dtype)
    rb_ref[...] = (b * e_neg).astype(rb_ref.dtype)
    rk_ref[...] = (k2 * e_neg).astype(rk_ref.dtype)
    sb_ref[...] = (b * e_end).astype(sb_ref.dtype)
    sk_ref[...] = (k2 * e_end).astype(sk_ref.dtype)


def _head_indicator(d):
    heads = d // RWKV_HEAD
    e = (jnp.arange(d)[:, None] // RWKV_HEAD == jnp.arange(LANES)[None, :]).astype(BF16)
    del heads
    return e, e.T


def _cumsum_matrix(half):
    i = jnp.arange(half)[:, None]
    j = jnp.arange(half)[None, :]
    return (((i // RWKV_CHUNK) == (j // RWKV_CHUNK)) & (j <= i)).astype(BF16)


def _pad_cols(w, n):
    return jnp.pad(w, ((0, 0), (0, n - w.shape[1])))


def _pad_rows(w, n):
    return jnp.pad(w, ((0, n - w.shape[0]), (0, 0)))


def _rwkv_pre(x3, gm, mu, vecs, wr, wk, wv, w1, w2, a1, a2, g1, g2, batch, seq):
    t, d = x3.shape
    tl = RWKV_TILE
    nt = seq // tl
    half = 256
    lora = lambda w_in, w_out: (_pad_cols(w_in, -(-w_in.shape[1] // LANES) * LANES).astype(BF16),
                                _pad_rows(w_out, -(-w_out.shape[0] // LANES) * LANES).astype(BF16))
    w1p, w2p = lora(w1, w2)
    a1p, a2p = lora(a1, a2)
    g1p, g2p = lora(g1, g2)
    eseg, eexp = _head_indicator(d)
    tri = _cumsum_matrix(half)
    row = lambda b, s: (b * nt + s, 0)
    fixed = lambda b, s: (0, 0)
    full = lambda a: pl.BlockSpec(a.shape, fixed)
    prev = lambda b, s: (jnp.maximum((b * nt + s) * (tl // 8) - 1, 0), 0)
    weights = (wr.astype(BF16), wk.astype(BF16), wv.astype(BF16), w1p, w2p, a1p, a2p, g1p, g2p, tri, eseg, eexp)
    outs = pl.pallas_call(
        _rwkv_pre_kernel,
        grid=(batch, nt),
        in_specs=[pl.BlockSpec((tl, d), row), pl.BlockSpec((8, d), prev), pl.BlockSpec((1, d), fixed),
                  full(mu), full(vecs)] + [full(w) for w in weights],
        out_specs=[pl.BlockSpec((tl, d), row)] * 9 + [pl.BlockSpec((tl // RWKV_CHUNK, d), row)],
        out_shape=[jax.ShapeDtypeStruct((t, d), BF16)] * 9 + [jax.ShapeDtypeStruct((t // RWKV_CHUNK, d), F32)],
        scratch_shapes=[pltpu.VMEM((tl, d), F32)],
        compiler_params=_cparams("arbitrary", "arbitrary"),
        name="rwkv_pre",
    )(x3, x3, gm.reshape(1, d), mu, vecs, *weights)
    return outs


def _rwkv_scan_kernel(la_ref, lr_ref, rb_ref, rk_ref, sb_ref, sk_ref, v_ref, gc_ref, bon_ref, g_ref, x_ref,
                      vec_ref, wo_ref, eseg_ref, eexp_ref, gf_ref, wrh_ref, wrl_ref, br_ref, tri_ref,
                      x4_ref, x4t_ref, rt_ref, size_ref, s_sc, y_sc, tot_sc):
    tl, d = x_ref.shape
    ch = RWKV_CHUNK
    pw = 2 * RWKV_HEAD
    npair = d // pw

    @pl.when(pl.program_id(1) == 0)
    def _():
        s_sc[...] = jnp.zeros(s_sc.shape, F32)

    ri = lax.broadcasted_iota(I32, (pw, pw), 0)
    ci = lax.broadcasted_iota(I32, (pw, pw), 1)
    same_half = (ri < RWKV_HEAD) == (ci < RWKV_HEAD)
    eye = (ri == ci).astype(F32)
    rt_ = lax.broadcasted_iota(I32, (ch, pw), 0)
    ct_ = lax.broadcasted_iota(I32, (ch, pw), 1)
    first = ct_ < RWKV_HEAD
    strict = rt_ > (ct_ & (RWKV_HEAD - 1))
    incl = rt_ >= (ct_ & (RWKV_HEAD - 1))
    m_s0, m_s1 = strict & first, strict & ~first
    m_i0, m_i1 = incl & first, incl & ~first
    first_full = ci < RWKV_HEAD
    zf = jnp.zeros((ch, pw), F32)
    gc_row = lax.broadcasted_iota(I32, (tl // ch, pw), 0)

    group = RWKV_CHUNKS_PER_ITER

    def chunks(cg, carry):
        pairs = range(npair)
        cols = [slice(p * pw, (p + 1) * pw) for p in pairs]
        cidx = [cg * group + k for k in range(group)]
        rows = [pl.ds(pl.multiple_of(c * ch, ch), ch) for c in cidx]
        units = [(k, p) for k in range(group) for p in pairs]
        un = range(len(units))
        cat0 = lambda a, b: jnp.concatenate([a, b], axis=0)
        ld = lambda ref, n: ref[rows[units[n][0]], cols[units[n][1]]]
        lcat = [cat0(ld(la_ref, n), ld(lr_ref, n)) for n in un]
        rb = [ld(rb_ref, n) for n in un]
        rk = [ld(rk_ref, n) for n in un]
        vv = [ld(v_ref, n) for n in un]
        zb = jnp.zeros_like(lcat[0])
        a0 = [_dot_nt(jnp.where(first_full, lcat[n], zb), cat0(rb[n], rk[n])) for n in un]
        a1 = [_dot_nt(jnp.where(first_full, zb, lcat[n]), cat0(rk[n], rb[n])) for n in un]
        pm = [cat0(jnp.where(m_s0, a0[n][:ch], zf), jnp.where(m_s1, a1[n][:ch], zf)) for n in un]
        aak = [cat0(jnp.where(m_s1, a0[n][:ch], zf), jnp.where(m_s0, a1[n][:ch], zf)).astype(BF16)
               for n in un]
        arbk = [jnp.concatenate([cat0(jnp.where(m_i0, a0[n][ch:], zf), jnp.where(m_i1, a1[n][ch:], zf)),
                                 cat0(jnp.where(m_i1, a0[n][ch:], zf), jnp.where(m_i0, a1[n][ch:], zf))],
                                axis=1).astype(BF16) for n in un]
        v2 = [cat0(vv[n], vv[n]) for n in un]
        av = [_dot(aak[n], v2[n]) for n in un]
        tm = [eye + pm[n] for n in un]
        qm = pm
        for _ in range(5):
            qb = [q.astype(BF16) for q in qm]
            qm = [_dot(qb[n], qb[n]) for n in un]
            tm = [tm[n] + _dot(tm[n].astype(BF16), qm[n].astype(BF16)) for n in un]
        tmb = [t.astype(BF16) for t in tm]
        for k in range(group):
            nk = [k * npair + p for p in pairs]
            st = [s_sc[p] for p in pairs]
            w12 = [_dot_nt(lcat[n], st[p].astype(BF16)) for p, n in zip(pairs, nk)]
            rhs = [cat0(w12[p][:ch], w12[p][:ch]) + av[n] for p, n in zip(pairs, nk)]
            ust = [_dot(tmb[n], rhs[p].astype(BF16)) for p, n in zip(pairs, nk)]
            yst = [cat0(w12[p][ch:], w12[p][ch:]) + _dot(arbk[n], cat0(ust[p].astype(BF16), v2[n]))
                   for p, n in zip(pairs, nk)]
            for p in pairs:
                y_sc[rows[k], cols[p]] = jnp.where(first, yst[p][:ch], yst[p][ch:])
            u = [jnp.where(first, ust[p][:ch], ust[p][ch:]).astype(BF16) for p in pairs]
            ds_ = [_dot_tn(cat0(u[p], vv[n]), cat0(sb_ref[rows[k], cols[p]], sk_ref[rows[k], cols[p]]))
                   for p, n in zip(pairs, nk)]
            for p in pairs:
                gcr = jnp.sum(jnp.where(gc_row == cidx[k], gc_ref[:, cols[p]], 0.0), axis=0, keepdims=True)
                s_sc[p] = jnp.where(same_half, st[p] * gcr + ds_[p], 0.0)
        return carry

    lax.fori_loop(0, tl // (ch * group), chunks, 0)

    y = y_sc[...]
    inv_n = 1.0 / RWKV_HEAD
    mean = _seg_sum(y, eseg_ref, eexp_ref) * inv_n
    dlt = y - mean
    var = _seg_sum(dlt * dlt, eseg_ref, eexp_ref) * inv_n
    yn = dlt * lax.rsqrt(var + RWKV_GN_EPS) * vec_ref[5:6, :] + vec_ref[6:7, :]
    z = (yn + bon_ref[...].astype(F32)) * g_ref[...].astype(F32)
    x4 = x_ref[...] + _dot(z.astype(BF16), wo_ref[...])
    x4_ref[...] = x4
    _store_token_tiles(x4t_ref, x4)
    first = jnp.logical_and(pl.program_id(0) == 0, pl.program_id(1) == 0)
    rt_ref[...] = _route(x4, gf_ref, wrh_ref, wrl_ref, br_ref, tri_ref, tot_sc, size_ref, first)


def _rwkv_scan(pre, x3, vecs, wo, gf, router, batch, seq):
    la, lr, rb, rk, sb, sk, v, bon, g, gc = pre
    t, d = x3.shape
    tl = RWKV_TILE
    nt = seq // tl
    eseg, eexp = _head_indicator(d)
    wrh, wrl, br = router
    row = lambda b, s: (b * nt + s, 0)
    fixed = lambda b, s: (0, 0)
    big = pl.BlockSpec((tl, d), row)
    full = lambda a: pl.BlockSpec(a.shape, fixed)
    pw = 2 * RWKV_HEAD
    tri = _strict_lower(tl)
    return pl.pallas_call(
        _rwkv_scan_kernel,
        grid=(batch, nt),
        in_specs=[big] * 7 + [pl.BlockSpec((tl // RWKV_CHUNK, d), row), big, big, big,
                              full(vecs), full(wo), full(eseg), full(eexp),
                              pl.BlockSpec((1, d), fixed), full(wrh), full(wrl), full(br), full(tri)],
        out_specs=[big, pl.BlockSpec((tl * SUBLANES, LANES), row), pl.BlockSpec((tl, LANES), row),
                   pl.BlockSpec((SUBLANES, LANES), fixed)],
        out_shape=[jax.ShapeDtypeStruct((t, d), F32), jax.ShapeDtypeStruct((t * SUBLANES, LANES), F32),
                   jax.ShapeDtypeStruct((t, LANES), F32), jax.ShapeDtypeStruct((SUBLANES, LANES), F32)],
        scratch_shapes=[pltpu.VMEM((d // pw, pw, pw), F32), pltpu.VMEM((tl, d), F32),
                        pltpu.VMEM((1, LANES), F32)],
        compiler_params=_cparams("arbitrary", "arbitrary"),
        name="rwkv_scan_out",
    )(la, lr, rb, rk, sb, sk, v, gc, bon, g, x3, vecs, wo, eseg, eexp, gf.reshape(1, d), wrh, wrl, br, tri)


def _moe_and_embed(x1, x1t, route, sizes, i, p, norm_ffn, moe_w_gu, moe_w_down, ple_norm, ple_gate, ple_proj,
                   final_norm, final):
    t = x1.shape[0]
    blk_e, n_used, dest, tail, n_blk = _moe_plan(route, sizes, t)
    xs = _dispatch(x1t, dest, tail, n_used, n_blk)
    yst = _experts(xs, blk_e, n_used, norm_ffn[i], moe_w_gu, moe_w_down, i)
    p_all = p.reshape(-1, p.shape[-1])
    return _combine(x1, route, yst, dest, p_all, i, ple_norm[i], ple_gate[i].astype(BF16),
                    ple_proj[i].astype(BF16), final_norm, final)


def kernel(x, p, rel_bias, norm_mix, ab_w_in, ab_w_out, lam_q1, lam_k1, lam_q2, lam_k2, sub_g, conv_w, conv_b, lru_wa, lru_ba, lru_wx, lru_bx, lru_lambda, rwkv_mu, rwkv_wr, rwkv_wk, rwkv_wv, rwkv_wo, rwkv_w0, rwkv_w1, rwkv_w2, rwkv_a0, rwkv_a1, rwkv_a2, rwkv_g1, rwkv_g2, rwkv_kk, rwkv_ka, rwkv_rk, rwkv_ln_g, rwkv_ln_b, norm_ffn, moe_wc, moe_bc, moe_wf, moe_bf, moe_w_gu, moe_w_down, ple_norm, ple_gate, ple_proj, final_norm):
    batch, seq, d = x.shape
    t = batch * seq
    depth = norm_mix.shape[0]
    assert depth == 2 and seq % RWKV_TILE == 0 and seq % LRU_TILE == 0 and t % ROW_TILE == 0
    assert d == SUBLANES * LANES and seq % ATTN_TILE == 0 and t % COMBINE_TILE == 0
    xs = x.reshape(t, d)
    for i in range(depth):
        j = i // 2
        router = _router_params(moe_wc[i], moe_bc[i], moe_wf[i], moe_bf[i])
        if i % 2 == 0:
            lam_init = 0.8 - 0.6 * math.exp(-0.3 * i)
            q, k, vt, xb, gb = _inproj(xs, norm_mix[i], ab_w_in[j].astype(BF16))
            lam_rows = jnp.stack([lam_q1[j], lam_k1[j], lam_q2[j], lam_k2[j]]).astype(F32)
            ya = _diff_attention(q, k, vt, rel_bias, lam_rows, sub_g[j], lam_init, batch, seq)
            yb = _rglru(xb, gb, conv_w[j], conv_b[j], lru_wa[j], lru_ba[j], lru_wx[j], lru_bx[j],
                        lru_lambda[j], batch, seq)
            x1, x1t, route, sizes = _outproj(xs, ya, yb, ab_w_out[j].astype(BF16), norm_ffn[i], router)
        else:
            vecs = jnp.stack([rwkv_w0[j], rwkv_a0[j], rwkv_kk[j], rwkv_ka[j], rwkv_rk[j].reshape(-1),
                              rwkv_ln_g[j], rwkv_ln_b[j], jnp.zeros((d,), F32)]).astype(F32)
            pre = _rwkv_pre(xs, norm_mix[i], rwkv_mu[j], vecs, rwkv_wr[j], rwkv_wk[j], rwkv_wv[j],
                            rwkv_w1[j], rwkv_w2[j], rwkv_a1[j], rwkv_a2[j], rwkv_g1[j], rwkv_g2[j],
                            batch, seq)
            x1, x1t, route, sizes = _rwkv_scan(pre, xs, vecs, rwkv_wo[j].astype(BF16), norm_ffn[i], router,
                                               batch, seq)
        xs = _moe_and_embed(x1, x1t, route, sizes, i, p, norm_ffn, moe_w_gu, moe_w_down, ple_norm, ple_gate,
                            ple_proj, final_norm, final=(i == depth - 1))
    return xs.reshape(batch, seq, d)
```

```python
import functools
import math

import jax
import jax.numpy as jnp
from jax import lax
from jax.experimental import pallas as pl
from jax.experimental.pallas import tpu as pltpu

F32, BF16, I32 = jnp.float32, jnp.bfloat16, jnp.int32

A_HEADS = 4
A_HEAD_DIM = 64
A_SCALE = A_HEAD_DIM ** -0.5
LOG2E = math.log2(math.e)
REL_BUCKETS = 32
REL_MAX_DIST = 128
LRU_BLOCKS = 8
CONV_W = 4
LRU_C = 8.0
RWKV_HEAD = 64
RWKV_GN_EPS = 64e-5
MOE_GROUPS = 4
EXPERTS_PER_GROUP = 8
N_EXPERTS = MOE_GROUPS * EXPERTS_PER_GROUP
MOE_TOPK = 2
NORM_EPS = 1e-6
SUBNORM_EPS = 1e-5

LANES = 128
VMEM_LIMIT = 56 * 1024 * 1024
ROW_TILE = 512
ATTN_TILE = 512
LRU_TILE = 512
RWKV_CHUNK = 64
RWKV_TILE = 512
RWKV_CHUNKS_PER_ITER = 2
MOE_BLOCK = 512
COMBINE_TILE = 256
NEG_BIG = -1e30


def _cparams(*sem, flags=None):
    return pltpu.CompilerParams(dimension_semantics=sem, vmem_limit_bytes=VMEM_LIMIT, flags=flags)


def _rms(x, g, eps):
    return x * lax.rsqrt(jnp.mean(x * x, axis=-1, keepdims=True) + eps) * g


def _dot(a, b):
    return jnp.dot(a, b, preferred_element_type=F32)


def _dot_nt(a, b):
    return lax.dot_general(a, b, (((1,), (1,)), ((), ())), preferred_element_type=F32)


def _dot_tn(a, b):
    return lax.dot_general(a, b, (((0,), (0,)), ((), ())), preferred_element_type=F32)


def _split2(x):
    hi = x.astype(BF16)
    lo = (x - hi.astype(F32)).astype(BF16)
    return hi, lo


def _softplus(x):
    return jnp.maximum(x, 0.0) + jnp.log1p(jnp.exp(-jnp.abs(x)))


def _neg_expm1(y, exp_y):
    poly = 1.0 + y * (1.0 / 5.0)
    for d in (4.0, 3.0, 2.0):
        poly = 1.0 + (y * (1.0 / d)) * poly
    return jnp.where(y > -1.0 / 16.0, -y * poly, 1.0 - exp_y)


SUBLANES = 8
DMA_PRIORITIES = 2


def _store_token_tiles(ref, x):
    rows = x.shape[0]
    for s in range(x.shape[1] // LANES):
        ref[pl.ds(s, rows, stride=SUBLANES), :] = x[:, s * LANES:(s + 1) * LANES]


def _load_token_tiles(ref, rows):
    return jnp.concatenate([ref[pl.ds(s, rows, stride=SUBLANES), :] for s in range(SUBLANES)], axis=1)


def _gelu_tanh(x):
    c = math.sqrt(2.0 / math.pi)
    return 0.5 * x * (1.0 + jnp.tanh(c * (x + 0.044715 * (x * x * x))))


def _inproj_kernel(x_ref, g_ref, w_ref, q_ref, k_ref, vt_ref, xb_ref, gb_ref):
    hn = _rms(x_ref[...], g_ref[...], NORM_EPS).astype(BF16)
    width = q_ref.shape[1]
    for c, o_ref in enumerate((q_ref, k_ref, vt_ref, xb_ref, gb_ref)):
        r = _dot(hn, w_ref[:, c * width:(c + 1) * width])
        if c == 0:
            r = r * (A_SCALE * LOG2E)
        if c == 2:
            r = r.T
        o_ref[...] = r.astype(o_ref.dtype)


def _inproj(x2, g, w_in):
    t, d = x2.shape
    width = w_in.shape[1] // 5
    tm = ROW_TILE
    row = lambda i: (i, 0)
    fixed = lambda i: (0, 0)
    rows = pl.BlockSpec((tm, width), row)
    return pl.pallas_call(
        _inproj_kernel,
        grid=(t // tm,),
        in_specs=[pl.BlockSpec((tm, d), row), pl.BlockSpec((1, d), fixed),
                  pl.BlockSpec(w_in.shape, fixed)],
        out_specs=[rows, rows, pl.BlockSpec((width, tm), lambda i: (0, i)), rows, rows],
        out_shape=[jax.ShapeDtypeStruct((t, width), BF16)] * 2 + [jax.ShapeDtypeStruct((width, t), BF16)]
        + [jax.ShapeDtypeStruct((t, width), F32)] * 2,
        compiler_params=_cparams("arbitrary"),
        name="inproj",
    )(x2, g.reshape(1, d), w_in)


def _t5_bucket(rel):
    n = jnp.maximum(rel, 0)
    max_exact = REL_BUCKETS // 2
    large = max_exact + (jnp.log(jnp.maximum(n, 1).astype(F32) / max_exact)
                         / math.log(REL_MAX_DIST / max_exact)
                         * (REL_BUCKETS - max_exact)).astype(I32)
    large = jnp.minimum(large, REL_BUCKETS - 1)
    return jnp.where(n < max_exact, n, large)


def _bias_tables(rel_bias, tile):
    span = 2 * tile

    def toeplitz(first_rel):
        rel = jnp.arange(span - 1) + first_rel
        vals = jnp.where((rel >= 0)[:, None], rel_bias[_t5_bucket(rel)].astype(F32), NEG_BIG)
        w = jnp.concatenate([vals, jnp.zeros((1, vals.shape[1]), F32)], axis=0).T
        skew = jnp.tile(w, (1, tile))[:, :tile * (span - 1)].reshape(-1, tile, span - 1)
        return skew[:, :, tile - 1:]

    return jnp.stack([toeplitz(1 - tile), toeplitz(1)], axis=1)


def _attn_kernel(cfar_ref, q_ref, k_ref, vt_ref, bias_ref, lam_ref, subg_ref, o_ref,
                 m_sc, l_sc, acc_sc, sa_sc, sb_sc, *, lam_init):
    h = pl.program_id(1)
    i = pl.program_id(2)
    tile = q_ref.shape[0]
    q = q_ref[...]
    lane = lax.broadcasted_iota(I32, q.shape, 1)
    zero = jnp.zeros_like(q)
    qm = (jnp.where(lane < A_HEAD_DIM, q, zero), jnp.where(lane >= A_HEAD_DIM, q, zero))
    m_sc[...] = jnp.full(m_sc.shape, NEG_BIG, F32)
    l_sc[...] = jnp.zeros(l_sc.shape, F32)
    acc_sc[...] = jnp.zeros(acc_sc.shape, F32)

    maps = range(2)

    def scores(j):
        kb = k_ref[pl.ds(pl.multiple_of(j * tile, tile), tile), :]
        return [_dot_nt(kb, qm[mi]) for mi in maps]

    def scores_to(dst, j):
        s = scores(j)
        for mi in maps:
            dst[mi] = s[mi]

    def consume(s, j, table, const):
        vtb = vt_ref[:, pl.ds(pl.multiple_of(j * tile, tile), tile)]
        if table is not None:
            s = [s[mi] + table for mi in maps]
        half = tile // 2
        for mi in maps:
            for qs in (slice(0, half), slice(half, tile)):
                sq = s[mi][:, qs]
                m_prev = m_sc[mi, :, qs]
                m_new = jnp.maximum(m_prev, jnp.max(sq, axis=0, keepdims=True) + const)
                p = jnp.exp2(sq - (m_new - const))
                alpha = jnp.exp2(m_prev - m_new)
                l_sc[mi, :, qs] = alpha * l_sc[mi, :, qs] + jnp.sum(p, axis=0, keepdims=True)
                acc_sc[mi, :, qs] = alpha * acc_sc[mi, :, qs] + _dot(vtb, p.astype(BF16))
                m_sc[mi, :, qs] = m_new

    held = lambda buf: [buf[mi] for mi in maps]
    cfar = cfar_ref[h]
    nfar = jnp.maximum(i - 1, 0)

    @pl.when(i >= 1)
    def _():
        scores_to(sa_sc, 0)

    def far_pair(t, c):
        j = 2 * t
        scores_to(sb_sc, j + 1)
        consume(held(sa_sc), j, None, cfar)

        @pl.when(j + 1 < nfar)
        def _():
            scores_to(sa_sc, j + 2)
            consume(held(sb_sc), j + 1, None, cfar)

        return c

    lax.fori_loop(0, (nfar + 1) // 2, far_pair, 0)

    for parity, (cur, oth) in enumerate(((sa_sc, sb_sc), (sb_sc, sa_sc))):
        @pl.when(jnp.logical_and(i >= 1, nfar % 2 == parity))
        def _():
            scores_to(oth, i)
            consume(held(cur), i - 1, bias_ref[0, 1], 0.0)
            consume(held(oth), i, bias_ref[0, 0], 0.0)

    @pl.when(i == 0)
    def _():
        consume(scores(0), 0, bias_ref[0, 0], 0.0)

    lam_rows = lam_ref[...]
    lam = (jnp.exp(jnp.sum(lam_rows[0:1] * lam_rows[1:2], axis=-1, keepdims=True))
           - jnp.exp(jnp.sum(lam_rows[2:3] * lam_rows[3:4], axis=-1, keepdims=True)) + lam_init)
    o = acc_sc[0] / l_sc[0] - lam * (acc_sc[1] / l_sc[1])
    o = o * lax.rsqrt(jnp.mean(o * o, axis=0, keepdims=True) + SUBNORM_EPS)
    o_ref[...] = (o.T * subg_ref[...] * (1.0 - lam_init)).astype(o_ref.dtype)


def _diff_attention(q, k, vt, rel_bias, lam_rows, sub_g, lam_init, batch, seq):
    t, width = q.shape
    hw = 2 * A_HEAD_DIM
    tile = ATTN_TILE
    nq = seq // tile
    tables = _bias_tables(rel_bias, tile) * LOG2E
    cfar = rel_bias[REL_BUCKETS - 1].astype(F32) * LOG2E
    return pl.pallas_call(
        functools.partial(_attn_kernel, lam_init=lam_init),
        grid=(batch, A_HEADS, nq),
        in_specs=[
            pl.BlockSpec(memory_space=pltpu.SMEM),
            pl.BlockSpec((tile, hw), lambda b, h, i: (b * nq + i, h)),
            pl.BlockSpec((seq, hw), lambda b, h, i: (b, h)),
            pl.BlockSpec((hw, seq), lambda b, h, i: (h, b)),
            pl.BlockSpec((1, 2, tile, tile), lambda b, h, i: (h, 0, 0, 0)),
            pl.BlockSpec(lam_rows.shape, lambda b, h, i: (0, 0)),
            pl.BlockSpec((1, hw), lambda b, h, i: (0, 0)),
        ],
        out_specs=pl.BlockSpec((tile, hw), lambda b, h, i: (b * nq + i, h)),
        out_shape=jax.ShapeDtypeStruct((t, width), BF16),
        scratch_shapes=[pltpu.VMEM((2, 1, tile), F32), pltpu.VMEM((2, 1, tile), F32),
                        pltpu.VMEM((2, hw, tile), F32),
                        pltpu.VMEM((2, tile, tile), F32), pltpu.VMEM((2, tile, tile), F32)],
        compiler_params=_cparams("arbitrary", "arbitrary", "arbitrary"),
        name="diff_attn",
    )(cfar, q, k, vt, tables, lam_rows, sub_g.reshape(1, hw))


def _lru_kernel(xb_ref, gb_ref, cw_ref, vec_ref, wa_ref, wx_ref, y_ref, prev_sc, h_sc, a_sc, b_sc, hs_sc):
    ts = xb_ref.shape[0]

    @pl.when(pl.program_id(1) == 0)
    def _():
        prev_sc[...] = jnp.zeros(prev_sc.shape, F32)
        h_sc[...] = jnp.zeros(h_sc.shape, F32)

    xb = xb_ref[...]
    hist = prev_sc.shape[0]
    ext = jnp.concatenate([prev_sc[...], xb], axis=0)
    u = vec_ref[0:1, :]
    for j in range(CONV_W):
        off = hist - (CONV_W - 1) + j
        u = u + cw_ref[j:j + 1, :] * ext[off:off + ts, :]
    prev_sc[...] = xb[ts - hist:, :]
    ub = u.astype(BF16)
    r = jax.nn.sigmoid(_dot(ub, wa_ref[...]) + vec_ref[1:2, :])
    ig = jax.nn.sigmoid(_dot(ub, wx_ref[...]) + vec_ref[2:3, :])
    log_a = (-LRU_C * r) * _softplus(-vec_ref[3:4, :])
    a = jnp.exp(log_a)
    a_sc[...] = a
    b_sc[...] = jnp.sqrt(_neg_expm1(2.0 * log_a, a * a)) * (ig * u)

    def step(tt, h):
        h = a_sc[pl.ds(tt, 1), :] * h + b_sc[pl.ds(tt, 1), :]
        hs_sc[pl.ds(tt, 1), :] = h
        return h

    h_sc[...] = lax.fori_loop(0, ts, step, h_sc[...], unroll=16)
    y_ref[...] = (hs_sc[...] * _gelu_tanh(gb_ref[...])).astype(y_ref.dtype)


def _block_diag(w):
    n, c, d = w.shape
    eye = jnp.eye(n, dtype=w.dtype)
    return (w[:, :, None, :] * eye[:, None, :, None]).reshape(n * c, n * d)


def _rglru(xb, gb, conv_w, conv_b, wa, ba, wx, bx, lru_lambda, batch, seq):
    t, w = xb.shape
    ts = LRU_TILE
    nt = seq // ts
    vec = jnp.zeros((8, w), F32).at[0].set(conv_b).at[1].set(ba).at[2].set(bx).at[3].set(lru_lambda)
    row = lambda b, s: (b * nt + s, 0)
    fixed = lambda b, s: (0, 0)
    return pl.pallas_call(
        _lru_kernel,
        grid=(batch, nt),
        in_specs=[pl.BlockSpec((ts, w), row), pl.BlockSpec((ts, w), row),
                  pl.BlockSpec((CONV_W, w), fixed), pl.BlockSpec((8, w), fixed),
                  pl.BlockSpec((w, w), fixed), pl.BlockSpec((w, w), fixed)],
        out_specs=pl.BlockSpec((ts, w), row),
        out_shape=jax.ShapeDtypeStruct((t, w), BF16),
        scratch_shapes=[pltpu.VMEM((8, w), F32), pltpu.VMEM((1, w), F32),
                        pltpu.VMEM((ts, w), F32), pltpu.VMEM((ts, w), F32), pltpu.VMEM((ts, w), F32)],
        compiler_params=_cparams("arbitrary", "arbitrary"),
        name="rglru",
    )(xb, gb, conv_w, vec, _block_diag(wa).astype(BF16), _block_diag(wx).astype(BF16))


def _route(x, gf_ref, wrh_ref, wrl_ref, br_ref, tri_ref, tot_sc, size_ref, first_step):
    @pl.when(first_step)
    def _():
        tot_sc[...] = jnp.zeros(tot_sc.shape, F32)

    hn = _rms(x, gf_ref[...], NORM_EPS)
    hh, hl = _split2(hn)
    wh = wrh_ref[...]
    lg = _dot(hh, wh) + _dot(hh, wrl_ref[...]) + _dot(hl, wh) + br_ref[...]
    lane = lax.broadcasted_iota(I32, lg.shape, 1)
    big = jnp.int32(1 << 20)
    is_g = lane < MOE_GROUPS
    gl = jnp.where(is_g, lg, NEG_BIG)
    gmax = jnp.max(gl, axis=-1, keepdims=True)
    gidx = jnp.min(jnp.where(gl == gmax, lane, big), axis=-1, keepdims=True)
    gsum = jnp.sum(jnp.where(is_g, jnp.exp(gl - gmax), 0.0), axis=-1, keepdims=True)
    gprob = 1.0 / gsum
    lo = MOE_GROUPS + gidx * EXPERTS_PER_GROUP
    fmask = (lane >= lo) & (lane < lo + EXPERTS_PER_GROUP)
    fl = jnp.where(fmask, lg, NEG_BIG)
    f1 = jnp.max(fl, axis=-1, keepdims=True)
    i1 = jnp.min(jnp.where(fl == f1, lane, big), axis=-1, keepdims=True)
    fl2 = jnp.where(lane == i1, NEG_BIG, fl)
    f2 = jnp.max(fl2, axis=-1, keepdims=True)
    i2 = jnp.min(jnp.where(fl2 == f2, lane, big), axis=-1, keepdims=True)
    e2 = jnp.exp(f2 - f1)
    den = 1.0 + e2
    g1 = gprob / den
    g2 = gprob * e2 / den
    id1 = (i1 - MOE_GROUPS).astype(F32)
    id2 = (i2 - MOE_GROUPS).astype(F32)
    onehot = jnp.where((lane == i1) | (lane == i2), 1.0, 0.0)
    before = _dot(tri_ref[...], onehot.astype(BF16)) + tot_sc[...]
    r1 = jnp.sum(jnp.where(lane == i1, before, 0.0), axis=-1, keepdims=True)
    r2 = jnp.sum(jnp.where(lane == i2, before, 0.0), axis=-1, keepdims=True)
    tot = tot_sc[...] + jnp.sum(onehot, axis=0, keepdims=True)
    tot_sc[...] = tot
    size_ref[...] = jnp.broadcast_to(tot, size_ref.shape)
    vals = (g1, g2, id1, id2, r1, r2)
    out = jnp.zeros(lg.shape, F32)
    for n, v in enumerate(vals):
        out = jnp.where(lane == n, v, out)
    return out


def _router_params(wc, bc, wf, bf):
    d = wc.shape[0]
    n = MOE_GROUPS + N_EXPERTS
    w = jnp.zeros((d, LANES), F32).at[:, :MOE_GROUPS].set(wc).at[:, MOE_GROUPS:n].set(wf)
    b = jnp.zeros((1, LANES), F32).at[0, :MOE_GROUPS].set(bc).at[0, MOE_GROUPS:n].set(bf)
    hi, lo = _split2(w)
    return hi, lo, b


def _outproj_kernel(x_ref, ya_ref, yb_ref, wo_ref, gf_ref, wrh_ref, wrl_ref, br_ref, tri_ref,
                    x1_ref, x1t_ref, rt_ref, rtt_ref, size_ref, tot_sc):
    half = ya_ref.shape[1]
    x1 = x_ref[...] + _dot(ya_ref[...], wo_ref[:half, :]) + _dot(yb_ref[...], wo_ref[half:, :])
    x1_ref[...] = x1
    _store_token_tiles(x1t_ref, x1)
    rt = _route(x1, gf_ref, wrh_ref, wrl_ref, br_ref, tri_ref, tot_sc, size_ref, pl.program_id(0) == 0)
    rt_ref[...] = rt
    rtt_ref[...] = rt.T[:SUBLANES, :]


def _outproj(x2, ya, yb, w_out, gf, router):
    t, d = x2.shape
    half = ya.shape[1]
    tm = ROW_TILE
    row = lambda i: (i, 0)
    fixed = lambda i: (0, 0)
    wrh, wrl, br = router
    tri = _strict_lower(tm)
    return pl.pallas_call(
        _outproj_kernel,
        grid=(t // tm,),
        in_specs=[pl.BlockSpec((tm, d), row), pl.BlockSpec((tm, half), row), pl.BlockSpec((tm, half), row),
                  pl.BlockSpec((2 * half, d), fixed), pl.BlockSpec((1, d), fixed),
                  pl.BlockSpec((d, LANES), fixed), pl.BlockSpec((d, LANES), fixed),
                  pl.BlockSpec((1, LANES), fixed), pl.BlockSpec((tm, tm), fixed)],
        out_specs=[pl.BlockSpec((tm, d), row), pl.BlockSpec((tm * SUBLANES, LANES), row),
                   pl.BlockSpec((tm, LANES), row), pl.BlockSpec((SUBLANES, tm), lambda i: (0, i)),
                   pl.BlockSpec((SUBLANES, LANES), fixed)],
        out_shape=[jax.ShapeDtypeStruct((t, d), F32), jax.ShapeDtypeStruct((t * SUBLANES, LANES), F32),
                   jax.ShapeDtypeStruct((t, LANES), F32), jax.ShapeDtypeStruct((SUBLANES, t), F32),
                   jax.ShapeDtypeStruct((SUBLANES, LANES), F32)],
        scratch_shapes=[pltpu.VMEM((1, LANES), F32)],
        compiler_params=_cparams("arbitrary"),
        name="outproj_route",
    )(x2, ya, yb, w_out, gf.reshape(1, d), wrh, wrl, br, tri)


def _strict_lower(n):
    return (jnp.arange(n)[None, :] < jnp.arange(n)[:, None]).astype(BF16)


def _moe_plan(route_t, sizes_tile, n_tok):
    m = n_tok * MOE_TOPK
    n_blk = -(-(m + N_EXPERTS * (MOE_BLOCK - 1)) // MOE_BLOCK)
    eid = route_t[2:4].astype(I32).T
    rank = route_t[4:6].astype(I32).T
    sizes = sizes_tile[0, MOE_GROUPS:MOE_GROUPS + N_EXPERTS].astype(I32)
    padded = (sizes + MOE_BLOCK - 1) // MOE_BLOCK * MOE_BLOCK
    pend = jnp.cumsum(padded)
    dest = (pend - padded)[eid] + rank
    blk_start = jnp.arange(n_blk, dtype=I32) * MOE_BLOCK
    blk_e = jnp.minimum(jnp.sum((pend[None, :] <= blk_start[:, None]).astype(I32), axis=1),
                        N_EXPERTS - 1).astype(I32)
    n_used = (pend[-1] // MOE_BLOCK).astype(I32).reshape(1)
    tail = jnp.where(padded > 0, pend - MOE_BLOCK, -1).astype(I32)
    return blk_e, n_used, dest.astype(I32), tail, n_blk


def _dispatch_kernel(tail_ref, nused_ref, dest_hbm, xt_ref, xs_hbm,
                     idx_sm0, idx_sm1, idx_sem, zero_sc, zero_sem, stage_sc, row_sem):
    i = pl.program_id(0)
    n = pl.num_programs(0)
    tm = idx_sm0.shape[0] // MOE_TOPK
    idx_sms = (idx_sm0, idx_sm1)
    tile_rows = MOE_BLOCK * SUBLANES
    n_blk = xs_hbm.shape[0] // tile_rows

    @pl.when(i == 0)
    def _():
        zero_sc[...] = jnp.zeros(zero_sc.shape, F32)

        def clear(row, wait):
            cp = pltpu.make_async_copy(
                zero_sc, xs_hbm.at[pl.ds(pl.multiple_of(row * SUBLANES, SUBLANES), tile_rows), :], zero_sem)
            if wait:
                cp.wait()
            else:
                cp.start()

        def fill(e, wait):
            @pl.when(tail_ref[e] >= 0)
            def _():
                clear(tail_ref[e], wait)

        for wait in (False, True):
            lax.fori_loop(0, N_EXPERTS, lambda e, c, w=wait: (fill(e, w), c)[1], 0)
            lax.fori_loop(nused_ref[0], n_blk, lambda b, c, w=wait: (clear(b * MOE_BLOCK, w), c)[1], 0)

    def idx_copy(step, sl):
        return pltpu.make_async_copy(dest_hbm.at[step], idx_sms[sl], idx_sem.at[sl])

    def all_copies(sl):
        for s in range(MOE_TOPK):
            pltpu.make_async_copy(stage_sc.at[sl], xs_hbm.at[pl.ds(0, tm * SUBLANES), :], row_sem.at[sl]).wait()

    @pl.when(i == 0)
    def _():
        idx_copy(0, 0).start()

    for sl in range(2):
        @pl.when(i % 2 == sl)
        def _():
            @pl.when(i >= 2)
            def _():
                all_copies(sl)

            stage_sc[sl] = xt_ref[...]
            idx_copy(i, sl).wait()

            @pl.when(i + 1 < n)
            def _():
                idx_copy(i + 1, 1 - sl).start()

            def issue(r, c):
                src = pl.multiple_of(r * SUBLANES, SUBLANES)
                for s in range(MOE_TOPK):
                    dst = pl.multiple_of(idx_sms[sl][MOE_TOPK * r + s], SUBLANES)
                    pltpu.make_async_copy(stage_sc.at[sl, pl.ds(src, SUBLANES), :],
                                          xs_hbm.at[pl.ds(dst, SUBLANES), :],
                                          row_sem.at[sl]).start(priority=s % DMA_PRIORITIES)
                return c

            lax.fori_loop(0, tm, issue, 0, unroll=8)

            @pl.when(i == n - 1)
            def _():
                all_copies(sl)

                @pl.when(n >= 2)
                def _():
                    all_copies(1 - sl)


def _dispatch(x1t, dest, tail, n_used, n_blk):
    t = dest.shape[0]
    tm = COMBINE_TILE
    pos = (dest * SUBLANES).reshape(t // tm, MOE_TOPK * tm)
    grid_spec = pltpu.PrefetchScalarGridSpec(
        num_scalar_prefetch=2,
        grid=(t // tm,),
        in_specs=[pl.BlockSpec(memory_space=pl.ANY),
                  pl.BlockSpec((tm * SUBLANES, LANES), lambda i, tl, nu: (i, 0))],
        out_specs=pl.BlockSpec(memory_space=pl.ANY),
        scratch_shapes=[pltpu.SMEM((MOE_TOPK * tm,), I32), pltpu.SMEM((MOE_TOPK * tm,), I32),
                        pltpu.SemaphoreType.DMA((2,)), pltpu.VMEM((MOE_BLOCK * SUBLANES, LANES), F32),
                        pltpu.SemaphoreType.DMA, pltpu.VMEM((2, tm * SUBLANES, LANES), F32),
                        pltpu.SemaphoreType.DMA((2,))],
    )
    return pl.pallas_call(
        _dispatch_kernel,
        grid_spec=grid_spec,
        out_shape=jax.ShapeDtypeStruct((n_blk * MOE_BLOCK * SUBLANES, LANES), F32),
        compiler_params=_cparams("arbitrary"),
        name="moe_dispatch",
    )(tail, n_used, pos, x1t)


def _gather_ahead(idx_hbm, idx_sms, idx_sem, issue_rows, n_active):
    i = pl.program_id(0)
    n = pl.num_programs(0)

    def idx_copy(step, slot):
        return pltpu.make_async_copy(idx_hbm.at[step], idx_sms[slot], idx_sem.at[slot])

    @pl.when(i == 0)
    def _():
        first = idx_copy(0, 0)
        first.start()
        first.wait()

        @pl.when(0 < n_active)
        def _():
            issue_rows(0)

        @pl.when(1 < n)
        def _():
            idx_copy(1, 1).start()

    for slot in range(2):
        @pl.when(jnp.logical_and(i + 1 < n, (i + 1) % 2 == slot))
        def _():
            idx_copy(i + 1, slot).wait()

            @pl.when(i + 1 < n_active)
            def _():
                issue_rows(slot)

        @pl.when(jnp.logical_and(i + 2 < n, i % 2 == slot))
        def _():
            idx_copy(i + 2, slot).start()


def _expert_kernel(blk_e_ref, nused_ref, xs_ref, gf_ref, wgu_ref, wd_ref, yst_ref, wgu_sc, wd_sc):
    i = pl.program_id(0)

    @pl.when(i >= nused_ref[0])
    def _():
        yst_ref[...] = jnp.zeros(yst_ref.shape, F32)

    @pl.when(jnp.logical_or(i == 0, blk_e_ref[i] != blk_e_ref[jnp.maximum(i - 1, 0)]))
    def _():
        wgu_sc[...] = wgu_ref[0, 0].astype(BF16)
        wd_sc[...] = wd_ref[0, 0].astype(BF16)

    @pl.when(i < nused_ref[0])
    def _():
        blk = xs_ref.shape[0] // SUBLANES
        x = _load_token_tiles(xs_ref, blk)
        hn = _rms(x, gf_ref[...], NORM_EPS).astype(BF16)
        gu = _dot(hn, wgu_sc[...])
        eh = gu.shape[1] // 2
        g = gu[:, :eh]
        act = (g * jax.nn.sigmoid(g) * gu[:, eh:]).astype(BF16)
        _store_token_tiles(yst_ref, _dot(act, wd_sc[...]))


def _experts(xs, blk_e, n_used, gf, w_gu, w_down, layer):
    d = gf.shape[0]
    blk = MOE_BLOCK
    n_blk = xs.shape[0] // (blk * SUBLANES)
    eh2 = w_gu.shape[3]
    rows = lambda i, e, n: (jnp.minimum(i, n[0] - 1), 0)
    grid_spec = pltpu.PrefetchScalarGridSpec(
        num_scalar_prefetch=2,
        grid=(n_blk,),
        in_specs=[
            pl.BlockSpec((blk * SUBLANES, LANES), rows),
            pl.BlockSpec((1, d), lambda i, e, n: (0, 0)),
            pl.BlockSpec((1, 1, d, eh2), lambda i, e, n: (layer, e[i], 0, 0)),
            pl.BlockSpec((1, 1, eh2 // 2, d), lambda i, e, n: (layer, e[i], 0, 0)),
        ],
        out_specs=pl.BlockSpec((blk * SUBLANES, LANES), lambda i, e, n: (i, 0)),
        scratch_shapes=[pltpu.VMEM((d, eh2), BF16), pltpu.VMEM((eh2 // 2, d), BF16)],
    )
    return pl.pallas_call(
        _expert_kernel,
        grid_spec=grid_spec,
        out_shape=jax.ShapeDtypeStruct(xs.shape, F32),
        compiler_params=_cparams("arbitrary"),
        name="moe_experts",
    )(blk_e, n_used, xs, gf.reshape(1, d), w_gu, w_down)


def _combine_kernel(pos_hbm, yst_hbm, x1_ref, rt_ref, p_ref, gn_ref, wg_ref, wp_ref, fn_ref, o_ref,
                    idx_sm0, idx_sm1, idx_sem, ybuf, row_sem, *, final):
    slot = pl.program_id(0) % 2
    tm = x1_ref.shape[0]
    idx_sms = (idx_sm0, idx_sm1)

    def issue_rows(sl):
        def issue(r, c):
            dst = pl.multiple_of(r * SUBLANES, SUBLANES)
            for s in range(MOE_TOPK):
                src = pl.multiple_of(idx_sms[sl][MOE_TOPK * r + s], SUBLANES)
                pltpu.make_async_copy(yst_hbm.at[pl.ds(src, SUBLANES), :],
                                      ybuf.at[sl, s, pl.ds(dst, SUBLANES), :],
                                      row_sem.at[sl]).start(priority=s % DMA_PRIORITIES)
            return c

        lax.fori_loop(0, tm, issue, 0, unroll=8)

    _gather_ahead(pos_hbm, idx_sms, idx_sem, issue_rows, pl.num_programs(0))
    for s in range(MOE_TOPK):
        pltpu.make_async_copy(yst_hbm.at[pl.ds(0, tm * SUBLANES), :], ybuf.at[slot, s], row_sem.at[slot]).wait()

    rt = rt_ref[...]
    y0 = _load_token_tiles(ybuf.at[slot, 0], tm)
    y1 = _load_token_tiles(ybuf.at[slot, 1], tm)
    x2 = x1_ref[...] + (rt[:, 0:1] * y0 + rt[:, 1:2] * y1)
    gate = jax.nn.sigmoid(_dot(_rms(x2, gn_ref[...], NORM_EPS).astype(BF16), wg_ref[...]))
    x3 = x2 + _dot(p_ref[...].astype(BF16), wp_ref[...]) * gate
    if final:
        x3 = _rms(x3, fn_ref[...], NORM_EPS)
    o_ref[...] = x3


def _combine(x1, route, yst, dest, p_all, layer, ple_norm, ple_gate, ple_proj, final_norm, final):
    t, d = x1.shape
    tm = COMBINE_TILE
    pd = p_all.shape[1]
    pos = (dest * SUBLANES).reshape(t // tm, MOE_TOPK * tm)
    row = lambda i: (i, 0)
    fixed = lambda i: (0, 0)
    return pl.pallas_call(
        functools.partial(_combine_kernel, final=final),
        grid=(t // tm,),
        in_specs=[pl.BlockSpec(memory_space=pl.ANY), pl.BlockSpec(memory_space=pl.ANY),
                  pl.BlockSpec((tm, d), row), pl.BlockSpec((tm, LANES), row),
                  pl.BlockSpec((tm, pd), lambda i: (layer * (t // tm) + i, 0)),
                  pl.BlockSpec((1, d), fixed), pl.BlockSpec((d, d), fixed), pl.BlockSpec((pd, d), fixed),
                  pl.BlockSpec((1, d), fixed)],
        out_specs=pl.BlockSpec((tm, d), row),
        out_shape=jax.ShapeDtypeStruct((t, d), F32),
        scratch_shapes=[pltpu.SMEM((MOE_TOPK * tm,), I32), pltpu.SMEM((MOE_TOPK * tm,), I32),
                        pltpu.SemaphoreType.DMA((2,)),
                        pltpu.VMEM((2, MOE_TOPK, tm * SUBLANES, LANES), F32), pltpu.SemaphoreType.DMA((2,))],
        compiler_params=_cparams("arbitrary"),
        name="moe_combine_ple",
    )(pos, yst, x1, route, p_all, ple_norm.reshape(1, d), ple_gate, ple_proj, final_norm.reshape(1, d))


def _seg_sum(x, eseg_ref, eexp_ref):
    s = _dot(x.astype(BF16), eseg_ref[...])
    return _dot(s.astype(BF16), eexp_ref[...])


def _rwkv_pre_kernel(x_ref, xp_ref, gm_ref, mu_ref, vec_ref, wr_ref, wk_ref, wv_ref, w1_ref, w2_ref,
                     a1_ref, a2_ref, g1_ref, g2_ref, tri_ref, eseg_ref, eexp_ref,
                     la_ref, lr_ref, rb_ref, rk_ref, sb_ref, sk_ref, v_ref, bon_ref, g_ref, gc_ref,
                     c_sc):
    tl = x_ref.shape[0]
    gm = gm_ref[...]
    h = _rms(x_ref[...], gm, NORM_EPS)
    hp = _rms(xp_ref[...], gm, NORM_EPS)[7:8, :]
    hp = jnp.where(pl.program_id(1) == 0, 0.0, hp)
    row = lax.broadcasted_iota(I32, h.shape, 0)
    hs = jnp.where(row == 0, hp, pltpu.roll(h, 1, axis=0))
    xx = hs - h
    mix = lambda n: (h + xx * mu_ref[n:n + 1, :]).astype(BF16)
    vec = lambda n: vec_ref[n:n + 1, :]
    r = _dot(mix(0), wr_ref[...])
    wl = vec(0) + _dot(jnp.tanh(_dot(mix(1), w1_ref[...])).astype(BF16), w2_ref[...])
    lw = -math.exp(-0.5) * jax.nn.sigmoid(wl)
    k = _dot(mix(2), wk_ref[...])
    v = _dot(mix(3), wv_ref[...])
    a = jax.nn.sigmoid(vec(1) + _dot(_dot(mix(4), a1_ref[...]).astype(BF16), a2_ref[...]))
    g = _dot(jax.nn.sigmoid(_dot(mix(5), g1_ref[...])).astype(BF16), g2_ref[...])
    kk = k * vec(2)
    kk = kk * lax.rsqrt(jnp.maximum(_seg_sum(kk * kk, eseg_ref, eexp_ref), 1e-24))
    k2 = k * (1.0 + (a - 1.0) * vec(3))
    bon_ref[...] = (_seg_sum(r * k2 * vec(4), eseg_ref, eexp_ref) * v).astype(bon_ref.dtype)
    v_ref[...] = v.astype(v_ref.dtype)
    g_ref[...] = g.astype(g_ref.dtype)

    half = tri_ref.shape[1]
    tri = tri_ref[...]
    for s in range(tl // half):
        p1, p2 = _split2(lw[s * half:(s + 1) * half, :])
        c_sc[s * half:(s + 1) * half, :] = _dot(tri, p1) + _dot(tri, p2)
    c_in = c_sc[...]
    nc = tl // RWKV_CHUNK
    ends = [c_sc[(n + 1) * RWKV_CHUNK - 1:(n + 1) * RWKV_CHUNK, :] for n in range(nc)]
    c_end = jnp.concatenate([ends[n] - c_sc[n * RWKV_CHUNK:(n + 1) * RWKV_CHUNK, :] for n in range(nc)],
                            axis=0)
    gc_ref[...] = jnp.exp(jnp.concatenate(ends, axis=0))
    e_in = jnp.exp(c_in)
    e_neg = jnp.exp(-c_in)
    e_end = jnp.exp(c_end)
    b = kk * a
    la_ref[...] = (-kk * jnp.exp(c_in - lw)).astype(la_ref.dtype)
    lr_ref[...] = (r * e_in).astype(lr_ref.dtype)
    rb_ref[...] = (b * e_neg).astype(rb_ref.dtype)
    rk_ref[...] = (k2 * e_neg).astype(rk_ref.dtype)
    sb_ref[...] = (b * e_end).astype(sb_ref.dtype)
    sk_ref[...] = (k2 * e_end).astype(sk_ref.dtype)


def _head_indicator(d):
    heads = d // RWKV_HEAD
    e = (jnp.arange(d)[:, None] // RWKV_HEAD == jnp.arange(LANES)[None, :]).astype(BF16)
    del heads
    return e, e.T


def _cumsum_matrix(half):
    i = jnp.arange(half)[:, None]
    j = jnp.arange(half)[None, :]
    return (((i // RWKV_CHUNK) == (j // RWKV_CHUNK)) & (j <= i)).astype(BF16)


def _pad_cols(w, n):
    return jnp.pad(w, ((0, 0), (0, n - w.shape[1])))


def _pad_rows(w, n):
    return jnp.pad(w, ((0, n - w.shape[0]), (0, 0)))


def _rwkv_pre(x3, gm, mu, vecs, wr, wk, wv, w1, w2, a1, a2, g1, g2, batch, seq):
    t, d = x3.shape
    tl = RWKV_TILE
    nt = seq // tl
    half = 256
    lora = lambda w_in, w_out: (_pad_cols(w_in, -(-w_in.shape[1] // LANES) * LANES).astype(BF16),
                                _pad_rows(w_out, -(-w_out.shape[0] // LANES) * LANES).astype(BF16))
    w1p, w2p = lora(w1, w2)
    a1p, a2p = lora(a1, a2)
    g1p, g2p = lora(g1, g2)
    eseg, eexp = _head_indicator(d)
    tri = _cumsum_matrix(half)
    row = lambda b, s: (b * nt + s, 0)
    fixed = lambda b, s: (0, 0)
    full = lambda a: pl.BlockSpec(a.shape, fixed)
    prev = lambda b, s: (jnp.maximum((b * nt + s) * (tl // 8) - 1, 0), 0)
    weights = (wr.astype(BF16), wk.astype(BF16), wv.astype(BF16), w1p, w2p, a1p, a2p, g1p, g2p, tri, eseg, eexp)
    outs = pl.pallas_call(
        _rwkv_pre_kernel,
        grid=(batch, nt),
        in_specs=[pl.BlockSpec((tl, d), row), pl.BlockSpec((8, d), prev), pl.BlockSpec((1, d), fixed),
                  full(mu), full(vecs)] + [full(w) for w in weights],
        out_specs=[pl.BlockSpec((tl, d), row)] * 9 + [pl.BlockSpec((tl // RWKV_CHUNK, d), row)],
        out_shape=[jax.ShapeDtypeStruct((t, d), BF16)] * 9 + [jax.ShapeDtypeStruct((t // RWKV_CHUNK, d), F32)],
        scratch_shapes=[pltpu.VMEM((tl, d), F32)],
        compiler_params=_cparams("arbitrary", "arbitrary"),
        name="rwkv_pre",
    )(x3, x3, gm.reshape(1, d), mu, vecs, *weights)
    return outs


def _rwkv_scan_kernel(la_ref, lr_ref, rb_ref, rk_ref, sb_ref, sk_ref, v_ref, gc_ref, bon_ref, g_ref, x_ref,
                      vec_ref, wo_ref, eseg_ref, eexp_ref, gf_ref, wrh_ref, wrl_ref, br_ref, tri_ref,
                      x4_ref, x4t_ref, rt_ref, rtt_ref, size_ref, s_sc, y_sc, tot_sc):
    tl, d = x_ref.shape
    ch = RWKV_CHUNK
    pw = 2 * RWKV_HEAD
    npair = d // pw

    @pl.when(pl.program_id(1) == 0)
    def _():
        s_sc[...] = jnp.zeros(s_sc.shape, F32)

    ri = lax.broadcasted_iota(I32, (pw, pw), 0)
    ci = lax.broadcasted_iota(I32, (pw, pw), 1)
    same_half = (ri < RWKV_HEAD) == (ci < RWKV_HEAD)
    eye = (ri == ci).astype(F32)
    rt_ = lax.broadcasted_iota(I32, (ch, pw), 0)
    ct_ = lax.broadcasted_iota(I32, (ch, pw), 1)
    first = ct_ < RWKV_HEAD
    strict = rt_ > (ct_ & (RWKV_HEAD - 1))
    incl = rt_ >= (ct_ & (RWKV_HEAD - 1))
    m_s0, m_s1 = strict & first, strict & ~first
    m_i0, m_i1 = incl & first, incl & ~first
    first_full = ci < RWKV_HEAD
    zf = jnp.zeros((ch, pw), F32)
    gc_row = lax.broadcasted_iota(I32, (tl // ch, pw), 0)

    group = RWKV_CHUNKS_PER_ITER

    def chunks(cg, carry):
        pairs = range(npair)
        cols = [slice(p * pw, (p + 1) * pw) for p in pairs]
        cidx = [cg * group + k for k in range(group)]
        rows = [pl.ds(pl.multiple_of(c * ch, ch), ch) for c in cidx]
        units = [(k, p) for k in range(group) for p in pairs]
        un = range(len(units))
        cat0 = lambda a, b: jnp.concatenate([a, b], axis=0)
        ld = lambda ref, n: ref[rows[units[n][0]], cols[units[n][1]]]
        lcat = [cat0(ld(la_ref, n), ld(lr_ref, n)) for n in un]
        rb = [ld(rb_ref, n) for n in un]
        rk = [ld(rk_ref, n) for n in un]
        vv = [ld(v_ref, n) for n in un]
        zb = jnp.zeros_like(lcat[0])
        a0 = [_dot_nt(jnp.where(first_full, lcat[n], zb), cat0(rb[n], rk[n])) for n in un]
        a1 = [_dot_nt(jnp.where(first_full, zb, lcat[n]), cat0(rk[n], rb[n])) for n in un]
        pm = [cat0(jnp.where(m_s0, a0[n][:ch], zf), jnp.where(m_s1, a1[n][:ch], zf)) for n in un]
        aak = [cat0(jnp.where(m_s1, a0[n][:ch], zf), jnp.where(m_s0, a1[n][:ch], zf)).astype(BF16)
               for n in un]
        arbk = [jnp.concatenate([cat0(jnp.where(m_i0, a0[n][ch:], zf), jnp.where(m_i1, a1[n][ch:], zf)),
                                 cat0(jnp.where(m_i1, a0[n][ch:], zf), jnp.where(m_i0, a1[n][ch:], zf))],
                                axis=1).astype(BF16) for n in un]
        v2 = [cat0(vv[n], vv[n]) for n in un]
        av = [_dot(aak[n], v2[n]) for n in un]
        tm = [eye + pm[n] for n in un]
        qm = pm
        for _ in range(5):
            qb = [q.astype(BF16) for q in qm]
            qm = [_dot(qb[n], qb[n]) for n in un]
            tm = [tm[n] + _dot(tm[n].astype(BF16), qm[n].astype(BF16)) for n in un]
        tmb = [t.astype(BF16) for t in tm]
        for k in range(group):
            nk = [k * npair + p for p in pairs]
            st = [s_sc[p] for p in pairs]
            w12 = [_dot_nt(lcat[n], st[p].astype(BF16)) for p, n in zip(pairs, nk)]
            rhs = [cat0(w12[p][:ch], w12[p][:ch]) + av[n] for p, n in zip(pairs, nk)]
            ust = [_dot(tmb[n], rhs[p].astype(BF16)) for p, n in zip(pairs, nk)]
            yst = [cat0(w12[p][ch:], w12[p][ch:]) + _dot(arbk[n], cat0(ust[p].astype(BF16), v2[n]))
                   for p, n in zip(pairs, nk)]
            for p in pairs:
                y_sc[rows[k], cols[p]] = jnp.where(first, yst[p][:ch], yst[p][ch:])
            u = [jnp.where(first, ust[p][:ch], ust[p][ch:]).astype(BF16) for p in pairs]
            ds_ = [_dot_tn(cat0(u[p], vv[n]), cat0(sb_ref[rows[k], cols[p]], sk_ref[rows[k], cols[p]]))
                   for p, n in zip(pairs, nk)]
            for p in pairs:
                gcr = jnp.sum(jnp.where(gc_row == cidx[k], gc_ref[:, cols[p]], 0.0), axis=0, keepdims=True)
                s_sc[p] = jnp.where(same_half, st[p] * gcr + ds_[p], 0.0)
        return carry

    lax.fori_loop(0, tl // (ch * group), chunks, 0)

    y = y_sc[...]
    inv_n = 1.0 / RWKV_HEAD
    mean = _seg_sum(y, eseg_ref, eexp_ref) * inv_n
    dlt = y - mean
    var = _seg_sum(dlt * dlt, eseg_ref, eexp_ref) * inv_n
    yn = dlt * lax.rsqrt(var + RWKV_GN_EPS) * vec_ref[5:6, :] + vec_ref[6:7, :]
    z = (yn + bon_ref[...].astype(F32)) * g_ref[...].astype(F32)
    x4 = x_ref[...] + _dot(z.astype(BF16), wo_ref[...])
    x4_ref[...] = x4
    _store_token_tiles(x4t_ref, x4)
    first = jnp.logical_and(pl.program_id(0) == 0, pl.program_id(1) == 0)
    rt = _route(x4, gf_ref, wrh_ref, wrl_ref, br_ref, tri_ref, tot_sc, size_ref, first)
    rt_ref[...] = rt
    rtt_ref[...] = rt.T[:SUBLANES, :]


def _rwkv_scan(pre, x3, vecs, wo, gf, router, batch, seq):
    la, lr, rb, rk, sb, sk, v, bon, g, gc = pre
    t, d = x3.shape
    tl = RWKV_TILE
    nt = seq // tl
    eseg, eexp = _head_indicator(d)
    wrh, wrl, br = router
    row = lambda b, s: (b * nt + s, 0)
    fixed = lambda b, s: (0, 0)
    big = pl.BlockSpec((tl, d), row)
    full = lambda a: pl.BlockSpec(a.shape, fixed)
    pw = 2 * RWKV_HEAD
    tri = _strict_lower(tl)
    return pl.pallas_call(
        _rwkv_scan_kernel,
        grid=(batch, nt),
        in_specs=[big] * 7 + [pl.BlockSpec((tl // RWKV_CHUNK, d), row), big, big, big,
                              full(vecs), full(wo), full(eseg), full(eexp),
                              pl.BlockSpec((1, d), fixed), full(wrh), full(wrl), full(br), full(tri)],
        out_specs=[big, pl.BlockSpec((tl * SUBLANES, LANES), row), pl.BlockSpec((tl, LANES), row),
                   pl.BlockSpec((SUBLANES, tl), lambda b, s: (0, b * nt + s)),
                   pl.BlockSpec((SUBLANES, LANES), fixed)],
        out_shape=[jax.ShapeDtypeStruct((t, d), F32), jax.ShapeDtypeStruct((t * SUBLANES, LANES), F32),
                   jax.ShapeDtypeStruct((t, LANES), F32), jax.ShapeDtypeStruct((SUBLANES, t), F32),
                   jax.ShapeDtypeStruct((SUBLANES, LANES), F32)],
        scratch_shapes=[pltpu.VMEM((d // pw, pw, pw), F32), pltpu.VMEM((tl, d), F32),
                        pltpu.VMEM((1, LANES), F32)],
        compiler_params=_cparams("arbitrary", "arbitrary"),
        name="rwkv_scan_out",
    )(la, lr, rb, rk, sb, sk, v, gc, bon, g, x3, vecs, wo, eseg, eexp, gf.reshape(1, d), wrh, wrl, br, tri)


def _moe_and_embed(x1, x1t, route, route_t, sizes, i, p, norm_ffn, moe_w_gu, moe_w_down, ple_norm, ple_gate,
                   ple_proj, final_norm, final):
    t = x1.shape[0]
    blk_e, n_used, dest, tail, n_blk = _moe_plan(route_t, sizes, t)
    xs = _dispatch(x1t, dest, tail, n_used, n_blk)
    yst = _experts(xs, blk_e, n_used, norm_ffn[i], moe_w_gu, moe_w_down, i)
    p_all = p.reshape(-1, p.shape[-1])
    return _combine(x1, route, yst, dest, p_all, i, ple_norm[i], ple_gate[i].astype(BF16),
                    ple_proj[i].astype(BF16), final_norm, final)


def kernel(x, p, rel_bias, norm_mix, ab_w_in, ab_w_out, lam_q1, lam_k1, lam_q2, lam_k2, sub_g, conv_w, conv_b, lru_wa, lru_ba, lru_wx, lru_bx, lru_lambda, rwkv_mu, rwkv_wr, rwkv_wk, rwkv_wv, rwkv_wo, rwkv_w0, rwkv_w1, rwkv_w2, rwkv_a0, rwkv_a1, rwkv_a2, rwkv_g1, rwkv_g2, rwkv_kk, rwkv_ka, rwkv_rk, rwkv_ln_g, rwkv_ln_b, norm_ffn, moe_wc, moe_bc, moe_wf, moe_bf, moe_w_gu, moe_w_down, ple_norm, ple_gate, ple_proj, final_norm):
    batch, seq, d = x.shape
    t = batch * seq
    depth = norm_mix.shape[0]
    assert depth == 2 and seq % RWKV_TILE == 0 and seq % LRU_TILE == 0 and t % ROW_TILE == 0
    assert d == SUBLANES * LANES and seq % ATTN_TILE == 0 and t % COMBINE_TILE == 0
    xs = x.reshape(t, d)
    for i in range(depth):
        j = i // 2
        router = _router_params(moe_wc[i], moe_bc[i], moe_wf[i], moe_bf[i])
        if i % 2 == 0:
            lam_init = 0.8 - 0.6 * math.exp(-0.3 * i)
            q, k, vt, xb, gb = _inproj(xs, norm_mix[i], ab_w_in[j].astype(BF16))
            lam_rows = jnp.stack([lam_q1[j], lam_k1[j], lam_q2[j], lam_k2[j]]).astype(F32)
            ya = _diff_attention(q, k, vt, rel_bias, lam_rows, sub_g[j], lam_init, batch, seq)
            yb = _rglru(xb, gb, conv_w[j], conv_b[j], lru_wa[j], lru_ba[j], lru_wx[j], lru_bx[j],
                        lru_lambda[j], batch, seq)
            x1, x1t, route, route_t, sizes = _outproj(xs, ya, yb, ab_w_out[j].astype(BF16), norm_ffn[i],
                                                      router)
        else:
            vecs = jnp.stack([rwkv_w0[j], rwkv_a0[j], rwkv_kk[j], rwkv_ka[j], rwkv_rk[j].reshape(-1),
                              rwkv_ln_g[j], rwkv_ln_b[j], jnp.zeros((d,), F32)]).astype(F32)
            pre = _rwkv_pre(xs, norm_mix[i], rwkv_mu[j], vecs, rwkv_wr[j], rwkv_wk[j], rwkv_wv[j],
                            rwkv_w1[j], rwkv_w2[j], rwkv_a1[j], rwkv_a2[j], rwkv_g1[j], rwkv_g2[j],
                            batch, seq)
            x1, x1t, route, route_t, sizes = _rwkv_scan(pre, xs, vecs, rwkv_wo[j].astype(BF16), norm_ffn[i],
                                                        router, batch, seq)
        xs = _moe_and_embed(x1, x1t, route, route_t, sizes, i, p, norm_ffn, moe_w_gu, moe_w_down, ple_norm, ple_gate,
                            ple_proj, final_norm, final=(i == depth - 1))
    return xs.reshape(batch, seq, d)
```

```python
import functools
import math

import jax
import jax.numpy as jnp
from jax import lax
from jax.experimental import pallas as pl
from jax.experimental.pallas import tpu as pltpu

F32, BF16, I32 = jnp.float32, jnp.bfloat16, jnp.int32

A_HEADS = 4
A_HEAD_DIM = 64
A_SCALE = A_HEAD_DIM ** -0.5
LOG2E = math.log2(math.e)
REL_BUCKETS = 32
REL_MAX_DIST = 128
LRU_BLOCKS = 8
CONV_W = 4
LRU_C = 8.0
RWKV_HEAD = 64
RWKV_GN_EPS = 64e-5
MOE_GROUPS = 4
EXPERTS_PER_GROUP = 8
N_EXPERTS = MOE_GROUPS * EXPERTS_PER_GROUP
MOE_TOPK = 2
NORM_EPS = 1e-6
SUBNORM_EPS = 1e-5

LANES = 128
VMEM_LIMIT = 56 * 1024 * 1024
ROW_TILE = 512
ATTN_TILE = 512
LRU_TILE = 512
RWKV_CHUNK = 64
RWKV_TILE = 512
RWKV_CHUNKS_PER_ITER = 2
MOE_BLOCK = 512
COMBINE_TILE = 256
NEG_BIG = -1e30


def _cparams(*sem, flags=None):
    return pltpu.CompilerParams(dimension_semantics=sem, vmem_limit_bytes=VMEM_LIMIT, flags=flags)


def _rms(x, g, eps):
    return x * lax.rsqrt(jnp.mean(x * x, axis=-1, keepdims=True) + eps) * g


def _dot(a, b):
    return jnp.dot(a, b, preferred_element_type=F32)


def _dot_nt(a, b):
    return lax.dot_general(a, b, (((1,), (1,)), ((), ())), preferred_element_type=F32)


def _dot_tn(a, b):
    return lax.dot_general(a, b, (((0,), (0,)), ((), ())), preferred_element_type=F32)


def _split2(x):
    hi = x.astype(BF16)
    lo = (x - hi.astype(F32)).astype(BF16)
    return hi, lo


def _softplus(x):
    return jnp.maximum(x, 0.0) + jnp.log1p(jnp.exp(-jnp.abs(x)))


def _neg_expm1(y, exp_y):
    poly = 1.0 + y * (1.0 / 5.0)
    for d in (4.0, 3.0, 2.0):
        poly = 1.0 + (y * (1.0 / d)) * poly
    return jnp.where(y > -1.0 / 16.0, -y * poly, 1.0 - exp_y)


SUBLANES = 8
DMA_PRIORITIES = 2


def _store_token_tiles(ref, x):
    rows = x.shape[0]
    for s in range(x.shape[1] // LANES):
        ref[pl.ds(s, rows, stride=SUBLANES), :] = x[:, s * LANES:(s + 1) * LANES]


def _load_token_tiles(ref, rows):
    return jnp.concatenate([ref[pl.ds(s, rows, stride=SUBLANES), :] for s in range(SUBLANES)], axis=1)


def _gelu_tanh(x):
    c = math.sqrt(2.0 / math.pi)
    return 0.5 * x * (1.0 + jnp.tanh(c * (x + 0.044715 * (x * x * x))))


def _inproj_kernel(x_ref, g_ref, w_ref, q_ref, k_ref, vt_ref, xb_ref, gb_ref):
    hn = _rms(x_ref[...], g_ref[...], NORM_EPS).astype(BF16)
    width = q_ref.shape[1]
    for c, o_ref in enumerate((q_ref, k_ref, vt_ref, xb_ref, gb_ref)):
        r = _dot(hn, w_ref[:, c * width:(c + 1) * width])
        if c == 0:
            r = r * (A_SCALE * LOG2E)
        if c == 2:
            r = r.T
        o_ref[...] = r.astype(o_ref.dtype)


def _inproj(x2, g, w_in):
    t, d = x2.shape
    width = w_in.shape[1] // 5
    tm = ROW_TILE
    row = lambda i: (i, 0)
    fixed = lambda i: (0, 0)
    rows = pl.BlockSpec((tm, width), row)
    return pl.pallas_call(
        _inproj_kernel,
        grid=(t // tm,),
        in_specs=[pl.BlockSpec((tm, d), row), pl.BlockSpec((1, d), fixed),
                  pl.BlockSpec(w_in.shape, fixed)],
        out_specs=[rows, rows, pl.BlockSpec((width, tm), lambda i: (0, i)), rows, rows],
        out_shape=[jax.ShapeDtypeStruct((t, width), BF16)] * 2 + [jax.ShapeDtypeStruct((width, t), BF16)]
        + [jax.ShapeDtypeStruct((t, width), F32)] * 2,
        compiler_params=_cparams("arbitrary"),
        name="inproj",
    )(x2, g.reshape(1, d), w_in)


def _t5_bucket(rel):
    n = jnp.maximum(rel, 0)
    max_exact = REL_BUCKETS // 2
    large = max_exact + (jnp.log(jnp.maximum(n, 1).astype(F32) / max_exact)
                         / math.log(REL_MAX_DIST / max_exact)
                         * (REL_BUCKETS - max_exact)).astype(I32)
    large = jnp.minimum(large, REL_BUCKETS - 1)
    return jnp.where(n < max_exact, n, large)


def _bias_tables(rel_bias, tile):
    span = 2 * tile

    def toeplitz(first_rel):
        rel = jnp.arange(span - 1) + first_rel
        vals = jnp.where((rel >= 0)[:, None], rel_bias[_t5_bucket(rel)].astype(F32), NEG_BIG)
        w = jnp.concatenate([vals, jnp.zeros((1, vals.shape[1]), F32)], axis=0).T
        skew = jnp.tile(w, (1, tile))[:, :tile * (span - 1)].reshape(-1, tile, span - 1)
        return skew[:, :, tile - 1:]

    return jnp.stack([toeplitz(1 - tile), toeplitz(1)], axis=1)


def _attn_kernel(cfar_ref, q_ref, k_ref, vt_ref, bias_ref, lam_ref, subg_ref, o_ref,
                 m_sc, l_sc, acc_sc, sa_sc, sb_sc, *, lam_init):
    h = pl.program_id(1)
    i = pl.program_id(2)
    tile = q_ref.shape[0]
    q = q_ref[...]
    lane = lax.broadcasted_iota(I32, q.shape, 1)
    zero = jnp.zeros_like(q)
    qm = (jnp.where(lane < A_HEAD_DIM, q, zero), jnp.where(lane >= A_HEAD_DIM, q, zero))
    m_sc[...] = jnp.full(m_sc.shape, NEG_BIG, F32)
    l_sc[...] = jnp.zeros(l_sc.shape, F32)
    acc_sc[...] = jnp.zeros(acc_sc.shape, F32)

    maps = range(2)

    def scores(j):
        kb = k_ref[pl.ds(pl.multiple_of(j * tile, tile), tile), :]
        return [_dot_nt(kb, qm[mi]) for mi in maps]

    def scores_to(dst, j):
        s = scores(j)
        for mi in maps:
            dst[mi] = s[mi]

    def consume(s, j, table, const):
        vtb = vt_ref[:, pl.ds(pl.multiple_of(j * tile, tile), tile)]
        if table is not None:
            s = [s[mi] + table for mi in maps]
        half = tile // 2
        for mi in maps:
            for qs in (slice(0, half), slice(half, tile)):
                sq = s[mi][:, qs]
                m_prev = m_sc[mi, :, qs]
                m_new = jnp.maximum(m_prev, jnp.max(sq, axis=0, keepdims=True) + const)
                p = jnp.exp2(sq - (m_new - const))
                alpha = jnp.exp2(m_prev - m_new)
                l_sc[mi, :, qs] = alpha * l_sc[mi, :, qs] + jnp.sum(p, axis=0, keepdims=True)
                acc_sc[mi, :, qs] = alpha * acc_sc[mi, :, qs] + _dot(vtb, p.astype(BF16))
                m_sc[mi, :, qs] = m_new

    held = lambda buf: [buf[mi] for mi in maps]
    cfar = cfar_ref[h]
    nfar = jnp.maximum(i - 1, 0)

    @pl.when(i >= 1)
    def _():
        scores_to(sa_sc, 0)

    def far_pair(t, c):
        j = 2 * t
        scores_to(sb_sc, j + 1)
        consume(held(sa_sc), j, None, cfar)
        scores_to(sa_sc, j + 2)
        consume(held(sb_sc), j + 1, None, cfar)
        return c

    lax.fori_loop(0, lax.shift_right_logical(nfar, 1), far_pair, 0)

    @pl.when(jnp.logical_and(i >= 1, nfar % 2 == 0))
    def _():
        scores_to(sb_sc, i)
        consume(held(sa_sc), i - 1, bias_ref[0, 1], 0.0)
        consume(held(sb_sc), i, bias_ref[0, 0], 0.0)

    @pl.when(nfar % 2 == 1)
    def _():
        scores_to(sb_sc, i - 1)
        consume(held(sa_sc), i - 2, None, cfar)
        scores_to(sa_sc, i)
        consume(held(sb_sc), i - 1, bias_ref[0, 1], 0.0)
        consume(held(sa_sc), i, bias_ref[0, 0], 0.0)

    @pl.when(i == 0)
    def _():
        consume(scores(0), 0, bias_ref[0, 0], 0.0)

    lam_rows = lam_ref[...]
    lam = (jnp.exp(jnp.sum(lam_rows[0:1] * lam_rows[1:2], axis=-1, keepdims=True))
           - jnp.exp(jnp.sum(lam_rows[2:3] * lam_rows[3:4], axis=-1, keepdims=True)) + lam_init)
    o = acc_sc[0] / l_sc[0] - lam * (acc_sc[1] / l_sc[1])
    o = o * lax.rsqrt(jnp.mean(o * o, axis=0, keepdims=True) + SUBNORM_EPS)
    o_ref[...] = (o.T * subg_ref[...] * (1.0 - lam_init)).astype(o_ref.dtype)


def _diff_attention(q, k, vt, rel_bias, lam_rows, sub_g, lam_init, batch, seq):
    t, width = q.shape
    hw = 2 * A_HEAD_DIM
    tile = ATTN_TILE
    nq = seq // tile
    tables = _bias_tables(rel_bias, tile) * LOG2E
    cfar = rel_bias[REL_BUCKETS - 1].astype(F32) * LOG2E
    return pl.pallas_call(
        functools.partial(_attn_kernel, lam_init=lam_init),
        grid=(batch, A_HEADS, nq),
        in_specs=[
            pl.BlockSpec(memory_space=pltpu.SMEM),
            pl.BlockSpec((tile, hw), lambda b, h, i: (b * nq + i, h)),
            pl.BlockSpec((seq, hw), lambda b, h, i: (b, h)),
            pl.BlockSpec((hw, seq), lambda b, h, i: (h, b)),
            pl.BlockSpec((1, 2, tile, tile), lambda b, h, i: (h, 0, 0, 0)),
            pl.BlockSpec(lam_rows.shape, lambda b, h, i: (0, 0)),
            pl.BlockSpec((1, hw), lambda b, h, i: (0, 0)),
        ],
        out_specs=pl.BlockSpec((tile, hw), lambda b, h, i: (b * nq + i, h)),
        out_shape=jax.ShapeDtypeStruct((t, width), BF16),
        scratch_shapes=[pltpu.VMEM((2, 1, tile), F32), pltpu.VMEM((2, 1, tile), F32),
                        pltpu.VMEM((2, hw, tile), F32),
                        pltpu.VMEM((2, tile, tile), F32), pltpu.VMEM((2, tile, tile), F32)],
        compiler_params=_cparams("arbitrary", "arbitrary", "arbitrary"),
        name="diff_attn",
    )(cfar, q, k, vt, tables, lam_rows, sub_g.reshape(1, hw))


def _lru_kernel(xb_ref, gb_ref, cw_ref, vec_ref, wa_ref, wx_ref, y_ref, prev_sc, h_sc, a_sc, b_sc, hs_sc):
    ts = xb_ref.shape[0]

    @pl.when(pl.program_id(1) == 0)
    def _():
        prev_sc[...] = jnp.zeros(prev_sc.shape, F32)
        h_sc[...] = jnp.zeros(h_sc.shape, F32)

    xb = xb_ref[...]
    hist = prev_sc.shape[0]
    ext = jnp.concatenate([prev_sc[...], xb], axis=0)
    u = vec_ref[0:1, :]
    for j in range(CONV_W):
        off = hist - (CONV_W - 1) + j
        u = u + cw_ref[j:j + 1, :] * ext[off:off + ts, :]
    prev_sc[...] = xb[ts - hist:, :]
    ub = u.astype(BF16)
    r = jax.nn.sigmoid(_dot(ub, wa_ref[...]) + vec_ref[1:2, :])
    ig = jax.nn.sigmoid(_dot(ub, wx_ref[...]) + vec_ref[2:3, :])
    log_a = (-LRU_C * r) * _softplus(-vec_ref[3:4, :])
    a = jnp.exp(log_a)
    a_sc[...] = a
    b_sc[...] = jnp.sqrt(_neg_expm1(2.0 * log_a, a * a)) * (ig * u)

    def step(tt, h):
        h = a_sc[pl.ds(tt, 1), :] * h + b_sc[pl.ds(tt, 1), :]
        hs_sc[pl.ds(tt, 1), :] = h
        return h

    h_sc[...] = lax.fori_loop(0, ts, step, h_sc[...], unroll=16)
    y_ref[...] = (hs_sc[...] * _gelu_tanh(gb_ref[...])).astype(y_ref.dtype)


def _block_diag(w):
    n, c, d = w.shape
    eye = jnp.eye(n, dtype=w.dtype)
    return (w[:, :, None, :] * eye[:, None, :, None]).reshape(n * c, n * d)


def _rglru(xb, gb, conv_w, conv_b, wa, ba, wx, bx, lru_lambda, batch, seq):
    t, w = xb.shape
    ts = LRU_TILE
    nt = seq // ts
    vec = jnp.zeros((8, w), F32).at[0].set(conv_b).at[1].set(ba).at[2].set(bx).at[3].set(lru_lambda)
    row = lambda b, s: (b * nt + s, 0)
    fixed = lambda b, s: (0, 0)
    return pl.pallas_call(
        _lru_kernel,
        grid=(batch, nt),
        in_specs=[pl.BlockSpec((ts, w), row), pl.BlockSpec((ts, w), row),
                  pl.BlockSpec((CONV_W, w), fixed), pl.BlockSpec((8, w), fixed),
                  pl.BlockSpec((w, w), fixed), pl.BlockSpec((w, w), fixed)],
        out_specs=pl.BlockSpec((ts, w), row),
        out_shape=jax.ShapeDtypeStruct((t, w), BF16),
        scratch_shapes=[pltpu.VMEM((8, w), F32), pltpu.VMEM((1, w), F32),
                        pltpu.VMEM((ts, w), F32), pltpu.VMEM((ts, w), F32), pltpu.VMEM((ts, w), F32)],
        compiler_params=_cparams("arbitrary", "arbitrary"),
        name="rglru",
    )(xb, gb, conv_w, vec, _block_diag(wa).astype(BF16), _block_diag(wx).astype(BF16))


def _route(x, gf_ref, wrh_ref, wrl_ref, br_ref, tri_ref, tot_sc, size_ref, first_step):
    @pl.when(first_step)
    def _():
        tot_sc[...] = jnp.zeros(tot_sc.shape, F32)

    hn = _rms(x, gf_ref[...], NORM_EPS)
    hh, hl = _split2(hn)
    wh = wrh_ref[...]
    lg = _dot(hh, wh) + _dot(hh, wrl_ref[...]) + _dot(hl, wh) + br_ref[...]
    lane = lax.broadcasted_iota(I32, lg.shape, 1)
    big = jnp.int32(1 << 20)
    is_g = lane < MOE_GROUPS
    gl = jnp.where(is_g, lg, NEG_BIG)
    gmax = jnp.max(gl, axis=-1, keepdims=True)
    gidx = jnp.min(jnp.where(gl == gmax, lane, big), axis=-1, keepdims=True)
    gsum = jnp.sum(jnp.where(is_g, jnp.exp(gl - gmax), 0.0), axis=-1, keepdims=True)
    gprob = 1.0 / gsum
    lo = MOE_GROUPS + gidx * EXPERTS_PER_GROUP
    fmask = (lane >= lo) & (lane < lo + EXPERTS_PER_GROUP)
    fl = jnp.where(fmask, lg, NEG_BIG)
    f1 = jnp.max(fl, axis=-1, keepdims=True)
    i1 = jnp.min(jnp.where(fl == f1, lane, big), axis=-1, keepdims=True)
    fl2 = jnp.where(lane == i1, NEG_BIG, fl)
    f2 = jnp.max(fl2, axis=-1, keepdims=True)
    i2 = jnp.min(jnp.where(fl2 == f2, lane, big), axis=-1, keepdims=True)
    e2 = jnp.exp(f2 - f1)
    den = 1.0 + e2
    g1 = gprob / den
    g2 = gprob * e2 / den
    id1 = (i1 - MOE_GROUPS).astype(F32)
    id2 = (i2 - MOE_GROUPS).astype(F32)
    onehot = jnp.where((lane == i1) | (lane == i2), 1.0, 0.0)
    before = _dot(tri_ref[...], onehot.astype(BF16)) + tot_sc[...]
    r1 = jnp.sum(jnp.where(lane == i1, before, 0.0), axis=-1, keepdims=True)
    r2 = jnp.sum(jnp.where(lane == i2, before, 0.0), axis=-1, keepdims=True)
    tot = tot_sc[...] + jnp.sum(onehot, axis=0, keepdims=True)
    tot_sc[...] = tot
    size_ref[...] = jnp.broadcast_to(tot, size_ref.shape)
    vals = (g1, g2, id1, id2, r1, r2)
    out = jnp.zeros(lg.shape, F32)
    for n, v in enumerate(vals):
        out = jnp.where(lane == n, v, out)
    return out


def _router_params(wc, bc, wf, bf):
    d = wc.shape[0]
    n = MOE_GROUPS + N_EXPERTS
    w = jnp.zeros((d, LANES), F32).at[:, :MOE_GROUPS].set(wc).at[:, MOE_GROUPS:n].set(wf)
    b = jnp.zeros((1, LANES), F32).at[0, :MOE_GROUPS].set(bc).at[0, MOE_GROUPS:n].set(bf)
    hi, lo = _split2(w)
    return hi, lo, b


def _outproj_kernel(x_ref, ya_ref, yb_ref, wo_ref, gf_ref, wrh_ref, wrl_ref, br_ref, tri_ref,
                    x1_ref, x1t_ref, rt_ref, rtt_ref, size_ref, tot_sc):
    half = ya_ref.shape[1]
    x1 = x_ref[...] + _dot(ya_ref[...], wo_ref[:half, :]) + _dot(yb_ref[...], wo_ref[half:, :])
    x1_ref[...] = x1
    _store_token_tiles(x1t_ref, x1)
    rt = _route(x1, gf_ref, wrh_ref, wrl_ref, br_ref, tri_ref, tot_sc, size_ref, pl.program_id(0) == 0)
    rt_ref[...] = rt
    rtt_ref[...] = rt.T[:SUBLANES, :]


def _outproj(x2, ya, yb, w_out, gf, router):
    t, d = x2.shape
    half = ya.shape[1]
    tm = ROW_TILE
    row = lambda i: (i, 0)
    fixed = lambda i: (0, 0)
    wrh, wrl, br = router
    tri = _strict_lower(tm)
    return pl.pallas_call(
        _outproj_kernel,
        grid=(t // tm,),
        in_specs=[pl.BlockSpec((tm, d), row), pl.BlockSpec((tm, half), row), pl.BlockSpec((tm, half), row),
                  pl.BlockSpec((2 * half, d), fixed), pl.BlockSpec((1, d), fixed),
                  pl.BlockSpec((d, LANES), fixed), pl.BlockSpec((d, LANES), fixed),
                  pl.BlockSpec((1, LANES), fixed), pl.BlockSpec((tm, tm), fixed)],
        out_specs=[pl.BlockSpec((tm, d), row), pl.BlockSpec((tm * SUBLANES, LANES), row),
                   pl.BlockSpec((tm, LANES), row), pl.BlockSpec((SUBLANES, tm), lambda i: (0, i)),
                   pl.BlockSpec((SUBLANES, LANES), fixed)],
        out_shape=[jax.ShapeDtypeStruct((t, d), F32), jax.ShapeDtypeStruct((t * SUBLANES, LANES), F32),
                   jax.ShapeDtypeStruct((t, LANES), F32), jax.ShapeDtypeStruct((SUBLANES, t), F32),
                   jax.ShapeDtypeStruct((SUBLANES, LANES), F32)],
        scratch_shapes=[pltpu.VMEM((1, LANES), F32)],
        compiler_params=_cparams("arbitrary"),
        name="outproj_route",
    )(x2, ya, yb, w_out, gf.reshape(1, d), wrh, wrl, br, tri)


def _strict_lower(n):
    return (jnp.arange(n)[None, :] < jnp.arange(n)[:, None]).astype(BF16)


def _moe_plan(route_t, sizes_tile, n_tok):
    m = n_tok * MOE_TOPK
    n_blk = -(-(m + N_EXPERTS * (MOE_BLOCK - 1)) // MOE_BLOCK)
    eid = route_t[2:4].astype(I32).T
    rank = route_t[4:6].astype(I32).T
    sizes = sizes_tile[0, MOE_GROUPS:MOE_GROUPS + N_EXPERTS].astype(I32)
    padded = (sizes + MOE_BLOCK - 1) // MOE_BLOCK * MOE_BLOCK
    pend = jnp.cumsum(padded)
    dest = (pend - padded)[eid] + rank
    blk_start = jnp.arange(n_blk, dtype=I32) * MOE_BLOCK
    blk_e = jnp.minimum(jnp.sum((pend[None, :] <= blk_start[:, None]).astype(I32), axis=1),
                        N_EXPERTS - 1).astype(I32)
    n_used = (pend[-1] // MOE_BLOCK).astype(I32).reshape(1)
    tail = jnp.where(padded > 0, pend - MOE_BLOCK, -1).astype(I32)
    return blk_e, n_used, dest.astype(I32), tail, n_blk


def _dispatch_kernel(tail_ref, nused_ref, dest_hbm, xt_ref, xs_hbm,
                     idx_sm0, idx_sm1, idx_sem, zero_sc, zero_sem, stage_sc, row_sem):
    i = pl.program_id(0)
    n = pl.num_programs(0)
    tm = idx_sm0.shape[0] // MOE_TOPK
    idx_sms = (idx_sm0, idx_sm1)
    tile_rows = MOE_BLOCK * SUBLANES
    n_blk = xs_hbm.shape[0] // tile_rows

    @pl.when(i == 0)
    def _():
        zero_sc[...] = jnp.zeros(zero_sc.shape, F32)

        def clear(row, wait):
            cp = pltpu.make_async_copy(
                zero_sc, xs_hbm.at[pl.ds(pl.multiple_of(row * SUBLANES, SUBLANES), tile_rows), :], zero_sem)
            if wait:
                cp.wait()
            else:
                cp.start()

        def fill(e, wait):
            @pl.when(tail_ref[e] >= 0)
            def _():
                clear(tail_ref[e], wait)

        for wait in (False, True):
            lax.fori_loop(0, N_EXPERTS, lambda e, c, w=wait: (fill(e, w), c)[1], 0)
            lax.fori_loop(nused_ref[0], n_blk, lambda b, c, w=wait: (clear(b * MOE_BLOCK, w), c)[1], 0)

    def idx_copy(step, sl):
        return pltpu.make_async_copy(dest_hbm.at[step], idx_sms[sl], idx_sem.at[sl])

    def all_copies(sl):
        for s in range(MOE_TOPK):
            pltpu.make_async_copy(stage_sc.at[sl], xs_hbm.at[pl.ds(0, tm * SUBLANES), :], row_sem.at[sl]).wait()

    @pl.when(i == 0)
    def _():
        idx_copy(0, 0).start()

    for sl in range(2):
        @pl.when(i % 2 == sl)
        def _():
            @pl.when(i >= 2)
            def _():
                all_copies(sl)

            stage_sc[sl] = xt_ref[...]
            idx_copy(i, sl).wait()

            @pl.when(i + 1 < n)
            def _():
                idx_copy(i + 1, 1 - sl).start()

            def issue(r, c):
                src = pl.multiple_of(r * SUBLANES, SUBLANES)
                for s in range(MOE_TOPK):
                    dst = pl.multiple_of(idx_sms[sl][MOE_TOPK * r + s], SUBLANES)
                    pltpu.make_async_copy(stage_sc.at[sl, pl.ds(src, SUBLANES), :],
                                          xs_hbm.at[pl.ds(dst, SUBLANES), :],
                                          row_sem.at[sl]).start(priority=s % DMA_PRIORITIES)
                return c

            lax.fori_loop(0, tm, issue, 0, unroll=8)

            @pl.when(i == n - 1)
            def _():
                all_copies(sl)

                @pl.when(n >= 2)
                def _():
                    all_copies(1 - sl)


def _dispatch(x1t, dest, tail, n_used, n_blk):
    t = dest.shape[0]
    tm = COMBINE_TILE
    pos = (dest * SUBLANES).reshape(t // tm, MOE_TOPK * tm)
    grid_spec = pltpu.PrefetchScalarGridSpec(
        num_scalar_prefetch=2,
        grid=(t // tm,),
        in_specs=[pl.BlockSpec(memory_space=pl.ANY),
                  pl.BlockSpec((tm * SUBLANES, LANES), lambda i, tl, nu: (i, 0))],
        out_specs=pl.BlockSpec(memory_space=pl.ANY),
        scratch_shapes=[pltpu.SMEM((MOE_TOPK * tm,), I32), pltpu.SMEM((MOE_TOPK * tm,), I32),
                        pltpu.SemaphoreType.DMA((2,)), pltpu.VMEM((MOE_BLOCK * SUBLANES, LANES), F32),
                        pltpu.SemaphoreType.DMA, pltpu.VMEM((2, tm * SUBLANES, LANES), F32),
                        pltpu.SemaphoreType.DMA((2,))],
    )
    return pl.pallas_call(
        _dispatch_kernel,
        grid_spec=grid_spec,
        out_shape=jax.ShapeDtypeStruct((n_blk * MOE_BLOCK * SUBLANES, LANES), F32),
        compiler_params=_cparams("arbitrary"),
        name="moe_dispatch",
    )(tail, n_used, pos, x1t)


def _gather_ahead(idx_hbm, idx_sms, idx_sem, issue_rows, n_active):
    i = pl.program_id(0)
    n = pl.num_programs(0)

    def idx_copy(step, slot):
        return pltpu.make_async_copy(idx_hbm.at[step], idx_sms[slot], idx_sem.at[slot])

    @pl.when(i == 0)
    def _():
        first = idx_copy(0, 0)
        first.start()
        first.wait()

        @pl.when(0 < n_active)
        def _():
            issue_rows(0)

        @pl.when(1 < n)
        def _():
            idx_copy(1, 1).start()

    for slot in range(2):
        @pl.when(jnp.logical_and(i + 1 < n, (i + 1) % 2 == slot))
        def _():
            idx_copy(i + 1, slot).wait()

            @pl.when(i + 1 < n_active)
            def _():
                issue_rows(slot)

        @pl.when(jnp.logical_and(i + 2 < n, i % 2 == slot))
        def _():
            idx_copy(i + 2, slot).start()


def _expert_kernel(blk_e_ref, nused_ref, xs_ref, gf_ref, wgu_ref, wd_ref, yst_ref, wgu_sc, wd_sc):
    i = pl.program_id(0)

    @pl.when(i >= nused_ref[0])
    def _():
        yst_ref[...] = jnp.zeros(yst_ref.shape, F32)

    @pl.when(jnp.logical_or(i == 0, blk_e_ref[i] != blk_e_ref[jnp.maximum(i - 1, 0)]))
    def _():
        wgu_sc[...] = wgu_ref[0, 0].astype(BF16)
        wd_sc[...] = wd_ref[0, 0].astype(BF16)

    @pl.when(i < nused_ref[0])
    def _():
        blk = xs_ref.shape[0] // SUBLANES
        x = _load_token_tiles(xs_ref, blk)
        hn = _rms(x, gf_ref[...], NORM_EPS).astype(BF16)
        gu = _dot(hn, wgu_sc[...])
        eh = gu.shape[1] // 2
        g = gu[:, :eh]
        act = (g * jax.nn.sigmoid(g) * gu[:, eh:]).astype(BF16)
        _store_token_tiles(yst_ref, _dot(act, wd_sc[...]))


def _experts(xs, blk_e, n_used, gf, w_gu, w_down, layer):
    d = gf.shape[0]
    blk = MOE_BLOCK
    n_blk = xs.shape[0] // (blk * SUBLANES)
    eh2 = w_gu.shape[3]
    rows = lambda i, e, n: (jnp.minimum(i, n[0] - 1), 0)
    grid_spec = pltpu.PrefetchScalarGridSpec(
        num_scalar_prefetch=2,
        grid=(n_blk,),
        in_specs=[
            pl.BlockSpec((blk * SUBLANES, LANES), rows),
            pl.BlockSpec((1, d), lambda i, e, n: (0, 0)),
            pl.BlockSpec((1, 1, d, eh2), lambda i, e, n: (layer, e[i], 0, 0)),
            pl.BlockSpec((1, 1, eh2 // 2, d), lambda i, e, n: (layer, e[i], 0, 0)),
        ],
        out_specs=pl.BlockSpec((blk * SUBLANES, LANES), lambda i, e, n: (i, 0)),
        scratch_shapes=[pltpu.VMEM((d, eh2), BF16), pltpu.VMEM((eh2 // 2, d), BF16)],
    )
    return pl.pallas_call(
        _expert_kernel,
        grid_spec=grid_spec,
        out_shape=jax.ShapeDtypeStruct(xs.shape, F32),
        compiler_params=_cparams("arbitrary"),
        name="moe_experts",
    )(blk_e, n_used, xs, gf.reshape(1, d), w_gu, w_down)


def _combine_kernel(pos_hbm, yst_hbm, x1_ref, rt_ref, p_ref, gn_ref, wg_ref, wp_ref, fn_ref, o_ref,
                    idx_sm0, idx_sm1, idx_sem, ybuf, row_sem, *, final):
    slot = pl.program_id(0) % 2
    tm = x1_ref.shape[0]
    idx_sms = (idx_sm0, idx_sm1)

    def issue_rows(sl):
        def issue(r, c):
            dst = pl.multiple_of(r * SUBLANES, SUBLANES)
            for s in range(MOE_TOPK):
                src = pl.multiple_of(idx_sms[sl][MOE_TOPK * r + s], SUBLANES)
                pltpu.make_async_copy(yst_hbm.at[pl.ds(src, SUBLANES), :],
                                      ybuf.at[sl, s, pl.ds(dst, SUBLANES), :],
                                      row_sem.at[sl]).start(priority=s % DMA_PRIORITIES)
            return c

        lax.fori_loop(0, tm, issue, 0, unroll=8)

    _gather_ahead(pos_hbm, idx_sms, idx_sem, issue_rows, pl.num_programs(0))
    for s in range(MOE_TOPK):
        pltpu.make_async_copy(yst_hbm.at[pl.ds(0, tm * SUBLANES), :], ybuf.at[slot, s], row_sem.at[slot]).wait()

    rt = rt_ref[...]
    y0 = _load_token_tiles(ybuf.at[slot, 0], tm)
    y1 = _load_token_tiles(ybuf.at[slot, 1], tm)
    x2 = x1_ref[...] + (rt[:, 0:1] * y0 + rt[:, 1:2] * y1)
    gate = jax.nn.sigmoid(_dot(_rms(x2, gn_ref[...], NORM_EPS).astype(BF16), wg_ref[...]))
    x3 = x2 + _dot(p_ref[...].astype(BF16), wp_ref[...]) * gate
    if final:
        x3 = _rms(x3, fn_ref[...], NORM_EPS)
    o_ref[...] = x3


def _combine(x1, route, yst, dest, p_all, layer, ple_norm, ple_gate, ple_proj, final_norm, final):
    t, d = x1.shape
    tm = COMBINE_TILE
    pd = p_all.shape[1]
    pos = (dest * SUBLANES).reshape(t // tm, MOE_TOPK * tm)
    row = lambda i: (i, 0)
    fixed = lambda i: (0, 0)
    return pl.pallas_call(
        functools.partial(_combine_kernel, final=final),
        grid=(t // tm,),
        in_specs=[pl.BlockSpec(memory_space=pl.ANY), pl.BlockSpec(memory_space=pl.ANY),
                  pl.BlockSpec((tm, d), row), pl.BlockSpec((tm, LANES), row),
                  pl.BlockSpec((tm, pd), lambda i: (layer * (t // tm) + i, 0)),
                  pl.BlockSpec((1, d), fixed), pl.BlockSpec((d, d), fixed), pl.BlockSpec((pd, d), fixed),
                  pl.BlockSpec((1, d), fixed)],
        out_specs=pl.BlockSpec((tm, d), row),
        out_shape=jax.ShapeDtypeStruct((t, d), F32),
        scratch_shapes=[pltpu.SMEM((MOE_TOPK * tm,), I32), pltpu.SMEM((MOE_TOPK * tm,), I32),
                        pltpu.SemaphoreType.DMA((2,)),
                        pltpu.VMEM((2, MOE_TOPK, tm * SUBLANES, LANES), F32), pltpu.SemaphoreType.DMA((2,))],
        compiler_params=_cparams("arbitrary"),
        name="moe_combine_ple",
    )(pos, yst, x1, route, p_all, ple_norm.reshape(1, d), ple_gate, ple_proj, final_norm.reshape(1, d))


def _seg_sum(x, eseg_ref, eexp_ref):
    s = _dot(x.astype(BF16), eseg_ref[...])
    return _dot(s.astype(BF16), eexp_ref[...])


def _rwkv_pre_kernel(x_ref, xp_ref, gm_ref, mu_ref, vec_ref, wr_ref, wk_ref, wv_ref, w1_ref, w2_ref,
                     a1_ref, a2_ref, g1_ref, g2_ref, tri_ref, eseg_ref, eexp_ref,
                     la_ref, lr_ref, rb_ref, rk_ref, sb_ref, sk_ref, v_ref, bon_ref, g_ref, gc_ref,
                     c_sc):
    tl = x_ref.shape[0]
    gm = gm_ref[...]
    h = _rms(x_ref[...], gm, NORM_EPS)
    hp = _rms(xp_ref[...], gm, NORM_EPS)[7:8, :]
    hp = jnp.where(pl.program_id(1) == 0, 0.0, hp)
    row = lax.broadcasted_iota(I32, h.shape, 0)
    hs = jnp.where(row == 0, hp, pltpu.roll(h, 1, axis=0))
    xx = hs - h
    mix = lambda n: (h + xx * mu_ref[n:n + 1, :]).astype(BF16)
    vec = lambda n: vec_ref[n:n + 1, :]
    r = _dot(mix(0), wr_ref[...])
    wl = vec(0) + _dot(jnp.tanh(_dot(mix(1), w1_ref[...])).astype(BF16), w2_ref[...])
    lw = -math.exp(-0.5) * jax.nn.sigmoid(wl)
    k = _dot(mix(2), wk_ref[...])
    v = _dot(mix(3), wv_ref[...])
    a = jax.nn.sigmoid(vec(1) + _dot(_dot(mix(4), a1_ref[...]).astype(BF16), a2_ref[...]))
    g = _dot(jax.nn.sigmoid(_dot(mix(5), g1_ref[...])).astype(BF16), g2_ref[...])
    kk = k * vec(2)
    kk = kk * lax.rsqrt(jnp.maximum(_seg_sum(kk * kk, eseg_ref, eexp_ref), 1e-24))
    k2 = k * (1.0 + (a - 1.0) * vec(3))
    bon_ref[...] = (_seg_sum(r * k2 * vec(4), eseg_ref, eexp_ref) * v).astype(bon_ref.dtype)
    v_ref[...] = v.astype(v_ref.dtype)
    g_ref[...] = g.astype(g_ref.dtype)

    half = tri_ref.shape[1]
    tri = tri_ref[...]
    for s in range(tl // half):
        p1, p2 = _split2(lw[s * half:(s + 1) * half, :])
        c_sc[s * half:(s + 1) * half, :] = _dot(tri, p1) + _dot(tri, p2)
    c_in = c_sc[...]
    nc = tl // RWKV_CHUNK
    ends = [c_sc[(n + 1) * RWKV_CHUNK - 1:(n + 1) * RWKV_CHUNK, :] for n in range(nc)]
    c_end = jnp.concatenate([ends[n] - c_sc[n * RWKV_CHUNK:(n + 1) * RWKV_CHUNK, :] for n in range(nc)],
                            axis=0)
    gc_ref[...] = jnp.exp(jnp.concatenate(ends, axis=0))
    e_in = jnp.exp(c_in)
    e_neg = jnp.exp(-c_in)
    e_end = jnp.exp(c_end)
    b = kk * a
    la_ref[...] = (-kk * jnp.exp(c_in - lw)).astype(la_ref.dtype)
    lr_ref[...] = (r * e_in).astype(lr_ref.dtype)
    rb_ref[...] = (b * e_neg).astype(rb_ref.dtype)
    rk_ref[...] = (k2 * e_neg).astype(rk_ref.dtype)
    sb_ref[...] = (b * e_end).astype(sb_ref.dtype)
    sk_ref[...] = (k2 * e_end).astype(sk_ref.dtype)


def _head_indicator(d):
    heads = d // RWKV_HEAD
    e = (jnp.arange(d)[:, None] // RWKV_HEAD == jnp.arange(LANES)[None, :]).astype(BF16)
    del heads
    return e, e.T


def _cumsum_matrix(half):
    i = jnp.arange(half)[:, None]
    j = jnp.arange(half)[None, :]
    return (((i // RWKV_CHUNK) == (j // RWKV_CHUNK)) & (j <= i)).astype(BF16)


def _pad_cols(w, n):
    return jnp.pad(w, ((0, 0), (0, n - w.shape[1])))


def _pad_rows(w, n):
    return jnp.pad(w, ((0, n - w.shape[0]), (0, 0)))


def _rwkv_pre(x3, gm, mu, vecs, wr, wk, wv, w1, w2, a1, a2, g1, g2, batch, seq):
    t, d = x3.shape
    tl = RWKV_TILE
    nt = seq // tl
    half = 256
    lora = lambda w_in, w_out: (_pad_cols(w_in, -(-w_in.shape[1] // LANES) * LANES).astype(BF16),
                                _pad_rows(w_out, -(-w_out.shape[0] // LANES) * LANES).astype(BF16))
    w1p, w2p = lora(w1, w2)
    a1p, a2p = lora(a1, a2)
    g1p, g2p = lora(g1, g2)
    eseg, eexp = _head_indicator(d)
    tri = _cumsum_matrix(half)
    row = lambda b, s: (b * nt + s, 0)
    fixed = lambda b, s: (0, 0)
    full = lambda a: pl.BlockSpec(a.shape, fixed)
    prev = lambda b, s: (jnp.maximum((b * nt + s) * (tl // 8) - 1, 0), 0)
    weights = (wr.astype(BF16), wk.astype(BF16), wv.astype(BF16), w1p, w2p, a1p, a2p, g1p, g2p, tri, eseg, eexp)
    outs = pl.pallas_call(
        _rwkv_pre_kernel,
        grid=(batch, nt),
        in_specs=[pl.BlockSpec((tl, d), row), pl.BlockSpec((8, d), prev), pl.BlockSpec((1, d), fixed),
                  full(mu), full(vecs)] + [full(w) for w in weights],
        out_specs=[pl.BlockSpec((tl, d), row)] * 9 + [pl.BlockSpec((tl // RWKV_CHUNK, d), row)],
        out_shape=[jax.ShapeDtypeStruct((t, d), BF16)] * 9 + [jax.ShapeDtypeStruct((t // RWKV_CHUNK, d), F32)],
        scratch_shapes=[pltpu.VMEM((tl, d), F32)],
        compiler_params=_cparams("arbitrary", "arbitrary"),
        name="rwkv_pre",
    )(x3, x3, gm.reshape(1, d), mu, vecs, *weights)
    return outs


def _rwkv_scan_kernel(la_ref, lr_ref, rb_ref, rk_ref, sb_ref, sk_ref, v_ref, gc_ref, bon_ref, g_ref, x_ref,
                      vec_ref, wo_ref, eseg_ref, eexp_ref, gf_ref, wrh_ref, wrl_ref, br_ref, tri_ref,
                      x4_ref, x4t_ref, rt_ref, rtt_ref, size_ref, s_sc, y_sc, tot_sc):
    tl, d = x_ref.shape
    ch = RWKV_CHUNK
    pw = 2 * RWKV_HEAD
    npair = d // pw

    @pl.when(pl.program_id(1) == 0)
    def _():
        s_sc[...] = jnp.zeros(s_sc.shape, F32)

    ri = lax.broadcasted_iota(I32, (pw, pw), 0)
    ci = lax.broadcasted_iota(I32, (pw, pw), 1)
    same_half = (ri < RWKV_HEAD) == (ci < RWKV_HEAD)
    eye = (ri == ci).astype(F32)
    rt_ = lax.broadcasted_iota(I32, (ch, pw), 0)
    ct_ = lax.broadcasted_iota(I32, (ch, pw), 1)
    first = ct_ < RWKV_HEAD
    strict = rt_ > (ct_ & (RWKV_HEAD - 1))
    incl = rt_ >= (ct_ & (RWKV_HEAD - 1))
    m_s0, m_s1 = strict & first, strict & ~first
    m_i0, m_i1 = incl & first, incl & ~first
    first_full = ci < RWKV_HEAD
    zf = jnp.zeros((ch, pw), F32)
    gc_row = lax.broadcasted_iota(I32, (tl // ch, pw), 0)

    group = RWKV_CHUNKS_PER_ITER

    def chunks(cg, carry):
        pairs = range(npair)
        cols = [slice(p * pw, (p + 1) * pw) for p in pairs]
        cidx = [cg * group + k for k in range(group)]
        rows = [pl.ds(pl.multiple_of(c * ch, ch), ch) for c in cidx]
        units = [(k, p) for k in range(group) for p in pairs]
        un = range(len(units))
        cat0 = lambda a, b: jnp.concatenate([a, b], axis=0)
        ld = lambda ref, n: ref[rows[units[n][0]], cols[units[n][1]]]
        lcat = [cat0(ld(la_ref, n), ld(lr_ref, n)) for n in un]
        rb = [ld(rb_ref, n) for n in un]
        rk = [ld(rk_ref, n) for n in un]
        vv = [ld(v_ref, n) for n in un]
        zb = jnp.zeros_like(lcat[0])
        a0 = [_dot_nt(jnp.where(first_full, lcat[n], zb), cat0(rb[n], rk[n])) for n in un]
        a1 = [_dot_nt(jnp.where(first_full, zb, lcat[n]), cat0(rk[n], rb[n])) for n in un]
        pm = [cat0(jnp.where(m_s0, a0[n][:ch], zf), jnp.where(m_s1, a1[n][:ch], zf)) for n in un]
        aak = [cat0(jnp.where(m_s1, a0[n][:ch], zf), jnp.where(m_s0, a1[n][:ch], zf)).astype(BF16)
               for n in un]
        arbk = [jnp.concatenate([cat0(jnp.where(m_i0, a0[n][ch:], zf), jnp.where(m_i1, a1[n][ch:], zf)),
                                 cat0(jnp.where(m_i1, a0[n][ch:], zf), jnp.where(m_i0, a1[n][ch:], zf))],
                                axis=1).astype(BF16) for n in un]
        v2 = [cat0(vv[n], vv[n]) for n in un]
        av = [_dot(aak[n], v2[n]) for n in un]
        tm = [eye + pm[n] for n in un]
        qm = pm
        for _ in range(5):
            qb = [q.astype(BF16) for q in qm]
            qm = [_dot(qb[n], qb[n]) for n in un]
            tm = [tm[n] + _dot(tm[n].astype(BF16), qm[n].astype(BF16)) for n in un]
        tmb = [t.astype(BF16) for t in tm]
        for k in range(group):
            nk = [k * npair + p for p in pairs]
            st = [s_sc[p] for p in pairs]
            w12 = [_dot_nt(lcat[n], st[p].astype(BF16)) for p, n in zip(pairs, nk)]
            rhs = [cat0(w12[p][:ch], w12[p][:ch]) + av[n] for p, n in zip(pairs, nk)]
            ust = [_dot(tmb[n], rhs[p].astype(BF16)) for p, n in zip(pairs, nk)]
            yst = [cat0(w12[p][ch:], w12[p][ch:]) + _dot(arbk[n], cat0(ust[p].astype(BF16), v2[n]))
                   for p, n in zip(pairs, nk)]
            for p in pairs:
                y_sc[rows[k], cols[p]] = jnp.where(first, yst[p][:ch], yst[p][ch:])
            u = [jnp.where(first, ust[p][:ch], ust[p][ch:]).astype(BF16) for p in pairs]
            ds_ = [_dot_tn(cat0(u[p], vv[n]), cat0(sb_ref[rows[k], cols[p]], sk_ref[rows[k], cols[p]]))
                   for p, n in zip(pairs, nk)]
            for p in pairs:
                gcr = jnp.sum(jnp.where(gc_row == cidx[k], gc_ref[:, cols[p]], 0.0), axis=0, keepdims=True)
                s_sc[p] = jnp.where(same_half, st[p] * gcr + ds_[p], 0.0)
        return carry

    lax.fori_loop(0, tl // (ch * group), chunks, 0)

    y = y_sc[...]
    inv_n = 1.0 / RWKV_HEAD
    mean = _seg_sum(y, eseg_ref, eexp_ref) * inv_n
    dlt = y - mean
    var = _seg_sum(dlt * dlt, eseg_ref, eexp_ref) * inv_n
    yn = dlt * lax.rsqrt(var + RWKV_GN_EPS) * vec_ref[5:6, :] + vec_ref[6:7, :]
    z = (yn + bon_ref[...].astype(F32)) * g_ref[...].astype(F32)
    x4 = x_ref[...] + _dot(z.astype(BF16), wo_ref[...])
    x4_ref[...] = x4
    _store_token_tiles(x4t_ref, x4)
    first = jnp.logical_and(pl.program_id(0) == 0, pl.program_id(1) == 0)
    rt = _route(x4, gf_ref, wrh_ref, wrl_ref, br_ref, tri_ref, tot_sc, size_ref, first)
    rt_ref[...] = rt
    rtt_ref[...] = rt.T[:SUBLANES, :]


def _rwkv_scan(pre, x3, vecs, wo, gf, router, batch, seq):
    la, lr, rb, rk, sb, sk, v, bon, g, gc = pre
    t, d = x3.shape
    tl = RWKV_TILE
    nt = seq // tl
    eseg, eexp = _head_indicator(d)
    wrh, wrl, br = router
    row = lambda b, s: (b * nt + s, 0)
    fixed = lambda b, s: (0, 0)
    big = pl.BlockSpec((tl, d), row)
    full = lambda a: pl.BlockSpec(a.shape, fixed)
    pw = 2 * RWKV_HEAD
    tri = _strict_lower(tl)
    return pl.pallas_call(
        _rwkv_scan_kernel,
        grid=(batch, nt),
        in_specs=[big] * 7 + [pl.BlockSpec((tl // RWKV_CHUNK, d), row), big, big, big,
                              full(vecs), full(wo), full(eseg), full(eexp),
                              pl.BlockSpec((1, d), fixed), full(wrh), full(wrl), full(br), full(tri)],
        out_specs=[big, pl.BlockSpec((tl * SUBLANES, LANES), row), pl.BlockSpec((tl, LANES), row),
                   pl.BlockSpec((SUBLANES, tl), lambda b, s: (0, b * nt + s)),
                   pl.BlockSpec((SUBLANES, LANES), fixed)],
        out_shape=[jax.ShapeDtypeStruct((t, d), F32), jax.ShapeDtypeStruct((t * SUBLANES, LANES), F32),
                   jax.ShapeDtypeStruct((t, LANES), F32), jax.ShapeDtypeStruct((SUBLANES, t), F32),
                   jax.ShapeDtypeStruct((SUBLANES, LANES), F32)],
        scratch_shapes=[pltpu.VMEM((d // pw, pw, pw), F32), pltpu.VMEM((tl, d), F32),
                        pltpu.VMEM((1, LANES), F32)],
        compiler_params=_cparams("arbitrary", "arbitrary"),
        name="rwkv_scan_out",
    )(la, lr, rb, rk, sb, sk, v, gc, bon, g, x3, vecs, wo, eseg, eexp, gf.reshape(1, d), wrh, wrl, br, tri)


def _moe_and_embed(x1, x1t, route, route_t, sizes, i, p, norm_ffn, moe_w_gu, moe_w_down, ple_norm, ple_gate,
                   ple_proj, final_norm, final):
    t = x1.shape[0]
    blk_e, n_used, dest, tail, n_blk = _moe_plan(route_t, sizes, t)
    xs = _dispatch(x1t, dest, tail, n_used, n_blk)
    yst = _experts(xs, blk_e, n_used, norm_ffn[i], moe_w_gu, moe_w_down, i)
    p_all = p.reshape(-1, p.shape[-1])
    return _combine(x1, route, yst, dest, p_all, i, ple_norm[i], ple_gate[i].astype(BF16),
                    ple_proj[i].astype(BF16), final_norm, final)


def kernel(x, p, rel_bias, norm_mix, ab_w_in, ab_w_out, lam_q1, lam_k1, lam_q2, lam_k2, sub_g, conv_w, conv_b, lru_wa, lru_ba, lru_wx, lru_bx, lru_lambda, rwkv_mu, rwkv_wr, rwkv_wk, rwkv_wv, rwkv_wo, rwkv_w0, rwkv_w1, rwkv_w2, rwkv_a0, rwkv_a1, rwkv_a2, rwkv_g1, rwkv_g2, rwkv_kk, rwkv_ka, rwkv_rk, rwkv_ln_g, rwkv_ln_b, norm_ffn, moe_wc, moe_bc, moe_wf, moe_bf, moe_w_gu, moe_w_down, ple_norm, ple_gate, ple_proj, final_norm):
    batch, seq, d = x.shape
    t = batch * seq
    depth = norm_mix.shape[0]
    assert depth == 2 and seq % RWKV_TILE == 0 and seq % LRU_TILE == 0 and t % ROW_TILE == 0
    assert d == SUBLANES * LANES and seq % ATTN_TILE == 0 and t % COMBINE_TILE == 0
    xs = x.reshape(t, d)
    for i in range(depth):
        j = i // 2
        router = _router_params(moe_wc[i], moe_bc[i], moe_wf[i], moe_bf[i])
        if i % 2 == 0:
            lam_init = 0.8 - 0.6 * math.exp(-0.3 * i)
            q, k, vt, xb, gb = _inproj(xs, norm_mix[i], ab_w_in[j].astype(BF16))
            lam_rows = jnp.stack([lam_q1[j], lam_k1[j], lam_q2[j], lam_k2[j]]).astype(F32)
            ya = _diff_attention(q, k, vt, rel_bias, lam_rows, sub_g[j], lam_init, batch, seq)
            yb = _rglru(xb, gb, conv_w[j], conv_b[j], lru_wa[j], lru_ba[j], lru_wx[j], lru_bx[j],
                        lru_lambda[j], batch, seq)
            x1, x1t, route, route_t, sizes = _outproj(xs, ya, yb, ab_w_out[j].astype(BF16), norm_ffn[i],
                                                      router)
        else:
            vecs = jnp.stack([rwkv_w0[j], rwkv_a0[j], rwkv_kk[j], rwkv_ka[j], rwkv_rk[j].reshape(-1),
                              rwkv_ln_g[j], rwkv_ln_b[j], jnp.zeros((d,), F32)]).astype(F32)
            pre = _rwkv_pre(xs, norm_mix[i], rwkv_mu[j], vecs, rwkv_wr[j], rwkv_wk[j], rwkv_wv[j],
                            rwkv_w1[j], rwkv_w2[j], rwkv_a1[j], rwkv_a2[j], rwkv_g1[j], rwkv_g2[j],
                            batch, seq)
            x1, x1t, route, route_t, sizes = _rwkv_scan(pre, xs, vecs, rwkv_wo[j].astype(BF16), norm_ffn[i],
                                                        router, batch, seq)
        xs = _moe_and_embed(x1, x1t, route, route_t, sizes, i, p, norm_ffn, moe_w_gu, moe_w_down, ple_norm, ple_gate,
                            ple_proj, final_norm, final=(i == depth - 1))
    return xs.reshape(batch, seq, d)
```

```python
import functools
import math

import jax
import jax.numpy as jnp
from jax import lax
from jax.experimental import pallas as pl
from jax.experimental.pallas import tpu as pltpu

F32, BF16, I32 = jnp.float32, jnp.bfloat16, jnp.int32

A_HEADS = 4
A_HEAD_DIM = 64
A_SCALE = A_HEAD_DIM ** -0.5
LOG2E = math.log2(math.e)
REL_BUCKETS = 32
REL_MAX_DIST = 128
LRU_BLOCKS = 8
CONV_W = 4
LRU_C = 8.0
RWKV_HEAD = 64
RWKV_GN_EPS = 64e-5
MOE_GROUPS = 4
EXPERTS_PER_GROUP = 8
N_EXPERTS = MOE_GROUPS * EXPERTS_PER_GROUP
MOE_TOPK = 2
NORM_EPS = 1e-6
SUBNORM_EPS = 1e-5

LANES = 128
VMEM_LIMIT = 56 * 1024 * 1024
ROW_TILE = 512
ATTN_TILE = 512
LRU_TILE = 512
RWKV_CHUNK = 64
RWKV_TILE = 512
RWKV_CHUNKS_PER_ITER = 2
MOE_BLOCK = 512
COMBINE_TILE = 256
NEG_BIG = -1e30


def _cparams(*sem, flags=None):
    return pltpu.CompilerParams(dimension_semantics=sem, vmem_limit_bytes=VMEM_LIMIT, flags=flags)


def _rms(x, g, eps):
    return x * lax.rsqrt(jnp.mean(x * x, axis=-1, keepdims=True) + eps) * g


def _dot(a, b):
    return jnp.dot(a, b, preferred_element_type=F32)


def _dot_nt(a, b):
    return lax.dot_general(a, b, (((1,), (1,)), ((), ())), preferred_element_type=F32)


def _dot_tn(a, b):
    return lax.dot_general(a, b, (((0,), (0,)), ((), ())), preferred_element_type=F32)


def _split2(x):
    hi = x.astype(BF16)
    lo = (x - hi.astype(F32)).astype(BF16)
    return hi, lo


def _softplus(x):
    return jnp.maximum(x, 0.0) + jnp.log1p(jnp.exp(-jnp.abs(x)))


def _neg_expm1(y, exp_y):
    poly = 1.0 + y * (1.0 / 5.0)
    for d in (4.0, 3.0, 2.0):
        poly = 1.0 + (y * (1.0 / d)) * poly
    return jnp.where(y > -1.0 / 16.0, -y * poly, 1.0 - exp_y)


SUBLANES = 8
DMA_PRIORITIES = 2


def _store_token_tiles(ref, x):
    rows = x.shape[0]
    for s in range(x.shape[1] // LANES):
        ref[pl.ds(s, rows, stride=SUBLANES), :] = x[:, s * LANES:(s + 1) * LANES]


def _load_token_tiles(ref, rows):
    return jnp.concatenate([ref[pl.ds(s, rows, stride=SUBLANES), :] for s in range(SUBLANES)], axis=1)


def _gelu_tanh(x):
    c = math.sqrt(2.0 / math.pi)
    return 0.5 * x * (1.0 + jnp.tanh(c * (x + 0.044715 * (x * x * x))))


def _inproj_kernel(x_ref, g_ref, w_ref, q_ref, k_ref, vt_ref, xb_ref, gb_ref):
    hn = _rms(x_ref[...], g_ref[...], NORM_EPS).astype(BF16)
    width = q_ref.shape[1]
    for c, o_ref in enumerate((q_ref, k_ref, vt_ref, xb_ref, gb_ref)):
        r = _dot(hn, w_ref[:, c * width:(c + 1) * width])
        if c == 0:
            r = r * (A_SCALE * LOG2E)
        if c == 2:
            r = r.T
        o_ref[...] = r.astype(o_ref.dtype)


def _inproj(x2, g, w_in):
    t, d = x2.shape
    width = w_in.shape[1] // 5
    tm = ROW_TILE
    row = lambda i: (i, 0)
    fixed = lambda i: (0, 0)
    rows = pl.BlockSpec((tm, width), row)
    return pl.pallas_call(
        _inproj_kernel,
        grid=(t // tm,),
        in_specs=[pl.BlockSpec((tm, d), row), pl.BlockSpec((1, d), fixed),
                  pl.BlockSpec(w_in.shape, fixed)],
        out_specs=[rows, rows, pl.BlockSpec((width, tm), lambda i: (0, i)), rows, rows],
        out_shape=[jax.ShapeDtypeStruct((t, width), BF16)] * 2 + [jax.ShapeDtypeStruct((width, t), BF16)]
        + [jax.ShapeDtypeStruct((t, width), F32)] * 2,
        compiler_params=_cparams("arbitrary"),
        name="inproj",
    )(x2, g.reshape(1, d), w_in)


def _t5_bucket(rel):
    n = jnp.maximum(rel, 0)
    max_exact = REL_BUCKETS // 2
    large = max_exact + (jnp.log(jnp.maximum(n, 1).astype(F32) / max_exact)
                         / math.log(REL_MAX_DIST / max_exact)
                         * (REL_BUCKETS - max_exact)).astype(I32)
    large = jnp.minimum(large, REL_BUCKETS - 1)
    return jnp.where(n < max_exact, n, large)


def _bias_tables(rel_bias, tile):
    span = 2 * tile

    def toeplitz(first_rel):
        rel = jnp.arange(span - 1) + first_rel
        vals = jnp.where((rel >= 0)[:, None], rel_bias[_t5_bucket(rel)].astype(F32), NEG_BIG)
        w = jnp.concatenate([vals, jnp.zeros((1, vals.shape[1]), F32)], axis=0).T
        skew = jnp.tile(w, (1, tile))[:, :tile * (span - 1)].reshape(-1, tile, span - 1)
        return skew[:, :, tile - 1:]

    return jnp.stack([toeplitz(1 - tile), toeplitz(1)], axis=1)


def _attn_kernel(cfar_ref, q_ref, k_ref, vt_ref, bias_ref, lam_ref, subg_ref, o_ref,
                 m_sc, l_sc, acc_sc, sa_sc, sb_sc, *, lam_init):
    h = pl.program_id(1)
    i = pl.program_id(2)
    tile = q_ref.shape[0]
    q = q_ref[...]
    lane = lax.broadcasted_iota(I32, q.shape, 1)
    zero = jnp.zeros_like(q)
    qm = (jnp.where(lane < A_HEAD_DIM, q, zero), jnp.where(lane >= A_HEAD_DIM, q, zero))
    m_sc[...] = jnp.full(m_sc.shape, NEG_BIG, F32)
    l_sc[...] = jnp.zeros(l_sc.shape, F32)
    acc_sc[...] = jnp.zeros(acc_sc.shape, F32)

    maps = range(2)

    def scores(j):
        kb = k_ref[pl.ds(pl.multiple_of(j * tile, tile), tile), :]
        return [_dot_nt(kb, qm[mi]) for mi in maps]

    def scores_to(dst, j):
        s = scores(j)
        for mi in maps:
            dst[mi] = s[mi]

    def consume(s, j, table, const):
        vtb = vt_ref[:, pl.ds(pl.multiple_of(j * tile, tile), tile)]
        if table is not None:
            s = [s[mi] + table for mi in maps]
        half = tile // 2
        for mi in maps:
            for qs in (slice(0, half), slice(half, tile)):
                sq = s[mi][:, qs]
                m_prev = m_sc[mi, :, qs]
                m_new = jnp.maximum(m_prev, jnp.max(sq, axis=0, keepdims=True) + const)
                p = jnp.exp2(sq - (m_new - const))
                alpha = jnp.exp2(m_prev - m_new)
                l_sc[mi, :, qs] = alpha * l_sc[mi, :, qs] + jnp.sum(p, axis=0, keepdims=True)
                acc_sc[mi, :, qs] = alpha * acc_sc[mi, :, qs] + _dot(vtb, p.astype(BF16))
                m_sc[mi, :, qs] = m_new

    held = lambda buf: [buf[mi] for mi in maps]
    cfar = cfar_ref[h]
    nfar = jnp.maximum(i - 1, 0)

    scores_to(sa_sc, 0)

    def far_pair(j):
        scores_to(sb_sc, j + 1)
        consume(held(sa_sc), j, None, cfar)
        scores_to(sa_sc, j + 2)
        consume(held(sb_sc), j + 1, None, cfar)

    def far_quad(t, c):
        far_pair(4 * t)
        far_pair(4 * t + 2)
        return c

    nquad = lax.shift_right_logical(nfar, 2)
    lax.fori_loop(0, nquad, far_quad, 0)

    @pl.when((nfar & 2) != 0)
    def _():
        far_pair(4 * nquad)

    @pl.when(jnp.logical_and(i >= 1, nfar % 2 == 0))
    def _():
        scores_to(sb_sc, i)
        consume(held(sa_sc), i - 1, bias_ref[0, 1], 0.0)
        consume(held(sb_sc), i, bias_ref[0, 0], 0.0)

    @pl.when(nfar % 2 == 1)
    def _():
        scores_to(sb_sc, i - 1)
        consume(held(sa_sc), i - 2, None, cfar)
        scores_to(sa_sc, i)
        consume(held(sb_sc), i - 1, bias_ref[0, 1], 0.0)
        consume(held(sa_sc), i, bias_ref[0, 0], 0.0)

    @pl.when(i == 0)
    def _():
        consume(held(sa_sc), 0, bias_ref[0, 0], 0.0)

    lam_rows = lam_ref[...]
    lam = (jnp.exp(jnp.sum(lam_rows[0:1] * lam_rows[1:2], axis=-1, keepdims=True))
           - jnp.exp(jnp.sum(lam_rows[2:3] * lam_rows[3:4], axis=-1, keepdims=True)) + lam_init)
    o = acc_sc[0] / l_sc[0] - lam * (acc_sc[1] / l_sc[1])
    o = o * lax.rsqrt(jnp.mean(o * o, axis=0, keepdims=True) + SUBNORM_EPS)
    o_ref[...] = (o.T * subg_ref[...] * (1.0 - lam_init)).astype(o_ref.dtype)


def _diff_attention(q, k, vt, rel_bias, lam_rows, sub_g, lam_init, batch, seq):
    t, width = q.shape
    hw = 2 * A_HEAD_DIM
    tile = ATTN_TILE
    nq = seq // tile
    tables = _bias_tables(rel_bias, tile) * LOG2E
    cfar = rel_bias[REL_BUCKETS - 1].astype(F32) * LOG2E
    return pl.pallas_call(
        functools.partial(_attn_kernel, lam_init=lam_init),
        grid=(batch, A_HEADS, nq),
        in_specs=[
            pl.BlockSpec(memory_space=pltpu.SMEM),
            pl.BlockSpec((tile, hw), lambda b, h, i: (b * nq + i, h)),
            pl.BlockSpec((seq, hw), lambda b, h, i: (b, h)),
            pl.BlockSpec((hw, seq), lambda b, h, i: (h, b)),
            pl.BlockSpec((1, 2, tile, tile), lambda b, h, i: (h, 0, 0, 0)),
            pl.BlockSpec(lam_rows.shape, lambda b, h, i: (0, 0)),
            pl.BlockSpec((1, hw), lambda b, h, i: (0, 0)),
        ],
        out_specs=pl.BlockSpec((tile, hw), lambda b, h, i: (b * nq + i, h)),
        out_shape=jax.ShapeDtypeStruct((t, width), BF16),
        scratch_shapes=[pltpu.VMEM((2, 1, tile), F32), pltpu.VMEM((2, 1, tile), F32),
                        pltpu.VMEM((2, hw, tile), F32),
                        pltpu.VMEM((2, tile, tile), F32), pltpu.VMEM((2, tile, tile), F32)],
        compiler_params=_cparams("arbitrary", "arbitrary", "arbitrary"),
        name="diff_attn",
    )(cfar, q, k, vt, tables, lam_rows, sub_g.reshape(1, hw))


def _lru_kernel(xb_ref, gb_ref, cw_ref, vec_ref, wa_ref, wx_ref, y_ref, prev_sc, h_sc, a_sc, b_sc, hs_sc):
    ts = xb_ref.shape[0]

    @pl.when(pl.program_id(1) == 0)
    def _():
        prev_sc[...] = jnp.zeros(prev_sc.shape, F32)
        h_sc[...] = jnp.zeros(h_sc.shape, F32)

    xb = xb_ref[...]
    hist = prev_sc.shape[0]
    ext = jnp.concatenate([prev_sc[...], xb], axis=0)
    u = vec_ref[0:1, :]
    for j in range(CONV_W):
        off = hist - (CONV_W - 1) + j
        u = u + cw_ref[j:j + 1, :] * ext[off:off + ts, :]
    prev_sc[...] = xb[ts - hist:, :]
    ub = u.astype(BF16)
    r = jax.nn.sigmoid(_dot(ub, wa_ref[...]) + vec_ref[1:2, :])
    ig = jax.nn.sigmoid(_dot(ub, wx_ref[...]) + vec_ref[2:3, :])
    log_a = (-LRU_C * r) * _softplus(-vec_ref[3:4, :])
    a = jnp.exp(log_a)
    a_sc[...] = a
    b_sc[...] = jnp.sqrt(_neg_expm1(2.0 * log_a, a * a)) * (ig * u)

    def step(tt, h):
        h = a_sc[pl.ds(tt, 1), :] * h + b_sc[pl.ds(tt, 1), :]
        hs_sc[pl.ds(tt, 1), :] = h
        return h

    h_sc[...] = lax.fori_loop(0, ts, step, h_sc[...], unroll=16)
    y_ref[...] = (hs_sc[...] * _gelu_tanh(gb_ref[...])).astype(y_ref.dtype)


def _block_diag(w):
    n, c, d = w.shape
    eye = jnp.eye(n, dtype=w.dtype)
    return (w[:, :, None, :] * eye[:, None, :, None]).reshape(n * c, n * d)


def _rglru(xb, gb, conv_w, conv_b, wa, ba, wx, bx, lru_lambda, batch, seq):
    t, w = xb.shape
    ts = LRU_TILE
    nt = seq // ts
    vec = jnp.zeros((8, w), F32).at[0].set(conv_b).at[1].set(ba).at[2].set(bx).at[3].set(lru_lambda)
    row = lambda b, s: (b * nt + s, 0)
    fixed = lambda b, s: (0, 0)
    return pl.pallas_call(
        _lru_kernel,
        grid=(batch, nt),
        in_specs=[pl.BlockSpec((ts, w), row), pl.BlockSpec((ts, w), row),
                  pl.BlockSpec((CONV_W, w), fixed), pl.BlockSpec((8, w), fixed),
                  pl.BlockSpec((w, w), fixed), pl.BlockSpec((w, w), fixed)],
        out_specs=pl.BlockSpec((ts, w), row),
        out_shape=jax.ShapeDtypeStruct((t, w), BF16),
        scratch_shapes=[pltpu.VMEM((8, w), F32), pltpu.VMEM((1, w), F32),
                        pltpu.VMEM((ts, w), F32), pltpu.VMEM((ts, w), F32), pltpu.VMEM((ts, w), F32)],
        compiler_params=_cparams("arbitrary", "arbitrary"),
        name="rglru",
    )(xb, gb, conv_w, vec, _block_diag(wa).astype(BF16), _block_diag(wx).astype(BF16))


def _route(x, gf_ref, wrh_ref, wrl_ref, br_ref, tri_ref, tot_sc, size_ref, first_step):
    @pl.when(first_step)
    def _():
        tot_sc[...] = jnp.zeros(tot_sc.shape, F32)

    hn = _rms(x, gf_ref[...], NORM_EPS)
    hh, hl = _split2(hn)
    wh = wrh_ref[...]
    lg = _dot(hh, wh) + _dot(hh, wrl_ref[...]) + _dot(hl, wh) + br_ref[...]
    lane = lax.broadcasted_iota(I32, lg.shape, 1)
    big = jnp.int32(1 << 20)
    is_g = lane < MOE_GROUPS
    gl = jnp.where(is_g, lg, NEG_BIG)
    gmax = jnp.max(gl, axis=-1, keepdims=True)
    gidx = jnp.min(jnp.where(gl == gmax, lane, big), axis=-1, keepdims=True)
    gsum = jnp.sum(jnp.where(is_g, jnp.exp(gl - gmax), 0.0), axis=-1, keepdims=True)
    gprob = 1.0 / gsum
    lo = MOE_GROUPS + gidx * EXPERTS_PER_GROUP
    fmask = (lane >= lo) & (lane < lo + EXPERTS_PER_GROUP)
    fl = jnp.where(fmask, lg, NEG_BIG)
    f1 = jnp.max(fl, axis=-1, keepdims=True)
    i1 = jnp.min(jnp.where(fl == f1, lane, big), axis=-1, keepdims=True)
    fl2 = jnp.where(lane == i1, NEG_BIG, fl)
    f2 = jnp.max(fl2, axis=-1, keepdims=True)
    i2 = jnp.min(jnp.where(fl2 == f2, lane, big), axis=-1, keepdims=True)
    e2 = jnp.exp(f2 - f1)
    den = 1.0 + e2
    g1 = gprob / den
    g2 = gprob * e2 / den
    id1 = (i1 - MOE_GROUPS).astype(F32)
    id2 = (i2 - MOE_GROUPS).astype(F32)
    onehot = jnp.where((lane == i1) | (lane == i2), 1.0, 0.0)
    before = _dot(tri_ref[...], onehot.astype(BF16)) + tot_sc[...]
    r1 = jnp.sum(jnp.where(lane == i1, before, 0.0), axis=-1, keepdims=True)
    r2 = jnp.sum(jnp.where(lane == i2, before, 0.0), axis=-1, keepdims=True)
    tot = tot_sc[...] + jnp.sum(onehot, axis=0, keepdims=True)
    tot_sc[...] = tot
    size_ref[...] = jnp.broadcast_to(tot, size_ref.shape)
    vals = (g1, g2, id1, id2, r1, r2)
    out = jnp.zeros(lg.shape, F32)
    for n, v in enumerate(vals):
        out = jnp.where(lane == n, v, out)
    return out


def _router_params(wc, bc, wf, bf):
    d = wc.shape[0]
    n = MOE_GROUPS + N_EXPERTS
    w = jnp.zeros((d, LANES), F32).at[:, :MOE_GROUPS].set(wc).at[:, MOE_GROUPS:n].set(wf)
    b = jnp.zeros((1, LANES), F32).at[0, :MOE_GROUPS].set(bc).at[0, MOE_GROUPS:n].set(bf)
    hi, lo = _split2(w)
    return hi, lo, b


def _outproj_kernel(x_ref, ya_ref, yb_ref, wo_ref, gf_ref, wrh_ref, wrl_ref, br_ref, tri_ref,
                    x1_ref, x1t_ref, rt_ref, rtt_ref, size_ref, tot_sc):
    half = ya_ref.shape[1]
    x1 = x_ref[...] + _dot(ya_ref[...], wo_ref[:half, :]) + _dot(yb_ref[...], wo_ref[half:, :])
    x1_ref[...] = x1
    _store_token_tiles(x1t_ref, x1)
    rt = _route(x1, gf_ref, wrh_ref, wrl_ref, br_ref, tri_ref, tot_sc, size_ref, pl.program_id(0) == 0)
    rt_ref[...] = rt
    rtt_ref[...] = rt.T[:SUBLANES, :]


def _outproj(x2, ya, yb, w_out, gf, router):
    t, d = x2.shape
    half = ya.shape[1]
    tm = ROW_TILE
    row = lambda i: (i, 0)
    fixed = lambda i: (0, 0)
    wrh, wrl, br = router
    tri = _strict_lower(tm)
    return pl.pallas_call(
        _outproj_kernel,
        grid=(t // tm,),
        in_specs=[pl.BlockSpec((tm, d), row), pl.BlockSpec((tm, half), row), pl.BlockSpec((tm, half), row),
                  pl.BlockSpec((2 * half, d), fixed), pl.BlockSpec((1, d), fixed),
                  pl.BlockSpec((d, LANES), fixed), pl.BlockSpec((d, LANES), fixed),
                  pl.BlockSpec((1, LANES), fixed), pl.BlockSpec(tri.shape, fixed)],
        out_specs=[pl.BlockSpec((tm, d), row), pl.BlockSpec((tm * SUBLANES, LANES), row),
                   pl.BlockSpec((tm, LANES), row), pl.BlockSpec((SUBLANES, tm), lambda i: (0, i)),
                   pl.BlockSpec((SUBLANES, LANES), fixed)],
        out_shape=[jax.ShapeDtypeStruct((t, d), F32), jax.ShapeDtypeStruct((t * SUBLANES, LANES), F32),
                   jax.ShapeDtypeStruct((t, LANES), F32), jax.ShapeDtypeStruct((SUBLANES, t), F32),
                   jax.ShapeDtypeStruct((SUBLANES, LANES), F32)],
        scratch_shapes=[pltpu.VMEM((1, LANES), F32)],
        compiler_params=_cparams("arbitrary"),
        name="outproj_route",
    )(x2, ya, yb, w_out, gf.reshape(1, d), wrh, wrl, br, tri)


def _strict_lower(n):
    return (jnp.arange(n)[None, :] < jnp.arange(n)[:, None]).astype(BF16)


def _moe_plan(route_t, sizes_tile, n_tok):
    m = n_tok * MOE_TOPK
    n_blk = -(-(m + N_EXPERTS * (MOE_BLOCK - 1)) // MOE_BLOCK)
    eid = route_t[2:4].astype(I32).T
    rank = route_t[4:6].astype(I32).T
    sizes = sizes_tile[0, MOE_GROUPS:MOE_GROUPS + N_EXPERTS].astype(I32)
    padded = (sizes + MOE_BLOCK - 1) // MOE_BLOCK * MOE_BLOCK
    pend = jnp.cumsum(padded)
    dest = (pend - padded)[eid] + rank
    blk_start = jnp.arange(n_blk, dtype=I32) * MOE_BLOCK
    blk_e = jnp.minimum(jnp.sum((pend[None, :] <= blk_start[:, None]).astype(I32), axis=1),
                        N_EXPERTS - 1).astype(I32)
    n_used = (pend[-1] // MOE_BLOCK).astype(I32).reshape(1)
    tail = jnp.where(padded > 0, pend - MOE_BLOCK, -1).astype(I32)
    return blk_e, n_used, dest.astype(I32), tail, n_blk


def _dispatch_kernel(tail_ref, nused_ref, dest_hbm, xt_ref, xs_hbm,
                     idx_sm0, idx_sm1, idx_sem, zero_sc, zero_sem, stage_sc, row_sem):
    i = pl.program_id(0)
    n = pl.num_programs(0)
    tm = idx_sm0.shape[0] // MOE_TOPK
    idx_sms = (idx_sm0, idx_sm1)
    tile_rows = MOE_BLOCK * SUBLANES
    n_blk = xs_hbm.shape[0] // tile_rows

    @pl.when(i == 0)
    def _():
        zero_sc[...] = jnp.zeros(zero_sc.shape, F32)

        def clear(row, wait):
            cp = pltpu.make_async_copy(
                zero_sc, xs_hbm.at[pl.ds(pl.multiple_of(row * SUBLANES, SUBLANES), tile_rows), :], zero_sem)
            if wait:
                cp.wait()
            else:
                cp.start()

        def fill(e, wait):
            @pl.when(tail_ref[e] >= 0)
            def _():
                clear(tail_ref[e], wait)

        for wait in (False, True):
            lax.fori_loop(0, N_EXPERTS, lambda e, c, w=wait: (fill(e, w), c)[1], 0)
            lax.fori_loop(nused_ref[0], n_blk, lambda b, c, w=wait: (clear(b * MOE_BLOCK, w), c)[1], 0)

    def idx_copy(step, sl):
        return pltpu.make_async_copy(dest_hbm.at[step], idx_sms[sl], idx_sem.at[sl])

    def all_copies(sl):
        for s in range(MOE_TOPK):
            pltpu.make_async_copy(stage_sc.at[sl], xs_hbm.at[pl.ds(0, tm * SUBLANES), :], row_sem.at[sl]).wait()

    @pl.when(i == 0)
    def _():
        idx_copy(0, 0).start()

    for sl in range(2):
        @pl.when(i % 2 == sl)
        def _():
            @pl.when(i >= 2)
            def _():
                all_copies(sl)

            stage_sc[sl] = xt_ref[...]
            idx_copy(i, sl).wait()

            @pl.when(i + 1 < n)
            def _():
                idx_copy(i + 1, 1 - sl).start()

            def issue(r, c):
                src = pl.multiple_of(r * SUBLANES, SUBLANES)
                for s in range(MOE_TOPK):
                    dst = pl.multiple_of(idx_sms[sl][MOE_TOPK * r + s], SUBLANES)
                    pltpu.make_async_copy(stage_sc.at[sl, pl.ds(src, SUBLANES), :],
                                          xs_hbm.at[pl.ds(dst, SUBLANES), :],
                                          row_sem.at[sl]).start(priority=s % DMA_PRIORITIES)
                return c

            lax.fori_loop(0, tm, issue, 0, unroll=8)

            @pl.when(i == n - 1)
            def _():
                all_copies(sl)

                @pl.when(n >= 2)
                def _():
                    all_copies(1 - sl)


def _dispatch(x1t, dest, tail, n_used, n_blk):
    t = dest.shape[0]
    tm = COMBINE_TILE
    pos = (dest * SUBLANES).reshape(t // tm, MOE_TOPK * tm)
    grid_spec = pltpu.PrefetchScalarGridSpec(
        num_scalar_prefetch=2,
        grid=(t // tm,),
        in_specs=[pl.BlockSpec(memory_space=pl.ANY),
                  pl.BlockSpec((tm * SUBLANES, LANES), lambda i, tl, nu: (i, 0))],
        out_specs=pl.BlockSpec(memory_space=pl.ANY),
        scratch_shapes=[pltpu.SMEM((MOE_TOPK * tm,), I32), pltpu.SMEM((MOE_TOPK * tm,), I32),
                        pltpu.SemaphoreType.DMA((2,)), pltpu.VMEM((MOE_BLOCK * SUBLANES, LANES), F32),
                        pltpu.SemaphoreType.DMA, pltpu.VMEM((2, tm * SUBLANES, LANES), F32),
                        pltpu.SemaphoreType.DMA((2,))],
    )
    return pl.pallas_call(
        _dispatch_kernel,
        grid_spec=grid_spec,
        out_shape=jax.ShapeDtypeStruct((n_blk * MOE_BLOCK * SUBLANES, LANES), F32),
        compiler_params=_cparams("arbitrary"),
        name="moe_dispatch",
    )(tail, n_used, pos, x1t)


def _gather_ahead(idx_hbm, idx_sms, idx_sem, issue_rows, n_active):
    i = pl.program_id(0)
    n = pl.num_programs(0)

    def idx_copy(step, slot):
        return pltpu.make_async_copy(idx_hbm.at[step], idx_sms[slot], idx_sem.at[slot])

    @pl.when(i == 0)
    def _():
        first = idx_copy(0, 0)
        first.start()
        first.wait()

        @pl.when(0 < n_active)
        def _():
            issue_rows(0)

        @pl.when(1 < n)
        def _():
            idx_copy(1, 1).start()

    for slot in range(2):
        @pl.when(jnp.logical_and(i + 1 < n, (i + 1) % 2 == slot))
        def _():
            idx_copy(i + 1, slot).wait()

            @pl.when(i + 1 < n_active)
            def _():
                issue_rows(slot)

        @pl.when(jnp.logical_and(i + 2 < n, i % 2 == slot))
        def _():
            idx_copy(i + 2, slot).start()


def _expert_kernel(blk_e_ref, nused_ref, xs_ref, gf_ref, wgu_ref, wd_ref, yst_ref, wgu_sc, wd_sc):
    i = pl.program_id(0)

    @pl.when(i >= nused_ref[0])
    def _():
        yst_ref[...] = jnp.zeros(yst_ref.shape, F32)

    @pl.when(jnp.logical_or(i == 0, blk_e_ref[i] != blk_e_ref[jnp.maximum(i - 1, 0)]))
    def _():
        wgu_sc[...] = wgu_ref[0, 0].astype(BF16)
        wd_sc[...] = wd_ref[0, 0].astype(BF16)

    @pl.when(i < nused_ref[0])
    def _():
        blk = xs_ref.shape[0] // SUBLANES
        x = _load_token_tiles(xs_ref, blk)
        hn = _rms(x, gf_ref[...], NORM_EPS).astype(BF16)
        gu = _dot(hn, wgu_sc[...])
        eh = gu.shape[1] // 2
        g = gu[:, :eh]
        act = (g * jax.nn.sigmoid(g) * gu[:, eh:]).astype(BF16)
        _store_token_tiles(yst_ref, _dot(act, wd_sc[...]))


def _experts(xs, blk_e, n_used, gf, w_gu, w_down, layer):
    d = gf.shape[0]
    blk = MOE_BLOCK
    n_blk = xs.shape[0] // (blk * SUBLANES)
    eh2 = w_gu.shape[3]
    rows = lambda i, e, n: (jnp.minimum(i, n[0] - 1), 0)
    grid_spec = pltpu.PrefetchScalarGridSpec(
        num_scalar_prefetch=2,
        grid=(n_blk,),
        in_specs=[
            pl.BlockSpec((blk * SUBLANES, LANES), rows),
            pl.BlockSpec((1, d), lambda i, e, n: (0, 0)),
            pl.BlockSpec((1, 1, d, eh2), lambda i, e, n: (layer, e[i], 0, 0)),
            pl.BlockSpec((1, 1, eh2 // 2, d), lambda i, e, n: (layer, e[i], 0, 0)),
        ],
        out_specs=pl.BlockSpec((blk * SUBLANES, LANES), lambda i, e, n: (i, 0)),
        scratch_shapes=[pltpu.VMEM((d, eh2), BF16), pltpu.VMEM((eh2 // 2, d), BF16)],
    )
    return pl.pallas_call(
        _expert_kernel,
        grid_spec=grid_spec,
        out_shape=jax.ShapeDtypeStruct(xs.shape, F32),
        compiler_params=_cparams("arbitrary"),
        name="moe_experts",
    )(blk_e, n_used, xs, gf.reshape(1, d), w_gu, w_down)


def _combine_kernel(pos_hbm, yst_hbm, x1_ref, rt_ref, p_ref, gn_ref, wg_ref, wp_ref, fn_ref, o_ref,
                    idx_sm0, idx_sm1, idx_sem, ybuf, row_sem, *, final):
    slot = pl.program_id(0) % 2
    tm = x1_ref.shape[0]
    idx_sms = (idx_sm0, idx_sm1)

    def issue_rows(sl):
        def issue(r, c):
            dst = pl.multiple_of(r * SUBLANES, SUBLANES)
            for s in range(MOE_TOPK):
                src = pl.multiple_of(idx_sms[sl][MOE_TOPK * r + s], SUBLANES)
                pltpu.make_async_copy(yst_hbm.at[pl.ds(src, SUBLANES), :],
                                      ybuf.at[sl, s, pl.ds(dst, SUBLANES), :],
                                      row_sem.at[sl]).start(priority=s % DMA_PRIORITIES)
            return c

        lax.fori_loop(0, tm, issue, 0, unroll=8)

    _gather_ahead(pos_hbm, idx_sms, idx_sem, issue_rows, pl.num_programs(0))
    for s in range(MOE_TOPK):
        pltpu.make_async_copy(yst_hbm.at[pl.ds(0, tm * SUBLANES), :], ybuf.at[slot, s], row_sem.at[slot]).wait()

    rt = rt_ref[...]
    y0 = _load_token_tiles(ybuf.at[slot, 0], tm)
    y1 = _load_token_tiles(ybuf.at[slot, 1], tm)
    x2 = x1_ref[...] + (rt[:, 0:1] * y0 + rt[:, 1:2] * y1)
    gate = jax.nn.sigmoid(_dot(_rms(x2, gn_ref[...], NORM_EPS).astype(BF16), wg_ref[...]))
    x3 = x2 + _dot(p_ref[...].astype(BF16), wp_ref[...]) * gate
    if final:
        x3 = _rms(x3, fn_ref[...], NORM_EPS)
    o_ref[...] = x3


def _combine(x1, route, yst, dest, p_all, layer, ple_norm, ple_gate, ple_proj, final_norm, final):
    t, d = x1.shape
    tm = COMBINE_TILE
    pd = p_all.shape[1]
    pos = (dest * SUBLANES).reshape(t // tm, MOE_TOPK * tm)
    row = lambda i: (i, 0)
    fixed = lambda i: (0, 0)
    return pl.pallas_call(
        functools.partial(_combine_kernel, final=final),
        grid=(t // tm,),
        in_specs=[pl.BlockSpec(memory_space=pl.ANY), pl.BlockSpec(memory_space=pl.ANY),
                  pl.BlockSpec((tm, d), row), pl.BlockSpec((tm, LANES), row),
                  pl.BlockSpec((tm, pd), lambda i: (layer * (t // tm) + i, 0)),
                  pl.BlockSpec((1, d), fixed), pl.BlockSpec((d, d), fixed), pl.BlockSpec((pd, d), fixed),
                  pl.BlockSpec((1, d), fixed)],
        out_specs=pl.BlockSpec((tm, d), row),
        out_shape=jax.ShapeDtypeStruct((t, d), F32),
        scratch_shapes=[pltpu.SMEM((MOE_TOPK * tm,), I32), pltpu.SMEM((MOE_TOPK * tm,), I32),
                        pltpu.SemaphoreType.DMA((2,)),
                        pltpu.VMEM((2, MOE_TOPK, tm * SUBLANES, LANES), F32), pltpu.SemaphoreType.DMA((2,))],
        compiler_params=_cparams("arbitrary"),
        name="moe_combine_ple",
    )(pos, yst, x1, route, p_all, ple_norm.reshape(1, d), ple_gate, ple_proj, final_norm.reshape(1, d))


def _seg_sum(x, eseg_ref, eexp_ref):
    s = _dot(x.astype(BF16), eseg_ref[...])
    return _dot(s.astype(BF16), eexp_ref[...])


def _rwkv_pre_kernel(x_ref, xp_ref, gm_ref, mu_ref, vec_ref, wr_ref, wk_ref, wv_ref, w1_ref, w2_ref,
                     a1_ref, a2_ref, g1_ref, g2_ref, tri_ref, eseg_ref, eexp_ref,
                     la_ref, lr_ref, rb_ref, rk_ref, sb_ref, sk_ref, v_ref, bon_ref, g_ref, gc_ref,
                     c_sc):
    tl = x_ref.shape[0]
    gm = gm_ref[...]
    h = _rms(x_ref[...], gm, NORM_EPS)
    hp = _rms(xp_ref[...], gm, NORM_EPS)[7:8, :]
    hp = jnp.where(pl.program_id(1) == 0, 0.0, hp)
    row = lax.broadcasted_iota(I32, h.shape, 0)
    hs = jnp.where(row == 0, hp, pltpu.roll(h, 1, axis=0))
    xx = hs - h
    mix = lambda n: (h + xx * mu_ref[n:n + 1, :]).astype(BF16)
    vec = lambda n: vec_ref[n:n + 1, :]
    r = _dot(mix(0), wr_ref[...])
    wl = vec(0) + _dot(jnp.tanh(_dot(mix(1), w1_ref[...])).astype(BF16), w2_ref[...])
    lw = -math.exp(-0.5) * jax.nn.sigmoid(wl)
    k = _dot(mix(2), wk_ref[...])
    v = _dot(mix(3), wv_ref[...])
    a = jax.nn.sigmoid(vec(1) + _dot(_dot(mix(4), a1_ref[...]).astype(BF16), a2_ref[...]))
    g = _dot(jax.nn.sigmoid(_dot(mix(5), g1_ref[...])).astype(BF16), g2_ref[...])
    kk = k * vec(2)
    kk = kk * lax.rsqrt(jnp.maximum(_seg_sum(kk * kk, eseg_ref, eexp_ref), 1e-24))
    k2 = k * (1.0 + (a - 1.0) * vec(3))
    bon_ref[...] = (_seg_sum(r * k2 * vec(4), eseg_ref, eexp_ref) * v).astype(bon_ref.dtype)
    v_ref[...] = v.astype(v_ref.dtype)
    g_ref[...] = g.astype(g_ref.dtype)

    half = tri_ref.shape[1]
    tri = tri_ref[...]
    for s in range(tl // half):
        p1, p2 = _split2(lw[s * half:(s + 1) * half, :])
        c_sc[s * half:(s + 1) * half, :] = _dot(tri, p1) + _dot(tri, p2)
    c_in = c_sc[...]
    nc = tl // RWKV_CHUNK
    ends = [c_sc[(n + 1) * RWKV_CHUNK - 1:(n + 1) * RWKV_CHUNK, :] for n in range(nc)]
    c_end = jnp.concatenate([ends[n] - c_sc[n * RWKV_CHUNK:(n + 1) * RWKV_CHUNK, :] for n in range(nc)],
                            axis=0)
    gc_ref[...] = jnp.exp(jnp.concatenate(ends, axis=0))
    e_in = jnp.exp(c_in)
    e_neg = jnp.exp(-c_in)
    e_end = jnp.exp(c_end)
    b = kk * a
    la_ref[...] = (-kk * jnp.exp(c_in - lw)).astype(la_ref.dtype)
    lr_ref[...] = (r * e_in).astype(lr_ref.dtype)
    rb_ref[...] = (b * e_neg).astype(rb_ref.dtype)
    rk_ref[...] = (k2 * e_neg).astype(rk_ref.dtype)
    sb_ref[...] = (b * e_end).astype(sb_ref.dtype)
    sk_ref[...] = (k2 * e_end).astype(sk_ref.dtype)


def _head_indicator(d):
    heads = d // RWKV_HEAD
    e = (jnp.arange(d)[:, None] // RWKV_HEAD == jnp.arange(LANES)[None, :]).astype(BF16)
    del heads
    return e, e.T


def _cumsum_matrix(half):
    i = jnp.arange(half)[:, None]
    j = jnp.arange(half)[None, :]
    return (((i // RWKV_CHUNK) == (j // RWKV_CHUNK)) & (j <= i)).astype(BF16)


def _pad_cols(w, n):
    return jnp.pad(w, ((0, 0), (0, n - w.shape[1])))


def _pad_rows(w, n):
    return jnp.pad(w, ((0, n - w.shape[0]), (0, 0)))


def _rwkv_pre(x3, gm, mu, vecs, wr, wk, wv, w1, w2, a1, a2, g1, g2, batch, seq):
    t, d = x3.shape
    tl = RWKV_TILE
    nt = seq // tl
    half = 256
    lora = lambda w_in, w_out: (_pad_cols(w_in, -(-w_in.shape[1] // LANES) * LANES).astype(BF16),
                                _pad_rows(w_out, -(-w_out.shape[0] // LANES) * LANES).astype(BF16))
    w1p, w2p = lora(w1, w2)
    a1p, a2p = lora(a1, a2)
    g1p, g2p = lora(g1, g2)
    eseg, eexp = _head_indicator(d)
    tri = _cumsum_matrix(half)
    row = lambda b, s: (b * nt + s, 0)
    fixed = lambda b, s: (0, 0)
    full = lambda a: pl.BlockSpec(a.shape, fixed)
    prev = lambda b, s: (jnp.maximum((b * nt + s) * (tl // 8) - 1, 0), 0)
    weights = (wr.astype(BF16), wk.astype(BF16), wv.astype(BF16), w1p, w2p, a1p, a2p, g1p, g2p, tri, eseg, eexp)
    outs = pl.pallas_call(
        _rwkv_pre_kernel,
        grid=(batch, nt),
        in_specs=[pl.BlockSpec((tl, d), row), pl.BlockSpec((8, d), prev), pl.BlockSpec((1, d), fixed),
                  full(mu), full(vecs)] + [full(w) for w in weights],
        out_specs=[pl.BlockSpec((tl, d), row)] * 9 + [pl.BlockSpec((tl // RWKV_CHUNK, d), row)],
        out_shape=[jax.ShapeDtypeStruct((t, d), BF16)] * 9 + [jax.ShapeDtypeStruct((t // RWKV_CHUNK, d), F32)],
        scratch_shapes=[pltpu.VMEM((tl, d), F32)],
        compiler_params=_cparams("arbitrary", "arbitrary"),
        name="rwkv_pre",
    )(x3, x3, gm.reshape(1, d), mu, vecs, *weights)
    return outs


def _rwkv_scan_kernel(la_ref, lr_ref, rb_ref, rk_ref, sb_ref, sk_ref, v_ref, gc_ref, bon_ref, g_ref, x_ref,
                      vec_ref, wo_ref, eseg_ref, eexp_ref, gf_ref, wrh_ref, wrl_ref, br_ref, tri_ref,
                      x4_ref, x4t_ref, rt_ref, rtt_ref, size_ref, s_sc, y_sc, tot_sc):
    tl, d = x_ref.shape
    ch = RWKV_CHUNK
    pw = 2 * RWKV_HEAD
    npair = d // pw

    @pl.when(pl.program_id(1) == 0)
    def _():
        s_sc[...] = jnp.zeros(s_sc.shape, F32)

    ri = lax.broadcasted_iota(I32, (pw, pw), 0)
    ci = lax.broadcasted_iota(I32, (pw, pw), 1)
    same_half = (ri < RWKV_HEAD) == (ci < RWKV_HEAD)
    eye = (ri == ci).astype(F32)
    rt_ = lax.broadcasted_iota(I32, (ch, pw), 0)
    ct_ = lax.broadcasted_iota(I32, (ch, pw), 1)
    first = ct_ < RWKV_HEAD
    strict = rt_ > (ct_ & (RWKV_HEAD - 1))
    incl = rt_ >= (ct_ & (RWKV_HEAD - 1))
    m_s0, m_s1 = strict & first, strict & ~first
    m_i0, m_i1 = incl & first, incl & ~first
    first_full = ci < RWKV_HEAD
    zf = jnp.zeros((ch, pw), F32)
    gc_row = lax.broadcasted_iota(I32, (tl // ch, pw), 0)

    group = RWKV_CHUNKS_PER_ITER

    def chunks(cg, carry):
        pairs = range(npair)
        cols = [slice(p * pw, (p + 1) * pw) for p in pairs]
        cidx = [cg * group + k for k in range(group)]
        rows = [pl.ds(pl.multiple_of(c * ch, ch), ch) for c in cidx]
        units = [(k, p) for k in range(group) for p in pairs]
        un = range(len(units))
        cat0 = lambda a, b: jnp.concatenate([a, b], axis=0)
        ld = lambda ref, n: ref[rows[units[n][0]], cols[units[n][1]]]
        lcat = [cat0(ld(la_ref, n), ld(lr_ref, n)) for n in un]
        rb = [ld(rb_ref, n) for n in un]
        rk = [ld(rk_ref, n) for n in un]
        vv = [ld(v_ref, n) for n in un]
        zb = jnp.zeros_like(lcat[0])
        a0 = [_dot_nt(jnp.where(first_full, lcat[n], zb), cat0(rb[n], rk[n])) for n in un]
        a1 = [_dot_nt(jnp.where(first_full, zb, lcat[n]), cat0(rk[n], rb[n])) for n in un]
        pm = [cat0(jnp.where(m_s0, a0[n][:ch], zf), jnp.where(m_s1, a1[n][:ch], zf)) for n in un]
        aak = [cat0(jnp.where(m_s1, a0[n][:ch], zf), jnp.where(m_s0, a1[n][:ch], zf)).astype(BF16)
               for n in un]
        arbk = [jnp.concatenate([cat0(jnp.where(m_i0, a0[n][ch:], zf), jnp.where(m_i1, a1[n][ch:], zf)),
                                 cat0(jnp.where(m_i1, a0[n][ch:], zf), jnp.where(m_i0, a1[n][ch:], zf))],
                                axis=1).astype(BF16) for n in un]
        v2 = [cat0(vv[n], vv[n]) for n in un]
        av = [_dot(aak[n], v2[n]) for n in un]
        tm = [eye + pm[n] for n in un]
        qm = pm
        for _ in range(5):
            qb = [q.astype(BF16) for q in qm]
            qm = [_dot(qb[n], qb[n]) for n in un]
            tm = [tm[n] + _dot(tm[n].astype(BF16), qm[n].astype(BF16)) for n in un]
        tmb = [t.astype(BF16) for t in tm]
        for k in range(group):
            nk = [k * npair + p for p in pairs]
            st = [s_sc[p] for p in pairs]
            w12 = [_dot_nt(lcat[n], st[p].astype(BF16)) for p, n in zip(pairs, nk)]
            rhs = [cat0(w12[p][:ch], w12[p][:ch]) + av[n] for p, n in zip(pairs, nk)]
            ust = [_dot(tmb[n], rhs[p].astype(BF16)) for p, n in zip(pairs, nk)]
            yst = [cat0(w12[p][ch:], w12[p][ch:]) + _dot(arbk[n], cat0(ust[p].astype(BF16), v2[n]))
                   for p, n in zip(pairs, nk)]
            for p in pairs:
                y_sc[rows[k], cols[p]] = jnp.where(first, yst[p][:ch], yst[p][ch:])
            u = [jnp.where(first, ust[p][:ch], ust[p][ch:]).astype(BF16) for p in pairs]
            ds_ = [_dot_tn(cat0(u[p], vv[n]), cat0(sb_ref[rows[k], cols[p]], sk_ref[rows[k], cols[p]]))
                   for p, n in zip(pairs, nk)]
            for p in pairs:
                gcr = jnp.sum(jnp.where(gc_row == cidx[k], gc_ref[:, cols[p]], 0.0), axis=0, keepdims=True)
                s_sc[p] = jnp.where(same_half, st[p] * gcr + ds_[p], 0.0)
        return carry

    lax.fori_loop(0, tl // (ch * group), chunks, 0)

    y = y_sc[...]
    inv_n = 1.0 / RWKV_HEAD
    mean = _seg_sum(y, eseg_ref, eexp_ref) * inv_n
    dlt = y - mean
    var = _seg_sum(dlt * dlt, eseg_ref, eexp_ref) * inv_n
    yn = dlt * lax.rsqrt(var + RWKV_GN_EPS) * vec_ref[5:6, :] + vec_ref[6:7, :]
    z = (yn + bon_ref[...].astype(F32)) * g_ref[...].astype(F32)
    x4 = x_ref[...] + _dot(z.astype(BF16), wo_ref[...])
    x4_ref[...] = x4
    _store_token_tiles(x4t_ref, x4)
    first = jnp.logical_and(pl.program_id(0) == 0, pl.program_id(1) == 0)
    rt = _route(x4, gf_ref, wrh_ref, wrl_ref, br_ref, tri_ref, tot_sc, size_ref, first)
    rt_ref[...] = rt
    rtt_ref[...] = rt.T[:SUBLANES, :]


def _rwkv_scan(pre, x3, vecs, wo, gf, router, batch, seq):
    la, lr, rb, rk, sb, sk, v, bon, g, gc = pre
    t, d = x3.shape
    tl = RWKV_TILE
    nt = seq // tl
    eseg, eexp = _head_indicator(d)
    wrh, wrl, br = router
    row = lambda b, s: (b * nt + s, 0)
    fixed = lambda b, s: (0, 0)
    big = pl.BlockSpec((tl, d), row)
    full = lambda a: pl.BlockSpec(a.shape, fixed)
    pw = 2 * RWKV_HEAD
    tri = _strict_lower(tl)
    return pl.pallas_call(
        _rwkv_scan_kernel,
        grid=(batch, nt),
        in_specs=[big] * 7 + [pl.BlockSpec((tl // RWKV_CHUNK, d), row), big, big, big,
                              full(vecs), full(wo), full(eseg), full(eexp),
                              pl.BlockSpec((1, d), fixed), full(wrh), full(wrl), full(br), full(tri)],
        out_specs=[big, pl.BlockSpec((tl * SUBLANES, LANES), row), pl.BlockSpec((tl, LANES), row),
                   pl.BlockSpec((SUBLANES, tl), lambda b, s: (0, b * nt + s)),
                   pl.BlockSpec((SUBLANES, LANES), fixed)],
        out_shape=[jax.ShapeDtypeStruct((t, d), F32), jax.ShapeDtypeStruct((t * SUBLANES, LANES), F32),
                   jax.ShapeDtypeStruct((t, LANES), F32), jax.ShapeDtypeStruct((SUBLANES, t), F32),
                   jax.ShapeDtypeStruct((SUBLANES, LANES), F32)],
        scratch_shapes=[pltpu.VMEM((d // pw, pw, pw), F32), pltpu.VMEM((tl, d), F32),
                        pltpu.VMEM((1, LANES), F32)],
        compiler_params=_cparams("arbitrary", "arbitrary"),
        name="rwkv_scan_out",
    )(la, lr, rb, rk, sb, sk, v, gc, bon, g, x3, vecs, wo, eseg, eexp, gf.reshape(1, d), wrh, wrl, br, tri)


def _moe_and_embed(x1, x1t, route, route_t, sizes, i, p, norm_ffn, moe_w_gu, moe_w_down, ple_norm, ple_gate,
                   ple_proj, final_norm, final):
    t = x1.shape[0]
    blk_e, n_used, dest, tail, n_blk = _moe_plan(route_t, sizes, t)
    xs = _dispatch(x1t, dest, tail, n_used, n_blk)
    yst = _experts(xs, blk_e, n_used, norm_ffn[i], moe_w_gu, moe_w_down, i)
    p_all = p.reshape(-1, p.shape[-1])
    return _combine(x1, route, yst, dest, p_all, i, ple_norm[i], ple_gate[i].astype(BF16),
                    ple_proj[i].astype(BF16), final_norm, final)


def kernel(x, p, rel_bias, norm_mix, ab_w_in, ab_w_out, lam_q1, lam_k1, lam_q2, lam_k2, sub_g, conv_w, conv_b, lru_wa, lru_ba, lru_wx, lru_bx, lru_lambda, rwkv_mu, rwkv_wr, rwkv_wk, rwkv_wv, rwkv_wo, rwkv_w0, rwkv_w1, rwkv_w2, rwkv_a0, rwkv_a1, rwkv_a2, rwkv_g1, rwkv_g2, rwkv_kk, rwkv_ka, rwkv_rk, rwkv_ln_g, rwkv_ln_b, norm_ffn, moe_wc, moe_bc, moe_wf, moe_bf, moe_w_gu, moe_w_down, ple_norm, ple_gate, ple_proj, final_norm):
    batch, seq, d = x.shape
    t = batch * seq
    depth = norm_mix.shape[0]
    assert depth == 2 and seq % RWKV_TILE == 0 and seq % LRU_TILE == 0 and t % ROW_TILE == 0
    assert d == SUBLANES * LANES and seq % ATTN_TILE == 0 and t % COMBINE_TILE == 0
    xs = x.reshape(t, d)
    for i in range(depth):
        j = i // 2
        router = _router_params(moe_wc[i], moe_bc[i], moe_wf[i], moe_bf[i])
        if i % 2 == 0:
            lam_init = 0.8 - 0.6 * math.exp(-0.3 * i)
            q, k, vt, xb, gb = _inproj(xs, norm_mix[i], ab_w_in[j].astype(BF16))
            lam_rows = jnp.stack([lam_q1[j], lam_k1[j], lam_q2[j], lam_k2[j]]).astype(F32)
            ya = _diff_attention(q, k, vt, rel_bias, lam_rows, sub_g[j], lam_init, batch, seq)
            yb = _rglru(xb, gb, conv_w[j], conv_b[j], lru_wa[j], lru_ba[j], lru_wx[j], lru_bx[j],
                        lru_lambda[j], batch, seq)
            x1, x1t, route, route_t, sizes = _outproj(xs, ya, yb, ab_w_out[j].astype(BF16), norm_ffn[i],
                                                      router)
        else:
            vecs = jnp.stack([rwkv_w0[j], rwkv_a0[j], rwkv_kk[j], rwkv_ka[j], rwkv_rk[j].reshape(-1),
                              rwkv_ln_g[j], rwkv_ln_b[j], jnp.zeros((d,), F32)]).astype(F32)
            pre = _rwkv_pre(xs, norm_mix[i], rwkv_mu[j], vecs, rwkv_wr[j], rwkv_wk[j], rwkv_wv[j],
                            rwkv_w1[j], rwkv_w2[j], rwkv_a1[j], rwkv_a2[j], rwkv_g1[j], rwkv_g2[j],
                            batch, seq)
            x1, x1t, route, route_t, sizes = _rwkv_scan(pre, xs, vecs, rwkv_wo[j].astype(BF16), norm_ffn[i],
                                                        router, batch, seq)
        xs = _moe_and_embed(x1, x1t, route, route_t, sizes, i, p, norm_ffn, moe_w_gu, moe_w_down, ple_norm, ple_gate,
                            ple_proj, final_norm, final=(i == depth - 1))
    return xs.reshape(batch, seq, d)
```

```python
import functools
import math

import jax
import jax.numpy as jnp
from jax import lax
from jax.experimental import pallas as pl
from jax.experimental.pallas import tpu as pltpu

F32, BF16, I32 = jnp.float32, jnp.bfloat16, jnp.int32

A_HEADS = 4
A_HEAD_DIM = 64
A_SCALE = A_HEAD_DIM ** -0.5
LOG2E = math.log2(math.e)
REL_BUCKETS = 32
REL_MAX_DIST = 128
CONV_W = 4
LRU_C = 8.0
RWKV_HEAD = 64
RWKV_GN_EPS = 64e-5
MOE_GROUPS = 4
EXPERTS_PER_GROUP = 8
N_EXPERTS = MOE_GROUPS * EXPERTS_PER_GROUP
MOE_TOPK = 2
NORM_EPS = 1e-6
SUBNORM_EPS = 1e-5

LANES = 128
VMEM_LIMIT = 56 * 1024 * 1024
ROW_TILE = 512
ATTN_TILE = 512
LRU_TILE = 512
RWKV_CHUNK = 64
RWKV_TILE = 512
RWKV_CHUNKS_PER_ITER = 2
MOE_BLOCK = 512
COMBINE_TILE = 512
NEG_BIG = -1e30


def _cparams(*sem):
    return pltpu.CompilerParams(dimension_semantics=sem, vmem_limit_bytes=VMEM_LIMIT)


def _rms(x, g, eps):
    return x * lax.rsqrt(jnp.mean(x * x, axis=-1, keepdims=True) + eps) * g


def _dot(a, b):
    return jnp.dot(a, b, preferred_element_type=F32)


def _dot_nt(a, b):
    return lax.dot_general(a, b, (((1,), (1,)), ((), ())), preferred_element_type=F32)


def _dot_tn(a, b):
    return lax.dot_general(a, b, (((0,), (0,)), ((), ())), preferred_element_type=F32)


def _split2(x):
    hi = x.astype(BF16)
    lo = (x - hi.astype(F32)).astype(BF16)
    return hi, lo


def _softplus(x):
    return jnp.maximum(x, 0.0) + jnp.log1p(jnp.exp(-jnp.abs(x)))


def _neg_expm1(y, exp_y):
    poly = 1.0 + y * (1.0 / 5.0)
    for d in (4.0, 3.0, 2.0):
        poly = 1.0 + (y * (1.0 / d)) * poly
    return jnp.where(y > -1.0 / 16.0, -y * poly, 1.0 - exp_y)


SUBLANES = 8
DMA_PRIORITIES = 2


def _store_token_tiles(ref, x):
    rows = x.shape[0]
    for s in range(x.shape[1] // LANES):
        ref[pl.ds(s, rows, stride=SUBLANES), :] = x[:, s * LANES:(s + 1) * LANES]


def _load_token_tiles(ref, rows):
    return jnp.concatenate([ref[pl.ds(s, rows, stride=SUBLANES), :] for s in range(SUBLANES)], axis=1)


def _gelu_tanh(x):
    c = math.sqrt(2.0 / math.pi)
    return 0.5 * x * (1.0 + jnp.tanh(c * (x + 0.044715 * (x * x * x))))


def _inproj_kernel(x_ref, g_ref, w_ref, q_ref, k_ref, vt_ref, xb_ref, gb_ref):
    hn = _rms(x_ref[...], g_ref[...], NORM_EPS).astype(BF16)
    width = q_ref.shape[1]
    for c, o_ref in enumerate((q_ref, k_ref, vt_ref, xb_ref, gb_ref)):
        r = _dot(hn, w_ref[:, c * width:(c + 1) * width])
        if c == 0:
            r = r * (A_SCALE * LOG2E)
        if c == 2:
            r = r.T
        o_ref[...] = r.astype(o_ref.dtype)


def _inproj(x2, g, w_in):
    t, d = x2.shape
    width = w_in.shape[1] // 5
    tm = ROW_TILE
    row = lambda i: (i, 0)
    fixed = lambda i: (0, 0)
    rows = pl.BlockSpec((tm, width), row)
    return pl.pallas_call(
        _inproj_kernel,
        grid=(t // tm,),
        in_specs=[pl.BlockSpec((tm, d), row), pl.BlockSpec((1, d), fixed),
                  pl.BlockSpec(w_in.shape, fixed)],
        out_specs=[rows, rows, pl.BlockSpec((width, tm), lambda i: (0, i)), rows, rows],
        out_shape=[jax.ShapeDtypeStruct((t, width), BF16)] * 2 + [jax.ShapeDtypeStruct((width, t), BF16)]
        + [jax.ShapeDtypeStruct((t, width), F32)] * 2,
        compiler_params=_cparams("arbitrary"),
        name="inproj",
    )(x2, g.reshape(1, d), w_in)


def _t5_bucket(rel):
    n = jnp.maximum(rel, 0)
    max_exact = REL_BUCKETS // 2
    large = max_exact + (jnp.log(jnp.maximum(n, 1).astype(F32) / max_exact)
                         / math.log(REL_MAX_DIST / max_exact)
                         * (REL_BUCKETS - max_exact)).astype(I32)
    large = jnp.minimum(large, REL_BUCKETS - 1)
    return jnp.where(n < max_exact, n, large)


def _bias_tables(rel_bias, tile):
    span = 2 * tile

    def toeplitz(first_rel):
        rel = jnp.arange(span - 1) + first_rel
        vals = jnp.where((rel >= 0)[:, None], rel_bias[_t5_bucket(rel)].astype(F32), NEG_BIG)
        w = jnp.concatenate([vals, jnp.zeros((1, vals.shape[1]), F32)], axis=0).T
        skew = jnp.tile(w, (1, tile))[:, :tile * (span - 1)].reshape(-1, tile, span - 1)
        return skew[:, :, tile - 1:]

    return jnp.stack([toeplitz(1 - tile), toeplitz(1)], axis=1)


def _attn_kernel(cfar_ref, q_ref, k_ref, vt_ref, bias_ref, lam_ref, subg_ref, o_ref,
                 m_sc, l_sc, acc_sc, sa_sc, sb_sc, *, lam_init):
    h = pl.program_id(1)
    i = pl.program_id(2)
    tile = q_ref.shape[0]
    q = q_ref[...]
    lane = lax.broadcasted_iota(I32, q.shape, 1)
    zero = jnp.zeros_like(q)
    qm = (jnp.where(lane < A_HEAD_DIM, q, zero), jnp.where(lane >= A_HEAD_DIM, q, zero))
    m_sc[...] = jnp.full(m_sc.shape, NEG_BIG, F32)
    l_sc[...] = jnp.zeros(l_sc.shape, F32)
    acc_sc[...] = jnp.zeros(acc_sc.shape, F32)

    maps = range(2)

    def scores(j):
        kb = k_ref[pl.ds(pl.multiple_of(j * tile, tile), tile), :]
        return [_dot_nt(kb, qm[mi]) for mi in maps]

    def scores_to(dst, j):
        s = scores(j)
        for mi in maps:
            dst[mi] = s[mi]

    def consume(s, j, table, const):
        vtb = vt_ref[:, pl.ds(pl.multiple_of(j * tile, tile), tile)]
        if table is not None:
            s = [s[mi] + table for mi in maps]
        half = tile // 2
        for mi in maps:
            for qs in (slice(0, half), slice(half, tile)):
                sq = s[mi][:, qs]
                m_prev = m_sc[mi, :, qs]
                m_new = jnp.maximum(m_prev, jnp.max(sq, axis=0, keepdims=True) + const)
                p = jnp.exp2(sq - (m_new - const))
                alpha = jnp.exp2(m_prev - m_new)
                l_sc[mi, :, qs] = alpha * l_sc[mi, :, qs] + jnp.sum(p, axis=0, keepdims=True)
                acc_sc[mi, :, qs] = alpha * acc_sc[mi, :, qs] + _dot(vtb, p.astype(BF16))
                m_sc[mi, :, qs] = m_new

    held = lambda buf: [buf[mi] for mi in maps]
    cfar = cfar_ref[h]
    nfar = jnp.maximum(i - 1, 0)

    scores_to(sa_sc, 0)

    def far_pair(j):
        scores_to(sb_sc, j + 1)
        consume(held(sa_sc), j, None, cfar)
        scores_to(sa_sc, j + 2)
        consume(held(sb_sc), j + 1, None, cfar)

    def far_quad(t, c):
        far_pair(4 * t)
        far_pair(4 * t + 2)
        return c

    nquad = lax.shift_right_logical(nfar, 2)
    lax.fori_loop(0, nquad, far_quad, 0)

    @pl.when((nfar & 2) != 0)
    def _():
        far_pair(4 * nquad)

    @pl.when(jnp.logical_and(i >= 1, nfar % 2 == 0))
    def _():
        scores_to(sb_sc, i)
        consume(held(sa_sc), i - 1, bias_ref[0, 1], 0.0)
        consume(held(sb_sc), i, bias_ref[0, 0], 0.0)

    @pl.when(nfar % 2 == 1)
    def _():
        scores_to(sb_sc, i - 1)
        consume(held(sa_sc), i - 2, None, cfar)
        scores_to(sa_sc, i)
        consume(held(sb_sc), i - 1, bias_ref[0, 1], 0.0)
        consume(held(sa_sc), i, bias_ref[0, 0], 0.0)

    @pl.when(i == 0)
    def _():
        consume(held(sa_sc), 0, bias_ref[0, 0], 0.0)

    lam_rows = lam_ref[...]
    lam = (jnp.exp(jnp.sum(lam_rows[0:1] * lam_rows[1:2], axis=-1, keepdims=True))
           - jnp.exp(jnp.sum(lam_rows[2:3] * lam_rows[3:4], axis=-1, keepdims=True)) + lam_init)
    o = acc_sc[0] / l_sc[0] - lam * (acc_sc[1] / l_sc[1])
    o = o * lax.rsqrt(jnp.mean(o * o, axis=0, keepdims=True) + SUBNORM_EPS)
    o_ref[...] = (o.T * subg_ref[...] * (1.0 - lam_init)).astype(o_ref.dtype)


def _diff_attention(q, k, vt, rel_bias, lam_rows, sub_g, lam_init, batch, seq):
    t, width = q.shape
    hw = 2 * A_HEAD_DIM
    tile = ATTN_TILE
    nq = seq // tile
    tables = _bias_tables(rel_bias, tile) * LOG2E
    cfar = rel_bias[REL_BUCKETS - 1].astype(F32) * LOG2E
    return pl.pallas_call(
        functools.partial(_attn_kernel, lam_init=lam_init),
        grid=(batch, A_HEADS, nq),
        in_specs=[
            pl.BlockSpec(memory_space=pltpu.SMEM),
            pl.BlockSpec((tile, hw), lambda b, h, i: (b * nq + i, h)),
            pl.BlockSpec((seq, hw), lambda b, h, i: (b, h)),
            pl.BlockSpec((hw, seq), lambda b, h, i: (h, b)),
            pl.BlockSpec((1, 2, tile, tile), lambda b, h, i: (h, 0, 0, 0)),
            pl.BlockSpec(lam_rows.shape, lambda b, h, i: (0, 0)),
            pl.BlockSpec((1, hw), lambda b, h, i: (0, 0)),
        ],
        out_specs=pl.BlockSpec((tile, hw), lambda b, h, i: (b * nq + i, h)),
        out_shape=jax.ShapeDtypeStruct((t, width), BF16),
        scratch_shapes=[pltpu.VMEM((2, 1, tile), F32), pltpu.VMEM((2, 1, tile), F32),
                        pltpu.VMEM((2, hw, tile), F32),
                        pltpu.VMEM((2, tile, tile), F32), pltpu.VMEM((2, tile, tile), F32)],
        compiler_params=_cparams("arbitrary", "arbitrary", "arbitrary"),
        name="diff_attn",
    )(cfar, q, k, vt, tables, lam_rows, sub_g.reshape(1, hw))


def _lru_kernel(xb_ref, gb_ref, cw_ref, vec_ref, wa_ref, wx_ref, y_ref, prev_sc, h_sc, a_sc, b_sc, hs_sc):
    ts = xb_ref.shape[0]

    @pl.when(pl.program_id(1) == 0)
    def _():
        prev_sc[...] = jnp.zeros(prev_sc.shape, F32)
        h_sc[...] = jnp.zeros(h_sc.shape, F32)

    xb = xb_ref[...]
    hist = prev_sc.shape[0]
    ext = jnp.concatenate([prev_sc[...], xb], axis=0)
    u = vec_ref[0:1, :]
    for j in range(CONV_W):
        off = hist - (CONV_W - 1) + j
        u = u + cw_ref[j:j + 1, :] * ext[off:off + ts, :]
    prev_sc[...] = xb[ts - hist:, :]
    ub = u.astype(BF16)
    r = jax.nn.sigmoid(_dot(ub, wa_ref[...]) + vec_ref[1:2, :])
    ig = jax.nn.sigmoid(_dot(ub, wx_ref[...]) + vec_ref[2:3, :])
    log_a = (-LRU_C * r) * _softplus(-vec_ref[3:4, :])
    a = jnp.exp(log_a)
    a_sc[...] = a
    b_sc[...] = jnp.sqrt(_neg_expm1(2.0 * log_a, a * a)) * (ig * u)

    def step(tt, h):
        h = a_sc[pl.ds(tt, 1), :] * h + b_sc[pl.ds(tt, 1), :]
        hs_sc[pl.ds(tt, 1), :] = h
        return h

    h_sc[...] = lax.fori_loop(0, ts, step, h_sc[...], unroll=16)
    y_ref[...] = (hs_sc[...] * _gelu_tanh(gb_ref[...])).astype(y_ref.dtype)


def _block_diag(w):
    n, c, d = w.shape
    eye = jnp.eye(n, dtype=w.dtype)
    return (w[:, :, None, :] * eye[:, None, :, None]).reshape(n * c, n * d)


def _rglru(xb, gb, conv_w, conv_b, wa, ba, wx, bx, lru_lambda, batch, seq):
    t, w = xb.shape
    ts = LRU_TILE
    nt = seq // ts
    vec = jnp.zeros((8, w), F32).at[0].set(conv_b).at[1].set(ba).at[2].set(bx).at[3].set(lru_lambda)
    row = lambda b, s: (b * nt + s, 0)
    fixed = lambda b, s: (0, 0)
    return pl.pallas_call(
        _lru_kernel,
        grid=(batch, nt),
        in_specs=[pl.BlockSpec((ts, w), row), pl.BlockSpec((ts, w), row),
                  pl.BlockSpec((CONV_W, w), fixed), pl.BlockSpec((8, w), fixed),
                  pl.BlockSpec((w, w), fixed), pl.BlockSpec((w, w), fixed)],
        out_specs=pl.BlockSpec((ts, w), row),
        out_shape=jax.ShapeDtypeStruct((t, w), BF16),
        scratch_shapes=[pltpu.VMEM((8, w), F32), pltpu.VMEM((1, w), F32),
                        pltpu.VMEM((ts, w), F32), pltpu.VMEM((ts, w), F32), pltpu.VMEM((ts, w), F32)],
        compiler_params=_cparams("arbitrary", "arbitrary"),
        name="rglru",
    )(xb, gb, conv_w, vec, _block_diag(wa).astype(BF16), _block_diag(wx).astype(BF16))


def _route(x, gf_ref, wrh_ref, wrl_ref, br_ref, tri_ref, tot_sc, size_ref, first_step):
    @pl.when(first_step)
    def _():
        tot_sc[...] = jnp.zeros(tot_sc.shape, F32)

    hn = _rms(x, gf_ref[...], NORM_EPS)
    hh, hl = _split2(hn)
    wh = wrh_ref[...]
    lg = _dot(hh, wh) + _dot(hh, wrl_ref[...]) + _dot(hl, wh) + br_ref[...]
    lane = lax.broadcasted_iota(I32, lg.shape, 1)
    big = jnp.int32(1 << 20)
    is_g = lane < MOE_GROUPS
    gl = jnp.where(is_g, lg, NEG_BIG)
    gmax = jnp.max(gl, axis=-1, keepdims=True)
    gidx = jnp.min(jnp.where(gl == gmax, lane, big), axis=-1, keepdims=True)
    gsum = jnp.sum(jnp.where(is_g, jnp.exp(gl - gmax), 0.0), axis=-1, keepdims=True)
    gprob = 1.0 / gsum
    lo = MOE_GROUPS + gidx * EXPERTS_PER_GROUP
    fmask = (lane >= lo) & (lane < lo + EXPERTS_PER_GROUP)
    fl = jnp.where(fmask, lg, NEG_BIG)
    f1 = jnp.max(fl, axis=-1, keepdims=True)
    i1 = jnp.min(jnp.where(fl == f1, lane, big), axis=-1, keepdims=True)
    fl2 = jnp.where(lane == i1, NEG_BIG, fl)
    f2 = jnp.max(fl2, axis=-1, keepdims=True)
    i2 = jnp.min(jnp.where(fl2 == f2, lane, big), axis=-1, keepdims=True)
    e2 = jnp.exp(f2 - f1)
    den = 1.0 + e2
    g1 = gprob / den
    g2 = gprob * e2 / den
    id1 = (i1 - MOE_GROUPS).astype(F32)
    id2 = (i2 - MOE_GROUPS).astype(F32)
    onehot = jnp.where((lane == i1) | (lane == i2), 1.0, 0.0)
    before = _dot(tri_ref[...], onehot.astype(BF16)) + tot_sc[...]
    r1 = jnp.sum(jnp.where(lane == i1, before, 0.0), axis=-1, keepdims=True)
    r2 = jnp.sum(jnp.where(lane == i2, before, 0.0), axis=-1, keepdims=True)
    tot = tot_sc[...] + jnp.sum(onehot, axis=0, keepdims=True)
    tot_sc[...] = tot
    size_ref[...] = jnp.broadcast_to(tot, size_ref.shape)
    vals = (g1, g2, id1, id2, r1, r2)
    out = jnp.zeros(lg.shape, F32)
    for n, v in enumerate(vals):
        out = jnp.where(lane == n, v, out)
    return out


def _router_params(wc, bc, wf, bf):
    d = wc.shape[0]
    n = MOE_GROUPS + N_EXPERTS
    w = jnp.zeros((d, LANES), F32).at[:, :MOE_GROUPS].set(wc).at[:, MOE_GROUPS:n].set(wf)
    b = jnp.zeros((1, LANES), F32).at[0, :MOE_GROUPS].set(bc).at[0, MOE_GROUPS:n].set(bf)
    hi, lo = _split2(w)
    return hi, lo, b


def _outproj_kernel(x_ref, ya_ref, yb_ref, wo_ref, gf_ref, wrh_ref, wrl_ref, br_ref, tri_ref,
                    x1_ref, x1t_ref, rt_ref, rtt_ref, size_ref, tot_sc):
    half = ya_ref.shape[1]
    x1 = x_ref[...] + _dot(ya_ref[...], wo_ref[:half, :]) + _dot(yb_ref[...], wo_ref[half:, :])
    x1_ref[...] = x1
    _store_token_tiles(x1t_ref, x1)
    rt = _route(x1, gf_ref, wrh_ref, wrl_ref, br_ref, tri_ref, tot_sc, size_ref, pl.program_id(0) == 0)
    rt_ref[...] = rt
    rtt_ref[...] = rt.T[:SUBLANES, :]


def _outproj(x2, ya, yb, w_out, gf, router):
    t, d = x2.shape
    half = ya.shape[1]
    tm = ROW_TILE
    row = lambda i: (i, 0)
    fixed = lambda i: (0, 0)
    wrh, wrl, br = router
    tri = _strict_lower(tm)
    return pl.pallas_call(
        _outproj_kernel,
        grid=(t // tm,),
        in_specs=[pl.BlockSpec((tm, d), row), pl.BlockSpec((tm, half), row), pl.BlockSpec((tm, half), row),
                  pl.BlockSpec((2 * half, d), fixed), pl.BlockSpec((1, d), fixed),
                  pl.BlockSpec((d, LANES), fixed), pl.BlockSpec((d, LANES), fixed),
                  pl.BlockSpec((1, LANES), fixed), pl.BlockSpec(tri.shape, fixed)],
        out_specs=[pl.BlockSpec((tm, d), row), pl.BlockSpec((tm * SUBLANES, LANES), row),
                   pl.BlockSpec((tm, LANES), row), pl.BlockSpec((SUBLANES, tm), lambda i: (0, i)),
                   pl.BlockSpec((SUBLANES, LANES), fixed)],
        out_shape=[jax.ShapeDtypeStruct((t, d), F32), jax.ShapeDtypeStruct((t * SUBLANES, LANES), F32),
                   jax.ShapeDtypeStruct((t, LANES), F32), jax.ShapeDtypeStruct((SUBLANES, t), F32),
                   jax.ShapeDtypeStruct((SUBLANES, LANES), F32)],
        scratch_shapes=[pltpu.VMEM((1, LANES), F32)],
        compiler_params=_cparams("arbitrary"),
        name="outproj_route",
    )(x2, ya, yb, w_out, gf.reshape(1, d), wrh, wrl, br, tri)


def _strict_lower(n):
    return (jnp.arange(n)[None, :] < jnp.arange(n)[:, None]).astype(BF16)


def _moe_plan(route_t, sizes_tile, n_tok):
    m = n_tok * MOE_TOPK
    n_blk = -(-(m + N_EXPERTS * (MOE_BLOCK - 1)) // MOE_BLOCK)
    eid = route_t[2:4].astype(I32).T
    rank = route_t[4:6].astype(I32).T
    sizes = sizes_tile[0, MOE_GROUPS:MOE_GROUPS + N_EXPERTS].astype(I32)
    padded = (sizes + MOE_BLOCK - 1) // MOE_BLOCK * MOE_BLOCK
    pend = jnp.cumsum(padded)
    dest = (pend - padded)[eid] + rank
    blk_start = jnp.arange(n_blk, dtype=I32) * MOE_BLOCK
    blk_e = jnp.minimum(jnp.sum((pend[None, :] <= blk_start[:, None]).astype(I32), axis=1),
                        N_EXPERTS - 1).astype(I32)
    n_used = (pend[-1] // MOE_BLOCK).astype(I32).reshape(1)
    tail = jnp.where(padded > 0, pend - MOE_BLOCK, -1).astype(I32)
    return blk_e, n_used, dest.astype(I32), tail, n_blk


def _dispatch_kernel(tail_ref, nused_ref, dest_hbm, xt_ref, xs_hbm,
                     idx_sm0, idx_sm1, idx_sem, zero_sc, zero_sem, stage_sc, row_sem):
    i = pl.program_id(0)
    n = pl.num_programs(0)
    tm = idx_sm0.shape[0] // MOE_TOPK
    idx_sms = (idx_sm0, idx_sm1)
    tile_rows = MOE_BLOCK * SUBLANES
    n_blk = xs_hbm.shape[0] // tile_rows

    @pl.when(i == 0)
    def _():
        zero_sc[...] = jnp.zeros(zero_sc.shape, F32)

        def clear(row, wait):
            cp = pltpu.make_async_copy(
                zero_sc, xs_hbm.at[pl.ds(pl.multiple_of(row * SUBLANES, SUBLANES), tile_rows), :], zero_sem)
            if wait:
                cp.wait()
            else:
                cp.start()

        def fill(e, wait):
            @pl.when(tail_ref[e] >= 0)
            def _():
                clear(tail_ref[e], wait)

        for wait in (False, True):
            lax.fori_loop(0, N_EXPERTS, lambda e, c, w=wait: (fill(e, w), c)[1], 0)
            lax.fori_loop(nused_ref[0], n_blk, lambda b, c, w=wait: (clear(b * MOE_BLOCK, w), c)[1], 0)

    def idx_copy(step, sl):
        return pltpu.make_async_copy(dest_hbm.at[step], idx_sms[sl], idx_sem.at[sl])

    def all_copies(sl):
        for s in range(MOE_TOPK):
            pltpu.make_async_copy(stage_sc.at[sl], xs_hbm.at[pl.ds(0, tm * SUBLANES), :], row_sem.at[sl]).wait()

    @pl.when(i == 0)
    def _():
        idx_copy(0, 0).start()

    for sl in range(2):
        @pl.when(i % 2 == sl)
        def _():
            @pl.when(i >= 2)
            def _():
                all_copies(sl)

            stage_sc[sl] = xt_ref[...]
            idx_copy(i, sl).wait()

            @pl.when(i + 1 < n)
            def _():
                idx_copy(i + 1, 1 - sl).start()

            def issue(r, c):
                src = pl.multiple_of(r * SUBLANES, SUBLANES)
                for s in range(MOE_TOPK):
                    dst = pl.multiple_of(idx_sms[sl][MOE_TOPK * r + s], SUBLANES)
                    pltpu.make_async_copy(stage_sc.at[sl, pl.ds(src, SUBLANES), :],
                                          xs_hbm.at[pl.ds(dst, SUBLANES), :],
                                          row_sem.at[sl]).start(priority=s % DMA_PRIORITIES)
                return c

            lax.fori_loop(0, tm, issue, 0, unroll=8)

            @pl.when(i == n - 1)
            def _():
                all_copies(sl)

                @pl.when(n >= 2)
                def _():
                    all_copies(1 - sl)


def _dispatch(x1t, dest, tail, n_used, n_blk):
    t = dest.shape[0]
    tm = COMBINE_TILE
    pos = (dest * SUBLANES).reshape(t // tm, MOE_TOPK * tm)
    grid_spec = pltpu.PrefetchScalarGridSpec(
        num_scalar_prefetch=2,
        grid=(t // tm,),
        in_specs=[pl.BlockSpec(memory_space=pl.ANY),
                  pl.BlockSpec((tm * SUBLANES, LANES), lambda i, tl, nu: (i, 0))],
        out_specs=pl.BlockSpec(memory_space=pl.ANY),
        scratch_shapes=[pltpu.SMEM((MOE_TOPK * tm,), I32), pltpu.SMEM((MOE_TOPK * tm,), I32),
                        pltpu.SemaphoreType.DMA((2,)), pltpu.VMEM((MOE_BLOCK * SUBLANES, LANES), F32),
                        pltpu.SemaphoreType.DMA, pltpu.VMEM((2, tm * SUBLANES, LANES), F32),
                        pltpu.SemaphoreType.DMA((2,))],
    )
    return pl.pallas_call(
        _dispatch_kernel,
        grid_spec=grid_spec,
        out_shape=jax.ShapeDtypeStruct((n_blk * MOE_BLOCK * SUBLANES, LANES), F32),
        compiler_params=_cparams("arbitrary"),
        name="moe_dispatch",
    )(tail, n_used, pos, x1t)


def _gather_ahead(idx_hbm, idx_sms, idx_sem, issue_rows, n_active):
    i = pl.program_id(0)
    n = pl.num_programs(0)

    def idx_copy(step, slot):
        return pltpu.make_async_copy(idx_hbm.at[step], idx_sms[slot], idx_sem.at[slot])

    @pl.when(i == 0)
    def _():
        first = idx_copy(0, 0)
        first.start()
        first.wait()

        @pl.when(0 < n_active)
        def _():
            issue_rows(0)

        @pl.when(1 < n)
        def _():
            idx_copy(1, 1).start()

    for slot in range(2):
        @pl.when(jnp.logical_and(i + 1 < n, (i + 1) % 2 == slot))
        def _():
            idx_copy(i + 1, slot).wait()

            @pl.when(i + 1 < n_active)
            def _():
                issue_rows(slot)

        @pl.when(jnp.logical_and(i + 2 < n, i % 2 == slot))
        def _():
            idx_copy(i + 2, slot).start()


def _expert_kernel(blk_e_ref, nused_ref, xs_ref, gf_ref, wgu_ref, wd_ref, yst_ref, wgu_sc, wd_sc):
    i = pl.program_id(0)

    @pl.when(i >= nused_ref[0])
    def _():
        yst_ref[...] = jnp.zeros(yst_ref.shape, F32)

    @pl.when(jnp.logical_or(i == 0, blk_e_ref[i] != blk_e_ref[jnp.maximum(i - 1, 0)]))
    def _():
        wgu_sc[...] = wgu_ref[0, 0].astype(BF16)
        wd_sc[...] = wd_ref[0, 0].astype(BF16)

    @pl.when(i < nused_ref[0])
    def _():
        blk = xs_ref.shape[0] // SUBLANES
        x = _load_token_tiles(xs_ref, blk)
        hn = _rms(x, gf_ref[...], NORM_EPS).astype(BF16)
        gu = _dot(hn, wgu_sc[...])
        eh = gu.shape[1] // 2
        g = gu[:, :eh]
        act = (g * jax.nn.sigmoid(g) * gu[:, eh:]).astype(BF16)
        _store_token_tiles(yst_ref, _dot(act, wd_sc[...]))


def _experts(xs, blk_e, n_used, gf, w_gu, w_down, layer):
    d = gf.shape[0]
    blk = MOE_BLOCK
    n_blk = xs.shape[0] // (blk * SUBLANES)
    eh2 = w_gu.shape[3]
    rows = lambda i, e, n: (jnp.minimum(i, n[0] - 1), 0)
    grid_spec = pltpu.PrefetchScalarGridSpec(
        num_scalar_prefetch=2,
        grid=(n_blk,),
        in_specs=[
            pl.BlockSpec((blk * SUBLANES, LANES), rows),
            pl.BlockSpec((1, d), lambda i, e, n: (0, 0)),
            pl.BlockSpec((1, 1, d, eh2), lambda i, e, n: (layer, e[i], 0, 0)),
            pl.BlockSpec((1, 1, eh2 // 2, d), lambda i, e, n: (layer, e[i], 0, 0)),
        ],
        out_specs=pl.BlockSpec((blk * SUBLANES, LANES), lambda i, e, n: (i, 0)),
        scratch_shapes=[pltpu.VMEM((d, eh2), BF16), pltpu.VMEM((eh2 // 2, d), BF16)],
    )
    return pl.pallas_call(
        _expert_kernel,
        grid_spec=grid_spec,
        out_shape=jax.ShapeDtypeStruct(xs.shape, F32),
        compiler_params=_cparams("arbitrary"),
        name="moe_experts",
    )(blk_e, n_used, xs, gf.reshape(1, d), w_gu, w_down)


def _combine_kernel(pos_hbm, yst_hbm, x1_ref, rt_ref, p_ref, gn_ref, wg_ref, wp_ref, fn_ref, o_ref,
                    idx_sm0, idx_sm1, idx_sem, ybuf, row_sem, *, final):
    slot = pl.program_id(0) % 2
    tm = x1_ref.shape[0]
    idx_sms = (idx_sm0, idx_sm1)

    def issue_rows(sl):
        def issue(r, c):
            dst = pl.multiple_of(r * SUBLANES, SUBLANES)
            for s in range(MOE_TOPK):
                src = pl.multiple_of(idx_sms[sl][MOE_TOPK * r + s], SUBLANES)
                pltpu.make_async_copy(yst_hbm.at[pl.ds(src, SUBLANES), :],
                                      ybuf.at[sl, s, pl.ds(dst, SUBLANES), :],
                                      row_sem.at[sl]).start(priority=s % DMA_PRIORITIES)
            return c

        lax.fori_loop(0, tm, issue, 0, unroll=8)

    _gather_ahead(pos_hbm, idx_sms, idx_sem, issue_rows, pl.num_programs(0))
    for s in range(MOE_TOPK):
        pltpu.make_async_copy(yst_hbm.at[pl.ds(0, tm * SUBLANES), :], ybuf.at[slot, s], row_sem.at[slot]).wait()

    rt = rt_ref[...]
    y0 = _load_token_tiles(ybuf.at[slot, 0], tm)
    y1 = _load_token_tiles(ybuf.at[slot, 1], tm)
    x2 = x1_ref[...] + (rt[:, 0:1] * y0 + rt[:, 1:2] * y1)
    gate = jax.nn.sigmoid(_dot(_rms(x2, gn_ref[...], NORM_EPS).astype(BF16), wg_ref[...]))
    x3 = x2 + _dot(p_ref[...].astype(BF16), wp_ref[...]) * gate
    if final:
        x3 = _rms(x3, fn_ref[...], NORM_EPS)
    o_ref[...] = x3


def _combine(x1, route, yst, dest, p_all, layer, ple_norm, ple_gate, ple_proj, final_norm, final):
    t, d = x1.shape
    tm = COMBINE_TILE
    pd = p_all.shape[1]
    pos = (dest * SUBLANES).reshape(t // tm, MOE_TOPK * tm)
    row = lambda i: (i, 0)
    fixed = lambda i: (0, 0)
    return pl.pallas_call(
        functools.partial(_combine_kernel, final=final),
        grid=(t // tm,),
        in_specs=[pl.BlockSpec(memory_space=pl.ANY), pl.BlockSpec(memory_space=pl.ANY),
                  pl.BlockSpec((tm, d), row), pl.BlockSpec((tm, LANES), row),
                  pl.BlockSpec((tm, pd), lambda i: (layer * (t // tm) + i, 0)),
                  pl.BlockSpec((1, d), fixed), pl.BlockSpec((d, d), fixed), pl.BlockSpec((pd, d), fixed),
                  pl.BlockSpec((1, d), fixed)],
        out_specs=pl.BlockSpec((tm, d), row),
        out_shape=jax.ShapeDtypeStruct((t, d), F32),
        scratch_shapes=[pltpu.SMEM((MOE_TOPK * tm,), I32), pltpu.SMEM((MOE_TOPK * tm,), I32),
                        pltpu.SemaphoreType.DMA((2,)),
                        pltpu.VMEM((2, MOE_TOPK, tm * SUBLANES, LANES), F32), pltpu.SemaphoreType.DMA((2,))],
        compiler_params=_cparams("arbitrary"),
        name="moe_combine_ple",
    )(pos, yst, x1, route, p_all, ple_norm.reshape(1, d), ple_gate, ple_proj, final_norm.reshape(1, d))


def _seg_sum(x, eseg_ref, eexp_ref):
    s = _dot(x.astype(BF16), eseg_ref[...])
    return _dot(s.astype(BF16), eexp_ref[...])


def _rwkv_pre_kernel(x_ref, xp_ref, gm_ref, mu_ref, vec_ref, wr_ref, wk_ref, wv_ref, w1_ref, w2_ref,
                     a1_ref, a2_ref, g1_ref, g2_ref, tri_ref, eseg_ref, eexp_ref,
                     la_ref, lr_ref, rb_ref, rk_ref, sb_ref, sk_ref, v_ref, bon_ref, g_ref, gc_ref,
                     c_sc):
    tl = x_ref.shape[0]
    gm = gm_ref[...]
    h = _rms(x_ref[...], gm, NORM_EPS)
    hp = _rms(xp_ref[...], gm, NORM_EPS)[7:8, :]
    hp = jnp.where(pl.program_id(1) == 0, 0.0, hp)
    row = lax.broadcasted_iota(I32, h.shape, 0)
    hs = jnp.where(row == 0, hp, pltpu.roll(h, 1, axis=0))
    xx = hs - h
    mix = lambda n: (h + xx * mu_ref[n:n + 1, :]).astype(BF16)
    vec = lambda n: vec_ref[n:n + 1, :]
    r = _dot(mix(0), wr_ref[...])
    wl = vec(0) + _dot(jnp.tanh(_dot(mix(1), w1_ref[...])).astype(BF16), w2_ref[...])
    lw = -math.exp(-0.5) * jax.nn.sigmoid(wl)
    k = _dot(mix(2), wk_ref[...])
    v = _dot(mix(3), wv_ref[...])
    a = jax.nn.sigmoid(vec(1) + _dot(_dot(mix(4), a1_ref[...]).astype(BF16), a2_ref[...]))
    g = _dot(jax.nn.sigmoid(_dot(mix(5), g1_ref[...])).astype(BF16), g2_ref[...])
    kk = k * vec(2)
    kk = kk * lax.rsqrt(jnp.maximum(_seg_sum(kk * kk, eseg_ref, eexp_ref), 1e-24))
    k2 = k * (1.0 + (a - 1.0) * vec(3))
    bon_ref[...] = (_seg_sum(r * k2 * vec(4), eseg_ref, eexp_ref) * v).astype(bon_ref.dtype)
    v_ref[...] = v.astype(v_ref.dtype)
    g_ref[...] = g.astype(g_ref.dtype)

    half = tri_ref.shape[1]
    tri = tri_ref[...]
    for s in range(tl // half):
        p1, p2 = _split2(lw[s * half:(s + 1) * half, :])
        c_sc[s * half:(s + 1) * half, :] = _dot(tri, p1) + _dot(tri, p2)
    c_in = c_sc[...]
    nc = tl // RWKV_CHUNK
    ends = [c_sc[(n + 1) * RWKV_CHUNK - 1:(n + 1) * RWKV_CHUNK, :] for n in range(nc)]
    c_end = jnp.concatenate([ends[n] - c_sc[n * RWKV_CHUNK:(n + 1) * RWKV_CHUNK, :] for n in range(nc)],
                            axis=0)
    gc_ref[...] = jnp.exp(jnp.concatenate(ends, axis=0))
    e_in = jnp.exp(c_in)
    e_neg = jnp.exp(-c_in)
    e_end = jnp.exp(c_end)
    b = kk * a
    la_ref[...] = (-kk * jnp.exp(c_in - lw)).astype(la_ref.dtype)
    lr_ref[...] = (r * e_in).astype(lr_ref.dtype)
    rb_ref[...] = (b * e_neg).astype(rb_ref.dtype)
    rk_ref[...] = (k2 * e_neg).astype(rk_ref.dtype)
    sb_ref[...] = (b * e_end).astype(sb_ref.dtype)
    sk_ref[...] = (k2 * e_end).astype(sk_ref.dtype)


def _head_indicator(d):
    e = (jnp.arange(d)[:, None] // RWKV_HEAD == jnp.arange(LANES)[None, :]).astype(BF16)
    return e, e.T


def _cumsum_matrix(half):
    i = jnp.arange(half)[:, None]
    j = jnp.arange(half)[None, :]
    return (((i // RWKV_CHUNK) == (j // RWKV_CHUNK)) & (j <= i)).astype(BF16)


def _pad_cols(w, n):
    return jnp.pad(w, ((0, 0), (0, n - w.shape[1])))


def _pad_rows(w, n):
    return jnp.pad(w, ((0, n - w.shape[0]), (0, 0)))


def _rwkv_pre(x3, gm, mu, vecs, wr, wk, wv, w1, w2, a1, a2, g1, g2, batch, seq):
    t, d = x3.shape
    tl = RWKV_TILE
    nt = seq // tl
    half = 256
    lora = lambda w_in, w_out: (_pad_cols(w_in, -(-w_in.shape[1] // LANES) * LANES).astype(BF16),
                                _pad_rows(w_out, -(-w_out.shape[0] // LANES) * LANES).astype(BF16))
    w1p, w2p = lora(w1, w2)
    a1p, a2p = lora(a1, a2)
    g1p, g2p = lora(g1, g2)
    eseg, eexp = _head_indicator(d)
    tri = _cumsum_matrix(half)
    row = lambda b, s: (b * nt + s, 0)
    fixed = lambda b, s: (0, 0)
    full = lambda a: pl.BlockSpec(a.shape, fixed)
    prev = lambda b, s: (jnp.maximum((b * nt + s) * (tl // 8) - 1, 0), 0)
    weights = (wr.astype(BF16), wk.astype(BF16), wv.astype(BF16), w1p, w2p, a1p, a2p, g1p, g2p, tri, eseg, eexp)
    outs = pl.pallas_call(
        _rwkv_pre_kernel,
        grid=(batch, nt),
        in_specs=[pl.BlockSpec((tl, d), row), pl.BlockSpec((8, d), prev), pl.BlockSpec((1, d), fixed),
                  full(mu), full(vecs)] + [full(w) for w in weights],
        out_specs=[pl.BlockSpec((tl, d), row)] * 9 + [pl.BlockSpec((tl // RWKV_CHUNK, d), row)],
        out_shape=[jax.ShapeDtypeStruct((t, d), BF16)] * 9 + [jax.ShapeDtypeStruct((t // RWKV_CHUNK, d), F32)],
        scratch_shapes=[pltpu.VMEM((tl, d), F32)],
        compiler_params=_cparams("arbitrary", "arbitrary"),
        name="rwkv_pre",
    )(x3, x3, gm.reshape(1, d), mu, vecs, *weights)
    return outs


def _rwkv_scan_kernel(la_ref, lr_ref, rb_ref, rk_ref, sb_ref, sk_ref, v_ref, gc_ref, bon_ref, g_ref, x_ref,
                      vec_ref, wo_ref, eseg_ref, eexp_ref, gf_ref, wrh_ref, wrl_ref, br_ref, tri_ref,
                      x4_ref, x4t_ref, rt_ref, rtt_ref, size_ref, s_sc, y_sc, tot_sc):
    tl, d = x_ref.shape
    ch = RWKV_CHUNK
    pw = 2 * RWKV_HEAD
    npair = d // pw

    @pl.when(pl.program_id(1) == 0)
    def _():
        s_sc[...] = jnp.zeros(s_sc.shape, F32)

    ri = lax.broadcasted_iota(I32, (pw, pw), 0)
    ci = lax.broadcasted_iota(I32, (pw, pw), 1)
    same_half = (ri < RWKV_HEAD) == (ci < RWKV_HEAD)
    eye = (ri == ci).astype(F32)
    rt_ = lax.broadcasted_iota(I32, (ch, pw), 0)
    ct_ = lax.broadcasted_iota(I32, (ch, pw), 1)
    first = ct_ < RWKV_HEAD
    strict = rt_ > (ct_ & (RWKV_HEAD - 1))
    incl = rt_ >= (ct_ & (RWKV_HEAD - 1))
    m_s0, m_s1 = strict & first, strict & ~first
    m_i0, m_i1 = incl & first, incl & ~first
    first_full = ci < RWKV_HEAD
    zf = jnp.zeros((ch, pw), F32)
    gc_row = lax.broadcasted_iota(I32, (tl // ch, pw), 0)

    group = RWKV_CHUNKS_PER_ITER

    def chunks(cg, carry):
        pairs = range(npair)
        cols = [slice(p * pw, (p + 1) * pw) for p in pairs]
        cidx = [cg * group + k for k in range(group)]
        rows = [pl.ds(pl.multiple_of(c * ch, ch), ch) for c in cidx]
        units = [(k, p) for k in range(group) for p in pairs]
        un = range(len(units))
        cat0 = lambda a, b: jnp.concatenate([a, b], axis=0)
        ld = lambda ref, n: ref[rows[units[n][0]], cols[units[n][1]]]
        lcat = [cat0(ld(la_ref, n), ld(lr_ref, n)) for n in un]
        rb = [ld(rb_ref, n) for n in un]
        rk = [ld(rk_ref, n) for n in un]
        vv = [ld(v_ref, n) for n in un]
        zb = jnp.zeros_like(lcat[0])
        a0 = [_dot_nt(jnp.where(first_full, lcat[n], zb), cat0(rb[n], rk[n])) for n in un]
        a1 = [_dot_nt(jnp.where(first_full, zb, lcat[n]), cat0(rk[n], rb[n])) for n in un]
        pm = [cat0(jnp.where(m_s0, a0[n][:ch], zf), jnp.where(m_s1, a1[n][:ch], zf)) for n in un]
        aak = [cat0(jnp.where(m_s1, a0[n][:ch], zf), jnp.where(m_s0, a1[n][:ch], zf)).astype(BF16)
               for n in un]
        arbk = [jnp.concatenate([cat0(jnp.where(m_i0, a0[n][ch:], zf), jnp.where(m_i1, a1[n][ch:], zf)),
                                 cat0(jnp.where(m_i1, a0[n][ch:], zf), jnp.where(m_i0, a1[n][ch:], zf))],
                                axis=1).astype(BF16) for n in un]
        v2 = [cat0(vv[n], vv[n]) for n in un]
        av = [_dot(aak[n], v2[n]) for n in un]
        tm = [eye + pm[n] for n in un]
        qm = pm
        for _ in range(5):
            qb = [q.astype(BF16) for q in qm]
            qm = [_dot(qb[n], qb[n]) for n in un]
            tm = [tm[n] + _dot(tm[n].astype(BF16), qm[n].astype(BF16)) for n in un]
        tmb = [t.astype(BF16) for t in tm]
        for k in range(group):
            nk = [k * npair + p for p in pairs]
            st = [s_sc[p] for p in pairs]
            w12 = [_dot_nt(lcat[n], st[p].astype(BF16)) for p, n in zip(pairs, nk)]
            rhs = [cat0(w12[p][:ch], w12[p][:ch]) + av[n] for p, n in zip(pairs, nk)]
            ust = [_dot(tmb[n], rhs[p].astype(BF16)) for p, n in zip(pairs, nk)]
            yst = [cat0(w12[p][ch:], w12[p][ch:]) + _dot(arbk[n], cat0(ust[p].astype(BF16), v2[n]))
                   for p, n in zip(pairs, nk)]
            for p in pairs:
                y_sc[rows[k], cols[p]] = jnp.where(first, yst[p][:ch], yst[p][ch:])
            u = [jnp.where(first, ust[p][:ch], ust[p][ch:]).astype(BF16) for p in pairs]
            ds_ = [_dot_tn(cat0(u[p], vv[n]), cat0(sb_ref[rows[k], cols[p]], sk_ref[rows[k], cols[p]]))
                   for p, n in zip(pairs, nk)]
            for p in pairs:
                gcr = jnp.sum(jnp.where(gc_row == cidx[k], gc_ref[:, cols[p]], 0.0), axis=0, keepdims=True)
                s_sc[p] = jnp.where(same_half, st[p] * gcr + ds_[p], 0.0)
        return carry

    lax.fori_loop(0, tl // (ch * group), chunks, 0)

    y = y_sc[...]
    inv_n = 1.0 / RWKV_HEAD
    mean = _seg_sum(y, eseg_ref, eexp_ref) * inv_n
    dlt = y - mean
    var = _seg_sum(dlt * dlt, eseg_ref, eexp_ref) * inv_n
    yn = dlt * lax.rsqrt(var + RWKV_GN_EPS) * vec_ref[5:6, :] + vec_ref[6:7, :]
    z = (yn + bon_ref[...].astype(F32)) * g_ref[...].astype(F32)
    x4 = x_ref[...] + _dot(z.astype(BF16), wo_ref[...])
    x4_ref[...] = x4
    _store_token_tiles(x4t_ref, x4)
    first = jnp.logical_and(pl.program_id(0) == 0, pl.program_id(1) == 0)
    rt = _route(x4, gf_ref, wrh_ref, wrl_ref, br_ref, tri_ref, tot_sc, size_ref, first)
    rt_ref[...] = rt
    rtt_ref[...] = rt.T[:SUBLANES, :]


def _rwkv_scan(pre, x3, vecs, wo, gf, router, batch, seq):
    la, lr, rb, rk, sb, sk, v, bon, g, gc = pre
    t, d = x3.shape
    tl = RWKV_TILE
    nt = seq // tl
    eseg, eexp = _head_indicator(d)
    wrh, wrl, br = router
    row = lambda b, s: (b * nt + s, 0)
    fixed = lambda b, s: (0, 0)
    big = pl.BlockSpec((tl, d), row)
    full = lambda a: pl.BlockSpec(a.shape, fixed)
    pw = 2 * RWKV_HEAD
    tri = _strict_lower(tl)
    return pl.pallas_call(
        _rwkv_scan_kernel,
        grid=(batch, nt),
        in_specs=[big] * 7 + [pl.BlockSpec((tl // RWKV_CHUNK, d), row), big, big, big,
                              full(vecs), full(wo), full(eseg), full(eexp),
                              pl.BlockSpec((1, d), fixed), full(wrh), full(wrl), full(br), full(tri)],
        out_specs=[big, pl.BlockSpec((tl * SUBLANES, LANES), row), pl.BlockSpec((tl, LANES), row),
                   pl.BlockSpec((SUBLANES, tl), lambda b, s: (0, b * nt + s)),
                   pl.BlockSpec((SUBLANES, LANES), fixed)],
        out_shape=[jax.ShapeDtypeStruct((t, d), F32), jax.ShapeDtypeStruct((t * SUBLANES, LANES), F32),
                   jax.ShapeDtypeStruct((t, LANES), F32), jax.ShapeDtypeStruct((SUBLANES, t), F32),
                   jax.ShapeDtypeStruct((SUBLANES, LANES), F32)],
        scratch_shapes=[pltpu.VMEM((d // pw, pw, pw), F32), pltpu.VMEM((tl, d), F32),
                        pltpu.VMEM((1, LANES), F32)],
        compiler_params=_cparams("arbitrary", "arbitrary"),
        name="rwkv_scan_out",
    )(la, lr, rb, rk, sb, sk, v, gc, bon, g, x3, vecs, wo, eseg, eexp, gf.reshape(1, d), wrh, wrl, br, tri)


def _moe_and_embed(x1, x1t, route, route_t, sizes, i, p, norm_ffn, moe_w_gu, moe_w_down, ple_norm, ple_gate,
                   ple_proj, final_norm, final):
    t = x1.shape[0]
    blk_e, n_used, dest, tail, n_blk = _moe_plan(route_t, sizes, t)
    xs = _dispatch(x1t, dest, tail, n_used, n_blk)
    yst = _experts(xs, blk_e, n_used, norm_ffn[i], moe_w_gu, moe_w_down, i)
    p_all = p.reshape(-1, p.shape[-1])
    return _combine(x1, route, yst, dest, p_all, i, ple_norm[i], ple_gate[i].astype(BF16),
                    ple_proj[i].astype(BF16), final_norm, final)


def kernel(x, p, rel_bias, norm_mix, ab_w_in, ab_w_out, lam_q1, lam_k1, lam_q2, lam_k2, sub_g, conv_w, conv_b, lru_wa, lru_ba, lru_wx, lru_bx, lru_lambda, rwkv_mu, rwkv_wr, rwkv_wk, rwkv_wv, rwkv_wo, rwkv_w0, rwkv_w1, rwkv_w2, rwkv_a0, rwkv_a1, rwkv_a2, rwkv_g1, rwkv_g2, rwkv_kk, rwkv_ka, rwkv_rk, rwkv_ln_g, rwkv_ln_b, norm_ffn, moe_wc, moe_bc, moe_wf, moe_bf, moe_w_gu, moe_w_down, ple_norm, ple_gate, ple_proj, final_norm):
    batch, seq, d = x.shape
    t = batch * seq
    depth = norm_mix.shape[0]
    assert depth == 2 and seq % RWKV_TILE == 0 and seq % LRU_TILE == 0 and t % ROW_TILE == 0
    assert d == SUBLANES * LANES and seq % ATTN_TILE == 0 and t % COMBINE_TILE == 0
    xs = x.reshape(t, d)
    for i in range(depth):
        j = i // 2
        router = _router_params(moe_wc[i], moe_bc[i], moe_wf[i], moe_bf[i])
        if i % 2 == 0:
            lam_init = 0.8 - 0.6 * math.exp(-0.3 * i)
            q, k, vt, xb, gb = _inproj(xs, norm_mix[i], ab_w_in[j].astype(BF16))
            lam_rows = jnp.stack([lam_q1[j], lam_k1[j], lam_q2[j], lam_k2[j]]).astype(F32)
            ya = _diff_attention(q, k, vt, rel_bias, lam_rows, sub_g[j], lam_init, batch, seq)
            yb = _rglru(xb, gb, conv_w[j], conv_b[j], lru_wa[j], lru_ba[j], lru_wx[j], lru_bx[j],
                        lru_lambda[j], batch, seq)
            x1, x1t, route, route_t, sizes = _outproj(xs, ya, yb, ab_w_out[j].astype(BF16), norm_ffn[i],
                                                      router)
        else:
            vecs = jnp.stack([rwkv_w0[j], rwkv_a0[j], rwkv_kk[j], rwkv_ka[j], rwkv_rk[j].reshape(-1),
                              rwkv_ln_g[j], rwkv_ln_b[j], jnp.zeros((d,), F32)]).astype(F32)
            pre = _rwkv_pre(xs, norm_mix[i], rwkv_mu[j], vecs, rwkv_wr[j], rwkv_wk[j], rwkv_wv[j],
                            rwkv_w1[j], rwkv_w2[j], rwkv_a1[j], rwkv_a2[j], rwkv_g1[j], rwkv_g2[j],
                            batch, seq)
            x1, x1t, route, route_t, sizes = _rwkv_scan(pre, xs, vecs, rwkv_wo[j].astype(BF16), norm_ffn[i],
                                                        router, batch, seq)
        xs = _moe_and_embed(x1, x1t, route, route_t, sizes, i, p, norm_ffn, moe_w_gu, moe_w_down, ple_norm, ple_gate,
                            ple_proj, final_norm, final=(i == depth - 1))
    return xs.reshape(batch, seq, d)
```

```python
import functools
import math

import jax
import jax.numpy as jnp
from jax import lax
from jax.experimental import pallas as pl
from jax.experimental.pallas import tpu as pltpu

F32, BF16, I32 = jnp.float32, jnp.bfloat16, jnp.int32

A_HEADS = 4
A_HEAD_DIM = 64
A_SCALE = A_HEAD_DIM ** -0.5
LOG2E = math.log2(math.e)
REL_BUCKETS = 32
REL_MAX_DIST = 128
CONV_W = 4
LRU_C = 8.0
RWKV_HEAD = 64
RWKV_GN_EPS = 64e-5
MOE_GROUPS = 4
EXPERTS_PER_GROUP = 8
N_EXPERTS = MOE_GROUPS * EXPERTS_PER_GROUP
MOE_TOPK = 2
NORM_EPS = 1e-6
SUBNORM_EPS = 1e-5

LANES = 128
VMEM_LIMIT = 56 * 1024 * 1024
ROW_TILE = 512
ATTN_TILE = 512
LRU_TILE = 512
RWKV_CHUNK = 64
RWKV_TILE = 512
RWKV_CHUNKS_PER_ITER = 2
MOE_BLOCK = 512
DISPATCH_TILE = 512
COMBINE_TILE = 256
NEG_BIG = -1e30


def _cparams(*sem):
    return pltpu.CompilerParams(dimension_semantics=sem, vmem_limit_bytes=VMEM_LIMIT)


def _rms(x, g, eps):
    return x * lax.rsqrt(jnp.mean(x * x, axis=-1, keepdims=True) + eps) * g


def _dot(a, b):
    return jnp.dot(a, b, preferred_element_type=F32)


def _dot_nt(a, b):
    return lax.dot_general(a, b, (((1,), (1,)), ((), ())), preferred_element_type=F32)


def _dot_tn(a, b):
    return lax.dot_general(a, b, (((0,), (0,)), ((), ())), preferred_element_type=F32)


def _split2(x):
    hi = x.astype(BF16)
    lo = (x - hi.astype(F32)).astype(BF16)
    return hi, lo


def _softplus(x):
    return jnp.maximum(x, 0.0) + jnp.log1p(jnp.exp(-jnp.abs(x)))


def _neg_expm1(y, exp_y):
    poly = 1.0 + y * (1.0 / 5.0)
    for d in (4.0, 3.0, 2.0):
        poly = 1.0 + (y * (1.0 / d)) * poly
    return jnp.where(y > -1.0 / 16.0, -y * poly, 1.0 - exp_y)


SUBLANES = 8
DMA_PRIORITIES = 2


def _store_token_tiles(ref, x):
    rows = x.shape[0]
    for s in range(x.shape[1] // LANES):
        ref[pl.ds(s, rows, stride=SUBLANES), :] = x[:, s * LANES:(s + 1) * LANES]


def _load_token_tiles(ref, rows):
    return jnp.concatenate([ref[pl.ds(s, rows, stride=SUBLANES), :] for s in range(SUBLANES)], axis=1)


def _gelu_tanh(x):
    c = math.sqrt(2.0 / math.pi)
    return 0.5 * x * (1.0 + jnp.tanh(c * (x + 0.044715 * (x * x * x))))


def _inproj_kernel(x_ref, g_ref, w_ref, q_ref, k_ref, vt_ref, xb_ref, gb_ref):
    hn = _rms(x_ref[...], g_ref[...], NORM_EPS).astype(BF16)
    width = q_ref.shape[1]
    for c, o_ref in enumerate((q_ref, k_ref, vt_ref, xb_ref, gb_ref)):
        r = _dot(hn, w_ref[:, c * width:(c + 1) * width])
        if c == 0:
            r = r * (A_SCALE * LOG2E)
        if c == 2:
            r = r.T
        o_ref[...] = r.astype(o_ref.dtype)


def _inproj(x2, g, w_in):
    t, d = x2.shape
    width = w_in.shape[1] // 5
    tm = ROW_TILE
    row = lambda i: (i, 0)
    fixed = lambda i: (0, 0)
    rows = pl.BlockSpec((tm, width), row)
    return pl.pallas_call(
        _inproj_kernel,
        grid=(t // tm,),
        in_specs=[pl.BlockSpec((tm, d), row), pl.BlockSpec((1, d), fixed),
                  pl.BlockSpec(w_in.shape, fixed)],
        out_specs=[rows, rows, pl.BlockSpec((width, tm), lambda i: (0, i)), rows, rows],
        out_shape=[jax.ShapeDtypeStruct((t, width), BF16)] * 2 + [jax.ShapeDtypeStruct((width, t), BF16)]
        + [jax.ShapeDtypeStruct((t, width), F32)] * 2,
        compiler_params=_cparams("arbitrary"),
        name="inproj",
    )(x2, g.reshape(1, d), w_in)


def _t5_bucket(rel):
    n = jnp.maximum(rel, 0)
    max_exact = REL_BUCKETS // 2
    large = max_exact + (jnp.log(jnp.maximum(n, 1).astype(F32) / max_exact)
                         / math.log(REL_MAX_DIST / max_exact)
                         * (REL_BUCKETS - max_exact)).astype(I32)
    large = jnp.minimum(large, REL_BUCKETS - 1)
    return jnp.where(n < max_exact, n, large)


def _bias_tables(rel_bias, tile):
    span = 2 * tile

    def toeplitz(first_rel):
        rel = jnp.arange(span - 1) + first_rel
        vals = jnp.where((rel >= 0)[:, None], rel_bias[_t5_bucket(rel)].astype(F32), NEG_BIG)
        w = jnp.concatenate([vals, jnp.zeros((1, vals.shape[1]), F32)], axis=0).T
        skew = jnp.tile(w, (1, tile))[:, :tile * (span - 1)].reshape(-1, tile, span - 1)
        return skew[:, :, tile - 1:]

    return jnp.stack([toeplitz(1 - tile), toeplitz(1)], axis=1)


def _attn_kernel(cfar_ref, q_ref, k_ref, vt_ref, bias_ref, lam_ref, subg_ref, o_ref,
                 m_sc, l_sc, acc_sc, sa_sc, sb_sc, *, lam_init):
    h = pl.program_id(1)
    i = pl.program_id(2)
    tile = q_ref.shape[0]
    q = q_ref[...]
    lane = lax.broadcasted_iota(I32, q.shape, 1)
    zero = jnp.zeros_like(q)
    qm = (jnp.where(lane < A_HEAD_DIM, q, zero), jnp.where(lane >= A_HEAD_DIM, q, zero))
    m_sc[...] = jnp.full(m_sc.shape, NEG_BIG, F32)
    l_sc[...] = jnp.zeros(l_sc.shape, F32)
    acc_sc[...] = jnp.zeros(acc_sc.shape, F32)

    maps = range(2)

    def scores(j):
        kb = k_ref[pl.ds(pl.multiple_of(j * tile, tile), tile), :]
        return [_dot_nt(kb, qm[mi]) for mi in maps]

    def scores_to(dst, j):
        s = scores(j)
        for mi in maps:
            dst[mi] = s[mi]

    def consume(s, j, table, const):
        vtb = vt_ref[:, pl.ds(pl.multiple_of(j * tile, tile), tile)]
        if table is not None:
            s = [s[mi] + table for mi in maps]
        half = tile // 2
        for mi in maps:
            for qs in (slice(0, half), slice(half, tile)):
                sq = s[mi][:, qs]
                m_prev = m_sc[mi, :, qs]
                m_new = jnp.maximum(m_prev, jnp.max(sq, axis=0, keepdims=True) + const)
                p = jnp.exp2(sq - (m_new - const))
                alpha = jnp.exp2(m_prev - m_new)
                l_sc[mi, :, qs] = alpha * l_sc[mi, :, qs] + jnp.sum(p, axis=0, keepdims=True)
                acc_sc[mi, :, qs] = alpha * acc_sc[mi, :, qs] + _dot(vtb, p.astype(BF16))
                m_sc[mi, :, qs] = m_new

    held = lambda buf: [buf[mi] for mi in maps]
    cfar = cfar_ref[h]
    nfar = jnp.maximum(i - 1, 0)

    scores_to(sa_sc, 0)

    def far_pair(j):
        scores_to(sb_sc, j + 1)
        consume(held(sa_sc), j, None, cfar)
        scores_to(sa_sc, j + 2)
        consume(held(sb_sc), j + 1, None, cfar)

    def far_quad(t, c):
        far_pair(4 * t)
        far_pair(4 * t + 2)
        return c

    nquad = lax.shift_right_logical(nfar, 2)
    lax.fori_loop(0, nquad, far_quad, 0)

    @pl.when((nfar & 2) != 0)
    def _():
        far_pair(4 * nquad)

    @pl.when(jnp.logical_and(i >= 1, nfar % 2 == 0))
    def _():
        scores_to(sb_sc, i)
        consume(held(sa_sc), i - 1, bias_ref[0, 1], 0.0)
        consume(held(sb_sc), i, bias_ref[0, 0], 0.0)

    @pl.when(nfar % 2 == 1)
    def _():
        scores_to(sb_sc, i - 1)
        consume(held(sa_sc), i - 2, None, cfar)
        scores_to(sa_sc, i)
        consume(held(sb_sc), i - 1, bias_ref[0, 1], 0.0)
        consume(held(sa_sc), i, bias_ref[0, 0], 0.0)

    @pl.when(i == 0)
    def _():
        consume(held(sa_sc), 0, bias_ref[0, 0], 0.0)

    lam_rows = lam_ref[...]
    lam = (jnp.exp(jnp.sum(lam_rows[0:1] * lam_rows[1:2], axis=-1, keepdims=True))
           - jnp.exp(jnp.sum(lam_rows[2:3] * lam_rows[3:4], axis=-1, keepdims=True)) + lam_init)
    o = acc_sc[0] / l_sc[0] - lam * (acc_sc[1] / l_sc[1])
    o = o * lax.rsqrt(jnp.mean(o * o, axis=0, keepdims=True) + SUBNORM_EPS)
    o_ref[...] = (o.T * subg_ref[...] * (1.0 - lam_init)).astype(o_ref.dtype)


def _diff_attention(q, k, vt, rel_bias, lam_rows, sub_g, lam_init, batch, seq):
    t, width = q.shape
    hw = 2 * A_HEAD_DIM
    tile = ATTN_TILE
    nq = seq // tile
    tables = _bias_tables(rel_bias, tile) * LOG2E
    cfar = rel_bias[REL_BUCKETS - 1].astype(F32) * LOG2E
    return pl.pallas_call(
        functools.partial(_attn_kernel, lam_init=lam_init),
        grid=(batch, A_HEADS, nq),
        in_specs=[
            pl.BlockSpec(memory_space=pltpu.SMEM),
            pl.BlockSpec((tile, hw), lambda b, h, i: (b * nq + i, h)),
            pl.BlockSpec((seq, hw), lambda b, h, i: (b, h)),
            pl.BlockSpec((hw, seq), lambda b, h, i: (h, b)),
            pl.BlockSpec((1, 2, tile, tile), lambda b, h, i: (h, 0, 0, 0)),
            pl.BlockSpec(lam_rows.shape, lambda b, h, i: (0, 0)),
            pl.BlockSpec((1, hw), lambda b, h, i: (0, 0)),
        ],
        out_specs=pl.BlockSpec((tile, hw), lambda b, h, i: (b * nq + i, h)),
        out_shape=jax.ShapeDtypeStruct((t, width), BF16),
        scratch_shapes=[pltpu.VMEM((2, 1, tile), F32), pltpu.VMEM((2, 1, tile), F32),
                        pltpu.VMEM((2, hw, tile), F32),
                        pltpu.VMEM((2, tile, tile), F32), pltpu.VMEM((2, tile, tile), F32)],
        compiler_params=_cparams("arbitrary", "arbitrary", "arbitrary"),
        name="diff_attn",
    )(cfar, q, k, vt, tables, lam_rows, sub_g.reshape(1, hw))


def _lru_kernel(xb_ref, gb_ref, cw_ref, vec_ref, wa_ref, wx_ref, y_ref, prev_sc, h_sc, a_sc, b_sc, hs_sc):
    ts = xb_ref.shape[0]

    @pl.when(pl.program_id(1) == 0)
    def _():
        prev_sc[...] = jnp.zeros(prev_sc.shape, F32)
        h_sc[...] = jnp.zeros(h_sc.shape, F32)

    xb = xb_ref[...]
    hist = prev_sc.shape[0]
    ext = jnp.concatenate([prev_sc[...], xb], axis=0)
    u = vec_ref[0:1, :]
    for j in range(CONV_W):
        off = hist - (CONV_W - 1) + j
        u = u + cw_ref[j:j + 1, :] * ext[off:off + ts, :]
    prev_sc[...] = xb[ts - hist:, :]
    ub = u.astype(BF16)
    r = jax.nn.sigmoid(_dot(ub, wa_ref[...]) + vec_ref[1:2, :])
    ig = jax.nn.sigmoid(_dot(ub, wx_ref[...]) + vec_ref[2:3, :])
    log_a = (-LRU_C * r) * _softplus(-vec_ref[3:4, :])
    a = jnp.exp(log_a)
    a_sc[...] = a
    b_sc[...] = jnp.sqrt(_neg_expm1(2.0 * log_a, a * a)) * (ig * u)

    def step(tt, h):
        h = a_sc[pl.ds(tt, 1), :] * h + b_sc[pl.ds(tt, 1), :]
        hs_sc[pl.ds(tt, 1), :] = h
        return h

    h_sc[...] = lax.fori_loop(0, ts, step, h_sc[...], unroll=16)
    y_ref[...] = (hs_sc[...] * _gelu_tanh(gb_ref[...])).astype(y_ref.dtype)


def _block_diag(w):
    n, c, d = w.shape
    eye = jnp.eye(n, dtype=w.dtype)
    return (w[:, :, None, :] * eye[:, None, :, None]).reshape(n * c, n * d)


def _rglru(xb, gb, conv_w, conv_b, wa, ba, wx, bx, lru_lambda, batch, seq):
    t, w = xb.shape
    ts = LRU_TILE
    nt = seq // ts
    vec = jnp.zeros((8, w), F32).at[0].set(conv_b).at[1].set(ba).at[2].set(bx).at[3].set(lru_lambda)
    row = lambda b, s: (b * nt + s, 0)
    fixed = lambda b, s: (0, 0)
    return pl.pallas_call(
        _lru_kernel,
        grid=(batch, nt),
        in_specs=[pl.BlockSpec((ts, w), row), pl.BlockSpec((ts, w), row),
                  pl.BlockSpec((CONV_W, w), fixed), pl.BlockSpec((8, w), fixed),
                  pl.BlockSpec((w, w), fixed), pl.BlockSpec((w, w), fixed)],
        out_specs=pl.BlockSpec((ts, w), row),
        out_shape=jax.ShapeDtypeStruct((t, w), BF16),
        scratch_shapes=[pltpu.VMEM((8, w), F32), pltpu.VMEM((1, w), F32),
                        pltpu.VMEM((ts, w), F32), pltpu.VMEM((ts, w), F32), pltpu.VMEM((ts, w), F32)],
        compiler_params=_cparams("arbitrary", "arbitrary"),
        name="rglru",
    )(xb, gb, conv_w, vec, _block_diag(wa).astype(BF16), _block_diag(wx).astype(BF16))


def _route(x, gf_ref, wrh_ref, wrl_ref, br_ref, tri_ref, tot_sc, size_ref, first_step):
    @pl.when(first_step)
    def _():
        tot_sc[...] = jnp.zeros(tot_sc.shape, F32)

    hn = _rms(x, gf_ref[...], NORM_EPS)
    hh, hl = _split2(hn)
    wh = wrh_ref[...]
    lg = _dot(hh, wh) + _dot(hh, wrl_ref[...]) + _dot(hl, wh) + br_ref[...]
    lane = lax.broadcasted_iota(I32, lg.shape, 1)
    big = jnp.int32(1 << 20)
    is_g = lane < MOE_GROUPS
    gl = jnp.where(is_g, lg, NEG_BIG)
    gmax = jnp.max(gl, axis=-1, keepdims=True)
    gidx = jnp.min(jnp.where(gl == gmax, lane, big), axis=-1, keepdims=True)
    gsum = jnp.sum(jnp.where(is_g, jnp.exp(gl - gmax), 0.0), axis=-1, keepdims=True)
    gprob = 1.0 / gsum
    lo = MOE_GROUPS + gidx * EXPERTS_PER_GROUP
    fmask = (lane >= lo) & (lane < lo + EXPERTS_PER_GROUP)
    fl = jnp.where(fmask, lg, NEG_BIG)
    f1 = jnp.max(fl, axis=-1, keepdims=True)
    i1 = jnp.min(jnp.where(fl == f1, lane, big), axis=-1, keepdims=True)
    fl2 = jnp.where(lane == i1, NEG_BIG, fl)
    f2 = jnp.max(fl2, axis=-1, keepdims=True)
    i2 = jnp.min(jnp.where(fl2 == f2, lane, big), axis=-1, keepdims=True)
    e2 = jnp.exp(f2 - f1)
    den = 1.0 + e2
    g1 = gprob / den
    g2 = gprob * e2 / den
    id1 = (i1 - MOE_GROUPS).astype(F32)
    id2 = (i2 - MOE_GROUPS).astype(F32)
    onehot = jnp.where((lane == i1) | (lane == i2), 1.0, 0.0)
    before = _dot(tri_ref[...], onehot.astype(BF16)) + tot_sc[...]
    r1 = jnp.sum(jnp.where(lane == i1, before, 0.0), axis=-1, keepdims=True)
    r2 = jnp.sum(jnp.where(lane == i2, before, 0.0), axis=-1, keepdims=True)
    tot = tot_sc[...] + jnp.sum(onehot, axis=0, keepdims=True)
    tot_sc[...] = tot
    size_ref[...] = jnp.broadcast_to(tot, size_ref.shape)
    vals = (g1, g2, id1, id2, r1, r2)
    out = jnp.zeros(lg.shape, F32)
    for n, v in enumerate(vals):
        out = jnp.where(lane == n, v, out)
    return out


def _router_params(wc, bc, wf, bf):
    d = wc.shape[0]
    n = MOE_GROUPS + N_EXPERTS
    w = jnp.zeros((d, LANES), F32).at[:, :MOE_GROUPS].set(wc).at[:, MOE_GROUPS:n].set(wf)
    b = jnp.zeros((1, LANES), F32).at[0, :MOE_GROUPS].set(bc).at[0, MOE_GROUPS:n].set(bf)
    hi, lo = _split2(w)
    return hi, lo, b


def _outproj_kernel(x_ref, ya_ref, yb_ref, wo_ref, gf_ref, wrh_ref, wrl_ref, br_ref, tri_ref,
                    x1_ref, x1t_ref, rt_ref, rtt_ref, size_ref, tot_sc):
    half = ya_ref.shape[1]
    x1 = x_ref[...] + _dot(ya_ref[...], wo_ref[:half, :]) + _dot(yb_ref[...], wo_ref[half:, :])
    x1_ref[...] = x1
    _store_token_tiles(x1t_ref, x1)
    rt = _route(x1, gf_ref, wrh_ref, wrl_ref, br_ref, tri_ref, tot_sc, size_ref, pl.program_id(0) == 0)
    rt_ref[...] = rt
    rtt_ref[...] = rt.T[:SUBLANES, :]


def _outproj(x2, ya, yb, w_out, gf, router):
    t, d = x2.shape
    half = ya.shape[1]
    tm = ROW_TILE
    row = lambda i: (i, 0)
    fixed = lambda i: (0, 0)
    wrh, wrl, br = router
    tri = _strict_lower(tm)
    return pl.pallas_call(
        _outproj_kernel,
        grid=(t // tm,),
        in_specs=[pl.BlockSpec((tm, d), row), pl.BlockSpec((tm, half), row), pl.BlockSpec((tm, half), row),
                  pl.BlockSpec((2 * half, d), fixed), pl.BlockSpec((1, d), fixed),
                  pl.BlockSpec((d, LANES), fixed), pl.BlockSpec((d, LANES), fixed),
                  pl.BlockSpec((1, LANES), fixed), pl.BlockSpec(tri.shape, fixed)],
        out_specs=[pl.BlockSpec((tm, d), row), pl.BlockSpec((tm * SUBLANES, LANES), row),
                   pl.BlockSpec((tm, LANES), row), pl.BlockSpec((SUBLANES, tm), lambda i: (0, i)),
                   pl.BlockSpec((SUBLANES, LANES), fixed)],
        out_shape=[jax.ShapeDtypeStruct((t, d), F32), jax.ShapeDtypeStruct((t * SUBLANES, LANES), F32),
                   jax.ShapeDtypeStruct((t, LANES), F32), jax.ShapeDtypeStruct((SUBLANES, t), F32),
                   jax.ShapeDtypeStruct((SUBLANES, LANES), F32)],
        scratch_shapes=[pltpu.VMEM((1, LANES), F32)],
        compiler_params=_cparams("arbitrary"),
        name="outproj_route",
    )(x2, ya, yb, w_out, gf.reshape(1, d), wrh, wrl, br, tri)


def _strict_lower(n):
    return (jnp.arange(n)[None, :] < jnp.arange(n)[:, None]).astype(BF16)


def _moe_plan(route_t, sizes_tile, n_tok):
    m = n_tok * MOE_TOPK
    n_blk = -(-(m + N_EXPERTS * (MOE_BLOCK - 1)) // MOE_BLOCK)
    eid = route_t[2:4].astype(I32).T
    rank = route_t[4:6].astype(I32).T
    sizes = sizes_tile[0, MOE_GROUPS:MOE_GROUPS + N_EXPERTS].astype(I32)
    padded = (sizes + MOE_BLOCK - 1) // MOE_BLOCK * MOE_BLOCK
    pend = jnp.cumsum(padded)
    dest = (pend - padded)[eid] + rank
    blk_start = jnp.arange(n_blk, dtype=I32) * MOE_BLOCK
    blk_e = jnp.minimum(jnp.sum((pend[None, :] <= blk_start[:, None]).astype(I32), axis=1),
                        N_EXPERTS - 1).astype(I32)
    n_used = (pend[-1] // MOE_BLOCK).astype(I32).reshape(1)
    tail = jnp.where(padded > 0, pend - MOE_BLOCK, -1).astype(I32)
    return blk_e, n_used, dest.astype(I32), tail, n_blk


def _dispatch_kernel(tail_ref, nused_ref, dest_hbm, xt_ref, xs_hbm,
                     idx_sm0, idx_sm1, idx_sem, zero_sc, zero_sem, stage_sc, row_sem):
    i = pl.program_id(0)
    n = pl.num_programs(0)
    tm = idx_sm0.shape[0] // MOE_TOPK
    idx_sms = (idx_sm0, idx_sm1)
    tile_rows = MOE_BLOCK * SUBLANES
    n_blk = xs_hbm.shape[0] // tile_rows

    @pl.when(i == 0)
    def _():
        zero_sc[...] = jnp.zeros(zero_sc.shape, F32)

        def clear(row, wait):
            cp = pltpu.make_async_copy(
                zero_sc, xs_hbm.at[pl.ds(pl.multiple_of(row * SUBLANES, SUBLANES), tile_rows), :], zero_sem)
            if wait:
                cp.wait()
            else:
                cp.start()

        def fill(e, wait):
            @pl.when(tail_ref[e] >= 0)
            def _():
                clear(tail_ref[e], wait)

        for wait in (False, True):
            lax.fori_loop(0, N_EXPERTS, lambda e, c, w=wait: (fill(e, w), c)[1], 0)
            lax.fori_loop(nused_ref[0], n_blk, lambda b, c, w=wait: (clear(b * MOE_BLOCK, w), c)[1], 0)

    def idx_copy(step, sl):
        return pltpu.make_async_copy(dest_hbm.at[step], idx_sms[sl], idx_sem.at[sl])

    def all_copies(sl):
        for s in range(MOE_TOPK):
            pltpu.make_async_copy(stage_sc.at[sl], xs_hbm.at[pl.ds(0, tm * SUBLANES), :], row_sem.at[sl]).wait()

    @pl.when(i == 0)
    def _():
        idx_copy(0, 0).start()

    for sl in range(2):
        @pl.when(i % 2 == sl)
        def _():
            @pl.when(i >= 2)
            def _():
                all_copies(sl)

            stage_sc[sl] = xt_ref[...]
            idx_copy(i, sl).wait()

            @pl.when(i + 1 < n)
            def _():
                idx_copy(i + 1, 1 - sl).start()

            def issue(r, c):
                src = pl.multiple_of(r * SUBLANES, SUBLANES)
                for s in range(MOE_TOPK):
                    dst = pl.multiple_of(idx_sms[sl][MOE_TOPK * r + s], SUBLANES)
                    pltpu.make_async_copy(stage_sc.at[sl, pl.ds(src, SUBLANES), :],
                                          xs_hbm.at[pl.ds(dst, SUBLANES), :],
                                          row_sem.at[sl]).start(priority=s % DMA_PRIORITIES)
                return c

            lax.fori_loop(0, tm, issue, 0, unroll=8)

            @pl.when(i == n - 1)
            def _():
                all_copies(sl)

                @pl.when(n >= 2)
                def _():
                    all_copies(1 - sl)


def _dispatch(x1t, dest, tail, n_used, n_blk):
    t = dest.shape[0]
    tm = DISPATCH_TILE
    pos = (dest * SUBLANES).reshape(t // tm, MOE_TOPK * tm)
    grid_spec = pltpu.PrefetchScalarGridSpec(
        num_scalar_prefetch=2,
        grid=(t // tm,),
        in_specs=[pl.BlockSpec(memory_space=pl.ANY),
                  pl.BlockSpec((tm * SUBLANES, LANES), lambda i, tl, nu: (i, 0))],
        out_specs=pl.BlockSpec(memory_space=pl.ANY),
        scratch_shapes=[pltpu.SMEM((MOE_TOPK * tm,), I32), pltpu.SMEM((MOE_TOPK * tm,), I32),
                        pltpu.SemaphoreType.DMA((2,)), pltpu.VMEM((MOE_BLOCK * SUBLANES, LANES), F32),
                        pltpu.SemaphoreType.DMA, pltpu.VMEM((2, tm * SUBLANES, LANES), F32),
                        pltpu.SemaphoreType.DMA((2,))],
    )
    return pl.pallas_call(
        _dispatch_kernel,
        grid_spec=grid_spec,
        out_shape=jax.ShapeDtypeStruct((n_blk * MOE_BLOCK * SUBLANES, LANES), F32),
        compiler_params=_cparams("arbitrary"),
        name="moe_dispatch",
    )(tail, n_used, pos, x1t)


def _gather_ahead(idx_hbm, idx_sms, idx_sem, issue_rows, n_active):
    i = pl.program_id(0)
    n = pl.num_programs(0)

    def idx_copy(step, slot):
        return pltpu.make_async_copy(idx_hbm.at[step], idx_sms[slot], idx_sem.at[slot])

    @pl.when(i == 0)
    def _():
        first = idx_copy(0, 0)
        first.start()
        first.wait()

        @pl.when(0 < n_active)
        def _():
            issue_rows(0)

        @pl.when(1 < n)
        def _():
            idx_copy(1, 1).start()

    for slot in range(2):
        @pl.when(jnp.logical_and(i + 1 < n, (i + 1) % 2 == slot))
        def _():
            idx_copy(i + 1, slot).wait()

            @pl.when(i + 1 < n_active)
            def _():
                issue_rows(slot)

        @pl.when(jnp.logical_and(i + 2 < n, i % 2 == slot))
        def _():
            idx_copy(i + 2, slot).start()


def _expert_kernel(blk_e_ref, nused_ref, xs_ref, gf_ref, wgu_ref, wd_ref, yst_ref, wgu_sc, wd_sc):
    i = pl.program_id(0)

    @pl.when(i >= nused_ref[0])
    def _():
        yst_ref[...] = jnp.zeros(yst_ref.shape, F32)

    @pl.when(jnp.logical_or(i == 0, blk_e_ref[i] != blk_e_ref[jnp.maximum(i - 1, 0)]))
    def _():
        wgu_sc[...] = wgu_ref[0, 0].astype(BF16)
        wd_sc[...] = wd_ref[0, 0].astype(BF16)

    @pl.when(i < nused_ref[0])
    def _():
        blk = xs_ref.shape[0] // SUBLANES
        x = _load_token_tiles(xs_ref, blk)
        hn = _rms(x, gf_ref[...], NORM_EPS).astype(BF16)
        gu = _dot(hn, wgu_sc[...])
        eh = gu.shape[1] // 2
        g = gu[:, :eh]
        act = (g * jax.nn.sigmoid(g) * gu[:, eh:]).astype(BF16)
        _store_token_tiles(yst_ref, _dot(act, wd_sc[...]))


def _experts(xs, blk_e, n_used, gf, w_gu, w_down, layer):
    d = gf.shape[0]
    blk = MOE_BLOCK
    n_blk = xs.shape[0] // (blk * SUBLANES)
    eh2 = w_gu.shape[3]
    rows = lambda i, e, n: (jnp.minimum(i, n[0] - 1), 0)
    grid_spec = pltpu.PrefetchScalarGridSpec(
        num_scalar_prefetch=2,
        grid=(n_blk,),
        in_specs=[
            pl.BlockSpec((blk * SUBLANES, LANES), rows),
            pl.BlockSpec((1, d), lambda i, e, n: (0, 0)),
            pl.BlockSpec((1, 1, d, eh2), lambda i, e, n: (layer, e[i], 0, 0)),
            pl.BlockSpec((1, 1, eh2 // 2, d), lambda i, e, n: (layer, e[i], 0, 0)),
        ],
        out_specs=pl.BlockSpec((blk * SUBLANES, LANES), lambda i, e, n: (i, 0)),
        scratch_shapes=[pltpu.VMEM((d, eh2), BF16), pltpu.VMEM((eh2 // 2, d), BF16)],
    )
    return pl.pallas_call(
        _expert_kernel,
        grid_spec=grid_spec,
        out_shape=jax.ShapeDtypeStruct(xs.shape, F32),
        compiler_params=_cparams("arbitrary"),
        name="moe_experts",
    )(blk_e, n_used, xs, gf.reshape(1, d), w_gu, w_down)


def _combine_kernel(pos_hbm, yst_hbm, x1_ref, rt_ref, p_ref, gn_ref, wg_ref, wp_ref, fn_ref, o_ref,
                    idx_sm0, idx_sm1, idx_sem, ybuf, row_sem, *, final):
    slot = pl.program_id(0) % 2
    tm = x1_ref.shape[0]
    idx_sms = (idx_sm0, idx_sm1)

    def issue_rows(sl):
        def issue(r, c):
            dst = pl.multiple_of(r * SUBLANES, SUBLANES)
            for s in range(MOE_TOPK):
                src = pl.multiple_of(idx_sms[sl][MOE_TOPK * r + s], SUBLANES)
                pltpu.make_async_copy(yst_hbm.at[pl.ds(src, SUBLANES), :],
                                      ybuf.at[sl, s, pl.ds(dst, SUBLANES), :],
                                      row_sem.at[sl]).start(priority=s % DMA_PRIORITIES)
            return c

        lax.fori_loop(0, tm, issue, 0, unroll=8)

    _gather_ahead(pos_hbm, idx_sms, idx_sem, issue_rows, pl.num_programs(0))
    for s in range(MOE_TOPK):
        pltpu.make_async_copy(yst_hbm.at[pl.ds(0, tm * SUBLANES), :], ybuf.at[slot, s], row_sem.at[slot]).wait()

    rt = rt_ref[...]
    y0 = _load_token_tiles(ybuf.at[slot, 0], tm)
    y1 = _load_token_tiles(ybuf.at[slot, 1], tm)
    x2 = x1_ref[...] + (rt[:, 0:1] * y0 + rt[:, 1:2] * y1)
    gate = jax.nn.sigmoid(_dot(_rms(x2, gn_ref[...], NORM_EPS).astype(BF16), wg_ref[...]))
    x3 = x2 + _dot(p_ref[...].astype(BF16), wp_ref[...]) * gate
    if final:
        x3 = _rms(x3, fn_ref[...], NORM_EPS)
    o_ref[...] = x3


def _combine(x1, route, yst, dest, p_all, layer, ple_norm, ple_gate, ple_proj, final_norm, final):
    t, d = x1.shape
    tm = COMBINE_TILE
    pd = p_all.shape[1]
    pos = (dest * SUBLANES).reshape(t // tm, MOE_TOPK * tm)
    row = lambda i: (i, 0)
    fixed = lambda i: (0, 0)
    return pl.pallas_call(
        functools.partial(_combine_kernel, final=final),
        grid=(t // tm,),
        in_specs=[pl.BlockSpec(memory_space=pl.ANY), pl.BlockSpec(memory_space=pl.ANY),
                  pl.BlockSpec((tm, d), row), pl.BlockSpec((tm, LANES), row),
                  pl.BlockSpec((tm, pd), lambda i: (layer * (t // tm) + i, 0)),
                  pl.BlockSpec((1, d), fixed), pl.BlockSpec((d, d), fixed), pl.BlockSpec((pd, d), fixed),
                  pl.BlockSpec((1, d), fixed)],
        out_specs=pl.BlockSpec((tm, d), row),
        out_shape=jax.ShapeDtypeStruct((t, d), F32),
        scratch_shapes=[pltpu.SMEM((MOE_TOPK * tm,), I32), pltpu.SMEM((MOE_TOPK * tm,), I32),
                        pltpu.SemaphoreType.DMA((2,)),
                        pltpu.VMEM((2, MOE_TOPK, tm * SUBLANES, LANES), F32), pltpu.SemaphoreType.DMA((2,))],
        compiler_params=_cparams("arbitrary"),
        name="moe_combine_ple",
    )(pos, yst, x1, route, p_all, ple_norm.reshape(1, d), ple_gate, ple_proj, final_norm.reshape(1, d))


def _seg_sum(x, eseg_ref, eexp_ref):
    s = _dot(x.astype(BF16), eseg_ref[...])
    return _dot(s.astype(BF16), eexp_ref[...])


def _rwkv_pre_kernel(x_ref, xp_ref, gm_ref, mu_ref, vec_ref, wr_ref, wk_ref, wv_ref, w1_ref, w2_ref,
                     a1_ref, a2_ref, g1_ref, g2_ref, tri_ref, eseg_ref, eexp_ref,
                     la_ref, lr_ref, rb_ref, rk_ref, sb_ref, sk_ref, v_ref, bon_ref, g_ref, gc_ref,
                     c_sc):
    tl = x_ref.shape[0]
    gm = gm_ref[...]
    h = _rms(x_ref[...], gm, NORM_EPS)
    hp = _rms(xp_ref[...], gm, NORM_EPS)[7:8, :]
    hp = jnp.where(pl.program_id(1) == 0, 0.0, hp)
    row = lax.broadcasted_iota(I32, h.shape, 0)
    hs = jnp.where(row == 0, hp, pltpu.roll(h, 1, axis=0))
    xx = hs - h
    mix = lambda n: (h + xx * mu_ref[n:n + 1, :]).astype(BF16)
    vec = lambda n: vec_ref[n:n + 1, :]
    r = _dot(mix(0), wr_ref[...])
    wl = vec(0) + _dot(jnp.tanh(_dot(mix(1), w1_ref[...])).astype(BF16), w2_ref[...])
    lw = -math.exp(-0.5) * jax.nn.sigmoid(wl)
    k = _dot(mix(2), wk_ref[...])
    v = _dot(mix(3), wv_ref[...])
    a = jax.nn.sigmoid(vec(1) + _dot(_dot(mix(4), a1_ref[...]).astype(BF16), a2_ref[...]))
    g = _dot(jax.nn.sigmoid(_dot(mix(5), g1_ref[...])).astype(BF16), g2_ref[...])
    kk = k * vec(2)
    kk = kk * lax.rsqrt(jnp.maximum(_seg_sum(kk * kk, eseg_ref, eexp_ref), 1e-24))
    k2 = k * (1.0 + (a - 1.0) * vec(3))
    bon_ref[...] = (_seg_sum(r * k2 * vec(4), eseg_ref, eexp_ref) * v).astype(bon_ref.dtype)
    v_ref[...] = v.astype(v_ref.dtype)
    g_ref[...] = g.astype(g_ref.dtype)

    half = tri_ref.shape[1]
    tri = tri_ref[...]
    for s in range(tl // half):
        p1, p2 = _split2(lw[s * half:(s + 1) * half, :])
        c_sc[s * half:(s + 1) * half, :] = _dot(tri, p1) + _dot(tri, p2)
    c_in = c_sc[...]
    nc = tl // RWKV_CHUNK
    ends = [c_sc[(n + 1) * RWKV_CHUNK - 1:(n + 1) * RWKV_CHUNK, :] for n in range(nc)]
    c_end = jnp.concatenate([ends[n] - c_sc[n * RWKV_CHUNK:(n + 1) * RWKV_CHUNK, :] for n in range(nc)],
                            axis=0)
    gc_ref[...] = jnp.exp(jnp.concatenate(ends, axis=0))
    e_in = jnp.exp(c_in)
    e_neg = jnp.exp(-c_in)
    e_end = jnp.exp(c_end)
    b = kk * a
    la_ref[...] = (-kk * jnp.exp(c_in - lw)).astype(la_ref.dtype)
    lr_ref[...] = (r * e_in).astype(lr_ref.dtype)
    rb_ref[...] = (b * e_neg).astype(rb_ref.dtype)
    rk_ref[...] = (k2 * e_neg).astype(rk_ref.dtype)
    sb_ref[...] = (b * e_end).astype(sb_ref.dtype)
    sk_ref[...] = (k2 * e_end).astype(sk_ref.dtype)


def _head_indicator(d):
    e = (jnp.arange(d)[:, None] // RWKV_HEAD == jnp.arange(LANES)[None, :]).astype(BF16)
    return e, e.T


def _cumsum_matrix(half):
    i = jnp.arange(half)[:, None]
    j = jnp.arange(half)[None, :]
    return (((i // RWKV_CHUNK) == (j // RWKV_CHUNK)) & (j <= i)).astype(BF16)


def _pad_cols(w, n):
    return jnp.pad(w, ((0, 0), (0, n - w.shape[1])))


def _pad_rows(w, n):
    return jnp.pad(w, ((0, n - w.shape[0]), (0, 0)))


def _rwkv_pre(x3, gm, mu, vecs, wr, wk, wv, w1, w2, a1, a2, g1, g2, batch, seq):
    t, d = x3.shape
    tl = RWKV_TILE
    nt = seq // tl
    half = 256
    lora = lambda w_in, w_out: (_pad_cols(w_in, -(-w_in.shape[1] // LANES) * LANES).astype(BF16),
                                _pad_rows(w_out, -(-w_out.shape[0] // LANES) * LANES).astype(BF16))
    w1p, w2p = lora(w1, w2)
    a1p, a2p = lora(a1, a2)
    g1p, g2p = lora(g1, g2)
    eseg, eexp = _head_indicator(d)
    tri = _cumsum_matrix(half)
    row = lambda b, s: (b * nt + s, 0)
    fixed = lambda b, s: (0, 0)
    full = lambda a: pl.BlockSpec(a.shape, fixed)
    prev = lambda b, s: (jnp.maximum((b * nt + s) * (tl // 8) - 1, 0), 0)
    weights = (wr.astype(BF16), wk.astype(BF16), wv.astype(BF16), w1p, w2p, a1p, a2p, g1p, g2p, tri, eseg, eexp)
    outs = pl.pallas_call(
        _rwkv_pre_kernel,
        grid=(batch, nt),
        in_specs=[pl.BlockSpec((tl, d), row), pl.BlockSpec((8, d), prev), pl.BlockSpec((1, d), fixed),
                  full(mu), full(vecs)] + [full(w) for w in weights],
        out_specs=[pl.BlockSpec((tl, d), row)] * 9 + [pl.BlockSpec((tl // RWKV_CHUNK, d), row)],
        out_shape=[jax.ShapeDtypeStruct((t, d), BF16)] * 9 + [jax.ShapeDtypeStruct((t // RWKV_CHUNK, d), F32)],
        scratch_shapes=[pltpu.VMEM((tl, d), F32)],
        compiler_params=_cparams("arbitrary", "arbitrary"),
        name="rwkv_pre",
    )(x3, x3, gm.reshape(1, d), mu, vecs, *weights)
    return outs


def _rwkv_scan_kernel(la_ref, lr_ref, rb_ref, rk_ref, sb_ref, sk_ref, v_ref, gc_ref, bon_ref, g_ref, x_ref,
                      vec_ref, wo_ref, eseg_ref, eexp_ref, gf_ref, wrh_ref, wrl_ref, br_ref, tri_ref,
                      x4_ref, x4t_ref, rt_ref, rtt_ref, size_ref, s_sc, y_sc, tot_sc):
    tl, d = x_ref.shape
    ch = RWKV_CHUNK
    pw = 2 * RWKV_HEAD
    npair = d // pw

    @pl.when(pl.program_id(1) == 0)
    def _():
        s_sc[...] = jnp.zeros(s_sc.shape, F32)

    ri = lax.broadcasted_iota(I32, (pw, pw), 0)
    ci = lax.broadcasted_iota(I32, (pw, pw), 1)
    same_half = (ri < RWKV_HEAD) == (ci < RWKV_HEAD)
    eye = (ri == ci).astype(F32)
    rt_ = lax.broadcasted_iota(I32, (ch, pw), 0)
    ct_ = lax.broadcasted_iota(I32, (ch, pw), 1)
    first = ct_ < RWKV_HEAD
    strict = rt_ > (ct_ & (RWKV_HEAD - 1))
    incl = rt_ >= (ct_ & (RWKV_HEAD - 1))
    m_s0, m_s1 = strict & first, strict & ~first
    m_i0, m_i1 = incl & first, incl & ~first
    first_full = ci < RWKV_HEAD
    zf = jnp.zeros((ch, pw), F32)
    gc_row = lax.broadcasted_iota(I32, (tl // ch, pw), 0)

    group = RWKV_CHUNKS_PER_ITER

    def chunks(cg, carry):
        pairs = range(npair)
        cols = [slice(p * pw, (p + 1) * pw) for p in pairs]
        cidx = [cg * group + k for k in range(group)]
        rows = [pl.ds(pl.multiple_of(c * ch, ch), ch) for c in cidx]
        units = [(k, p) for k in range(group) for p in pairs]
        un = range(len(units))
        cat0 = lambda a, b: jnp.concatenate([a, b], axis=0)
        ld = lambda ref, n: ref[rows[units[n][0]], cols[units[n][1]]]
        lcat = [cat0(ld(la_ref, n), ld(lr_ref, n)) for n in un]
        rb = [ld(rb_ref, n) for n in un]
        rk = [ld(rk_ref, n) for n in un]
        vv = [ld(v_ref, n) for n in un]
        zb = jnp.zeros_like(lcat[0])
        a0 = [_dot_nt(jnp.where(first_full, lcat[n], zb), cat0(rb[n], rk[n])) for n in un]
        a1 = [_dot_nt(jnp.where(first_full, zb, lcat[n]), cat0(rk[n], rb[n])) for n in un]
        pm = [cat0(jnp.where(m_s0, a0[n][:ch], zf), jnp.where(m_s1, a1[n][:ch], zf)) for n in un]
        aak = [cat0(jnp.where(m_s1, a0[n][:ch], zf), jnp.where(m_s0, a1[n][:ch], zf)).astype(BF16)
               for n in un]
        arbk = [jnp.concatenate([cat0(jnp.where(m_i0, a0[n][ch:], zf), jnp.where(m_i1, a1[n][ch:], zf)),
                                 cat0(jnp.where(m_i1, a0[n][ch:], zf), jnp.where(m_i0, a1[n][ch:], zf))],
                                axis=1).astype(BF16) for n in un]
        v2 = [cat0(vv[n], vv[n]) for n in un]
        av = [_dot(aak[n], v2[n]) for n in un]
        tm = [eye + pm[n] for n in un]
        qm = pm
        for _ in range(5):
            qb = [q.astype(BF16) for q in qm]
            qm = [_dot(qb[n], qb[n]) for n in un]
            tm = [tm[n] + _dot(tm[n].astype(BF16), qm[n].astype(BF16)) for n in un]
        tmb = [t.astype(BF16) for t in tm]
        for k in range(group):
            nk = [k * npair + p for p in pairs]
            st = [s_sc[p] for p in pairs]
            w12 = [_dot_nt(lcat[n], st[p].astype(BF16)) for p, n in zip(pairs, nk)]
            rhs = [cat0(w12[p][:ch], w12[p][:ch]) + av[n] for p, n in zip(pairs, nk)]
            ust = [_dot(tmb[n], rhs[p].astype(BF16)) for p, n in zip(pairs, nk)]
            yst = [cat0(w12[p][ch:], w12[p][ch:]) + _dot(arbk[n], cat0(ust[p].astype(BF16), v2[n]))
                   for p, n in zip(pairs, nk)]
            for p in pairs:
                y_sc[rows[k], cols[p]] = jnp.where(first, yst[p][:ch], yst[p][ch:])
            u = [jnp.where(first, ust[p][:ch], ust[p][ch:]).astype(BF16) for p in pairs]
            ds_ = [_dot_tn(cat0(u[p], vv[n]), cat0(sb_ref[rows[k], cols[p]], sk_ref[rows[k], cols[p]]))
                   for p, n in zip(pairs, nk)]
            for p in pairs:
                gcr = jnp.sum(jnp.where(gc_row == cidx[k], gc_ref[:, cols[p]], 0.0), axis=0, keepdims=True)
                s_sc[p] = jnp.where(same_half, st[p] * gcr + ds_[p], 0.0)
        return carry

    lax.fori_loop(0, tl // (ch * group), chunks, 0)

    y = y_sc[...]
    inv_n = 1.0 / RWKV_HEAD
    mean = _seg_sum(y, eseg_ref, eexp_ref) * inv_n
    dlt = y - mean
    var = _seg_sum(dlt * dlt, eseg_ref, eexp_ref) * inv_n
    yn = dlt * lax.rsqrt(var + RWKV_GN_EPS) * vec_ref[5:6, :] + vec_ref[6:7, :]
    z = (yn + bon_ref[...].astype(F32)) * g_ref[...].astype(F32)
    x4 = x_ref[...] + _dot(z.astype(BF16), wo_ref[...])
    x4_ref[...] = x4
    _store_token_tiles(x4t_ref, x4)
    first = jnp.logical_and(pl.program_id(0) == 0, pl.program_id(1) == 0)
    rt = _route(x4, gf_ref, wrh_ref, wrl_ref, br_ref, tri_ref, tot_sc, size_ref, first)
    rt_ref[...] = rt
    rtt_ref[...] = rt.T[:SUBLANES, :]


def _rwkv_scan(pre, x3, vecs, wo, gf, router, batch, seq):
    la, lr, rb, rk, sb, sk, v, bon, g, gc = pre
    t, d = x3.shape
    tl = RWKV_TILE
    nt = seq // tl
    eseg, eexp = _head_indicator(d)
    wrh, wrl, br = router
    row = lambda b, s: (b * nt + s, 0)
    fixed = lambda b, s: (0, 0)
    big = pl.BlockSpec((tl, d), row)
    full = lambda a: pl.BlockSpec(a.shape, fixed)
    pw = 2 * RWKV_HEAD
    tri = _strict_lower(tl)
    return pl.pallas_call(
        _rwkv_scan_kernel,
        grid=(batch, nt),
        in_specs=[big] * 7 + [pl.BlockSpec((tl // RWKV_CHUNK, d), row), big, big, big,
                              full(vecs), full(wo), full(eseg), full(eexp),
                              pl.BlockSpec((1, d), fixed), full(wrh), full(wrl), full(br), full(tri)],
        out_specs=[big, pl.BlockSpec((tl * SUBLANES, LANES), row), pl.BlockSpec((tl, LANES), row),
                   pl.BlockSpec((SUBLANES, tl), lambda b, s: (0, b * nt + s)),
                   pl.BlockSpec((SUBLANES, LANES), fixed)],
        out_shape=[jax.ShapeDtypeStruct((t, d), F32), jax.ShapeDtypeStruct((t * SUBLANES, LANES), F32),
                   jax.ShapeDtypeStruct((t, LANES), F32), jax.ShapeDtypeStruct((SUBLANES, t), F32),
                   jax.ShapeDtypeStruct((SUBLANES, LANES), F32)],
        scratch_shapes=[pltpu.VMEM((d // pw, pw, pw), F32), pltpu.VMEM((tl, d), F32),
                        pltpu.VMEM((1, LANES), F32)],
        compiler_params=_cparams("arbitrary", "arbitrary"),
        name="rwkv_scan_out",
    )(la, lr, rb, rk, sb, sk, v, gc, bon, g, x3, vecs, wo, eseg, eexp, gf.reshape(1, d), wrh, wrl, br, tri)


def _moe_and_embed(x1, x1t, route, route_t, sizes, i, p, norm_ffn, moe_w_gu, moe_w_down, ple_norm, ple_gate,
                   ple_proj, final_norm, final):
    t = x1.shape[0]
    blk_e, n_used, dest, tail, n_blk = _moe_plan(route_t, sizes, t)
    xs = _dispatch(x1t, dest, tail, n_used, n_blk)
    yst = _experts(xs, blk_e, n_used, norm_ffn[i], moe_w_gu, moe_w_down, i)
    p_all = p.reshape(-1, p.shape[-1])
    return _combine(x1, route, yst, dest, p_all, i, ple_norm[i], ple_gate[i].astype(BF16),
                    ple_proj[i].astype(BF16), final_norm, final)


def kernel(x, p, rel_bias, norm_mix, ab_w_in, ab_w_out, lam_q1, lam_k1, lam_q2, lam_k2, sub_g, conv_w, conv_b, lru_wa, lru_ba, lru_wx, lru_bx, lru_lambda, rwkv_mu, rwkv_wr, rwkv_wk, rwkv_wv, rwkv_wo, rwkv_w0, rwkv_w1, rwkv_w2, rwkv_a0, rwkv_a1, rwkv_a2, rwkv_g1, rwkv_g2, rwkv_kk, rwkv_ka, rwkv_rk, rwkv_ln_g, rwkv_ln_b, norm_ffn, moe_wc, moe_bc, moe_wf, moe_bf, moe_w_gu, moe_w_down, ple_norm, ple_gate, ple_proj, final_norm):
    batch, seq, d = x.shape
    t = batch * seq
    depth = norm_mix.shape[0]
    assert depth == 2 and seq % RWKV_TILE == 0 and seq % LRU_TILE == 0 and t % ROW_TILE == 0
    assert d == SUBLANES * LANES and seq % ATTN_TILE == 0 and t % DISPATCH_TILE == 0 and t % COMBINE_TILE == 0
    xs = x.reshape(t, d)
    for i in range(depth):
        j = i // 2
        router = _router_params(moe_wc[i], moe_bc[i], moe_wf[i], moe_bf[i])
        if i % 2 == 0:
            lam_init = 0.8 - 0.6 * math.exp(-0.3 * i)
            q, k, vt, xb, gb = _inproj(xs, norm_mix[i], ab_w_in[j].astype(BF16))
            lam_rows = jnp.stack([lam_q1[j], lam_k1[j], lam_q2[j], lam_k2[j]]).astype(F32)
            ya = _diff_attention(q, k, vt, rel_bias, lam_rows, sub_g[j], lam_init, batch, seq)
            yb = _rglru(xb, gb, conv_w[j], conv_b[j], lru_wa[j], lru_ba[j], lru_wx[j], lru_bx[j],
                        lru_lambda[j], batch, seq)
            x1, x1t, route, route_t, sizes = _outproj(xs, ya, yb, ab_w_out[j].astype(BF16), norm_ffn[i],
                                                      router)
        else:
            vecs = jnp.stack([rwkv_w0[j], rwkv_a0[j], rwkv_kk[j], rwkv_ka[j], rwkv_rk[j].reshape(-1),
                              rwkv_ln_g[j], rwkv_ln_b[j], jnp.zeros((d,), F32)]).astype(F32)
            pre = _rwkv_pre(xs, norm_mix[i], rwkv_mu[j], vecs, rwkv_wr[j], rwkv_wk[j], rwkv_wv[j],
                            rwkv_w1[j], rwkv_w2[j], rwkv_a1[j], rwkv_a2[j], rwkv_g1[j], rwkv_g2[j],
                            batch, seq)
            x1, x1t, route, route_t, sizes = _rwkv_scan(pre, xs, vecs, rwkv_wo[j].astype(BF16), norm_ffn[i],
                                                        router, batch, seq)
        xs = _moe_and_embed(x1, x1t, route, route_t, sizes, i, p, norm_ffn, moe_w_gu, moe_w_down, ple_norm, ple_gate,
                            ple_proj, final_norm, final=(i == depth - 1))
    return xs.reshape(batch, seq, d)
```

```python
import functools
import math

import jax
import jax.numpy as jnp
from jax import lax
from jax.experimental import pallas as pl
from jax.experimental.pallas import tpu as pltpu

F32, BF16, I32 = jnp.float32, jnp.bfloat16, jnp.int32

A_HEADS = 4
A_HEAD_DIM = 64
A_SCALE = A_HEAD_DIM ** -0.5
LOG2E = math.log2(math.e)
REL_BUCKETS = 32
REL_MAX_DIST = 128
CONV_W = 4
LRU_C = 8.0
RWKV_HEAD = 64
RWKV_GN_EPS = 64e-5
MOE_GROUPS = 4
EXPERTS_PER_GROUP = 8
N_EXPERTS = MOE_GROUPS * EXPERTS_PER_GROUP
MOE_TOPK = 2
NORM_EPS = 1e-6
SUBNORM_EPS = 1e-5

LANES = 128
VMEM_LIMIT = 56 * 1024 * 1024
ROW_TILE = 512
ATTN_TILE = 512
LRU_TILE = 512
RWKV_CHUNK = 64
RWKV_TILE = 512
RWKV_CHUNKS_PER_ITER = 2
MOE_BLOCK = 512
DISPATCH_TILE = 512
COMBINE_TILE = 256
NEG_BIG = -1e30


def _cparams(*sem):
    return pltpu.CompilerParams(dimension_semantics=sem, vmem_limit_bytes=VMEM_LIMIT)


def _rms(x, g, eps):
    return x * lax.rsqrt(jnp.mean(x * x, axis=-1, keepdims=True) + eps) * g


def _dot(a, b):
    return jnp.dot(a, b, preferred_element_type=F32)


def _dot_nt(a, b):
    return lax.dot_general(a, b, (((1,), (1,)), ((), ())), preferred_element_type=F32)


def _dot_tn(a, b):
    return lax.dot_general(a, b, (((0,), (0,)), ((), ())), preferred_element_type=F32)


def _split2(x):
    hi = x.astype(BF16)
    lo = (x - hi.astype(F32)).astype(BF16)
    return hi, lo


def _softplus(x):
    return jnp.maximum(x, 0.0) + jnp.log1p(jnp.exp(-jnp.abs(x)))


def _neg_expm1(y, exp_y):
    poly = 1.0 + y * (1.0 / 5.0)
    for d in (4.0, 3.0, 2.0):
        poly = 1.0 + (y * (1.0 / d)) * poly
    return jnp.where(y > -1.0 / 16.0, -y * poly, 1.0 - exp_y)


SUBLANES = 8
DMA_PRIORITIES = 2


def _store_token_tiles(ref, x):
    rows = x.shape[0]
    for s in range(x.shape[1] // LANES):
        ref[pl.ds(s, rows, stride=SUBLANES), :] = x[:, s * LANES:(s + 1) * LANES]


def _load_token_tiles(ref, rows):
    return jnp.concatenate([ref[pl.ds(s, rows, stride=SUBLANES), :] for s in range(SUBLANES)], axis=1)


def _gelu_tanh(x):
    c = math.sqrt(2.0 / math.pi)
    return 0.5 * x * (1.0 + jnp.tanh(c * (x + 0.044715 * (x * x * x))))


def _inproj_kernel(x_ref, g_ref, w_ref, q_ref, k_ref, vt_ref, xb_ref, gb_ref):
    hn = _rms(x_ref[...], g_ref[...], NORM_EPS).astype(BF16)
    width = q_ref.shape[1]
    for c, o_ref in enumerate((q_ref, k_ref, vt_ref, xb_ref, gb_ref)):
        r = _dot(hn, w_ref[:, c * width:(c + 1) * width])
        if c == 0:
            r = r * (A_SCALE * LOG2E)
        if c == 2:
            r = r.T
        o_ref[...] = r.astype(o_ref.dtype)


def _inproj(x2, g, w_in):
    t, d = x2.shape
    width = w_in.shape[1] // 5
    tm = ROW_TILE
    row = lambda i: (i, 0)
    fixed = lambda i: (0, 0)
    rows = pl.BlockSpec((tm, width), row)
    return pl.pallas_call(
        _inproj_kernel,
        grid=(t // tm,),
        in_specs=[pl.BlockSpec((tm, d), row), pl.BlockSpec((1, d), fixed),
                  pl.BlockSpec(w_in.shape, fixed)],
        out_specs=[rows, rows, pl.BlockSpec((width, tm), lambda i: (0, i)), rows, rows],
        out_shape=[jax.ShapeDtypeStruct((t, width), BF16)] * 2 + [jax.ShapeDtypeStruct((width, t), BF16)]
        + [jax.ShapeDtypeStruct((t, width), F32)] * 2,
        compiler_params=_cparams("arbitrary"),
        name="inproj",
    )(x2, g.reshape(1, d), w_in)


def _t5_bucket(rel):
    n = jnp.maximum(rel, 0)
    max_exact = REL_BUCKETS // 2
    large = max_exact + (jnp.log(jnp.maximum(n, 1).astype(F32) / max_exact)
                         / math.log(REL_MAX_DIST / max_exact)
                         * (REL_BUCKETS - max_exact)).astype(I32)
    large = jnp.minimum(large, REL_BUCKETS - 1)
    return jnp.where(n < max_exact, n, large)


def _bias_tables(rel_bias, tile):
    span = 2 * tile

    def toeplitz(first_rel):
        rel = jnp.arange(span - 1) + first_rel
        vals = jnp.where((rel >= 0)[:, None], rel_bias[_t5_bucket(rel)].astype(F32), NEG_BIG)
        w = jnp.concatenate([vals, jnp.zeros((1, vals.shape[1]), F32)], axis=0).T
        skew = jnp.tile(w, (1, tile))[:, :tile * (span - 1)].reshape(-1, tile, span - 1)
        return skew[:, :, tile - 1:]

    return jnp.stack([toeplitz(1 - tile), toeplitz(1)], axis=1)


def _attn_kernel(cfar_ref, q_ref, qn_ref, k_ref, vt_ref, bias_ref, lam_ref, subg_ref, o_ref,
                 m_sc, l_sc, acc_sc, sa_sc, sb_sc, *, lam_init):
    h = pl.program_id(1)
    i = pl.program_id(2)
    tile = q_ref.shape[0]
    lane = lax.broadcasted_iota(I32, q_ref.shape, 1)

    def map_queries(q):
        zero = jnp.zeros_like(q)
        return (jnp.where(lane < A_HEAD_DIM, q, zero), jnp.where(lane >= A_HEAD_DIM, q, zero))

    qm = map_queries(q_ref[...])
    m_sc[...] = jnp.full(m_sc.shape, NEG_BIG, F32)
    l_sc[...] = jnp.zeros(l_sc.shape, F32)
    acc_sc[...] = jnp.zeros(acc_sc.shape, F32)

    maps = range(2)

    def scores_to(dst, j, queries=qm):
        kb = k_ref[pl.ds(pl.multiple_of(j * tile, tile), tile), :]
        for mi in maps:
            dst[mi] = _dot_nt(kb, queries[mi])

    def consume(s, j, table, const):
        vtb = vt_ref[:, pl.ds(pl.multiple_of(j * tile, tile), tile)]
        if table is not None:
            s = [s[mi] + table for mi in maps]
        half = tile // 2
        for mi in maps:
            for qs in (slice(0, half), slice(half, tile)):
                sq = s[mi][:, qs]
                m_prev = m_sc[mi, :, qs]
                m_new = jnp.maximum(m_prev, jnp.max(sq, axis=0, keepdims=True) + const)
                p = jnp.exp2(sq - (m_new - const))
                alpha = jnp.exp2(m_prev - m_new)
                l_sc[mi, :, qs] = alpha * l_sc[mi, :, qs] + jnp.sum(p, axis=0, keepdims=True)
                acc_sc[mi, :, qs] = alpha * acc_sc[mi, :, qs] + _dot(vtb, p.astype(BF16))
                m_sc[mi, :, qs] = m_new

    held = lambda buf: [buf[mi] for mi in maps]
    cfar = cfar_ref[h]
    nfar = jnp.maximum(i - 1, 0)

    @pl.when(i == 0)
    def _():
        scores_to(sa_sc, 0)

    def far_pair(j):
        scores_to(sb_sc, j + 1)
        consume(held(sa_sc), j, None, cfar)
        scores_to(sa_sc, j + 2)
        consume(held(sb_sc), j + 1, None, cfar)

    def far_quad(t, c):
        far_pair(4 * t)
        far_pair(4 * t + 2)
        return c

    nquad = lax.shift_right_logical(nfar, 2)
    lax.fori_loop(0, nquad, far_quad, 0)

    @pl.when((nfar & 2) != 0)
    def _():
        far_pair(4 * nquad)

    @pl.when(jnp.logical_and(i >= 1, nfar % 2 == 0))
    def _():
        scores_to(sb_sc, i)
        consume(held(sa_sc), i - 1, bias_ref[0, 1], 0.0)
        consume(held(sb_sc), i, bias_ref[0, 0], 0.0)

    @pl.when(nfar % 2 == 1)
    def _():
        scores_to(sb_sc, i - 1)
        consume(held(sa_sc), i - 2, None, cfar)
        scores_to(sa_sc, i)
        consume(held(sb_sc), i - 1, bias_ref[0, 1], 0.0)
        consume(held(sa_sc), i, bias_ref[0, 0], 0.0)

    @pl.when(i == 0)
    def _():
        consume(held(sa_sc), 0, bias_ref[0, 0], 0.0)

    lam_rows = lam_ref[...]
    lam = (jnp.exp(jnp.sum(lam_rows[0:1] * lam_rows[1:2], axis=-1, keepdims=True))
           - jnp.exp(jnp.sum(lam_rows[2:3] * lam_rows[3:4], axis=-1, keepdims=True)) + lam_init)
    o = acc_sc[0] / l_sc[0] - lam * (acc_sc[1] / l_sc[1])
    o = o * lax.rsqrt(jnp.mean(o * o, axis=0, keepdims=True) + SUBNORM_EPS)
    o_ref[...] = (o.T * subg_ref[...] * (1.0 - lam_init)).astype(o_ref.dtype)
    scores_to(sa_sc, 0, map_queries(qn_ref[...]))


def _diff_attention(q, k, vt, rel_bias, lam_rows, sub_g, lam_init, batch, seq):
    t, width = q.shape
    hw = 2 * A_HEAD_DIM
    tile = ATTN_TILE
    nq = seq // tile
    tables = _bias_tables(rel_bias, tile) * LOG2E
    cfar = rel_bias[REL_BUCKETS - 1].astype(F32) * LOG2E
    return pl.pallas_call(
        functools.partial(_attn_kernel, lam_init=lam_init),
        grid=(batch, A_HEADS, nq),
        in_specs=[
            pl.BlockSpec(memory_space=pltpu.SMEM),
            pl.BlockSpec((tile, hw), lambda b, h, i: (b * nq + i, h)),
            pl.BlockSpec((tile, hw), lambda b, h, i: (b * nq + jnp.minimum(i + 1, nq - 1), h)),
            pl.BlockSpec((seq, hw), lambda b, h, i: (b, h)),
            pl.BlockSpec((hw, seq), lambda b, h, i: (h, b)),
            pl.BlockSpec((1, 2, tile, tile), lambda b, h, i: (h, 0, 0, 0)),
            pl.BlockSpec(lam_rows.shape, lambda b, h, i: (0, 0)),
            pl.BlockSpec((1, hw), lambda b, h, i: (0, 0)),
        ],
        out_specs=pl.BlockSpec((tile, hw), lambda b, h, i: (b * nq + i, h)),
        out_shape=jax.ShapeDtypeStruct((t, width), BF16),
        scratch_shapes=[pltpu.VMEM((2, 1, tile), F32), pltpu.VMEM((2, 1, tile), F32),
                        pltpu.VMEM((2, hw, tile), F32),
                        pltpu.VMEM((2, tile, tile), F32), pltpu.VMEM((2, tile, tile), F32)],
        compiler_params=_cparams("arbitrary", "arbitrary", "arbitrary"),
        name="diff_attn",
    )(cfar, q, q, k, vt, tables, lam_rows, sub_g.reshape(1, hw))


def _lru_kernel(xb_ref, gb_ref, cw_ref, vec_ref, wa_ref, wx_ref, y_ref, prev_sc, h_sc, a_sc, b_sc, hs_sc):
    ts = xb_ref.shape[0]

    @pl.when(pl.program_id(1) == 0)
    def _():
        prev_sc[...] = jnp.zeros(prev_sc.shape, F32)
        h_sc[...] = jnp.zeros(h_sc.shape, F32)

    xb = xb_ref[...]
    hist = prev_sc.shape[0]
    ext = jnp.concatenate([prev_sc[...], xb], axis=0)
    u = vec_ref[0:1, :]
    for j in range(CONV_W):
        off = hist - (CONV_W - 1) + j
        u = u + cw_ref[j:j + 1, :] * ext[off:off + ts, :]
    prev_sc[...] = xb[ts - hist:, :]
    ub = u.astype(BF16)
    r = jax.nn.sigmoid(_dot(ub, wa_ref[...]) + vec_ref[1:2, :])
    ig = jax.nn.sigmoid(_dot(ub, wx_ref[...]) + vec_ref[2:3, :])
    log_a = (-LRU_C * r) * _softplus(-vec_ref[3:4, :])
    a = jnp.exp(log_a)
    a_sc[...] = a
    b_sc[...] = jnp.sqrt(_neg_expm1(2.0 * log_a, a * a)) * (ig * u)

    def step(tt, h):
        h = a_sc[pl.ds(tt, 1), :] * h + b_sc[pl.ds(tt, 1), :]
        hs_sc[pl.ds(tt, 1), :] = h
        return h

    h_sc[...] = lax.fori_loop(0, ts, step, h_sc[...], unroll=16)
    y_ref[...] = (hs_sc[...] * _gelu_tanh(gb_ref[...])).astype(y_ref.dtype)


def _block_diag(w):
    n, c, d = w.shape
    eye = jnp.eye(n, dtype=w.dtype)
    return (w[:, :, None, :] * eye[:, None, :, None]).reshape(n * c, n * d)


def _rglru(xb, gb, conv_w, conv_b, wa, ba, wx, bx, lru_lambda, batch, seq):
    t, w = xb.shape
    ts = LRU_TILE
    nt = seq // ts
    vec = jnp.zeros((8, w), F32).at[0].set(conv_b).at[1].set(ba).at[2].set(bx).at[3].set(lru_lambda)
    row = lambda b, s: (b * nt + s, 0)
    fixed = lambda b, s: (0, 0)
    return pl.pallas_call(
        _lru_kernel,
        grid=(batch, nt),
        in_specs=[pl.BlockSpec((ts, w), row), pl.BlockSpec((ts, w), row),
                  pl.BlockSpec((CONV_W, w), fixed), pl.BlockSpec((8, w), fixed),
                  pl.BlockSpec((w, w), fixed), pl.BlockSpec((w, w), fixed)],
        out_specs=pl.BlockSpec((ts, w), row),
        out_shape=jax.ShapeDtypeStruct((t, w), BF16),
        scratch_shapes=[pltpu.VMEM((8, w), F32), pltpu.VMEM((1, w), F32),
                        pltpu.VMEM((ts, w), F32), pltpu.VMEM((ts, w), F32), pltpu.VMEM((ts, w), F32)],
        compiler_params=_cparams("arbitrary", "arbitrary"),
        name="rglru",
    )(xb, gb, conv_w, vec, _block_diag(wa).astype(BF16), _block_diag(wx).astype(BF16))


def _route(x, gf_ref, wrh_ref, wrl_ref, br_ref, tri_ref, tot_sc, size_ref, first_step):
    @pl.when(first_step)
    def _():
        tot_sc[...] = jnp.zeros(tot_sc.shape, F32)

    hn = _rms(x, gf_ref[...], NORM_EPS)
    hh, hl = _split2(hn)
    wh = wrh_ref[...]
    lg = _dot(hh, wh) + _dot(hh, wrl_ref[...]) + _dot(hl, wh) + br_ref[...]
    lane = lax.broadcasted_iota(I32, lg.shape, 1)
    big = jnp.int32(1 << 20)
    is_g = lane < MOE_GROUPS
    gl = jnp.where(is_g, lg, NEG_BIG)
    gmax = jnp.max(gl, axis=-1, keepdims=True)
    gidx = jnp.min(jnp.where(gl == gmax, lane, big), axis=-1, keepdims=True)
    gsum = jnp.sum(jnp.where(is_g, jnp.exp(gl - gmax), 0.0), axis=-1, keepdims=True)
    gprob = 1.0 / gsum
    lo = MOE_GROUPS + gidx * EXPERTS_PER_GROUP
    fmask = (lane >= lo) & (lane < lo + EXPERTS_PER_GROUP)
    fl = jnp.where(fmask, lg, NEG_BIG)
    f1 = jnp.max(fl, axis=-1, keepdims=True)
    i1 = jnp.min(jnp.where(fl == f1, lane, big), axis=-1, keepdims=True)
    fl2 = jnp.where(lane == i1, NEG_BIG, fl)
    f2 = jnp.max(fl2, axis=-1, keepdims=True)
    i2 = jnp.min(jnp.where(fl2 == f2, lane, big), axis=-1, keepdims=True)
    e2 = jnp.exp(f2 - f1)
    den = 1.0 + e2
    g1 = gprob / den
    g2 = gprob * e2 / den
    id1 = (i1 - MOE_GROUPS).astype(F32)
    id2 = (i2 - MOE_GROUPS).astype(F32)
    onehot = jnp.where((lane == i1) | (lane == i2), 1.0, 0.0)
    before = _dot(tri_ref[...], onehot.astype(BF16)) + tot_sc[...]
    r1 = jnp.sum(jnp.where(lane == i1, before, 0.0), axis=-1, keepdims=True)
    r2 = jnp.sum(jnp.where(lane == i2, before, 0.0), axis=-1, keepdims=True)
    tot = tot_sc[...] + jnp.sum(onehot, axis=0, keepdims=True)
    tot_sc[...] = tot
    size_ref[...] = jnp.broadcast_to(tot, size_ref.shape)
    vals = (g1, g2, id1, id2, r1, r2)
    out = jnp.zeros(lg.shape, F32)
    for n, v in enumerate(vals):
        out = jnp.where(lane == n, v, out)
    return out


def _router_params(wc, bc, wf, bf):
    d = wc.shape[0]
    n = MOE_GROUPS + N_EXPERTS
    w = jnp.zeros((d, LANES), F32).at[:, :MOE_GROUPS].set(wc).at[:, MOE_GROUPS:n].set(wf)
    b = jnp.zeros((1, LANES), F32).at[0, :MOE_GROUPS].set(bc).at[0, MOE_GROUPS:n].set(bf)
    hi, lo = _split2(w)
    return hi, lo, b


def _outproj_kernel(x_ref, ya_ref, yb_ref, wo_ref, gf_ref, wrh_ref, wrl_ref, br_ref, tri_ref,
                    x1_ref, x1t_ref, rt_ref, rtt_ref, size_ref, tot_sc):
    half = ya_ref.shape[1]
    x1 = x_ref[...] + _dot(ya_ref[...], wo_ref[:half, :]) + _dot(yb_ref[...], wo_ref[half:, :])
    x1_ref[...] = x1
    _store_token_tiles(x1t_ref, x1)
    rt = _route(x1, gf_ref, wrh_ref, wrl_ref, br_ref, tri_ref, tot_sc, size_ref, pl.program_id(0) == 0)
    rt_ref[...] = rt
    rtt_ref[...] = rt.T[:SUBLANES, :]


def _outproj(x2, ya, yb, w_out, gf, router):
    t, d = x2.shape
    half = ya.shape[1]
    tm = ROW_TILE
    row = lambda i: (i, 0)
    fixed = lambda i: (0, 0)
    wrh, wrl, br = router
    tri = _strict_lower(tm)
    return pl.pallas_call(
        _outproj_kernel,
        grid=(t // tm,),
        in_specs=[pl.BlockSpec((tm, d), row), pl.BlockSpec((tm, half), row), pl.BlockSpec((tm, half), row),
                  pl.BlockSpec((2 * half, d), fixed), pl.BlockSpec((1, d), fixed),
                  pl.BlockSpec((d, LANES), fixed), pl.BlockSpec((d, LANES), fixed),
                  pl.BlockSpec((1, LANES), fixed), pl.BlockSpec(tri.shape, fixed)],
        out_specs=[pl.BlockSpec((tm, d), row), pl.BlockSpec((tm * SUBLANES, LANES), row),
                   pl.BlockSpec((tm, LANES), row), pl.BlockSpec((SUBLANES, tm), lambda i: (0, i)),
                   pl.BlockSpec((SUBLANES, LANES), fixed)],
        out_shape=[jax.ShapeDtypeStruct((t, d), F32), jax.ShapeDtypeStruct((t * SUBLANES, LANES), F32),
                   jax.ShapeDtypeStruct((t, LANES), F32), jax.ShapeDtypeStruct((SUBLANES, t), F32),
                   jax.ShapeDtypeStruct((SUBLANES, LANES), F32)],
        scratch_shapes=[pltpu.VMEM((1, LANES), F32)],
        compiler_params=_cparams("arbitrary"),
        name="outproj_route",
    )(x2, ya, yb, w_out, gf.reshape(1, d), wrh, wrl, br, tri)


def _strict_lower(n):
    return (jnp.arange(n)[None, :] < jnp.arange(n)[:, None]).astype(BF16)


def _moe_plan(route_t, sizes_tile, n_tok):
    m = n_tok * MOE_TOPK
    n_blk = -(-(m + N_EXPERTS * (MOE_BLOCK - 1)) // MOE_BLOCK)
    eid = route_t[2:4].astype(I32).T
    rank = route_t[4:6].astype(I32).T
    sizes = sizes_tile[0, MOE_GROUPS:MOE_GROUPS + N_EXPERTS].astype(I32)
    padded = (sizes + MOE_BLOCK - 1) // MOE_BLOCK * MOE_BLOCK
    pend = jnp.cumsum(padded)
    dest = (pend - padded)[eid] + rank
    blk_start = jnp.arange(n_blk, dtype=I32) * MOE_BLOCK
    blk_e = jnp.minimum(jnp.sum((pend[None, :] <= blk_start[:, None]).astype(I32), axis=1),
                        N_EXPERTS - 1).astype(I32)
    n_used = (pend[-1] // MOE_BLOCK).astype(I32).reshape(1)
    tail = jnp.where(padded > 0, pend - MOE_BLOCK, -1).astype(I32)
    return blk_e, n_used, dest.astype(I32), tail, n_blk


def _dispatch_kernel(tail_ref, nused_ref, dest_hbm, xt_ref, xs_hbm,
                     idx_sm0, idx_sm1, idx_sem, zero_sc, zero_sem, stage_sc, row_sem):
    i = pl.program_id(0)
    n = pl.num_programs(0)
    tm = idx_sm0.shape[0] // MOE_TOPK
    idx_sms = (idx_sm0, idx_sm1)
    tile_rows = MOE_BLOCK * SUBLANES
    n_blk = xs_hbm.shape[0] // tile_rows

    @pl.when(i == 0)
    def _():
        zero_sc[...] = jnp.zeros(zero_sc.shape, F32)

        def clear(row, wait):
            cp = pltpu.make_async_copy(
                zero_sc, xs_hbm.at[pl.ds(pl.multiple_of(row * SUBLANES, SUBLANES), tile_rows), :], zero_sem)
            if wait:
                cp.wait()
            else:
                cp.start()

        def fill(e, wait):
            @pl.when(tail_ref[e] >= 0)
            def _():
                clear(tail_ref[e], wait)

        for wait in (False, True):
            lax.fori_loop(0, N_EXPERTS, lambda e, c, w=wait: (fill(e, w), c)[1], 0)
            lax.fori_loop(nused_ref[0], n_blk, lambda b, c, w=wait: (clear(b * MOE_BLOCK, w), c)[1], 0)

    def idx_copy(step, sl):
        return pltpu.make_async_copy(dest_hbm.at[step], idx_sms[sl], idx_sem.at[sl])

    def all_copies(sl):
        for s in range(MOE_TOPK):
            pltpu.make_async_copy(stage_sc.at[sl], xs_hbm.at[pl.ds(0, tm * SUBLANES), :], row_sem.at[sl]).wait()

    @pl.when(i == 0)
    def _():
        idx_copy(0, 0).start()

    for sl in range(2):
        @pl.when(i % 2 == sl)
        def _():
            @pl.when(i >= 2)
            def _():
                all_copies(sl)

            stage_sc[sl] = xt_ref[...]
            idx_copy(i, sl).wait()

            @pl.when(i + 1 < n)
            def _():
                idx_copy(i + 1, 1 - sl).start()

            def issue(r, c):
                src = pl.multiple_of(r * SUBLANES, SUBLANES)
                for s in range(MOE_TOPK):
                    dst = pl.multiple_of(idx_sms[sl][MOE_TOPK * r + s], SUBLANES)
                    pltpu.make_async_copy(stage_sc.at[sl, pl.ds(src, SUBLANES), :],
                                          xs_hbm.at[pl.ds(dst, SUBLANES), :],
                                          row_sem.at[sl]).start(priority=s % DMA_PRIORITIES)
                return c

            lax.fori_loop(0, tm, issue, 0, unroll=8)

            @pl.when(i == n - 1)
            def _():
                all_copies(sl)

                @pl.when(n >= 2)
                def _():
                    all_copies(1 - sl)


def _dispatch(x1t, dest, tail, n_used, n_blk):
    t = dest.shape[0]
    tm = DISPATCH_TILE
    pos = (dest * SUBLANES).reshape(t // tm, MOE_TOPK * tm)
    grid_spec = pltpu.PrefetchScalarGridSpec(
        num_scalar_prefetch=2,
        grid=(t // tm,),
        in_specs=[pl.BlockSpec(memory_space=pl.ANY),
                  pl.BlockSpec((tm * SUBLANES, LANES), lambda i, tl, nu: (i, 0))],
        out_specs=pl.BlockSpec(memory_space=pl.ANY),
        scratch_shapes=[pltpu.SMEM((MOE_TOPK * tm,), I32), pltpu.SMEM((MOE_TOPK * tm,), I32),
                        pltpu.SemaphoreType.DMA((2,)), pltpu.VMEM((MOE_BLOCK * SUBLANES, LANES), F32),
                        pltpu.SemaphoreType.DMA, pltpu.VMEM((2, tm * SUBLANES, LANES), F32),
                        pltpu.SemaphoreType.DMA((2,))],
    )
    return pl.pallas_call(
        _dispatch_kernel,
        grid_spec=grid_spec,
        out_shape=jax.ShapeDtypeStruct((n_blk * MOE_BLOCK * SUBLANES, LANES), F32),
        compiler_params=_cparams("arbitrary"),
        name="moe_dispatch",
    )(tail, n_used, pos, x1t)


def _gather_ahead(idx_hbm, idx_sms, idx_sem, issue_rows, n_active):
    i = pl.program_id(0)
    n = pl.num_programs(0)

    def idx_copy(step, slot):
        return pltpu.make_async_copy(idx_hbm.at[step], idx_sms[slot], idx_sem.at[slot])

    @pl.when(i == 0)
    def _():
        first = idx_copy(0, 0)
        first.start()
        first.wait()

        @pl.when(0 < n_active)
        def _():
            issue_rows(0)

        @pl.when(1 < n)
        def _():
            idx_copy(1, 1).start()

    for slot in range(2):
        @pl.when(jnp.logical_and(i + 1 < n, (i + 1) % 2 == slot))
        def _():
            idx_copy(i + 1, slot).wait()

            @pl.when(i + 1 < n_active)
            def _():
                issue_rows(slot)

        @pl.when(jnp.logical_and(i + 2 < n, i % 2 == slot))
        def _():
            idx_copy(i + 2, slot).start()


def _expert_kernel(blk_e_ref, nused_ref, xs_ref, gf_ref, wgu_ref, wd_ref, yst_ref, wgu_sc, wd_sc):
    i = pl.program_id(0)

    @pl.when(i >= nused_ref[0])
    def _():
        yst_ref[...] = jnp.zeros(yst_ref.shape, F32)

    @pl.when(jnp.logical_or(i == 0, blk_e_ref[i] != blk_e_ref[jnp.maximum(i - 1, 0)]))
    def _():
        wgu_sc[...] = wgu_ref[0, 0].astype(BF16)
        wd_sc[...] = wd_ref[0, 0].astype(BF16)

    @pl.when(i < nused_ref[0])
    def _():
        blk = xs_ref.shape[0] // SUBLANES
        x = _load_token_tiles(xs_ref, blk)
        hn = _rms(x, gf_ref[...], NORM_EPS).astype(BF16)
        gu = _dot(hn, wgu_sc[...])
        eh = gu.shape[1] // 2
        g = gu[:, :eh]
        act = (g * jax.nn.sigmoid(g) * gu[:, eh:]).astype(BF16)
        _store_token_tiles(yst_ref, _dot(act, wd_sc[...]))


def _experts(xs, blk_e, n_used, gf, w_gu, w_down, layer):
    d = gf.shape[0]
    blk = MOE_BLOCK
    n_blk = xs.shape[0] // (blk * SUBLANES)
    eh2 = w_gu.shape[3]
    rows = lambda i, e, n: (jnp.minimum(i, n[0] - 1), 0)
    grid_spec = pltpu.PrefetchScalarGridSpec(
        num_scalar_prefetch=2,
        grid=(n_blk,),
        in_specs=[
            pl.BlockSpec((blk * SUBLANES, LANES), rows),
            pl.BlockSpec((1, d), lambda i, e, n: (0, 0)),
            pl.BlockSpec((1, 1, d, eh2), lambda i, e, n: (layer, e[i], 0, 0)),
            pl.BlockSpec((1, 1, eh2 // 2, d), lambda i, e, n: (layer, e[i], 0, 0)),
        ],
        out_specs=pl.BlockSpec((blk * SUBLANES, LANES), lambda i, e, n: (i, 0)),
        scratch_shapes=[pltpu.VMEM((d, eh2), BF16), pltpu.VMEM((eh2 // 2, d), BF16)],
    )
    return pl.pallas_call(
        _expert_kernel,
        grid_spec=grid_spec,
        out_shape=jax.ShapeDtypeStruct(xs.shape, F32),
        compiler_params=_cparams("arbitrary"),
        name="moe_experts",
    )(blk_e, n_used, xs, gf.reshape(1, d), w_gu, w_down)


def _combine_kernel(pos_hbm, yst_hbm, x1_ref, rt_ref, p_ref, gn_ref, wg_ref, wp_ref, fn_ref, o_ref,
                    idx_sm0, idx_sm1, idx_sem, ybuf, row_sem, *, final):
    slot = pl.program_id(0) % 2
    tm = x1_ref.shape[0]
    idx_sms = (idx_sm0, idx_sm1)

    def issue_rows(sl):
        def issue(r, c):
            dst = pl.multiple_of(r * SUBLANES, SUBLANES)
            for s in range(MOE_TOPK):
                src = pl.multiple_of(idx_sms[sl][MOE_TOPK * r + s], SUBLANES)
                pltpu.make_async_copy(yst_hbm.at[pl.ds(src, SUBLANES), :],
                                      ybuf.at[sl, s, pl.ds(dst, SUBLANES), :],
                                      row_sem.at[sl]).start(priority=s % DMA_PRIORITIES)
            return c

        lax.fori_loop(0, tm, issue, 0, unroll=8)

    _gather_ahead(pos_hbm, idx_sms, idx_sem, issue_rows, pl.num_programs(0))
    for s in range(MOE_TOPK):
        pltpu.make_async_copy(yst_hbm.at[pl.ds(0, tm * SUBLANES), :], ybuf.at[slot, s], row_sem.at[slot]).wait()

    rt = rt_ref[...]
    y0 = _load_token_tiles(ybuf.at[slot, 0], tm)
    y1 = _load_token_tiles(ybuf.at[slot, 1], tm)
    x2 = x1_ref[...] + (rt[:, 0:1] * y0 + rt[:, 1:2] * y1)
    gate = jax.nn.sigmoid(_dot(_rms(x2, gn_ref[...], NORM_EPS).astype(BF16), wg_ref[...]))
    x3 = x2 + _dot(p_ref[...].astype(BF16), wp_ref[...]) * gate
    if final:
        x3 = _rms(x3, fn_ref[...], NORM_EPS)
    o_ref[...] = x3


def _combine(x1, route, yst, dest, p_all, layer, ple_norm, ple_gate, ple_proj, final_norm, final):
    t, d = x1.shape
    tm = COMBINE_TILE
    pd = p_all.shape[1]
    pos = (dest * SUBLANES).reshape(t // tm, MOE_TOPK * tm)
    row = lambda i: (i, 0)
    fixed = lambda i: (0, 0)
    return pl.pallas_call(
        functools.partial(_combine_kernel, final=final),
        grid=(t // tm,),
        in_specs=[pl.BlockSpec(memory_space=pl.ANY), pl.BlockSpec(memory_space=pl.ANY),
                  pl.BlockSpec((tm, d), row), pl.BlockSpec((tm, LANES), row),
                  pl.BlockSpec((tm, pd), lambda i: (layer * (t // tm) + i, 0)),
                  pl.BlockSpec((1, d), fixed), pl.BlockSpec((d, d), fixed), pl.BlockSpec((pd, d), fixed),
                  pl.BlockSpec((1, d), fixed)],
        out_specs=pl.BlockSpec((tm, d), row),
        out_shape=jax.ShapeDtypeStruct((t, d), F32),
        scratch_shapes=[pltpu.SMEM((MOE_TOPK * tm,), I32), pltpu.SMEM((MOE_TOPK * tm,), I32),
                        pltpu.SemaphoreType.DMA((2,)),
                        pltpu.VMEM((2, MOE_TOPK, tm * SUBLANES, LANES), F32), pltpu.SemaphoreType.DMA((2,))],
        compiler_params=_cparams("arbitrary"),
        name="moe_combine_ple",
    )(pos, yst, x1, route, p_all, ple_norm.reshape(1, d), ple_gate, ple_proj, final_norm.reshape(1, d))


def _seg_sum(x, eseg_ref, eexp_ref):
    s = _dot(x.astype(BF16), eseg_ref[...])
    return _dot(s.astype(BF16), eexp_ref[...])


def _rwkv_pre_kernel(x_ref, xp_ref, gm_ref, mu_ref, vec_ref, wr_ref, wk_ref, wv_ref, w1_ref, w2_ref,
                     a1_ref, a2_ref, g1_ref, g2_ref, tri_ref, eseg_ref, eexp_ref,
                     la_ref, lr_ref, rb_ref, rk_ref, sb_ref, sk_ref, v_ref, bon_ref, g_ref, gc_ref,
                     c_sc):
    tl = x_ref.shape[0]
    gm = gm_ref[...]
    h = _rms(x_ref[...], gm, NORM_EPS)
    hp = _rms(xp_ref[...], gm, NORM_EPS)[7:8, :]
    hp = jnp.where(pl.program_id(1) == 0, 0.0, hp)
    row = lax.broadcasted_iota(I32, h.shape, 0)
    hs = jnp.where(row == 0, hp, pltpu.roll(h, 1, axis=0))
    xx = hs - h
    mix = lambda n: (h + xx * mu_ref[n:n + 1, :]).astype(BF16)
    vec = lambda n: vec_ref[n:n + 1, :]
    r = _dot(mix(0), wr_ref[...])
    wl = vec(0) + _dot(jnp.tanh(_dot(mix(1), w1_ref[...])).astype(BF16), w2_ref[...])
    lw = -math.exp(-0.5) * jax.nn.sigmoid(wl)
    k = _dot(mix(2), wk_ref[...])
    v = _dot(mix(3), wv_ref[...])
    a = jax.nn.sigmoid(vec(1) + _dot(_dot(mix(4), a1_ref[...]).astype(BF16), a2_ref[...]))
    g = _dot(jax.nn.sigmoid(_dot(mix(5), g1_ref[...])).astype(BF16), g2_ref[...])
    kk = k * vec(2)
    kk = kk * lax.rsqrt(jnp.maximum(_seg_sum(kk * kk, eseg_ref, eexp_ref), 1e-24))
    k2 = k * (1.0 + (a - 1.0) * vec(3))
    bon_ref[...] = (_seg_sum(r * k2 * vec(4), eseg_ref, eexp_ref) * v).astype(bon_ref.dtype)
    v_ref[...] = v.astype(v_ref.dtype)
    g_ref[...] = g.astype(g_ref.dtype)

    half = tri_ref.shape[1]
    tri = tri_ref[...]
    for s in range(tl // half):
        p1, p2 = _split2(lw[s * half:(s + 1) * half, :])
        c_sc[s * half:(s + 1) * half, :] = _dot(tri, p1) + _dot(tri, p2)
    c_in = c_sc[...]
    nc = tl // RWKV_CHUNK
    ends = [c_sc[(n + 1) * RWKV_CHUNK - 1:(n + 1) * RWKV_CHUNK, :] for n in range(nc)]
    c_end = jnp.concatenate([ends[n] - c_sc[n * RWKV_CHUNK:(n + 1) * RWKV_CHUNK, :] for n in range(nc)],
                            axis=0)
    gc_ref[...] = jnp.exp(jnp.concatenate(ends, axis=0))
    e_in = jnp.exp(c_in)
    e_neg = jnp.exp(-c_in)
    e_end = jnp.exp(c_end)
    b = kk * a
    la_ref[...] = (-kk * jnp.exp(c_in - lw)).astype(la_ref.dtype)
    lr_ref[...] = (r * e_in).astype(lr_ref.dtype)
    rb_ref[...] = (b * e_neg).astype(rb_ref.dtype)
    rk_ref[...] = (k2 * e_neg).astype(rk_ref.dtype)
    sb_ref[...] = (b * e_end).astype(sb_ref.dtype)
    sk_ref[...] = (k2 * e_end).astype(sk_ref.dtype)


def _head_indicator(d):
    e = (jnp.arange(d)[:, None] // RWKV_HEAD == jnp.arange(LANES)[None, :]).astype(BF16)
    return e, e.T


def _cumsum_matrix(half):
    i = jnp.arange(half)[:, None]
    j = jnp.arange(half)[None, :]
    return (((i // RWKV_CHUNK) == (j // RWKV_CHUNK)) & (j <= i)).astype(BF16)


def _pad_cols(w, n):
    return jnp.pad(w, ((0, 0), (0, n - w.shape[1])))


def _pad_rows(w, n):
    return jnp.pad(w, ((0, n - w.shape[0]), (0, 0)))


def _rwkv_pre(x3, gm, mu, vecs, wr, wk, wv, w1, w2, a1, a2, g1, g2, batch, seq):
    t, d = x3.shape
    tl = RWKV_TILE
    nt = seq // tl
    half = 256
    lora = lambda w_in, w_out: (_pad_cols(w_in, -(-w_in.shape[1] // LANES) * LANES).astype(BF16),
                                _pad_rows(w_out, -(-w_out.shape[0] // LANES) * LANES).astype(BF16))
    w1p, w2p = lora(w1, w2)
    a1p, a2p = lora(a1, a2)
    g1p, g2p = lora(g1, g2)
    eseg, eexp = _head_indicator(d)
    tri = _cumsum_matrix(half)
    row = lambda b, s: (b * nt + s, 0)
    fixed = lambda b, s: (0, 0)
    full = lambda a: pl.BlockSpec(a.shape, fixed)
    prev = lambda b, s: (jnp.maximum((b * nt + s) * (tl // 8) - 1, 0), 0)
    weights = (wr.astype(BF16), wk.astype(BF16), wv.astype(BF16), w1p, w2p, a1p, a2p, g1p, g2p, tri, eseg, eexp)
    outs = pl.pallas_call(
        _rwkv_pre_kernel,
        grid=(batch, nt),
        in_specs=[pl.BlockSpec((tl, d), row), pl.BlockSpec((8, d), prev), pl.BlockSpec((1, d), fixed),
                  full(mu), full(vecs)] + [full(w) for w in weights],
        out_specs=[pl.BlockSpec((tl, d), row)] * 9 + [pl.BlockSpec((tl // RWKV_CHUNK, d), row)],
        out_shape=[jax.ShapeDtypeStruct((t, d), BF16)] * 9 + [jax.ShapeDtypeStruct((t // RWKV_CHUNK, d), F32)],
        scratch_shapes=[pltpu.VMEM((tl, d), F32)],
        compiler_params=_cparams("arbitrary", "arbitrary"),
        name="rwkv_pre",
    )(x3, x3, gm.reshape(1, d), mu, vecs, *weights)
    return outs


def _rwkv_scan_kernel(la_ref, lr_ref, rb_ref, rk_ref, sb_ref, sk_ref, v_ref, gc_ref, bon_ref, g_ref, x_ref,
                      vec_ref, wo_ref, eseg_ref, eexp_ref, gf_ref, wrh_ref, wrl_ref, br_ref, tri_ref,
                      x4_ref, x4t_ref, rt_ref, rtt_ref, size_ref, s_sc, y_sc, tot_sc):
    tl, d = x_ref.shape
    ch = RWKV_CHUNK
    pw = 2 * RWKV_HEAD
    npair = d // pw

    @pl.when(pl.program_id(1) == 0)
    def _():
        s_sc[...] = jnp.zeros(s_sc.shape, F32)

    ri = lax.broadcasted_iota(I32, (pw, pw), 0)
    ci = lax.broadcasted_iota(I32, (pw, pw), 1)
    same_half = (ri < RWKV_HEAD) == (ci < RWKV_HEAD)
    eye = (ri == ci).astype(F32)
    rt_ = lax.broadcasted_iota(I32, (ch, pw), 0)
    ct_ = lax.broadcasted_iota(I32, (ch, pw), 1)
    first = ct_ < RWKV_HEAD
    strict = rt_ > (ct_ & (RWKV_HEAD - 1))
    incl = rt_ >= (ct_ & (RWKV_HEAD - 1))
    m_s0, m_s1 = strict & first, strict & ~first
    m_i0, m_i1 = incl & first, incl & ~first
    first_full = ci < RWKV_HEAD
    zf = jnp.zeros((ch, pw), F32)
    gc_row = lax.broadcasted_iota(I32, (tl // ch, pw), 0)

    group = RWKV_CHUNKS_PER_ITER

    def chunks(cg, carry):
        pairs = range(npair)
        cols = [slice(p * pw, (p + 1) * pw) for p in pairs]
        cidx = [cg * group + k for k in range(group)]
        rows = [pl.ds(pl.multiple_of(c * ch, ch), ch) for c in cidx]
        units = [(k, p) for k in range(group) for p in pairs]
        un = range(len(units))
        cat0 = lambda a, b: jnp.concatenate([a, b], axis=0)
        ld = lambda ref, n: ref[rows[units[n][0]], cols[units[n][1]]]
        lcat = [cat0(ld(la_ref, n), ld(lr_ref, n)) for n in un]
        rb = [ld(rb_ref, n) for n in un]
        rk = [ld(rk_ref, n) for n in un]
        vv = [ld(v_ref, n) for n in un]
        zb = jnp.zeros_like(lcat[0])
        a0 = [_dot_nt(jnp.where(first_full, lcat[n], zb), cat0(rb[n], rk[n])) for n in un]
        a1 = [_dot_nt(jnp.where(first_full, zb, lcat[n]), cat0(rk[n], rb[n])) for n in un]
        pm = [cat0(jnp.where(m_s0, a0[n][:ch], zf), jnp.where(m_s1, a1[n][:ch], zf)) for n in un]
        aak = [cat0(jnp.where(m_s1, a0[n][:ch], zf), jnp.where(m_s0, a1[n][:ch], zf)).astype(BF16)
               for n in un]
        arbk = [jnp.concatenate([cat0(jnp.where(m_i0, a0[n][ch:], zf), jnp.where(m_i1, a1[n][ch:], zf)),
                                 cat0(jnp.where(m_i1, a0[n][ch:], zf), jnp.where(m_i0, a1[n][ch:], zf))],
                                axis=1).astype(BF16) for n in un]
        v2 = [cat0(vv[n], vv[n]) for n in un]
        av = [_dot(aak[n], v2[n]) for n in un]
        tm = [eye + pm[n] for n in un]
        qm = pm
        for _ in range(5):
            qb = [q.astype(BF16) for q in qm]
            qm = [_dot(qb[n], qb[n]) for n in un]
            tm = [tm[n] + _dot(tm[n].astype(BF16), qm[n].astype(BF16)) for n in un]
        tmb = [t.astype(BF16) for t in tm]
        for k in range(group):
            nk = [k * npair + p for p in pairs]
            st = [s_sc[p] for p in pairs]
            w12 = [_dot_nt(lcat[n], st[p].astype(BF16)) for p, n in zip(pairs, nk)]
            rhs = [cat0(w12[p][:ch], w12[p][:ch]) + av[n] for p, n in zip(pairs, nk)]
            ust = [_dot(tmb[n], rhs[p].astype(BF16)) for p, n in zip(pairs, nk)]
            yst = [cat0(w12[p][ch:], w12[p][ch:]) + _dot(arbk[n], cat0(ust[p].astype(BF16), v2[n]))
                   for p, n in zip(pairs, nk)]
            for p in pairs:
                y_sc[rows[k], cols[p]] = jnp.where(first, yst[p][:ch], yst[p][ch:])
            u = [jnp.where(first, ust[p][:ch], ust[p][ch:]).astype(BF16) for p in pairs]
            ds_ = [_dot_tn(cat0(u[p], vv[n]), cat0(sb_ref[rows[k], cols[p]], sk_ref[rows[k], cols[p]]))
                   for p, n in zip(pairs, nk)]
            for p in pairs:
                gcr = jnp.sum(jnp.where(gc_row == cidx[k], gc_ref[:, cols[p]], 0.0), axis=0, keepdims=True)
                s_sc[p] = jnp.where(same_half, st[p] * gcr + ds_[p], 0.0)
        return carry

    lax.fori_loop(0, tl // (ch * group), chunks, 0)

    y = y_sc[...]
    inv_n = 1.0 / RWKV_HEAD
    mean = _seg_sum(y, eseg_ref, eexp_ref) * inv_n
    dlt = y - mean
    var = _seg_sum(dlt * dlt, eseg_ref, eexp_ref) * inv_n
    yn = dlt * lax.rsqrt(var + RWKV_GN_EPS) * vec_ref[5:6, :] + vec_ref[6:7, :]
    z = (yn + bon_ref[...].astype(F32)) * g_ref[...].astype(F32)
    x4 = x_ref[...] + _dot(z.astype(BF16), wo_ref[...])
    x4_ref[...] = x4
    _store_token_tiles(x4t_ref, x4)
    first = jnp.logical_and(pl.program_id(0) == 0, pl.program_id(1) == 0)
    rt = _route(x4, gf_ref, wrh_ref, wrl_ref, br_ref, tri_ref, tot_sc, size_ref, first)
    rt_ref[...] = rt
    rtt_ref[...] = rt.T[:SUBLANES, :]


def _rwkv_scan(pre, x3, vecs, wo, gf, router, batch, seq):
    la, lr, rb, rk, sb, sk, v, bon, g, gc = pre
    t, d = x3.shape
    tl = RWKV_TILE
    nt = seq // tl
    eseg, eexp = _head_indicator(d)
    wrh, wrl, br = router
    row = lambda b, s: (b * nt + s, 0)
    fixed = lambda b, s: (0, 0)
    big = pl.BlockSpec((tl, d), row)
    full = lambda a: pl.BlockSpec(a.shape, fixed)
    pw = 2 * RWKV_HEAD
    tri = _strict_lower(tl)
    return pl.pallas_call(
        _rwkv_scan_kernel,
        grid=(batch, nt),
        in_specs=[big] * 7 + [pl.BlockSpec((tl // RWKV_CHUNK, d), row), big, big, big,
                              full(vecs), full(wo), full(eseg), full(eexp),
                              pl.BlockSpec((1, d), fixed), full(wrh), full(wrl), full(br), full(tri)],
        out_specs=[big, pl.BlockSpec((tl * SUBLANES, LANES), row), pl.BlockSpec((tl, LANES), row),
                   pl.BlockSpec((SUBLANES, tl), lambda b, s: (0, b * nt + s)),
                   pl.BlockSpec((SUBLANES, LANES), fixed)],
        out_shape=[jax.ShapeDtypeStruct((t, d), F32), jax.ShapeDtypeStruct((t * SUBLANES, LANES), F32),
                   jax.ShapeDtypeStruct((t, LANES), F32), jax.ShapeDtypeStruct((SUBLANES, t), F32),
                   jax.ShapeDtypeStruct((SUBLANES, LANES), F32)],
        scratch_shapes=[pltpu.VMEM((d // pw, pw, pw), F32), pltpu.VMEM((tl, d), F32),
                        pltpu.VMEM((1, LANES), F32)],
        compiler_params=_cparams("arbitrary", "arbitrary"),
        name="rwkv_scan_out",
    )(la, lr, rb, rk, sb, sk, v, gc, bon, g, x3, vecs, wo, eseg, eexp, gf.reshape(1, d), wrh, wrl, br, tri)


def _moe_and_embed(x1, x1t, route, route_t, sizes, i, p, norm_ffn, moe_w_gu, moe_w_down, ple_norm, ple_gate,
                   ple_proj, final_norm, final):
    t = x1.shape[0]
    blk_e, n_used, dest, tail, n_blk = _moe_plan(route_t, sizes, t)
    xs = _dispatch(x1t, dest, tail, n_used, n_blk)
    yst = _experts(xs, blk_e, n_used, norm_ffn[i], moe_w_gu, moe_w_down, i)
    p_all = p.reshape(-1, p.shape[-1])
    return _combine(x1, route, yst, dest, p_all, i, ple_norm[i], ple_gate[i].astype(BF16),
                    ple_proj[i].astype(BF16), final_norm, final)


def kernel(x, p, rel_bias, norm_mix, ab_w_in, ab_w_out, lam_q1, lam_k1, lam_q2, lam_k2, sub_g, conv_w, conv_b, lru_wa, lru_ba, lru_wx, lru_bx, lru_lambda, rwkv_mu, rwkv_wr, rwkv_wk, rwkv_wv, rwkv_wo, rwkv_w0, rwkv_w1, rwkv_w2, rwkv_a0, rwkv_a1, rwkv_a2, rwkv_g1, rwkv_g2, rwkv_kk, rwkv_ka, rwkv_rk, rwkv_ln_g, rwkv_ln_b, norm_ffn, moe_wc, moe_bc, moe_wf, moe_bf, moe_w_gu, moe_w_down, ple_norm, ple_gate, ple_proj, final_norm):
    batch, seq, d = x.shape
    t = batch * seq
    depth = norm_mix.shape[0]
    assert depth == 2 and seq % RWKV_TILE == 0 and seq % LRU_TILE == 0 and t % ROW_TILE == 0
    assert d == SUBLANES * LANES and seq % ATTN_TILE == 0 and t % DISPATCH_TILE == 0 and t % COMBINE_TILE == 0
    xs = x.reshape(t, d)
    for i in range(depth):
        j = i // 2
        router = _router_params(moe_wc[i], moe_bc[i], moe_wf[i], moe_bf[i])
        if i % 2 == 0:
            lam_init = 0.8 - 0.6 * math.exp(-0.3 * i)
            q, k, vt, xb, gb = _inproj(xs, norm_mix[i], ab_w_in[j].astype(BF16))
            lam_rows = jnp.stack([lam_q1[j], lam_k1[j], lam_q2[j], lam_k2[j]]).astype(F32)
            ya = _diff_attention(q, k, vt, rel_bias, lam_rows, sub_g[j], lam_init, batch, seq)
            yb = _rglru(xb, gb, conv_w[j], conv_b[j], lru_wa[j], lru_ba[j], lru_wx[j], lru_bx[j],
                        lru_lambda[j], batch, seq)
            x1, x1t, route, route_t, sizes = _outproj(xs, ya, yb, ab_w_out[j].astype(BF16), norm_ffn[i],
                                                      router)
        else:
            vecs = jnp.stack([rwkv_w0[j], rwkv_a0[j], rwkv_kk[j], rwkv_ka[j], rwkv_rk[j].reshape(-1),
                              rwkv_ln_g[j], rwkv_ln_b[j], jnp.zeros((d,), F32)]).astype(F32)
            pre = _rwkv_pre(xs, norm_mix[i], rwkv_mu[j], vecs, rwkv_wr[j], rwkv_wk[j], rwkv_wv[j],
                            rwkv_w1[j], rwkv_w2[j], rwkv_a1[j], rwkv_a2[j], rwkv_g1[j], rwkv_g2[j],
                            batch, seq)
            x1, x1t, route, route_t, sizes = _rwkv_scan(pre, xs, vecs, rwkv_wo[j].astype(BF16), norm_ffn[i],
                                                        router, batch, seq)
        xs = _moe_and_embed(x1, x1t, route, route_t, sizes, i, p, norm_ffn, moe_w_gu, moe_w_down, ple_norm, ple_gate,
                            ple_proj, final_norm, final=(i == depth - 1))
    return xs.reshape(batch, seq, d)
```
